```python
import math
import numpy as np
import jax
import jax.numpy as jnp
from jax import lax

D_MODEL = 2048
BATCH = 4
SEQ = 4096
DEPTH = 2

HEAD_DIM = 64
BLOCK_Q = 128
SB_HEADS = 8
SB_W = SB_HEADS * HEAD_DIM
DIFF_HEADS = 4
DIFF_QK_DIM = 64
DIFF_V_DIM = 2 * DIFF_QK_DIM
DIFF_QK_W = DIFF_HEADS * 2 * DIFF_QK_DIM
DIFF_W = DIFF_HEADS * DIFF_V_DIM
NSA_HEADS = 8
NSA_KV_HEADS = 2
NSA_GROUP = NSA_HEADS // NSA_KV_HEADS
NSA_W = NSA_HEADS * HEAD_DIM
NSA_KV_W = NSA_KV_HEADS * HEAD_DIM
CMP_BLOCK = 32
CMP_STRIDE = 16
CMP_HIDDEN = 256
SLC_BLOCK = 64
SLC_TOPN = 16
WINDOW = 512
N_BRANCH = 3
CONV_CH = 512
CONV_WIDTH = 3
MIX_W = SB_W + DIFF_W + NSA_W + CONV_CH
D_FF = 5632
ALPHA = (2 * DEPTH) ** 0.25
BETA = (8 * DEPTH) ** -0.25
LN_EPS = 1e-5
RMS_EPS = 1e-5
NEG_BIG = -1e30

SEGMENTS = (
    (SB_W, False), (SB_W, False), (SB_W, True),
    (DIFF_QK_W, False), (DIFF_QK_W, False), (DIFF_W, True),
    (NSA_W, False),
    (NSA_KV_W, False), (NSA_KV_W, True),
    (NSA_KV_W, False), (NSA_KV_W, True),
    (NSA_KV_W, False), (NSA_KV_W, True),
    (NSA_HEADS * N_BRANCH, False),
    (CONV_CH, False), (CONV_CH, False), (CONV_CH, True),
)
IN_W = sum(n for n, _ in SEGMENTS)

kernel_name = 'hybrid_sb_diff_nsa_conv_macaron_deepnorm'


def layer_norm(x, g, b):
    xf = x.astype(jnp.float32)
    mu = jnp.mean(xf, axis=-1, keepdims=True)
    var = jnp.mean(jnp.square(xf - mu), axis=-1, keepdims=True)
    y = (xf - mu) * lax.rsqrt(var + LN_EPS)
    return (y * g + b).astype(x.dtype)


def swiglu(x, w1, w3, w2):
    return (jax.nn.silu(x @ w1) * (x @ w3)) @ w2


def split_heads(x, n):
    b, t, _ = x.shape
    return x.reshape(b, t, n, -1).transpose(0, 2, 1, 3)


def merge_heads(x):
    b, h, t, d = x.shape
    return x.transpose(0, 2, 1, 3).reshape(b, t, h * d)


def masked_softmax(s, mask):
    s = jnp.where(mask, s.astype(jnp.float32), NEG_BIG)
    return jnp.where(mask, jax.nn.softmax(s, axis=-1), 0.0)


def query_block(a, i, axis):
    return lax.dynamic_slice_in_dim(a, i * BLOCK_Q, BLOCK_Q, axis=axis)


def sweep_query_blocks(block_fn, seq_len):
    n_blk = seq_len // BLOCK_Q
    out = lax.map(block_fn, jnp.arange(n_blk))
    out = jnp.moveaxis(out, 0, -3)
    return out.reshape(out.shape[:-3] + (n_blk * BLOCK_Q, out.shape[-1]))


def stick_breaking_attention(q, k, v):
    seq_len = q.shape[2]
    scale = q.shape[-1] ** -0.5
    kpos = jnp.arange(seq_len)

    def block(i):
        qb = query_block(q, i, 2)
        t = i * BLOCK_Q + jnp.arange(BLOCK_Q)
        z = jnp.einsum('bhqd,bhkd->bhqk', qb, k).astype(jnp.float32) * scale
        causal = kpos[None, :] < t[:, None]
        log_1m = jnp.where(causal, jax.nn.log_sigmoid(-z), 0.0)
        between = lax.cumsum(log_1m, axis=3, reverse=True) - log_1m
        a = jnp.where(causal, jnp.exp(jax.nn.log_sigmoid(z) + between), 0.0)
        return jnp.einsum('bhqk,bhkd->bhqd', a.astype(v.dtype), v)

    return sweep_query_blocks(block, seq_len)


def differential_attention(q, k, v, lam_params, sub_gain, lam_init):
    seq_len = q.shape[2]
    scale = q.shape[-1] ** -0.5
    lp = lam_params.astype(jnp.float32)
    lam = jnp.exp(jnp.sum(lp[0] * lp[1])) - jnp.exp(jnp.sum(lp[2] * lp[3])) + lam_init
    kpos = jnp.arange(seq_len)

    def block(i):
        qb = query_block(q, i, 2)
        t = i * BLOCK_Q + jnp.arange(BLOCK_Q)
        s = jnp.einsum('bhqnd,bhknd->bhnqk', qb, k) * scale
        p = masked_softmax(s, kpos[None, :] <= t[:, None])
        pd = p[:, :, 0] - lam * p[:, :, 1]
        return jnp.einsum('bhqk,bhkd->bhqd', pd.astype(v.dtype), v)

    o = sweep_query_blocks(block, seq_len).astype(jnp.float32)
    o = o * lax.rsqrt(jnp.mean(jnp.square(o), axis=-1, keepdims=True) + RMS_EPS)
    return (o * sub_gain * (1.0 - lam_init)).astype(v.dtype)


def nsa_attention(q, kc, vc, ks, vs, kw, vw, gates, cmp_pos, cmp_wk1, cmp_wk2, cmp_wv1, cmp_wv2):
    b, hkv, seq_len, d = kc.shape
    scale = d ** -0.5
    qg = q.reshape(b, hkv, NSA_GROUP, seq_len, d)
    t_all = jnp.arange(seq_len)

    n_cmp = (seq_len - CMP_BLOCK) // CMP_STRIDE + 1
    tok = CMP_STRIDE * np.arange(n_cmp)[:, None] + np.arange(CMP_BLOCK)[None, :]

    def compress(x, w1, w2):
        blk = (x[:, :, tok] + cmp_pos).reshape(b, hkv, n_cmp, CMP_BLOCK * d)
        return jax.nn.gelu(blk @ w1) @ w2

    ck = compress(kc, cmp_wk1, cmp_wk2)
    cv = compress(vc, cmp_wv1, cmp_wv2)
    cmp_end = CMP_STRIDE * jnp.arange(n_cmp) + CMP_BLOCK - 1
    cmp_mask = cmp_end[None, :] <= t_all[:, None]
    s_cmp = jnp.einsum('bhgtd,bhcd->bhgtc', qg, ck) * scale
    p_cmp = masked_softmax(s_cmp, cmp_mask)
    o_cmp = jnp.einsum('bhgtc,bhcd->bhgtd', p_cmp.astype(cv.dtype), cv)

    n_slc = seq_len // SLC_BLOCK
    top_n = min(SLC_TOPN, n_slc)
    c_start = CMP_STRIDE * np.arange(n_cmp)
    j_start = SLC_BLOCK * np.arange(n_slc)
    overlap = ((c_start[:, None] < j_start[None, :] + SLC_BLOCK)
               & (c_start[:, None] + CMP_BLOCK > j_start[None, :])).astype(np.float32)
    imp = jnp.einsum('bhgtc,cj->bhtj', p_cmp, jnp.asarray(overlap))
    blk_id = jnp.arange(n_slc)[None, :]
    cur = (t_all // SLC_BLOCK)[:, None]
    valid = blk_id <= cur
    forced = (blk_id == 0) | (blk_id == cur) | (blk_id == cur - 1)
    score = jnp.where(forced, jnp.inf, jnp.where(valid, imp, -jnp.inf))
    _, sel = lax.top_k(score, top_n)

    ks_blk = ks.reshape(b, hkv, n_slc, SLC_BLOCK, d)
    vs_blk = vs.reshape(b, hkv, n_slc, SLC_BLOCK, d)
    bi = jnp.arange(b)[:, None, None, None]
    hi = jnp.arange(hkv)[None, :, None, None]

    def slc_block(i):
        qb = query_block(qg, i, 3)
        idx = query_block(sel, i, 2)
        t = i * BLOCK_Q + jnp.arange(BLOCK_Q)
        kg = ks_blk[bi, hi, idx].reshape(b, hkv, BLOCK_Q, top_n * SLC_BLOCK, d)
        vg = vs_blk[bi, hi, idx].reshape(b, hkv, BLOCK_Q, top_n * SLC_BLOCK, d)
        kpos = (idx[..., None] * SLC_BLOCK + jnp.arange(SLC_BLOCK)).reshape(b, hkv, BLOCK_Q, -1)
        mask = (kpos <= t[:, None])[:, :, None]
        s = jnp.einsum('bhgqd,bhqkd->bhgqk', qb, kg) * scale
        p = masked_softmax(s, mask)
        return jnp.einsum('bhgqk,bhqkd->bhgqd', p.astype(vg.dtype), vg)

    o_slc = sweep_query_blocks(slc_block, seq_len)

    kw_pad = jnp.pad(kw, ((0, 0), (0, 0), (WINDOW, 0), (0, 0)))
    vw_pad = jnp.pad(vw, ((0, 0), (0, 0), (WINDOW, 0), (0, 0)))

    def win_block(i):
        start = i * BLOCK_Q
        qb = query_block(qg, i, 3)
        kb = lax.dynamic_slice_in_dim(kw_pad, start, WINDOW + BLOCK_Q, axis=2)
        vb = lax.dynamic_slice_in_dim(vw_pad, start, WINDOW + BLOCK_Q, axis=2)
        t = (start + jnp.arange(BLOCK_Q))[:, None]
        kpos = (start - WINDOW + jnp.arange(WINDOW + BLOCK_Q))[None, :]
        mask = (kpos <= t) & (kpos > t - WINDOW) & (kpos >= 0)
        s = jnp.einsum('bhgqd,bhkd->bhgqk', qb, kb) * scale
        p = masked_softmax(s, mask)
        return jnp.einsum('bhgqk,bhkd->bhgqd', p.astype(vb.dtype), vb)

    o_win = sweep_query_blocks(win_block, seq_len)

    g = jax.nn.sigmoid(gates.astype(jnp.float32)).reshape(b, seq_len, hkv, NSA_GROUP, N_BRANCH)
    g = g.transpose(0, 2, 3, 1, 4).astype(o_cmp.dtype)
    o = g[..., 0:1] * o_cmp + g[..., 1:2] * o_slc + g[..., 2:3] * o_win
    return o.reshape(b, NSA_HEADS, seq_len, d)


def short_conv_mixer(gate_b, gate_c, h, conv_w):
    u = gate_c * h
    y = lax.conv_general_dilated(u, conv_w[:, None, :], window_strides=(1,),
                                 padding=((CONV_WIDTH - 1, 0),),
                                 dimension_numbers=('NWC', 'WIO', 'NWC'),
                                 feature_group_count=CONV_CH)
    return gate_b * y


def hybrid_mixer(x, w_in, w_out, diff_lam, diff_gain, cmp_pos, cmp_wk1, cmp_wk2,
                 cmp_wv1, cmp_wv2, conv_w, layer_idx):
    b, t, _ = x.shape
    proj = x @ w_in
    offsets = np.cumsum([n for n, _ in SEGMENTS])[:-1].tolist()
    (sb_q, sb_k, sb_v, df_q, df_k, df_v, ns_q, ns_kc, ns_vc, ns_ks, ns_vs,
     ns_kw, ns_vw, ns_g, cv_b, cv_c, cv_h) = jnp.split(proj, offsets, axis=-1)

    o_sb = stick_breaking_attention(split_heads(sb_q, SB_HEADS), split_heads(sb_k, SB_HEADS),
                                    split_heads(sb_v, SB_HEADS))

    lam_init = 0.8 - 0.6 * math.exp(-0.3 * layer_idx)
    dq = df_q.reshape(b, t, DIFF_HEADS, 2, DIFF_QK_DIM).transpose(0, 2, 1, 3, 4)
    dk = df_k.reshape(b, t, DIFF_HEADS, 2, DIFF_QK_DIM).transpose(0, 2, 1, 3, 4)
    o_df = differential_attention(dq, dk, split_heads(df_v, DIFF_HEADS), diff_lam, diff_gain, lam_init)

    o_ns = nsa_attention(split_heads(ns_q, NSA_HEADS),
                         split_heads(ns_kc, NSA_KV_HEADS), split_heads(ns_vc, NSA_KV_HEADS),
                         split_heads(ns_ks, NSA_KV_HEADS), split_heads(ns_vs, NSA_KV_HEADS),
                         split_heads(ns_kw, NSA_KV_HEADS), split_heads(ns_vw, NSA_KV_HEADS),
                         ns_g, cmp_pos, cmp_wk1, cmp_wk2, cmp_wv1, cmp_wv2)

    o_cv = short_conv_mixer(cv_b, cv_c, cv_h, conv_w)

    y = jnp.concatenate([merge_heads(o_sb), merge_heads(o_df), merge_heads(o_ns), o_cv], axis=-1)
    return y @ w_out


def setup_inputs(seed: int = 0) -> dict:
    key = jax.random.key(seed)
    k = jax.random.split(key, 16)

    def nrm(kk, shape, scale):
        return jax.random.normal(kk, shape, jnp.float32) * scale

    col_scale = jnp.concatenate([jnp.full((n,), BETA if is_v else 1.0, jnp.float32)
                                 for n, is_v in SEGMENTS])
    return {
        'x': nrm(k[0], (BATCH, SEQ, D_MODEL), 1.0),
        'ln_g': 1.0 + nrm(k[1], (DEPTH, 3, D_MODEL), 0.02),
        'ln_b': nrm(k[2], (DEPTH, 3, D_MODEL), 0.02),
        'ffn_w1': nrm(k[3], (DEPTH, 2, D_MODEL, D_FF), BETA * D_MODEL ** -0.5),
        'ffn_w3': nrm(k[4], (DEPTH, 2, D_MODEL, D_FF), BETA * D_MODEL ** -0.5),
        'ffn_w2': nrm(k[5], (DEPTH, 2, D_FF, D_MODEL), BETA * D_FF ** -0.5),
        'w_in': nrm(k[6], (DEPTH, D_MODEL, IN_W), D_MODEL ** -0.5) * col_scale,
        'w_out': nrm(k[7], (DEPTH, MIX_W, D_MODEL), BETA * MIX_W ** -0.5),
        'diff_lam': nrm(k[8], (DEPTH, 4, DIFF_QK_DIM), 0.1),
        'diff_gain': 1.0 + nrm(k[9], (DEPTH, DIFF_V_DIM), 0.02),
        'cmp_pos': nrm(k[10], (DEPTH, CMP_BLOCK, HEAD_DIM), 0.1),
        'cmp_wk1': nrm(k[11], (DEPTH, CMP_BLOCK * HEAD_DIM, CMP_HIDDEN), (CMP_BLOCK * HEAD_DIM) ** -0.5),
        'cmp_wk2': nrm(k[12], (DEPTH, CMP_HIDDEN, HEAD_DIM), CMP_HIDDEN ** -0.5),
        'cmp_wv1': nrm(k[13], (DEPTH, CMP_BLOCK * HEAD_DIM, CMP_HIDDEN), (CMP_BLOCK * HEAD_DIM) ** -0.5),
        'cmp_wv2': nrm(k[14], (DEPTH, CMP_HIDDEN, HEAD_DIM), CMP_HIDDEN ** -0.5),
        'conv_w': nrm(k[15], (DEPTH, CONV_WIDTH, CONV_CH), CONV_WIDTH ** -0.5),
    }


def reference(x, ln_g, ln_b, ffn_w1, ffn_w3, ffn_w2, w_in, w_out, diff_lam, diff_gain,
              cmp_pos, cmp_wk1, cmp_wk2, cmp_wv1, cmp_wv2, conv_w):
    h = x
    for l in range(DEPTH):
        h = layer_norm(ALPHA * h + 0.5 * swiglu(h, ffn_w1[l, 0], ffn_w3[l, 0], ffn_w2[l, 0]),
                       ln_g[l, 0], ln_b[l, 0])
        mix = hybrid_mixer(h, w_in[l], w_out[l], diff_lam[l], diff_gain[l], cmp_pos[l],
                           cmp_wk1[l], cmp_wk2[l], cmp_wv1[l], cmp_wv2[l], conv_w[l], l)
        h = layer_norm(ALPHA * h + mix, ln_g[l, 1], ln_b[l, 1])
        h = layer_norm(ALPHA * h + 0.5 * swiglu(h, ffn_w1[l, 1], ffn_w3[l, 1], ffn_w2[l, 1]),
                       ln_g[l, 2], ln_b[l, 2])
    return h
```

```python
import functools
import math

import numpy as np
import jax
import jax.numpy as jnp
from jax import lax
from jax.experimental import pallas as pl
from jax.experimental.pallas import tpu as pltpu

F32 = jnp.float32
BF16 = jnp.bfloat16

HEAD_DIM = 64
SB_HEADS = 8
DIFF_HEADS = 4
NSA_HEADS = 8
NSA_KV_HEADS = 2
NSA_GROUP = NSA_HEADS // NSA_KV_HEADS
CMP_BLOCK = 32
CMP_STRIDE = 16
SLC_BLOCK = 64
SLC_TOPN = 16
WINDOW = 512
N_BRANCH = 3
CONV_CH = 512
CONV_WIDTH = 3
LN_EPS = 1e-5
RMS_EPS = 1e-5
NEG_BIG = -1e30
LANES = 128
VMEM_LIMIT = 56 * 1024 * 1024

_SEG_NAMES = ("sb_q", "sb_k", "sb_v", "df_q", "df_k", "df_v", "ns_q", "ns_kc", "ns_vc",
              "ns_ks", "ns_vs", "ns_kw", "ns_vw", "ns_g", "cv_b", "cv_c", "cv_h")
_SEG_WIDTHS = (512, 512, 512, 512, 512, 512, 512, 128, 128, 128, 128, 128, 128,
               NSA_HEADS * N_BRANCH, 512, 512, 512)
_SEG_OFF = dict(zip(_SEG_NAMES, np.cumsum((0,) + _SEG_WIDTHS[:-1]).tolist()))
_SEG_W = dict(zip(_SEG_NAMES, _SEG_WIDTHS))


def _cols(name):
    return np.arange(_SEG_OFF[name], _SEG_OFF[name] + _SEG_W[name])


_ATT_COLS = np.concatenate([_cols(n) for n in
                            ("sb_q", "sb_k", "sb_v", "df_q", "df_k", "df_v", "ns_q",
                             "ns_ks", "ns_vs", "ns_kw", "ns_vw")])
_GATES_PER_KV = NSA_GROUP * N_BRANCH
_F32_COLS = np.concatenate([_cols(n) for n in ("cv_b", "cv_c", "cv_h", "ns_kc", "ns_vc")])


def _params(sem):
    return pltpu.CompilerParams(dimension_semantics=sem, vmem_limit_bytes=VMEM_LIMIT)


def _layer_norm(y, g, b):
    mu = jnp.mean(y, axis=-1, keepdims=True)
    d = y - mu
    var = jnp.mean(d * d, axis=-1, keepdims=True)
    return d * lax.rsqrt(var + LN_EPS) * g + b


def _dot(a, b):
    return jnp.dot(a, b, preferred_element_type=F32)


def _dot_nt(a, b):
    return lax.dot_general(a, b, (((1,), (1,)), ((), ())), preferred_element_type=F32)


def _ffn_ln_kernel(x_ref, w1_ref, w3_ref, w2_ref, g_ref, b_ref, o_ref, ob_ref, xb_ref, acc_ref,
                   *, alpha):
    j = pl.program_id(1)

    @pl.when(j == 0)
    def _():
        xb_ref[...] = x_ref[...].astype(BF16)
        acc_ref[...] = jnp.zeros_like(acc_ref)

    xb = xb_ref[...]
    a = _dot(xb, w1_ref[...])
    b = _dot(xb, w3_ref[...])
    hm = (a / (1.0 + jnp.exp(-a))) * b
    acc_ref[...] += _dot(hm.astype(BF16), w2_ref[...])

    @pl.when(j == pl.num_programs(1) - 1)
    def _():
        y = alpha * x_ref[...] + 0.5 * acc_ref[...]
        o = _layer_norm(y, g_ref[...], b_ref[...])
        o_ref[...] = o
        ob_ref[...] = o.astype(BF16)


def _ffn_ln(x, w1, w3, w2, g, b, *, alpha, tm, tf):
    n, d = x.shape
    f = w1.shape[1]
    return pl.pallas_call(
        functools.partial(_ffn_ln_kernel, alpha=alpha),
        grid=(n // tm, f // tf),
        in_specs=[
            pl.BlockSpec((tm, d), lambda i, j: (i, 0)),
            pl.BlockSpec((d, tf), lambda i, j: (0, j)),
            pl.BlockSpec((d, tf), lambda i, j: (0, j)),
            pl.BlockSpec((tf, d), lambda i, j: (j, 0)),
            pl.BlockSpec((1, d), lambda i, j: (0, 0)),
            pl.BlockSpec((1, d), lambda i, j: (0, 0)),
        ],
        out_specs=[pl.BlockSpec((tm, d), lambda i, j: (i, 0)),
                   pl.BlockSpec((tm, d), lambda i, j: (i, 0))],
        out_shape=[jax.ShapeDtypeStruct((n, d), F32), jax.ShapeDtypeStruct((n, d), BF16)],
        scratch_shapes=[pltpu.VMEM((tm, d), BF16), pltpu.VMEM((tm, d), F32)],
        compiler_params=_params(("parallel", "arbitrary")),
        name="ffn_ln",
    )(x, w1, w3, w2, g.reshape(1, d), b.reshape(1, d))


def _matmul_kernel(x_ref, w_ref, o_ref):
    o_ref[...] = _dot(x_ref[...], w_ref[...]).astype(o_ref.dtype)


def _matmul(x, w, out_dtype, *, tm, tn, name):
    n, d = x.shape
    m = w.shape[1]
    return pl.pallas_call(
        _matmul_kernel,
        grid=(n // tm, m // tn),
        in_specs=[pl.BlockSpec((tm, d), lambda i, j: (i, 0)),
                  pl.BlockSpec((d, tn), lambda i, j: (0, j))],
        out_specs=pl.BlockSpec((tm, tn), lambda i, j: (i, j)),
        out_shape=jax.ShapeDtypeStruct((n, m), out_dtype),
        compiler_params=_params(("parallel", "arbitrary")),
        name=name,
    )(x, w)


def _sb_kernel(q_ref, k_ref, v_ref, o_ref, *, tq):
    i = pl.program_id(2)
    lane = lax.broadcasted_iota(jnp.int32, (tq, LANES), 1)
    row = lax.broadcasted_iota(jnp.int32, (tq, tq), 0)
    col = lax.broadcasted_iota(jnp.int32, (tq, tq), 1)
    strictly_causal = col < row
    tri = jnp.where(row >= col, 1.0, 0.0).astype(BF16)
    qs = (q_ref[0].astype(F32) * (HEAD_DIM ** -0.5)).astype(BF16)

    def tile(kt, qh, c, acc, diagonal):
        start = pl.multiple_of(kt * tq, tq)
        ks = k_ref[0, pl.ds(start, tq), :]
        vs = v_ref[0, pl.ds(start, tq), :]
        z = _dot_nt(qh, ks)
        lg = -(jnp.maximum(z, 0.0) + jnp.log1p(jnp.exp(-jnp.abs(z))))
        if diagonal:
            lg = jnp.where(strictly_causal, lg, 0.0)
        hi = lg.astype(BF16)
        lo = (lg - hi.astype(F32)).astype(BF16)
        suffix = _dot(hi, tri) + _dot(lo, tri)
        a = jnp.exp(z + suffix + c)
        if diagonal:
            a = jnp.where(strictly_causal, a, 0.0)
        acc = acc + _dot(a.astype(BF16), vs)
        return c + suffix[:, 0:1], acc

    outs = []
    for h in range(2):
        qh = jnp.where((lane < HEAD_DIM) == (h == 0), qs, jnp.zeros_like(qs))
        c0 = jnp.zeros((tq, 1), F32)
        acc0 = jnp.zeros((tq, LANES), F32)
        c1, acc1 = tile(i, qh, c0, acc0, True)

        def body(n, carry, qh=qh):
            return tile(i - n, qh, carry[0], carry[1], False)

        _, acc = lax.fori_loop(1, i + 1, body, (c1, acc1))
        outs.append(acc)
    o_ref[0] = jnp.where(lane < HEAD_DIM, outs[0], outs[1]).astype(o_ref.dtype)


def _sb_attention(pa, *, tq, q_blk, k_blk, v_blk):
    b, t, _ = pa.shape
    n_pairs = SB_HEADS // 2
    return pl.pallas_call(
        functools.partial(_sb_kernel, tq=tq),
        grid=(b, n_pairs, t // tq),
        in_specs=[
            pl.BlockSpec((1, tq, LANES), lambda bi, p, i: (bi, i, q_blk + p)),
            pl.BlockSpec((1, t, LANES), lambda bi, p, i: (bi, 0, k_blk + p)),
            pl.BlockSpec((1, t, LANES), lambda bi, p, i: (bi, 0, v_blk + p)),
        ],
        out_specs=pl.BlockSpec((1, tq, LANES), lambda bi, p, i: (bi, i, p)),
        out_shape=jax.ShapeDtypeStruct((b, t, n_pairs * LANES), BF16),
        compiler_params=_params(("parallel", "parallel", "arbitrary")),
        name="sb_attention",
    )(pa, pa, pa)


def _diff_kernel(lam_ref, gain_ref, q_ref, k_ref, v_ref, o_ref, *, tq, lam_init):
    i = pl.program_id(2)
    lane = lax.broadcasted_iota(jnp.int32, (tq, LANES), 1)
    row = lax.broadcasted_iota(jnp.int32, (tq, tq), 0)
    col = lax.broadcasted_iota(jnp.int32, (tq, tq), 1)
    causal = col <= row
    causal2 = jnp.concatenate([causal, causal], axis=0)
    qs = (q_ref[0].astype(F32) * (HEAD_DIM ** -0.5)).astype(BF16)
    zero = jnp.zeros_like(qs)
    q2 = jnp.concatenate([jnp.where(lane < HEAD_DIM, qs, zero),
                          jnp.where(lane < HEAD_DIM, zero, qs)], axis=0)

    def tile(kt, m, l, acc, diagonal):
        start = pl.multiple_of(kt * tq, tq)
        ks = k_ref[0, pl.ds(start, tq), :]
        vs = v_ref[0, pl.ds(start, tq), :]
        s = _dot_nt(q2, ks)
        if diagonal:
            s = jnp.where(causal2, s, NEG_BIG)
        m_new = jnp.maximum(m, jnp.max(s, axis=1, keepdims=True))
        p = jnp.exp(s - m_new)
        scale = jnp.exp(m - m_new)
        l = scale * l + jnp.sum(p, axis=1, keepdims=True)
        acc = scale * acc + _dot(p.astype(BF16), vs)
        return m_new, l, acc

    m0 = jnp.full((2 * tq, 1), NEG_BIG, F32)
    l0 = jnp.zeros((2 * tq, 1), F32)
    acc0 = jnp.zeros((2 * tq, LANES), F32)
    carry = tile(i, m0, l0, acc0, True)

    def body(n, carry):
        return tile(i - n, carry[0], carry[1], carry[2], False)

    _, l, acc = lax.fori_loop(1, i + 1, body, carry)
    o = acc / l
    lp = lam_ref[...]
    lam = (jnp.exp(jnp.sum(lp[0:1] * lp[1:2], axis=1, keepdims=True))
           - jnp.exp(jnp.sum(lp[2:3] * lp[3:4], axis=1, keepdims=True)) + lam_init)
    od = o[:tq] - lam * o[tq:]
    od = od * lax.rsqrt(jnp.mean(od * od, axis=-1, keepdims=True) + RMS_EPS)
    o_ref[0] = (od * gain_ref[...] * (1.0 - lam_init)).astype(o_ref.dtype)


def _diff_attention(pa, lam_params, gain, *, tq, q_blk, k_blk, v_blk, lam_init):
    b, t, _ = pa.shape
    return pl.pallas_call(
        functools.partial(_diff_kernel, tq=tq, lam_init=lam_init),
        grid=(b, DIFF_HEADS, t // tq),
        in_specs=[
            pl.BlockSpec((4, HEAD_DIM), lambda bi, h, i: (0, 0)),
            pl.BlockSpec((1, LANES), lambda bi, h, i: (0, 0)),
            pl.BlockSpec((1, tq, LANES), lambda bi, h, i: (bi, i, q_blk + h)),
            pl.BlockSpec((1, t, LANES), lambda bi, h, i: (bi, 0, k_blk + h)),
            pl.BlockSpec((1, t, LANES), lambda bi, h, i: (bi, 0, v_blk + h)),
        ],
        out_specs=pl.BlockSpec((1, tq, LANES), lambda bi, h, i: (bi, i, h)),
        out_shape=jax.ShapeDtypeStruct((b, t, DIFF_HEADS * LANES), BF16),
        compiler_params=_params(("parallel", "parallel", "arbitrary")),
        name="diff_attention",
    )(lam_params, gain.reshape(1, LANES), pa, pa, pa)


def _compress_kernel(x_ref, pos_ref, w1_ref, w2_ref, o_ref):
    x = x_ref[0, 0]
    n16, half = x.shape
    xa = (x + pos_ref[0:1]).astype(BF16)
    xb = (x + pos_ref[1:2]).astype(BF16)
    w1 = w1_ref[0]
    first = _dot(xa, w1[:half])
    second = _dot(xb, w1[half:])
    h = first + pltpu.roll(second, n16 - 1, 0)
    g = 0.5 * h * (1.0 + jnp.tanh(math.sqrt(2.0 / math.pi) * (h + 0.044715 * (h * h * h))))
    o_ref[0, 0] = _dot(g.astype(BF16), w2_ref[0])


def _compress(x16, pos2, w1, w2):
    b, n_streams, n16, half = x16.shape
    d = w2.shape[-1]
    return pl.pallas_call(
        _compress_kernel,
        grid=(b, n_streams),
        in_specs=[
            pl.BlockSpec((1, 1, n16, half), lambda bi, s: (bi, s, 0, 0)),
            pl.BlockSpec((2, half), lambda bi, s: (0, 0)),
            pl.BlockSpec((1, 2 * half, w1.shape[-1]), lambda bi, s: (s // NSA_KV_HEADS, 0, 0)),
            pl.BlockSpec((1, w2.shape[1], d), lambda bi, s: (s // NSA_KV_HEADS, 0, 0)),
        ],
        out_specs=pl.BlockSpec((1, 1, n16, d), lambda bi, s: (bi, s, 0, 0)),
        out_shape=jax.ShapeDtypeStruct((b, n_streams, n16, d), F32),
        compiler_params=_params(("parallel", "arbitrary")),
        name="nsa_compress",
    )(x16, pos2, w1, w2)


def _nsa_kernel(q_ref, g_ref, ck_ref, cv_ref, ovt_ref, ks_ref, vs_ref, kw_ref, vw_ref, o_ref,
                *, tq, top_n):
    i = pl.program_id(2)
    grp = NSA_GROUP
    d = HEAD_DIM
    q0 = i * tq
    n_cmp = ck_ref.shape[2]
    n_slc = ovt_ref.shape[0]

    q = q_ref[0]
    q4 = jnp.concatenate([q[:, h * d:(h + 1) * d] for h in range(grp)], axis=0)
    q4 = (q4.astype(F32) * (d ** -0.5)).astype(BF16)
    t4 = q0 + lax.rem(lax.broadcasted_iota(jnp.int32, (grp * tq, 1), 0), tq)

    def rep4(x):
        return jnp.concatenate([x] * grp, axis=0)

    ck = ck_ref[0, 0].astype(BF16)
    cv = cv_ref[0, 0].astype(BF16)
    s = _dot_nt(q4, ck)
    cmp_end = CMP_STRIDE * lax.broadcasted_iota(jnp.int32, (1, n_cmp), 1) + (CMP_BLOCK - 1)
    cmask = cmp_end <= t4
    s = jnp.where(cmask, s, NEG_BIG)
    e = jnp.exp(s - jnp.max(s, axis=1, keepdims=True))
    p = e / jnp.sum(e, axis=1, keepdims=True)
    p = jnp.where(cmask, p, 0.0)
    o_cmp = _dot(p.astype(BF16), cv)

    pg = p[0:tq]
    for h in range(1, grp):
        pg = pg + p[h * tq:(h + 1) * tq]
    p_hi = pg.astype(BF16)
    r1 = pg - p_hi.astype(F32)
    p_mid = r1.astype(BF16)
    p_lo = (r1 - p_mid.astype(F32)).astype(BF16)
    ovt = ovt_ref[...]
    imp_t = _dot_nt(ovt, p_hi) + _dot_nt(ovt, p_mid) + _dot_nt(ovt, p_lo)
    blk = lax.broadcasted_iota(jnp.int32, (n_slc, tq), 0)
    cur = (q0 + lax.broadcasted_iota(jnp.int32, (n_slc, tq), 1)) // SLC_BLOCK
    forced = (blk == 0) | (blk == cur) | (blk == cur - 1)
    score = jnp.where(forced, jnp.inf, jnp.where(blk <= cur, imp_t, -jnp.inf))
    rank = jnp.zeros((n_slc, tq), F32)
    for j in range(n_slc):
        sj = score[j:j + 1, :]
        tie = jnp.where(blk > j, 1.0, 0.0)
        rank = rank + jnp.where(sj > score, 1.0, jnp.where(sj == score, tie, 0.0))
    sel_t = jnp.where(rank < top_n, 1.0, 0.0).astype(BF16)
    eye = jnp.where(lax.broadcasted_iota(jnp.int32, (tq, tq), 0)
                    == lax.broadcasted_iota(jnp.int32, (tq, tq), 1), 1.0, 0.0).astype(BF16)
    sel = _dot_nt(eye, sel_t).astype(BF16)

    row = lax.broadcasted_iota(jnp.int32, (tq, tq), 0)
    col = lax.broadcasted_iota(jnp.int32, (tq, tq), 1)
    causal4 = rep4(col <= row)

    def softmax_step(s, m, l, acc, v):
        m_new = jnp.maximum(m, jnp.max(s, axis=1, keepdims=True))
        pt = jnp.exp(s - m_new)
        scale = jnp.exp(m - m_new)
        return m_new, scale * l + jnp.sum(pt, axis=1, keepdims=True), scale * acc + _dot(pt.astype(BF16), v)

    def init():
        return (jnp.full((grp * tq, 1), NEG_BIG, F32), jnp.zeros((grp * tq, 1), F32),
                jnp.zeros((grp * tq, d), F32))

    def slc_tile(kt, m, l, acc, diagonal):
        start = pl.multiple_of(kt * tq, tq)
        k = ks_ref[0, 0, pl.ds(start, tq), :]
        v = vs_ref[0, 0, pl.ds(start, tq), :]
        s = _dot_nt(q4, k)
        key_blk = (start + lax.broadcasted_iota(jnp.int32, (n_slc, tq), 1)) // SLC_BLOCK
        expand = jnp.where(key_blk == blk, 1.0, 0.0).astype(BF16)
        keep = rep4(_dot(sel, expand)) > 0.5
        if diagonal:
            keep = keep & causal4
        return softmax_step(jnp.where(keep, s, NEG_BIG), m, l, acc, v)

    carry = slc_tile(i, *init(), True)
    _, l, acc = lax.fori_loop(1, i + 1, lambda n, c: slc_tile(i - n, c[0], c[1], c[2], False), carry)
    o_slc = acc / l

    def win_tile(kt, m, l, acc, diagonal):
        start = pl.multiple_of(kt * tq, tq)
        k = kw_ref[0, 0, pl.ds(start, tq), :]
        v = vw_ref[0, 0, pl.ds(start, tq), :]
        s = _dot_nt(q4, k)
        if diagonal:
            keep = causal4
        else:
            kpos = start + lax.broadcasted_iota(jnp.int32, (1, tq), 1)
            keep = kpos > t4 - WINDOW
        return softmax_step(jnp.where(keep, s, NEG_BIG), m, l, acc, v)

    carry = win_tile(i, *init(), True)
    n_back = jnp.minimum(i, -(-WINDOW // tq))
    _, l, acc = lax.fori_loop(1, n_back + 1, lambda n, c: win_tile(i - n, c[0], c[1], c[2], False), carry)
    o_win = acc / l

    gate = 1.0 / (1.0 + jnp.exp(-g_ref[0]))
    heads = []
    for h in range(grp):
        rows = slice(h * tq, (h + 1) * tq)
        heads.append(gate[:, 3 * h:3 * h + 1] * o_cmp[rows]
                     + gate[:, 3 * h + 1:3 * h + 2] * o_slc[rows]
                     + gate[:, 3 * h + 2:3 * h + 3] * o_win[rows])
    o_ref[0] = jnp.concatenate(heads, axis=1).astype(o_ref.dtype)


def _nsa_attention(pa, pf, ckv, ovt, kv4, *, tq, q_blk, g_blk, top_n):
    b, t, _ = pa.shape
    n16 = ckv.shape[2]
    n_slc = ovt.shape[0]
    qw = NSA_GROUP * HEAD_DIM
    kv_spec = pl.BlockSpec((1, 1, t, HEAD_DIM), lambda bi, j, i: (bi, j, 0, 0))
    return pl.pallas_call(
        functools.partial(_nsa_kernel, tq=tq, top_n=top_n),
        grid=(b, NSA_KV_HEADS, t // tq),
        in_specs=[
            pl.BlockSpec((1, tq, qw), lambda bi, j, i: (bi, i, q_blk + j)),
            pl.BlockSpec((1, tq, LANES), lambda bi, j, i: (bi, i, g_blk + j)),
            pl.BlockSpec((1, 1, n16, HEAD_DIM), lambda bi, j, i: (bi, j, 0, 0)),
            pl.BlockSpec((1, 1, n16, HEAD_DIM), lambda bi, j, i: (bi, NSA_KV_HEADS + j, 0, 0)),
            pl.BlockSpec((n_slc, n16), lambda bi, j, i: (0, 0)),
            kv_spec, kv_spec, kv_spec, kv_spec,
        ],
        out_specs=pl.BlockSpec((1, tq, qw), lambda bi, j, i: (bi, i, j)),
        out_shape=jax.ShapeDtypeStruct((b, t, NSA_HEADS * HEAD_DIM), BF16),
        compiler_params=_params(("parallel", "parallel", "arbitrary")),
        name="nsa_attention",
    )(pa, pf, ckv, ckv, ovt, kv4[0], kv4[1], kv4[2], kv4[3])


def _conv_kernel(cb_ref, cc_ref, ch_ref, w_ref, o_ref, prev_ref):
    @pl.when(pl.program_id(1) == 0)
    def _():
        prev_ref[...] = jnp.zeros_like(prev_ref)

    u = cc_ref[0] * ch_ref[0]
    tt = u.shape[0]
    row = lax.broadcasted_iota(jnp.int32, u.shape, 0)
    last1 = prev_ref[7:8]
    last2 = prev_ref[6:7]
    u1 = jnp.where(row >= 1, pltpu.roll(u, 1, 0), last1)
    u2 = jnp.where(row >= 2, pltpu.roll(u, 2, 0), jnp.where(row == 1, last1, last2))
    w = w_ref[...]
    o_ref[0] = (cb_ref[0] * (w[0:1] * u2 + w[1:2] * u1 + w[2:3] * u)).astype(o_ref.dtype)
    prev_ref[...] = u[tt - 8:tt]


def _conv_mixer(pf, conv_w, *, tt):
    b, t, _ = pf.shape
    c = CONV_CH
    return pl.pallas_call(
        _conv_kernel,
        grid=(b, t // tt),
        in_specs=[
            pl.BlockSpec((1, tt, c), lambda bi, i: (bi, i, 0)),
            pl.BlockSpec((1, tt, c), lambda bi, i: (bi, i, 1)),
            pl.BlockSpec((1, tt, c), lambda bi, i: (bi, i, 2)),
            pl.BlockSpec((CONV_WIDTH, c), lambda bi, i: (0, 0)),
        ],
        out_specs=pl.BlockSpec((1, tt, c), lambda bi, i: (bi, i, 0)),
        out_shape=jax.ShapeDtypeStruct((b, t, c), BF16),
        scratch_shapes=[pltpu.VMEM((8, c), F32)],
        compiler_params=_params(("parallel", "arbitrary")),
        name="conv_mixer",
    )(pf, pf, pf, conv_w)


def _out_ln_kernel(a_ref, b_ref, c_ref, d_ref, w_ref, x_ref, g_ref, beta_ref, o_ref, ob_ref, *, alpha):
    kw = a_ref.shape[1]
    mix = _dot(a_ref[...], w_ref[0:kw])
    for n, r in enumerate((b_ref, c_ref, d_ref), start=1):
        mix = mix + _dot(r[...], w_ref[n * kw:(n + 1) * kw])
    o = _layer_norm(alpha * x_ref[...] + mix, g_ref[...], beta_ref[...])
    o_ref[...] = o
    ob_ref[...] = o.astype(BF16)


def _out_ln(parts, w, x, g, b, *, alpha, tm):
    n, d = x.shape
    kw = parts[0].shape[1]
    part_spec = pl.BlockSpec((tm, kw), lambda i: (i, 0))
    return pl.pallas_call(
        functools.partial(_out_ln_kernel, alpha=alpha),
        grid=(n // tm,),
        in_specs=[part_spec, part_spec, part_spec, part_spec,
                  pl.BlockSpec(w.shape, lambda i: (0, 0)),
                  pl.BlockSpec((tm, d), lambda i: (i, 0)),
                  pl.BlockSpec((1, d), lambda i: (0, 0)),
                  pl.BlockSpec((1, d), lambda i: (0, 0))],
        out_specs=[pl.BlockSpec((tm, d), lambda i: (i, 0)), pl.BlockSpec((tm, d), lambda i: (i, 0))],
        out_shape=[jax.ShapeDtypeStruct((n, d), F32), jax.ShapeDtypeStruct((n, d), BF16)],
        compiler_params=_params(("parallel",)),
        name="out_ln",
    )(*parts, w, x, g.reshape(1, d), b.reshape(1, d))


def _overlap_t(t):
    n16 = t // CMP_STRIDE
    n_slc = t // SLC_BLOCK
    c_start = CMP_STRIDE * np.arange(n16)
    j_start = SLC_BLOCK * np.arange(n_slc)
    ov = ((c_start[None, :] < j_start[:, None] + SLC_BLOCK)
          & (c_start[None, :] + CMP_BLOCK > j_start[:, None])).astype(np.float32)
    ov[:, n16 - 1] = 0.0
    return jnp.asarray(ov, BF16)


def _mixer(hf, hb, batch, w_att, w_f32, w_out, ln_g, ln_b, diff_lam, diff_gain, pos2, cmp_w1, cmp_w2,
           conv_w, layer_idx, alpha):
    n, _ = hf.shape
    t = n // batch
    pa = _matmul(hb, w_att, BF16, tm=1024, tn=512, name="proj_att").reshape(batch, t, -1)
    pf = _matmul(hb, w_f32, F32, tm=1024, tn=512, name="proj_f32").reshape(batch, t, -1)

    o_sb = _sb_attention(pa, tq=256, q_blk=0, k_blk=4, v_blk=8)
    lam_init = 0.8 - 0.6 * math.exp(-0.3 * layer_idx)
    o_df = _diff_attention(pa, diff_lam, diff_gain, tq=256, q_blk=12, k_blk=16, v_blk=20, lam_init=lam_init)

    n16 = t // CMP_STRIDE
    kvc = pf[:, :, 3 * CONV_CH:3 * CONV_CH + 2 * LANES].reshape(batch, t, 2, NSA_KV_HEADS, HEAD_DIM)
    x16 = kvc.transpose(0, 2, 3, 1, 4).reshape(batch, 2 * NSA_KV_HEADS, n16, CMP_STRIDE * HEAD_DIM)
    ckv = _compress(x16, pos2, cmp_w1, cmp_w2)
    kv4 = pa[:, :, 28 * LANES:32 * LANES].reshape(batch, t, 4, NSA_KV_HEADS, HEAD_DIM).transpose(2, 0, 3, 1, 4)
    n_slc = t // SLC_BLOCK
    o_ns = _nsa_attention(pa, pf, ckv, _overlap_t(t), kv4, tq=128, q_blk=12, g_blk=14,
                          top_n=min(SLC_TOPN, n_slc))

    o_cv = _conv_mixer(pf, conv_w, tt=512)
    parts = [o.reshape(n, -1) for o in (o_sb, o_df, o_ns, o_cv)]
    return _out_ln(parts, w_out, hf, ln_g, ln_b, alpha=alpha, tm=512)


def kernel(x, ln_g, ln_b, ffn_w1, ffn_w3, ffn_w2, w_in, w_out, diff_lam, diff_gain, cmp_pos, cmp_wk1,
           cmp_wk2, cmp_wv1, cmp_wv2, conv_w):
    batch, t, d = x.shape
    depth = ln_g.shape[0]
    alpha = (2 * depth) ** 0.25
    n = batch * t

    w1b, w3b, w2b = ffn_w1.astype(BF16), ffn_w3.astype(BF16), ffn_w2.astype(BF16)
    w_att = w_in[:, :, _ATT_COLS].astype(BF16)
    gate_cols = []
    for j in range(NSA_KV_HEADS):
        real = w_in[:, :, _SEG_OFF["ns_g"] + j * _GATES_PER_KV:_SEG_OFF["ns_g"] + (j + 1) * _GATES_PER_KV]
        gate_cols.append(jnp.pad(real, ((0, 0), (0, 0), (0, LANES - _GATES_PER_KV))))
    w_f32 = jnp.concatenate([w_in[:, :, _F32_COLS]] + gate_cols, axis=-1).astype(BF16)
    w_outb = w_out.astype(BF16)
    half = CMP_STRIDE * HEAD_DIM
    pos2 = cmp_pos.reshape(depth, 2, half)
    cmp_w1 = jnp.stack([cmp_wk1, cmp_wv1], axis=1).astype(BF16)
    cmp_w2 = jnp.stack([cmp_wk2, cmp_wv2], axis=1).astype(BF16)

    hf = x.reshape(n, d)
    for l in range(depth):
        hf, hb = _ffn_ln(hf, w1b[l, 0], w3b[l, 0], w2b[l, 0], ln_g[l, 0], ln_b[l, 0], alpha=alpha, tm=512, tf=512)
        hf, _ = _mixer(hf, hb, batch, w_att[l], w_f32[l], w_outb[l], ln_g[l, 1], ln_b[l, 1], diff_lam[l],
                       diff_gain[l], pos2[l], cmp_w1[l], cmp_w2[l], conv_w[l], l, alpha)
        hf, _ = _ffn_ln(hf, w1b[l, 1], w3b[l, 1], w2b[l, 1], ln_g[l, 2], ln_b[l, 2], alpha=alpha, tm=512, tf=512)
    return hf.reshape(batch, t, d)
```

```python
import functools
import math

import numpy as np
import jax
import jax.numpy as jnp
from jax import lax
from jax.experimental import pallas as pl
from jax.experimental.pallas import tpu as pltpu

F32 = jnp.float32
BF16 = jnp.bfloat16

HEAD_DIM = 64
SB_HEADS = 8
DIFF_HEADS = 4
NSA_HEADS = 8
NSA_KV_HEADS = 2
NSA_GROUP = NSA_HEADS // NSA_KV_HEADS
CMP_BLOCK = 32
CMP_STRIDE = 16
SLC_BLOCK = 64
SLC_TOPN = 16
WINDOW = 512
N_BRANCH = 3
CONV_CH = 512
CONV_WIDTH = 3
LN_EPS = 1e-5
RMS_EPS = 1e-5
NEG_BIG = -1e30
EXP_UNDERFLOW = -104.0
LANES = 128
VMEM_LIMIT = 56 * 1024 * 1024

_SEG_NAMES = ("sb_q", "sb_k", "sb_v", "df_q", "df_k", "df_v", "ns_q", "ns_kc", "ns_vc",
              "ns_ks", "ns_vs", "ns_kw", "ns_vw", "ns_g", "cv_b", "cv_c", "cv_h")
_SEG_WIDTHS = (512, 512, 512, 512, 512, 512, 512, 128, 128, 128, 128, 128, 128,
               NSA_HEADS * N_BRANCH, 512, 512, 512)
_SEG_OFF = dict(zip(_SEG_NAMES, np.cumsum((0,) + _SEG_WIDTHS[:-1]).tolist()))
_SEG_W = dict(zip(_SEG_NAMES, _SEG_WIDTHS))


def _cols(name):
    return np.arange(_SEG_OFF[name], _SEG_OFF[name] + _SEG_W[name])


_ATT_COLS = np.concatenate([_cols(n) for n in
                            ("sb_q", "sb_k", "sb_v", "df_q", "df_k", "df_v", "ns_q",
                             "ns_ks", "ns_vs", "ns_kw", "ns_vw")])
_GATES_PER_KV = NSA_GROUP * N_BRANCH
_F32_COLS = np.concatenate([_cols(n) for n in ("cv_b", "cv_c", "cv_h", "ns_kc", "ns_vc")])


def _params(sem):
    return pltpu.CompilerParams(dimension_semantics=sem, vmem_limit_bytes=VMEM_LIMIT)


def _layer_norm(y, g, b):
    mu = jnp.mean(y, axis=-1, keepdims=True)
    d = y - mu
    var = jnp.mean(d * d, axis=-1, keepdims=True)
    return d * lax.rsqrt(var + LN_EPS) * g + b


def _dot(a, b):
    return jnp.dot(a, b, preferred_element_type=F32)


def _dot_nt(a, b):
    return lax.dot_general(a, b, (((1,), (1,)), ((), ())), preferred_element_type=F32)


def _ffn_ln_kernel(x_ref, w1_ref, w3_ref, w2_ref, g_ref, b_ref, o_ref, ob_ref, xb_ref, acc_ref,
                   *, alpha):
    j = pl.program_id(1)

    @pl.when(j == 0)
    def _():
        xb_ref[...] = x_ref[...].astype(BF16)
        acc_ref[...] = jnp.zeros_like(acc_ref)

    xb = xb_ref[...]
    a = _dot(xb, w1_ref[...])
    b = _dot(xb, w3_ref[...])
    hm = (a / (1.0 + jnp.exp(-a))) * b
    acc_ref[...] += _dot(hm.astype(BF16), w2_ref[...])

    @pl.when(j == pl.num_programs(1) - 1)
    def _():
        y = alpha * x_ref[...] + 0.5 * acc_ref[...]
        o = _layer_norm(y, g_ref[...], b_ref[...])
        o_ref[...] = o
        ob_ref[...] = o.astype(BF16)


def _ffn_ln(x, w1, w3, w2, g, b, *, alpha, tm, tf):
    n, d = x.shape
    f = w1.shape[1]
    return pl.pallas_call(
        functools.partial(_ffn_ln_kernel, alpha=alpha),
        grid=(n // tm, f // tf),
        in_specs=[
            pl.BlockSpec((tm, d), lambda i, j: (i, 0)),
            pl.BlockSpec((d, tf), lambda i, j: (0, j)),
            pl.BlockSpec((d, tf), lambda i, j: (0, j)),
            pl.BlockSpec((tf, d), lambda i, j: (j, 0)),
            pl.BlockSpec((1, d), lambda i, j: (0, 0)),
            pl.BlockSpec((1, d), lambda i, j: (0, 0)),
        ],
        out_specs=[pl.BlockSpec((tm, d), lambda i, j: (i, 0)),
                   pl.BlockSpec((tm, d), lambda i, j: (i, 0))],
        out_shape=[jax.ShapeDtypeStruct((n, d), F32), jax.ShapeDtypeStruct((n, d), BF16)],
        scratch_shapes=[pltpu.VMEM((tm, d), BF16), pltpu.VMEM((tm, d), F32)],
        compiler_params=_params(("parallel", "arbitrary")),
        name="ffn_ln",
    )(x, w1, w3, w2, g.reshape(1, d), b.reshape(1, d))


def _matmul_kernel(x_ref, w_ref, o_ref):
    o_ref[...] = _dot(x_ref[...], w_ref[...]).astype(o_ref.dtype)


def _matmul(x, w, out_dtype, *, tm, tn, name):
    n, d = x.shape
    m = w.shape[1]
    return pl.pallas_call(
        _matmul_kernel,
        grid=(n // tm, m // tn),
        in_specs=[pl.BlockSpec((tm, d), lambda i, j: (i, 0)),
                  pl.BlockSpec((d, tn), lambda i, j: (0, j))],
        out_specs=pl.BlockSpec((tm, tn), lambda i, j: (i, j)),
        out_shape=jax.ShapeDtypeStruct((n, m), out_dtype),
        compiler_params=_params(("parallel", "arbitrary")),
        name=name,
    )(x, w)


def _sb_kernel(q_ref, k_ref, v_ref, o_ref, *, tq):
    i = pl.program_id(2)
    lane = lax.broadcasted_iota(jnp.int32, (tq, LANES), 1)
    row = lax.broadcasted_iota(jnp.int32, (tq, tq), 0)
    col = lax.broadcasted_iota(jnp.int32, (tq, tq), 1)
    strictly_causal = col < row
    causal2 = jnp.concatenate([strictly_causal, strictly_causal], axis=0)
    tri = jnp.where(row >= col, 1.0, 0.0).astype(BF16)
    qs = (q_ref[0].astype(F32) * (HEAD_DIM ** -0.5)).astype(BF16)
    zero = jnp.zeros_like(qs)
    q2 = jnp.concatenate([jnp.where(lane < HEAD_DIM, qs, zero),
                          jnp.where(lane < HEAD_DIM, zero, qs)], axis=0)

    def tile(kt, c, acc, diagonal):
        start = pl.multiple_of(kt * tq, tq)
        ks = k_ref[0, pl.ds(start, tq), :]
        vs = v_ref[0, pl.ds(start, tq), :]
        z = _dot_nt(q2, ks)
        lg = -(jnp.maximum(z, 0.0) + jnp.log1p(jnp.exp(-jnp.abs(z))))
        if diagonal:
            lg = jnp.where(causal2, lg, 0.0)
        hi = lg.astype(BF16)
        lo = (lg - hi.astype(F32)).astype(BF16)
        suffix = _dot(hi, tri) + _dot(lo, tri)
        a = jnp.exp(z + suffix + c)
        if diagonal:
            a = jnp.where(causal2, a, 0.0)
        return c + suffix[:, 0:1], acc + _dot(a.astype(BF16), vs)

    c1, acc1 = tile(i, jnp.zeros((2 * tq, 1), F32), jnp.zeros((2 * tq, LANES), F32), True)

    def live(c):
        return (jnp.max(c) > EXP_UNDERFLOW).astype(jnp.int32)

    def cond(carry):
        return (carry[0] <= i) & (carry[1] > 0)

    def body(carry):
        n, _, c, acc = carry
        c, acc = tile(i - n, c, acc, False)
        return n + 1, live(c), c, acc

    _, _, _, acc = lax.while_loop(cond, body, (jnp.int32(1), live(c1), c1, acc1))
    o_ref[0] = jnp.where(lane < HEAD_DIM, acc[:tq], acc[tq:]).astype(o_ref.dtype)


def _sb_attention(pa, *, tq, q_blk, k_blk, v_blk):
    b, t, _ = pa.shape
    n_pairs = SB_HEADS // 2
    return pl.pallas_call(
        functools.partial(_sb_kernel, tq=tq),
        grid=(b, n_pairs, t // tq),
        in_specs=[
            pl.BlockSpec((1, tq, LANES), lambda bi, p, i: (bi, i, q_blk + p)),
            pl.BlockSpec((1, t, LANES), lambda bi, p, i: (bi, 0, k_blk + p)),
            pl.BlockSpec((1, t, LANES), lambda bi, p, i: (bi, 0, v_blk + p)),
        ],
        out_specs=pl.BlockSpec((1, tq, LANES), lambda bi, p, i: (bi, i, p)),
        out_shape=jax.ShapeDtypeStruct((b, t, n_pairs * LANES), BF16),
        compiler_params=_params(("parallel", "parallel", "arbitrary")),
        name="sb_attention",
    )(pa, pa, pa)


def _diff_kernel(lam_ref, gain_ref, q_ref, k_ref, v_ref, o_ref, *, tq, lam_init):
    i = pl.program_id(2)
    lane = lax.broadcasted_iota(jnp.int32, (tq, LANES), 1)
    row = lax.broadcasted_iota(jnp.int32, (tq, tq), 0)
    col = lax.broadcasted_iota(jnp.int32, (tq, tq), 1)
    causal = col <= row
    causal2 = jnp.concatenate([causal, causal], axis=0)
    qs = (q_ref[0].astype(F32) * (HEAD_DIM ** -0.5)).astype(BF16)
    zero = jnp.zeros_like(qs)
    q2 = jnp.concatenate([jnp.where(lane < HEAD_DIM, qs, zero),
                          jnp.where(lane < HEAD_DIM, zero, qs)], axis=0)

    def tile(kt, m, l, acc, diagonal):
        start = pl.multiple_of(kt * tq, tq)
        ks = k_ref[0, pl.ds(start, tq), :]
        vs = v_ref[0, pl.ds(start, tq), :]
        s = _dot_nt(q2, ks)
        if diagonal:
            s = jnp.where(causal2, s, NEG_BIG)
        m_new = jnp.maximum(m, jnp.max(s, axis=1, keepdims=True))
        p = jnp.exp(s - m_new)
        scale = jnp.exp(m - m_new)
        l = scale * l + jnp.sum(p, axis=1, keepdims=True)
        acc = scale * acc + _dot(p.astype(BF16), vs)
        return m_new, l, acc

    m0 = jnp.full((2 * tq, 1), NEG_BIG, F32)
    l0 = jnp.zeros((2 * tq, 1), F32)
    acc0 = jnp.zeros((2 * tq, LANES), F32)
    carry = tile(i, m0, l0, acc0, True)

    def body(n, carry):
        return tile(i - n, carry[0], carry[1], carry[2], False)

    _, l, acc = lax.fori_loop(1, i + 1, body, carry)
    o = acc / l
    lp = lam_ref[...]
    lam = (jnp.exp(jnp.sum(lp[0:1] * lp[1:2], axis=1, keepdims=True))
           - jnp.exp(jnp.sum(lp[2:3] * lp[3:4], axis=1, keepdims=True)) + lam_init)
    od = o[:tq] - lam * o[tq:]
    od = od * lax.rsqrt(jnp.mean(od * od, axis=-1, keepdims=True) + RMS_EPS)
    o_ref[0] = (od * gain_ref[...] * (1.0 - lam_init)).astype(o_ref.dtype)


def _diff_attention(pa, lam_params, gain, *, tq, q_blk, k_blk, v_blk, lam_init):
    b, t, _ = pa.shape
    return pl.pallas_call(
        functools.partial(_diff_kernel, tq=tq, lam_init=lam_init),
        grid=(b, DIFF_HEADS, t // tq),
        in_specs=[
            pl.BlockSpec((4, HEAD_DIM), lambda bi, h, i: (0, 0)),
            pl.BlockSpec((1, LANES), lambda bi, h, i: (0, 0)),
            pl.BlockSpec((1, tq, LANES), lambda bi, h, i: (bi, i, q_blk + h)),
            pl.BlockSpec((1, t, LANES), lambda bi, h, i: (bi, 0, k_blk + h)),
            pl.BlockSpec((1, t, LANES), lambda bi, h, i: (bi, 0, v_blk + h)),
        ],
        out_specs=pl.BlockSpec((1, tq, LANES), lambda bi, h, i: (bi, i, h)),
        out_shape=jax.ShapeDtypeStruct((b, t, DIFF_HEADS * LANES), BF16),
        compiler_params=_params(("parallel", "parallel", "arbitrary")),
        name="diff_attention",
    )(lam_params, gain.reshape(1, LANES), pa, pa, pa)


def _compress_kernel(x_ref, pos_ref, w1_ref, w2_ref, o_ref):
    x = x_ref[0, 0]
    n16, half = x.shape
    xa = (x + pos_ref[0:1]).astype(BF16)
    xb = (x + pos_ref[1:2]).astype(BF16)
    w1 = w1_ref[0]
    first = _dot(xa, w1[:half])
    second = _dot(xb, w1[half:])
    h = first + pltpu.roll(second, n16 - 1, 0)
    g = 0.5 * h * (1.0 + jnp.tanh(math.sqrt(2.0 / math.pi) * (h + 0.044715 * (h * h * h))))
    o_ref[0, 0] = _dot(g.astype(BF16), w2_ref[0])


def _compress(x16, pos2, w1, w2):
    b, n_streams, n16, half = x16.shape
    d = w2.shape[-1]
    return pl.pallas_call(
        _compress_kernel,
        grid=(b, n_streams),
        in_specs=[
            pl.BlockSpec((1, 1, n16, half), lambda bi, s: (bi, s, 0, 0)),
            pl.BlockSpec((2, half), lambda bi, s: (0, 0)),
            pl.BlockSpec((1, 2 * half, w1.shape[-1]), lambda bi, s: (s // NSA_KV_HEADS, 0, 0)),
            pl.BlockSpec((1, w2.shape[1], d), lambda bi, s: (s // NSA_KV_HEADS, 0, 0)),
        ],
        out_specs=pl.BlockSpec((1, 1, n16, d), lambda bi, s: (bi, s, 0, 0)),
        out_shape=jax.ShapeDtypeStruct((b, n_streams, n16, d), F32),
        compiler_params=_params(("parallel", "arbitrary")),
        name="nsa_compress",
    )(x16, pos2, w1, w2)


def _nsa_kernel(q_ref, g_ref, ck_ref, cv_ref, ovt_ref, ks_ref, vs_ref, kw_ref, vw_ref, o_ref,
                *, tq, top_n):
    i = pl.program_id(2)
    grp = NSA_GROUP
    d = HEAD_DIM
    q0 = i * tq
    n_cmp = ck_ref.shape[2]
    n_slc = ovt_ref.shape[0]

    q = q_ref[0]
    q4 = jnp.concatenate([q[:, h * d:(h + 1) * d] for h in range(grp)], axis=0)
    q4 = (q4.astype(F32) * (d ** -0.5)).astype(BF16)
    t4 = q0 + lax.rem(lax.broadcasted_iota(jnp.int32, (grp * tq, 1), 0), tq)

    def rep4(x):
        return jnp.concatenate([x] * grp, axis=0)

    ck = ck_ref[0, 0].astype(BF16)
    cv = cv_ref[0, 0].astype(BF16)
    s = _dot_nt(q4, ck)
    cmp_end = CMP_STRIDE * lax.broadcasted_iota(jnp.int32, (1, n_cmp), 1) + (CMP_BLOCK - 1)
    cmask = cmp_end <= t4
    s = jnp.where(cmask, s, NEG_BIG)
    e = jnp.exp(s - jnp.max(s, axis=1, keepdims=True))
    p = e / jnp.sum(e, axis=1, keepdims=True)
    p = jnp.where(cmask, p, 0.0)
    o_cmp = _dot(p.astype(BF16), cv)

    pg = p[0:tq]
    for h in range(1, grp):
        pg = pg + p[h * tq:(h + 1) * tq]
    p_hi = pg.astype(BF16)
    r1 = pg - p_hi.astype(F32)
    p_mid = r1.astype(BF16)
    p_lo = (r1 - p_mid.astype(F32)).astype(BF16)
    ovt = ovt_ref[...]
    imp_t = _dot_nt(ovt, p_hi) + _dot_nt(ovt, p_mid) + _dot_nt(ovt, p_lo)
    blk = lax.broadcasted_iota(jnp.int32, (n_slc, tq), 0)
    cur = (q0 + lax.broadcasted_iota(jnp.int32, (n_slc, tq), 1)) // SLC_BLOCK
    forced = (blk == 0) | (blk == cur) | (blk == cur - 1)
    score = jnp.where(forced, jnp.inf, jnp.where(blk <= cur, imp_t, -jnp.inf))
    rank = jnp.zeros((n_slc, tq), F32)
    for j in range(n_slc):
        sj = score[j:j + 1, :]
        tie = jnp.where(blk > j, 1.0, 0.0)
        rank = rank + jnp.where(sj > score, 1.0, jnp.where(sj == score, tie, 0.0))
    sel_t = jnp.where(rank < top_n, 1.0, 0.0).astype(BF16)
    eye = jnp.where(lax.broadcasted_iota(jnp.int32, (tq, tq), 0)
                    == lax.broadcasted_iota(jnp.int32, (tq, tq), 1), 1.0, 0.0).astype(BF16)
    sel = _dot_nt(eye, sel_t).astype(BF16)

    row = lax.broadcasted_iota(jnp.int32, (tq, tq), 0)
    col = lax.broadcasted_iota(jnp.int32, (tq, tq), 1)
    causal4 = rep4(col <= row)

    def softmax_step(s, m, l, acc, v):
        m_new = jnp.maximum(m, jnp.max(s, axis=1, keepdims=True))
        pt = jnp.exp(s - m_new)
        scale = jnp.exp(m - m_new)
        return m_new, scale * l + jnp.sum(pt, axis=1, keepdims=True), scale * acc + _dot(pt.astype(BF16), v)

    def init():
        return (jnp.full((grp * tq, 1), NEG_BIG, F32), jnp.zeros((grp * tq, 1), F32),
                jnp.zeros((grp * tq, d), F32))

    def slc_tile(kt, m, l, acc, diagonal):
        start = pl.multiple_of(kt * tq, tq)
        k = ks_ref[0, 0, pl.ds(start, tq), :]
        v = vs_ref[0, 0, pl.ds(start, tq), :]
        s = _dot_nt(q4, k)
        key_blk = (start + lax.broadcasted_iota(jnp.int32, (n_slc, tq), 1)) // SLC_BLOCK
        expand = jnp.where(key_blk == blk, 1.0, 0.0).astype(BF16)
        keep = rep4(_dot(sel, expand)) > 0.5
        if diagonal:
            keep = keep & causal4
        return softmax_step(jnp.where(keep, s, NEG_BIG), m, l, acc, v)

    carry = slc_tile(i, *init(), True)
    _, l, acc = lax.fori_loop(1, i + 1, lambda n, c: slc_tile(i - n, c[0], c[1], c[2], False), carry)
    o_slc = acc / l

    def win_tile(kt, m, l, acc, diagonal):
        start = pl.multiple_of(kt * tq, tq)
        k = kw_ref[0, 0, pl.ds(start, tq), :]
        v = vw_ref[0, 0, pl.ds(start, tq), :]
        s = _dot_nt(q4, k)
        if diagonal:
            keep = causal4
        else:
            kpos = start + lax.broadcasted_iota(jnp.int32, (1, tq), 1)
            keep = kpos > t4 - WINDOW
        return softmax_step(jnp.where(keep, s, NEG_BIG), m, l, acc, v)

    carry = win_tile(i, *init(), True)
    n_back = jnp.minimum(i, -(-WINDOW // tq))
    _, l, acc = lax.fori_loop(1, n_back + 1, lambda n, c: win_tile(i - n, c[0], c[1], c[2], False), carry)
    o_win = acc / l

    gate = 1.0 / (1.0 + jnp.exp(-g_ref[0]))
    heads = []
    for h in range(grp):
        rows = slice(h * tq, (h + 1) * tq)
        heads.append(gate[:, 3 * h:3 * h + 1] * o_cmp[rows]
                     + gate[:, 3 * h + 1:3 * h + 2] * o_slc[rows]
                     + gate[:, 3 * h + 2:3 * h + 3] * o_win[rows])
    o_ref[0] = jnp.concatenate(heads, axis=1).astype(o_ref.dtype)


def _nsa_attention(pa, pf, ckv, ovt, kv4, *, tq, q_blk, g_blk, top_n):
    b, t, _ = pa.shape
    n16 = ckv.shape[2]
    n_slc = ovt.shape[0]
    qw = NSA_GROUP * HEAD_DIM
    kv_spec = pl.BlockSpec((1, 1, t, HEAD_DIM), lambda bi, j, i: (bi, j, 0, 0))
    return pl.pallas_call(
        functools.partial(_nsa_kernel, tq=tq, top_n=top_n),
        grid=(b, NSA_KV_HEADS, t // tq),
        in_specs=[
            pl.BlockSpec((1, tq, qw), lambda bi, j, i: (bi, i, q_blk + j)),
            pl.BlockSpec((1, tq, LANES), lambda bi, j, i: (bi, i, g_blk + j)),
            pl.BlockSpec((1, 1, n16, HEAD_DIM), lambda bi, j, i: (bi, j, 0, 0)),
            pl.BlockSpec((1, 1, n16, HEAD_DIM), lambda bi, j, i: (bi, NSA_KV_HEADS + j, 0, 0)),
            pl.BlockSpec((n_slc, n16), lambda bi, j, i: (0, 0)),
            kv_spec, kv_spec, kv_spec, kv_spec,
        ],
        out_specs=pl.BlockSpec((1, tq, qw), lambda bi, j, i: (bi, i, j)),
        out_shape=jax.ShapeDtypeStruct((b, t, NSA_HEADS * HEAD_DIM), BF16),
        compiler_params=_params(("parallel", "parallel", "arbitrary")),
        name="nsa_attention",
    )(pa, pf, ckv, ckv, ovt, kv4[0], kv4[1], kv4[2], kv4[3])


def _conv_kernel(cb_ref, cc_ref, ch_ref, w_ref, o_ref, prev_ref):
    @pl.when(pl.program_id(1) == 0)
    def _():
        prev_ref[...] = jnp.zeros_like(prev_ref)

    u = cc_ref[0] * ch_ref[0]
    tt = u.shape[0]
    row = lax.broadcasted_iota(jnp.int32, u.shape, 0)
    last1 = prev_ref[7:8]
    last2 = prev_ref[6:7]
    u1 = jnp.where(row >= 1, pltpu.roll(u, 1, 0), last1)
    u2 = jnp.where(row >= 2, pltpu.roll(u, 2, 0), jnp.where(row == 1, last1, last2))
    w = w_ref[...]
    o_ref[0] = (cb_ref[0] * (w[0:1] * u2 + w[1:2] * u1 + w[2:3] * u)).astype(o_ref.dtype)
    prev_ref[...] = u[tt - 8:tt]


def _conv_mixer(pf, conv_w, *, tt):
    b, t, _ = pf.shape
    c = CONV_CH
    return pl.pallas_call(
        _conv_kernel,
        grid=(b, t // tt),
        in_specs=[
            pl.BlockSpec((1, tt, c), lambda bi, i: (bi, i, 0)),
            pl.BlockSpec((1, tt, c), lambda bi, i: (bi, i, 1)),
            pl.BlockSpec((1, tt, c), lambda bi, i: (bi, i, 2)),
            pl.BlockSpec((CONV_WIDTH, c), lambda bi, i: (0, 0)),
        ],
        out_specs=pl.BlockSpec((1, tt, c), lambda bi, i: (bi, i, 0)),
        out_shape=jax.ShapeDtypeStruct((b, t, c), BF16),
        scratch_shapes=[pltpu.VMEM((8, c), F32)],
        compiler_params=_params(("parallel", "arbitrary")),
        name="conv_mixer",
    )(pf, pf, pf, conv_w)


def _out_ln_kernel(a_ref, b_ref, c_ref, d_ref, w_ref, x_ref, g_ref, beta_ref, o_ref, ob_ref, *, alpha):
    kw = a_ref.shape[1]
    mix = _dot(a_ref[...], w_ref[0:kw])
    for n, r in enumerate((b_ref, c_ref, d_ref), start=1):
        mix = mix + _dot(r[...], w_ref[n * kw:(n + 1) * kw])
    o = _layer_norm(alpha * x_ref[...] + mix, g_ref[...], beta_ref[...])
    o_ref[...] = o
    ob_ref[...] = o.astype(BF16)


def _out_ln(parts, w, x, g, b, *, alpha, tm):
    n, d = x.shape
    kw = parts[0].shape[1]
    part_spec = pl.BlockSpec((tm, kw), lambda i: (i, 0))
    return pl.pallas_call(
        functools.partial(_out_ln_kernel, alpha=alpha),
        grid=(n // tm,),
        in_specs=[part_spec, part_spec, part_spec, part_spec,
                  pl.BlockSpec(w.shape, lambda i: (0, 0)),
                  pl.BlockSpec((tm, d), lambda i: (i, 0)),
                  pl.BlockSpec((1, d), lambda i: (0, 0)),
                  pl.BlockSpec((1, d), lambda i: (0, 0))],
        out_specs=[pl.BlockSpec((tm, d), lambda i: (i, 0)), pl.BlockSpec((tm, d), lambda i: (i, 0))],
        out_shape=[jax.ShapeDtypeStruct((n, d), F32), jax.ShapeDtypeStruct((n, d), BF16)],
        compiler_params=_params(("parallel",)),
        name="out_ln",
    )(*parts, w, x, g.reshape(1, d), b.reshape(1, d))


def _overlap_t(t):
    n16 = t // CMP_STRIDE
    n_slc = t // SLC_BLOCK
    c_start = CMP_STRIDE * np.arange(n16)
    j_start = SLC_BLOCK * np.arange(n_slc)
    ov = ((c_start[None, :] < j_start[:, None] + SLC_BLOCK)
          & (c_start[None, :] + CMP_BLOCK > j_start[:, None])).astype(np.float32)
    ov[:, n16 - 1] = 0.0
    return jnp.asarray(ov, BF16)


def _mixer(hf, hb, batch, w_att, w_f32, w_out, ln_g, ln_b, diff_lam, diff_gain, pos2, cmp_w1, cmp_w2,
           conv_w, layer_idx, alpha):
    n, _ = hf.shape
    t = n // batch
    pa = _matmul(hb, w_att, BF16, tm=1024, tn=512, name="proj_att").reshape(batch, t, -1)
    pf = _matmul(hb, w_f32, F32, tm=1024, tn=512, name="proj_f32").reshape(batch, t, -1)

    o_sb = _sb_attention(pa, tq=256, q_blk=0, k_blk=4, v_blk=8)
    lam_init = 0.8 - 0.6 * math.exp(-0.3 * layer_idx)
    o_df = _diff_attention(pa, diff_lam, diff_gain, tq=256, q_blk=12, k_blk=16, v_blk=20, lam_init=lam_init)

    n16 = t // CMP_STRIDE
    kvc = pf[:, :, 3 * CONV_CH:3 * CONV_CH + 2 * LANES].reshape(batch, t, 2, NSA_KV_HEADS, HEAD_DIM)
    x16 = kvc.transpose(0, 2, 3, 1, 4).reshape(batch, 2 * NSA_KV_HEADS, n16, CMP_STRIDE * HEAD_DIM)
    ckv = _compress(x16, pos2, cmp_w1, cmp_w2)
    kv4 = pa[:, :, 28 * LANES:32 * LANES].reshape(batch, t, 4, NSA_KV_HEADS, HEAD_DIM).transpose(2, 0, 3, 1, 4)
    n_slc = t // SLC_BLOCK
    o_ns = _nsa_attention(pa, pf, ckv, _overlap_t(t), kv4, tq=128, q_blk=12, g_blk=14,
                          top_n=min(SLC_TOPN, n_slc))

    o_cv = _conv_mixer(pf, conv_w, tt=512)
    parts = [o.reshape(n, -1) for o in (o_sb, o_df, o_ns, o_cv)]
    return _out_ln(parts, w_out, hf, ln_g, ln_b, alpha=alpha, tm=512)


def kernel(x, ln_g, ln_b, ffn_w1, ffn_w3, ffn_w2, w_in, w_out, diff_lam, diff_gain, cmp_pos, cmp_wk1,
           cmp_wk2, cmp_wv1, cmp_wv2, conv_w):
    batch, t, d = x.shape
    depth = ln_g.shape[0]
    alpha = (2 * depth) ** 0.25
    n = batch * t

    w1b, w3b, w2b = ffn_w1.astype(BF16), ffn_w3.astype(BF16), ffn_w2.astype(BF16)
    w_att = w_in[:, :, _ATT_COLS].astype(BF16)
    gate_cols = []
    for j in range(NSA_KV_HEADS):
        real = w_in[:, :, _SEG_OFF["ns_g"] + j * _GATES_PER_KV:_SEG_OFF["ns_g"] + (j + 1) * _GATES_PER_KV]
        gate_cols.append(jnp.pad(real, ((0, 0), (0, 0), (0, LANES - _GATES_PER_KV))))
    w_f32 = jnp.concatenate([w_in[:, :, _F32_COLS]] + gate_cols, axis=-1).astype(BF16)
    w_outb = w_out.astype(BF16)
    half = CMP_STRIDE * HEAD_DIM
    pos2 = cmp_pos.reshape(depth, 2, half)
    cmp_w1 = jnp.stack([cmp_wk1, cmp_wv1], axis=1).astype(BF16)
    cmp_w2 = jnp.stack([cmp_wk2, cmp_wv2], axis=1).astype(BF16)

    hf = x.reshape(n, d)
    for l in range(depth):
        hf, hb = _ffn_ln(hf, w1b[l, 0], w3b[l, 0], w2b[l, 0], ln_g[l, 0], ln_b[l, 0], alpha=alpha, tm=512, tf=512)
        hf, _ = _mixer(hf, hb, batch, w_att[l], w_f32[l], w_outb[l], ln_g[l, 1], ln_b[l, 1], diff_lam[l],
                       diff_gain[l], pos2[l], cmp_w1[l], cmp_w2[l], conv_w[l], l, alpha)
        hf, _ = _ffn_ln(hf, w1b[l, 1], w3b[l, 1], w2b[l, 1], ln_g[l, 2], ln_b[l, 2], alpha=alpha, tm=512, tf=512)
    return hf.reshape(batch, t, d)
```

```python
import functools
import math

import numpy as np
import jax
import jax.numpy as jnp
from jax import lax
from jax.experimental import pallas as pl
from jax.experimental.pallas import tpu as pltpu

F32 = jnp.float32
BF16 = jnp.bfloat16

HEAD_DIM = 64
SB_HEADS = 8
DIFF_HEADS = 4
NSA_HEADS = 8
NSA_KV_HEADS = 2
NSA_GROUP = NSA_HEADS // NSA_KV_HEADS
CMP_BLOCK = 32
CMP_STRIDE = 16
SLC_BLOCK = 64
SLC_TOPN = 16
WINDOW = 512
N_BRANCH = 3
CONV_CH = 512
CONV_WIDTH = 3
LN_EPS = 1e-5
RMS_EPS = 1e-5
NEG_BIG = -1e30
LOG2_E = math.log2(math.e)
EXP_UNDERFLOW = -104.0
LANES = 128
VMEM_LIMIT = 56 * 1024 * 1024

_SEG_NAMES = ("sb_q", "sb_k", "sb_v", "df_q", "df_k", "df_v", "ns_q", "ns_kc", "ns_vc",
              "ns_ks", "ns_vs", "ns_kw", "ns_vw", "ns_g", "cv_b", "cv_c", "cv_h")
_SEG_WIDTHS = (512, 512, 512, 512, 512, 512, 512, 128, 128, 128, 128, 128, 128,
               NSA_HEADS * N_BRANCH, 512, 512, 512)
_SEG_OFF = dict(zip(_SEG_NAMES, np.cumsum((0,) + _SEG_WIDTHS[:-1]).tolist()))
_SEG_W = dict(zip(_SEG_NAMES, _SEG_WIDTHS))


def _cols(name):
    return np.arange(_SEG_OFF[name], _SEG_OFF[name] + _SEG_W[name])


_ATT_COLS = np.concatenate([_cols(n) for n in
                            ("sb_q", "sb_k", "sb_v", "df_q", "df_k", "df_v", "ns_q",
                             "ns_ks", "ns_vs", "ns_kw", "ns_vw")])
_GATES_PER_KV = NSA_GROUP * N_BRANCH
_F32_COLS = np.concatenate([_cols(n) for n in ("cv_b", "cv_c", "cv_h", "ns_kc", "ns_vc")])


def _params(sem):
    return pltpu.CompilerParams(dimension_semantics=sem, vmem_limit_bytes=VMEM_LIMIT)


def _layer_norm(y, g, b):
    mu = jnp.mean(y, axis=-1, keepdims=True)
    d = y - mu
    var = jnp.mean(d * d, axis=-1, keepdims=True)
    return d * lax.rsqrt(var + LN_EPS) * g + b


def _dot(a, b):
    return jnp.dot(a, b, preferred_element_type=F32)


def _lane_group_max(s, mx):
    for g in range(s.shape[1] // LANES):
        mx = jnp.maximum(mx, s[:, g * LANES:(g + 1) * LANES])
    return mx


def _dot_nt(a, b):
    return lax.dot_general(a, b, (((1,), (1,)), ((), ())), preferred_element_type=F32)


def _ffn_ln_kernel(x_ref, w1_ref, w3_ref, w2_ref, g_ref, b_ref, o_ref, ob_ref, xb_ref, acc_ref,
                   *, alpha):
    j = pl.program_id(1)

    @pl.when(j == 0)
    def _():
        xb_ref[...] = x_ref[...].astype(BF16)
        acc_ref[...] = jnp.zeros_like(acc_ref)

    xb = xb_ref[...]
    a = _dot(xb, w1_ref[...])
    b = _dot(xb, w3_ref[...])
    hm = (a / (1.0 + jnp.exp(-a))) * b
    acc_ref[...] += _dot(hm.astype(BF16), w2_ref[...])

    @pl.when(j == pl.num_programs(1) - 1)
    def _():
        y = alpha * x_ref[...] + 0.5 * acc_ref[...]
        o = _layer_norm(y, g_ref[...], b_ref[...])
        o_ref[...] = o
        ob_ref[...] = o.astype(BF16)


def _ffn_ln(x, w1, w3, w2, g, b, *, alpha, tm, tf):
    n, d = x.shape
    f = w1.shape[1]
    return pl.pallas_call(
        functools.partial(_ffn_ln_kernel, alpha=alpha),
        grid=(n // tm, f // tf),
        in_specs=[
            pl.BlockSpec((tm, d), lambda i, j: (i, 0)),
            pl.BlockSpec((d, tf), lambda i, j: (0, j)),
            pl.BlockSpec((d, tf), lambda i, j: (0, j)),
            pl.BlockSpec((tf, d), lambda i, j: (j, 0)),
            pl.BlockSpec((1, d), lambda i, j: (0, 0)),
            pl.BlockSpec((1, d), lambda i, j: (0, 0)),
        ],
        out_specs=[pl.BlockSpec((tm, d), lambda i, j: (i, 0)),
                   pl.BlockSpec((tm, d), lambda i, j: (i, 0))],
        out_shape=[jax.ShapeDtypeStruct((n, d), F32), jax.ShapeDtypeStruct((n, d), BF16)],
        scratch_shapes=[pltpu.VMEM((tm, d), BF16), pltpu.VMEM((tm, d), F32)],
        compiler_params=_params(("parallel", "arbitrary")),
        name="ffn_ln",
    )(x, w1, w3, w2, g.reshape(1, d), b.reshape(1, d))


def _matmul_kernel(x_ref, w_ref, o_ref):
    o_ref[...] = _dot(x_ref[...], w_ref[...]).astype(o_ref.dtype)


def _matmul(x, w, out_dtype, *, tm, tn, name):
    n, d = x.shape
    m = w.shape[1]
    return pl.pallas_call(
        _matmul_kernel,
        grid=(n // tm, m // tn),
        in_specs=[pl.BlockSpec((tm, d), lambda i, j: (i, 0)),
                  pl.BlockSpec((d, tn), lambda i, j: (0, j))],
        out_specs=pl.BlockSpec((tm, tn), lambda i, j: (i, j)),
        out_shape=jax.ShapeDtypeStruct((n, m), out_dtype),
        compiler_params=_params(("parallel", "arbitrary")),
        name=name,
    )(x, w)


def _sb_kernel(q_ref, k_ref, v_ref, o_ref, *, tq):
    i = pl.program_id(2)
    lane = lax.broadcasted_iota(jnp.int32, (tq, LANES), 1)
    row = lax.broadcasted_iota(jnp.int32, (tq, tq), 0)
    col = lax.broadcasted_iota(jnp.int32, (tq, tq), 1)
    strictly_causal = col < row
    causal2 = jnp.concatenate([strictly_causal, strictly_causal], axis=0)
    tri = jnp.where(row >= col, 1.0, 0.0).astype(BF16)
    qs = (q_ref[0].astype(F32) * (HEAD_DIM ** -0.5)).astype(BF16)
    zero = jnp.zeros_like(qs)
    q2 = jnp.concatenate([jnp.where(lane < HEAD_DIM, qs, zero),
                          jnp.where(lane < HEAD_DIM, zero, qs)], axis=0)

    def tile(kt, c, acc, diagonal):
        start = pl.multiple_of(kt * tq, tq)
        ks = k_ref[0, pl.ds(start, tq), :]
        vs = v_ref[0, pl.ds(start, tq), :]
        z = _dot_nt(q2, ks)
        lg = -(jnp.maximum(z, 0.0) + jnp.log1p(jnp.exp(-jnp.abs(z))))
        if diagonal:
            lg = jnp.where(causal2, lg, 0.0)
        hi = lg.astype(BF16)
        lo = (lg - hi.astype(F32)).astype(BF16)
        suffix = _dot(hi, tri) + _dot(lo, tri)
        a = jnp.exp(z + suffix + c)
        if diagonal:
            a = jnp.where(causal2, a, 0.0)
        return c + suffix[:, 0:1], acc + _dot(a.astype(BF16), vs)

    c1, acc1 = tile(i, jnp.zeros((2 * tq, 1), F32), jnp.zeros((2 * tq, LANES), F32), True)

    def live(c):
        return (jnp.max(c) > EXP_UNDERFLOW).astype(jnp.int32)

    def cond(carry):
        return (carry[0] <= i) & (carry[1] > 0)

    def body(carry):
        n, _, c, acc = carry
        c, acc = tile(i - n, c, acc, False)
        return n + 1, live(c), c, acc

    _, _, _, acc = lax.while_loop(cond, body, (jnp.int32(1), live(c1), c1, acc1))
    o_ref[0] = jnp.where(lane < HEAD_DIM, acc[:tq], acc[tq:]).astype(o_ref.dtype)


def _sb_attention(pa, *, tq, q_blk, k_blk, v_blk):
    b, t, _ = pa.shape
    n_pairs = SB_HEADS // 2
    return pl.pallas_call(
        functools.partial(_sb_kernel, tq=tq),
        grid=(b, n_pairs, t // tq),
        in_specs=[
            pl.BlockSpec((1, tq, LANES), lambda bi, p, i: (bi, i, q_blk + p)),
            pl.BlockSpec((1, t, LANES), lambda bi, p, i: (bi, 0, k_blk + p)),
            pl.BlockSpec((1, t, LANES), lambda bi, p, i: (bi, 0, v_blk + p)),
        ],
        out_specs=pl.BlockSpec((1, tq, LANES), lambda bi, p, i: (bi, i, p)),
        out_shape=jax.ShapeDtypeStruct((b, t, n_pairs * LANES), BF16),
        compiler_params=_params(("parallel", "parallel", "arbitrary")),
        name="sb_attention",
    )(pa, pa, pa)


def _diff_kernel(lam_ref, gain_ref, q_ref, k_ref, v_ref, o_ref, s_ref, *, tq, lam_init):
    i = pl.program_id(2)
    lane = lax.broadcasted_iota(jnp.int32, (tq, LANES), 1)
    row = lax.broadcasted_iota(jnp.int32, (tq, tq), 0)
    col = lax.broadcasted_iota(jnp.int32, (tq, tq), 1)
    causal = col <= row
    causal2 = jnp.concatenate([causal, causal], axis=0)
    qs = (q_ref[0].astype(F32) * (HEAD_DIM ** -0.5)).astype(BF16)
    zero = jnp.zeros_like(qs)
    q2 = jnp.concatenate([jnp.where(lane < HEAD_DIM, qs, zero),
                          jnp.where(lane < HEAD_DIM, zero, qs)], axis=0)

    def scores(kt, mx, diagonal):
        start = pl.multiple_of(kt * tq, tq)
        s = _dot_nt(q2, k_ref[0, pl.ds(start, tq), :]) * LOG2_E
        if diagonal:
            s = jnp.where(causal2, s, NEG_BIG)
        s_ref[:, pl.ds(start, tq)] = s
        return _lane_group_max(s, mx)

    mx = lax.fori_loop(0, i, lambda kt, mx: scores(kt, mx, False),
                       jnp.full((2 * tq, LANES), NEG_BIG, F32))
    mx = scores(i, mx, True)
    m = jnp.broadcast_to(jnp.max(mx, axis=1, keepdims=True), (2 * tq, LANES))
    m = jnp.concatenate([m] * (tq // LANES), axis=1)

    ones = jnp.ones((tq, LANES), BF16)

    def weigh(kt, acc):
        start = pl.multiple_of(kt * tq, tq)
        p = jnp.exp2(s_ref[:, pl.ds(start, tq)] - m).astype(BF16)
        vx = jnp.concatenate([v_ref[0, pl.ds(start, tq), :], ones], axis=1)
        return acc + _dot(p, vx)

    acc = lax.fori_loop(0, i + 1, weigh, jnp.zeros((2 * tq, 2 * LANES), F32))
    o = acc[:, :LANES] / acc[:, LANES:]
    lp = lam_ref[...]
    lam = (jnp.exp(jnp.sum(lp[0:1] * lp[1:2], axis=1, keepdims=True))
           - jnp.exp(jnp.sum(lp[2:3] * lp[3:4], axis=1, keepdims=True)) + lam_init)
    od = o[:tq] - lam * o[tq:]
    od = od * lax.rsqrt(jnp.mean(od * od, axis=-1, keepdims=True) + RMS_EPS)
    o_ref[0] = (od * gain_ref[...] * (1.0 - lam_init)).astype(o_ref.dtype)


def _diff_attention(pa, lam_params, gain, *, tq, q_blk, k_blk, v_blk, lam_init):
    b, t, _ = pa.shape
    return pl.pallas_call(
        functools.partial(_diff_kernel, tq=tq, lam_init=lam_init),
        grid=(b, DIFF_HEADS, t // tq),
        in_specs=[
            pl.BlockSpec((4, HEAD_DIM), lambda bi, h, i: (0, 0)),
            pl.BlockSpec((1, LANES), lambda bi, h, i: (0, 0)),
            pl.BlockSpec((1, tq, LANES), lambda bi, h, i: (bi, i, q_blk + h)),
            pl.BlockSpec((1, t, LANES), lambda bi, h, i: (bi, 0, k_blk + h)),
            pl.BlockSpec((1, t, LANES), lambda bi, h, i: (bi, 0, v_blk + h)),
        ],
        out_specs=pl.BlockSpec((1, tq, LANES), lambda bi, h, i: (bi, i, h)),
        out_shape=jax.ShapeDtypeStruct((b, t, DIFF_HEADS * LANES), BF16),
        scratch_shapes=[pltpu.VMEM((2 * tq, t), F32)],
        compiler_params=_params(("parallel", "parallel", "arbitrary")),
        name="diff_attention",
    )(lam_params, gain.reshape(1, LANES), pa, pa, pa)


def _compress_kernel(x_ref, pos_ref, w1_ref, w2_ref, o_ref):
    x = x_ref[0, 0]
    n16, half = x.shape
    xa = (x + pos_ref[0:1]).astype(BF16)
    xb = (x + pos_ref[1:2]).astype(BF16)
    w1 = w1_ref[0]
    first = _dot(xa, w1[:half])
    second = _dot(xb, w1[half:])
    h = first + pltpu.roll(second, n16 - 1, 0)
    g = 0.5 * h * (1.0 + jnp.tanh(math.sqrt(2.0 / math.pi) * (h + 0.044715 * (h * h * h))))
    o_ref[0, 0] = _dot(g.astype(BF16), w2_ref[0])


def _compress(x16, pos2, w1, w2):
    b, n_streams, n16, half = x16.shape
    d = w2.shape[-1]
    return pl.pallas_call(
        _compress_kernel,
        grid=(b, n_streams),
        in_specs=[
            pl.BlockSpec((1, 1, n16, half), lambda bi, s: (bi, s, 0, 0)),
            pl.BlockSpec((2, half), lambda bi, s: (0, 0)),
            pl.BlockSpec((1, 2 * half, w1.shape[-1]), lambda bi, s: (s // NSA_KV_HEADS, 0, 0)),
            pl.BlockSpec((1, w2.shape[1], d), lambda bi, s: (s // NSA_KV_HEADS, 0, 0)),
        ],
        out_specs=pl.BlockSpec((1, 1, n16, d), lambda bi, s: (bi, s, 0, 0)),
        out_shape=jax.ShapeDtypeStruct((b, n_streams, n16, d), F32),
        compiler_params=_params(("parallel", "arbitrary")),
        name="nsa_compress",
    )(x16, pos2, w1, w2)


def _nsa_kernel(q_ref, g_ref, ck_ref, cv_ref, ovt_ref, ks_ref, vs_ref, kw_ref, vw_ref, o_ref,
                *, tq, top_n):
    i = pl.program_id(2)
    grp = NSA_GROUP
    d = HEAD_DIM
    q0 = i * tq
    n_cmp = ck_ref.shape[2]
    n_slc = ovt_ref.shape[0]

    q = q_ref[0]
    q4 = jnp.concatenate([q[:, h * d:(h + 1) * d] for h in range(grp)], axis=0)
    q4 = (q4.astype(F32) * (d ** -0.5)).astype(BF16)
    t4 = q0 + lax.rem(lax.broadcasted_iota(jnp.int32, (grp * tq, 1), 0), tq)

    def rep4(x):
        return jnp.concatenate([x] * grp, axis=0)

    ck = ck_ref[0, 0].astype(BF16)
    cv = cv_ref[0, 0].astype(BF16)
    s = _dot_nt(q4, ck)
    cmp_end = CMP_STRIDE * lax.broadcasted_iota(jnp.int32, (1, n_cmp), 1) + (CMP_BLOCK - 1)
    cmask = cmp_end <= t4
    s = jnp.where(cmask, s, NEG_BIG)
    e = jnp.exp(s - jnp.max(s, axis=1, keepdims=True))
    p = e / jnp.sum(e, axis=1, keepdims=True)
    p = jnp.where(cmask, p, 0.0)
    o_cmp = _dot(p.astype(BF16), cv)

    pg = p[0:tq]
    for h in range(1, grp):
        pg = pg + p[h * tq:(h + 1) * tq]
    p_hi = pg.astype(BF16)
    r1 = pg - p_hi.astype(F32)
    p_mid = r1.astype(BF16)
    p_lo = (r1 - p_mid.astype(F32)).astype(BF16)
    ovt = ovt_ref[...]
    imp_t = _dot_nt(ovt, p_hi) + _dot_nt(ovt, p_mid) + _dot_nt(ovt, p_lo)
    blk = lax.broadcasted_iota(jnp.int32, (n_slc, tq), 0)
    cur = (q0 + lax.broadcasted_iota(jnp.int32, (n_slc, tq), 1)) // SLC_BLOCK
    forced = (blk == 0) | (blk == cur) | (blk == cur - 1)
    score = jnp.where(forced, jnp.inf, jnp.where(blk <= cur, imp_t, -jnp.inf))
    rank = jnp.zeros((n_slc, tq), F32)
    for j in range(n_slc):
        sj = score[j:j + 1, :]
        tie = jnp.where(blk > j, 1.0, 0.0)
        rank = rank + jnp.where(sj > score, 1.0, jnp.where(sj == score, tie, 0.0))
    sel_t = jnp.where(rank < top_n, 1.0, 0.0).astype(BF16)
    eye = jnp.where(lax.broadcasted_iota(jnp.int32, (tq, tq), 0)
                    == lax.broadcasted_iota(jnp.int32, (tq, tq), 1), 1.0, 0.0).astype(BF16)
    sel = _dot_nt(eye, sel_t).astype(BF16)

    row = lax.broadcasted_iota(jnp.int32, (tq, tq), 0)
    col = lax.broadcasted_iota(jnp.int32, (tq, tq), 1)
    causal4 = rep4(col <= row)

    def softmax_step(s, m, l, acc, v):
        m_new = jnp.maximum(m, jnp.max(s, axis=1, keepdims=True))
        pt = jnp.exp(s - m_new)
        scale = jnp.exp(m - m_new)
        return m_new, scale * l + jnp.sum(pt, axis=1, keepdims=True), scale * acc + _dot(pt.astype(BF16), v)

    def init():
        return (jnp.full((grp * tq, 1), NEG_BIG, F32), jnp.zeros((grp * tq, 1), F32),
                jnp.zeros((grp * tq, d), F32))

    def slc_tile(kt, m, l, acc, diagonal):
        start = pl.multiple_of(kt * tq, tq)
        k = ks_ref[0, 0, pl.ds(start, tq), :]
        v = vs_ref[0, 0, pl.ds(start, tq), :]
        s = _dot_nt(q4, k)
        key_blk = (start + lax.broadcasted_iota(jnp.int32, (n_slc, tq), 1)) // SLC_BLOCK
        expand = jnp.where(key_blk == blk, 1.0, 0.0).astype(BF16)
        keep = rep4(_dot(sel, expand)) > 0.5
        if diagonal:
            keep = keep & causal4
        return softmax_step(jnp.where(keep, s, NEG_BIG), m, l, acc, v)

    carry = slc_tile(i, *init(), True)
    _, l, acc = lax.fori_loop(1, i + 1, lambda n, c: slc_tile(i - n, c[0], c[1], c[2], False), carry)
    o_slc = acc / l

    def win_tile(kt, m, l, acc, diagonal):
        start = pl.multiple_of(kt * tq, tq)
        k = kw_ref[0, 0, pl.ds(start, tq), :]
        v = vw_ref[0, 0, pl.ds(start, tq), :]
        s = _dot_nt(q4, k)
        if diagonal:
            keep = causal4
        else:
            kpos = start + lax.broadcasted_iota(jnp.int32, (1, tq), 1)
            keep = kpos > t4 - WINDOW
        return softmax_step(jnp.where(keep, s, NEG_BIG), m, l, acc, v)

    carry = win_tile(i, *init(), True)
    n_back = jnp.minimum(i, -(-WINDOW // tq))
    _, l, acc = lax.fori_loop(1, n_back + 1, lambda n, c: win_tile(i - n, c[0], c[1], c[2], False), carry)
    o_win = acc / l

    gate = 1.0 / (1.0 + jnp.exp(-g_ref[0]))
    heads = []
    for h in range(grp):
        rows = slice(h * tq, (h + 1) * tq)
        heads.append(gate[:, 3 * h:3 * h + 1] * o_cmp[rows]
                     + gate[:, 3 * h + 1:3 * h + 2] * o_slc[rows]
                     + gate[:, 3 * h + 2:3 * h + 3] * o_win[rows])
    o_ref[0] = jnp.concatenate(heads, axis=1).astype(o_ref.dtype)


def _nsa_attention(pa, pf, ckv, ovt, kv4, *, tq, q_blk, g_blk, top_n):
    b, t, _ = pa.shape
    n16 = ckv.shape[2]
    n_slc = ovt.shape[0]
    qw = NSA_GROUP * HEAD_DIM
    kv_spec = pl.BlockSpec((1, 1, t, HEAD_DIM), lambda bi, j, i: (bi, j, 0, 0))
    return pl.pallas_call(
        functools.partial(_nsa_kernel, tq=tq, top_n=top_n),
        grid=(b, NSA_KV_HEADS, t // tq),
        in_specs=[
            pl.BlockSpec((1, tq, qw), lambda bi, j, i: (bi, i, q_blk + j)),
            pl.BlockSpec((1, tq, LANES), lambda bi, j, i: (bi, i, g_blk + j)),
            pl.BlockSpec((1, 1, n16, HEAD_DIM), lambda bi, j, i: (bi, j, 0, 0)),
            pl.BlockSpec((1, 1, n16, HEAD_DIM), lambda bi, j, i: (bi, NSA_KV_HEADS + j, 0, 0)),
            pl.BlockSpec((n_slc, n16), lambda bi, j, i: (0, 0)),
            kv_spec, kv_spec, kv_spec, kv_spec,
        ],
        out_specs=pl.BlockSpec((1, tq, qw), lambda bi, j, i: (bi, i, j)),
        out_shape=jax.ShapeDtypeStruct((b, t, NSA_HEADS * HEAD_DIM), BF16),
        compiler_params=_params(("parallel", "parallel", "arbitrary")),
        name="nsa_attention",
    )(pa, pf, ckv, ckv, ovt, kv4[0], kv4[1], kv4[2], kv4[3])


def _conv_kernel(cb_ref, cc_ref, ch_ref, w_ref, o_ref, prev_ref):
    @pl.when(pl.program_id(1) == 0)
    def _():
        prev_ref[...] = jnp.zeros_like(prev_ref)

    u = cc_ref[0] * ch_ref[0]
    tt = u.shape[0]
    row = lax.broadcasted_iota(jnp.int32, u.shape, 0)
    last1 = prev_ref[7:8]
    last2 = prev_ref[6:7]
    u1 = jnp.where(row >= 1, pltpu.roll(u, 1, 0), last1)
    u2 = jnp.where(row >= 2, pltpu.roll(u, 2, 0), jnp.where(row == 1, last1, last2))
    w = w_ref[...]
    o_ref[0] = (cb_ref[0] * (w[0:1] * u2 + w[1:2] * u1 + w[2:3] * u)).astype(o_ref.dtype)
    prev_ref[...] = u[tt - 8:tt]


def _conv_mixer(pf, conv_w, *, tt):
    b, t, _ = pf.shape
    c = CONV_CH
    return pl.pallas_call(
        _conv_kernel,
        grid=(b, t // tt),
        in_specs=[
            pl.BlockSpec((1, tt, c), lambda bi, i: (bi, i, 0)),
            pl.BlockSpec((1, tt, c), lambda bi, i: (bi, i, 1)),
            pl.BlockSpec((1, tt, c), lambda bi, i: (bi, i, 2)),
            pl.BlockSpec((CONV_WIDTH, c), lambda bi, i: (0, 0)),
        ],
        out_specs=pl.BlockSpec((1, tt, c), lambda bi, i: (bi, i, 0)),
        out_shape=jax.ShapeDtypeStruct((b, t, c), BF16),
        scratch_shapes=[pltpu.VMEM((8, c), F32)],
        compiler_params=_params(("parallel", "arbitrary")),
        name="conv_mixer",
    )(pf, pf, pf, conv_w)


def _out_ln_kernel(a_ref, b_ref, c_ref, d_ref, w_ref, x_ref, g_ref, beta_ref, o_ref, ob_ref, *, alpha):
    kw = a_ref.shape[1]
    mix = _dot(a_ref[...], w_ref[0:kw])
    for n, r in enumerate((b_ref, c_ref, d_ref), start=1):
        mix = mix + _dot(r[...], w_ref[n * kw:(n + 1) * kw])
    o = _layer_norm(alpha * x_ref[...] + mix, g_ref[...], beta_ref[...])
    o_ref[...] = o
    ob_ref[...] = o.astype(BF16)


def _out_ln(parts, w, x, g, b, *, alpha, tm):
    n, d = x.shape
    kw = parts[0].shape[1]
    part_spec = pl.BlockSpec((tm, kw), lambda i: (i, 0))
    return pl.pallas_call(
        functools.partial(_out_ln_kernel, alpha=alpha),
        grid=(n // tm,),
        in_specs=[part_spec, part_spec, part_spec, part_spec,
                  pl.BlockSpec(w.shape, lambda i: (0, 0)),
                  pl.BlockSpec((tm, d), lambda i: (i, 0)),
                  pl.BlockSpec((1, d), lambda i: (0, 0)),
                  pl.BlockSpec((1, d), lambda i: (0, 0))],
        out_specs=[pl.BlockSpec((tm, d), lambda i: (i, 0)), pl.BlockSpec((tm, d), lambda i: (i, 0))],
        out_shape=[jax.ShapeDtypeStruct((n, d), F32), jax.ShapeDtypeStruct((n, d), BF16)],
        compiler_params=_params(("parallel",)),
        name="out_ln",
    )(*parts, w, x, g.reshape(1, d), b.reshape(1, d))


def _overlap_t(t):
    n16 = t // CMP_STRIDE
    n_slc = t // SLC_BLOCK
    c_start = CMP_STRIDE * np.arange(n16)
    j_start = SLC_BLOCK * np.arange(n_slc)
    ov = ((c_start[None, :] < j_start[:, None] + SLC_BLOCK)
          & (c_start[None, :] + CMP_BLOCK > j_start[:, None])).astype(np.float32)
    ov[:, n16 - 1] = 0.0
    return jnp.asarray(ov, BF16)


def _mixer(hf, hb, batch, w_att, w_f32, w_out, ln_g, ln_b, diff_lam, diff_gain, pos2, cmp_w1, cmp_w2,
           conv_w, layer_idx, alpha):
    n, _ = hf.shape
    t = n // batch
    pa = _matmul(hb, w_att, BF16, tm=1024, tn=512, name="proj_att").reshape(batch, t, -1)
    pf = _matmul(hb, w_f32, F32, tm=1024, tn=512, name="proj_f32").reshape(batch, t, -1)

    o_sb = _sb_attention(pa, tq=256, q_blk=0, k_blk=4, v_blk=8)
    lam_init = 0.8 - 0.6 * math.exp(-0.3 * layer_idx)
    o_df = _diff_attention(pa, diff_lam, diff_gain, tq=512, q_blk=12, k_blk=16, v_blk=20, lam_init=lam_init)

    n16 = t // CMP_STRIDE
    kvc = pf[:, :, 3 * CONV_CH:3 * CONV_CH + 2 * LANES].reshape(batch, t, 2, NSA_KV_HEADS, HEAD_DIM)
    x16 = kvc.transpose(0, 2, 3, 1, 4).reshape(batch, 2 * NSA_KV_HEADS, n16, CMP_STRIDE * HEAD_DIM)
    ckv = _compress(x16, pos2, cmp_w1, cmp_w2)
    kv4 = pa[:, :, 28 * LANES:32 * LANES].reshape(batch, t, 4, NSA_KV_HEADS, HEAD_DIM).transpose(2, 0, 3, 1, 4)
    n_slc = t // SLC_BLOCK
    o_ns = _nsa_attention(pa, pf, ckv, _overlap_t(t), kv4, tq=128, q_blk=12, g_blk=14,
                          top_n=min(SLC_TOPN, n_slc))

    o_cv = _conv_mixer(pf, conv_w, tt=512)
    parts = [o.reshape(n, -1) for o in (o_sb, o_df, o_ns, o_cv)]
    return _out_ln(parts, w_out, hf, ln_g, ln_b, alpha=alpha, tm=512)


def kernel(x, ln_g, ln_b, ffn_w1, ffn_w3, ffn_w2, w_in, w_out, diff_lam, diff_gain, cmp_pos, cmp_wk1,
           cmp_wk2, cmp_wv1, cmp_wv2, conv_w):
    batch, t, d = x.shape
    depth = ln_g.shape[0]
    alpha = (2 * depth) ** 0.25
    n = batch * t

    w1b, w3b, w2b = ffn_w1.astype(BF16), ffn_w3.astype(BF16), ffn_w2.astype(BF16)
    w_att = w_in[:, :, _ATT_COLS].astype(BF16)
    gate_cols = []
    for j in range(NSA_KV_HEADS):
        real = w_in[:, :, _SEG_OFF["ns_g"] + j * _GATES_PER_KV:_SEG_OFF["ns_g"] + (j + 1) * _GATES_PER_KV]
        gate_cols.append(jnp.pad(real, ((0, 0), (0, 0), (0, LANES - _GATES_PER_KV))))
    w_f32 = jnp.concatenate([w_in[:, :, _F32_COLS]] + gate_cols, axis=-1).astype(BF16)
    w_outb = w_out.astype(BF16)
    half = CMP_STRIDE * HEAD_DIM
    pos2 = cmp_pos.reshape(depth, 2, half)
    cmp_w1 = jnp.stack([cmp_wk1, cmp_wv1], axis=1).astype(BF16)
    cmp_w2 = jnp.stack([cmp_wk2, cmp_wv2], axis=1).astype(BF16)

    hf = x.reshape(n, d)
    for l in range(depth):
        hf, hb = _ffn_ln(hf, w1b[l, 0], w3b[l, 0], w2b[l, 0], ln_g[l, 0], ln_b[l, 0], alpha=alpha, tm=512, tf=512)
        hf, _ = _mixer(hf, hb, batch, w_att[l], w_f32[l], w_outb[l], ln_g[l, 1], ln_b[l, 1], diff_lam[l],
                       diff_gain[l], pos2[l], cmp_w1[l], cmp_w2[l], conv_w[l], l, alpha)
        hf, _ = _ffn_ln(hf, w1b[l, 1], w3b[l, 1], w2b[l, 1], ln_g[l, 2], ln_b[l, 2], alpha=alpha, tm=512, tf=512)
    return hf.reshape(batch, t, d)
```

```python
import functools
import math

import numpy as np
import jax
import jax.numpy as jnp
from jax import lax
from jax.experimental import pallas as pl
from jax.experimental.pallas import tpu as pltpu

F32 = jnp.float32
BF16 = jnp.bfloat16

HEAD_DIM = 64
SB_HEADS = 8
DIFF_HEADS = 4
NSA_HEADS = 8
NSA_KV_HEADS = 2
NSA_GROUP = NSA_HEADS // NSA_KV_HEADS
CMP_BLOCK = 32
CMP_STRIDE = 16
SLC_BLOCK = 64
SLC_TOPN = 16
WINDOW = 512
N_BRANCH = 3
CONV_CH = 512
CONV_WIDTH = 3
LN_EPS = 1e-5
RMS_EPS = 1e-5
NEG_BIG = -1e30
LOG2_E = math.log2(math.e)
EXP_UNDERFLOW = -104.0
LANES = 128
VMEM_LIMIT = 56 * 1024 * 1024

_SEG_NAMES = ("sb_q", "sb_k", "sb_v", "df_q", "df_k", "df_v", "ns_q", "ns_kc", "ns_vc",
              "ns_ks", "ns_vs", "ns_kw", "ns_vw", "ns_g", "cv_b", "cv_c", "cv_h")
_SEG_WIDTHS = (512, 512, 512, 512, 512, 512, 512, 128, 128, 128, 128, 128, 128,
               NSA_HEADS * N_BRANCH, 512, 512, 512)
_SEG_OFF = dict(zip(_SEG_NAMES, np.cumsum((0,) + _SEG_WIDTHS[:-1]).tolist()))
_SEG_W = dict(zip(_SEG_NAMES, _SEG_WIDTHS))


def _cols(name):
    return np.arange(_SEG_OFF[name], _SEG_OFF[name] + _SEG_W[name])


_ATT_COLS = np.concatenate([_cols(n) for n in
                            ("sb_q", "sb_k", "sb_v", "df_q", "df_k", "df_v", "ns_q",
                             "ns_ks", "ns_vs", "ns_kw", "ns_vw")])
_GATES_PER_KV = NSA_GROUP * N_BRANCH
_F32_COLS = np.concatenate([_cols(n) for n in ("cv_b", "cv_c", "cv_h", "ns_kc", "ns_vc")])


def _params(sem):
    return pltpu.CompilerParams(dimension_semantics=sem, vmem_limit_bytes=VMEM_LIMIT)


def _layer_norm(y, g, b):
    mu = jnp.mean(y, axis=-1, keepdims=True)
    d = y - mu
    var = jnp.mean(d * d, axis=-1, keepdims=True)
    return d * lax.rsqrt(var + LN_EPS) * g + b


def _dot(a, b):
    return jnp.dot(a, b, preferred_element_type=F32)


def _lane_group_max(s, mx):
    for g in range(s.shape[1] // LANES):
        mx = jnp.maximum(mx, s[:, g * LANES:(g + 1) * LANES])
    return mx


def _dot_nt(a, b):
    return lax.dot_general(a, b, (((1,), (1,)), ((), ())), preferred_element_type=F32)


def _ffn_ln_kernel(x_ref, w1_ref, w3_ref, w2_ref, g_ref, b_ref, o_ref, ob_ref, xb_ref, acc_ref,
                   *, alpha):
    j = pl.program_id(1)

    @pl.when(j == 0)
    def _():
        xb_ref[...] = x_ref[...].astype(BF16)
        acc_ref[...] = jnp.zeros_like(acc_ref)

    xb = xb_ref[...]
    a = _dot(xb, w1_ref[...])
    b = _dot(xb, w3_ref[...])
    hm = (a / (1.0 + jnp.exp(-a))) * b
    acc_ref[...] += _dot(hm.astype(BF16), w2_ref[...])

    @pl.when(j == pl.num_programs(1) - 1)
    def _():
        y = alpha * x_ref[...] + 0.5 * acc_ref[...]
        o = _layer_norm(y, g_ref[...], b_ref[...])
        o_ref[...] = o
        ob_ref[...] = o.astype(BF16)


def _ffn_ln(x, w1, w3, w2, g, b, *, alpha, tm, tf):
    n, d = x.shape
    f = w1.shape[1]
    return pl.pallas_call(
        functools.partial(_ffn_ln_kernel, alpha=alpha),
        grid=(n // tm, f // tf),
        in_specs=[
            pl.BlockSpec((tm, d), lambda i, j: (i, 0)),
            pl.BlockSpec((d, tf), lambda i, j: (0, j)),
            pl.BlockSpec((d, tf), lambda i, j: (0, j)),
            pl.BlockSpec((tf, d), lambda i, j: (j, 0)),
            pl.BlockSpec((1, d), lambda i, j: (0, 0)),
            pl.BlockSpec((1, d), lambda i, j: (0, 0)),
        ],
        out_specs=[pl.BlockSpec((tm, d), lambda i, j: (i, 0)),
                   pl.BlockSpec((tm, d), lambda i, j: (i, 0))],
        out_shape=[jax.ShapeDtypeStruct((n, d), F32), jax.ShapeDtypeStruct((n, d), BF16)],
        scratch_shapes=[pltpu.VMEM((tm, d), BF16), pltpu.VMEM((tm, d), F32)],
        compiler_params=_params(("parallel", "arbitrary")),
        name="ffn_ln",
    )(x, w1, w3, w2, g.reshape(1, d), b.reshape(1, d))


def _matmul_kernel(x_ref, w_ref, o_ref):
    o_ref[...] = _dot(x_ref[...], w_ref[...]).astype(o_ref.dtype)


def _matmul(x, w, out_dtype, *, tm, tn, name):
    n, d = x.shape
    m = w.shape[1]
    return pl.pallas_call(
        _matmul_kernel,
        grid=(n // tm, m // tn),
        in_specs=[pl.BlockSpec((tm, d), lambda i, j: (i, 0)),
                  pl.BlockSpec((d, tn), lambda i, j: (0, j))],
        out_specs=pl.BlockSpec((tm, tn), lambda i, j: (i, j)),
        out_shape=jax.ShapeDtypeStruct((n, m), out_dtype),
        compiler_params=_params(("parallel", "arbitrary")),
        name=name,
    )(x, w)


def _sb_kernel(q_ref, k_ref, v_ref, o_ref, *, tq):
    i = pl.program_id(2)
    lane = lax.broadcasted_iota(jnp.int32, (tq, LANES), 1)
    row = lax.broadcasted_iota(jnp.int32, (tq, tq), 0)
    col = lax.broadcasted_iota(jnp.int32, (tq, tq), 1)
    strictly_causal = col < row
    causal2 = jnp.concatenate([strictly_causal, strictly_causal], axis=0)
    tri = jnp.where(row >= col, 1.0, 0.0).astype(BF16)
    qs = (q_ref[0].astype(F32) * (HEAD_DIM ** -0.5)).astype(BF16)
    zero = jnp.zeros_like(qs)
    q2 = jnp.concatenate([jnp.where(lane < HEAD_DIM, qs, zero),
                          jnp.where(lane < HEAD_DIM, zero, qs)], axis=0)

    def tile(kt, c, acc, diagonal):
        start = pl.multiple_of(kt * tq, tq)
        ks = k_ref[0, pl.ds(start, tq), :]
        vs = v_ref[0, pl.ds(start, tq), :]
        z = _dot_nt(q2, ks)
        lg = -(jnp.maximum(z, 0.0) + jnp.log1p(jnp.exp(-jnp.abs(z))))
        if diagonal:
            lg = jnp.where(causal2, lg, 0.0)
        hi = lg.astype(BF16)
        lo = (lg - hi.astype(F32)).astype(BF16)
        suffix = _dot(hi, tri) + _dot(lo, tri)
        a = jnp.exp(z + suffix + c)
        if diagonal:
            a = jnp.where(causal2, a, 0.0)
        return c + suffix[:, 0:1], acc + _dot(a.astype(BF16), vs)

    c1, acc1 = tile(i, jnp.zeros((2 * tq, 1), F32), jnp.zeros((2 * tq, LANES), F32), True)

    def live(c):
        return (jnp.max(c) > EXP_UNDERFLOW).astype(jnp.int32)

    def cond(carry):
        return (carry[0] <= i) & (carry[1] > 0)

    def body(carry):
        n, _, c, acc = carry
        c, acc = tile(i - n, c, acc, False)
        return n + 1, live(c), c, acc

    _, _, _, acc = lax.while_loop(cond, body, (jnp.int32(1), live(c1), c1, acc1))
    o_ref[0] = jnp.where(lane < HEAD_DIM, acc[:tq], acc[tq:]).astype(o_ref.dtype)


def _sb_attention(pa, *, tq, q_blk, k_blk, v_blk):
    b, t, _ = pa.shape
    n_pairs = SB_HEADS // 2
    return pl.pallas_call(
        functools.partial(_sb_kernel, tq=tq),
        grid=(b, n_pairs, t // tq),
        in_specs=[
            pl.BlockSpec((1, tq, LANES), lambda bi, p, i: (bi, i, q_blk + p)),
            pl.BlockSpec((1, t, LANES), lambda bi, p, i: (bi, 0, k_blk + p)),
            pl.BlockSpec((1, t, LANES), lambda bi, p, i: (bi, 0, v_blk + p)),
        ],
        out_specs=pl.BlockSpec((1, tq, LANES), lambda bi, p, i: (bi, i, p)),
        out_shape=jax.ShapeDtypeStruct((b, t, n_pairs * LANES), BF16),
        compiler_params=_params(("parallel", "parallel", "arbitrary")),
        name="sb_attention",
    )(pa, pa, pa)


def _diff_kernel(lam_ref, gain_ref, q_ref, k_ref, v_ref, o_ref, s_ref, *, tq, lam_init):
    i = pl.program_id(2)
    lane = lax.broadcasted_iota(jnp.int32, (tq, LANES), 1)
    row = lax.broadcasted_iota(jnp.int32, (tq, tq), 0)
    col = lax.broadcasted_iota(jnp.int32, (tq, tq), 1)
    causal = col <= row
    causal2 = jnp.concatenate([causal, causal], axis=0)
    qs = (q_ref[0].astype(F32) * (HEAD_DIM ** -0.5)).astype(BF16)
    zero = jnp.zeros_like(qs)
    q2 = jnp.concatenate([jnp.where(lane < HEAD_DIM, qs, zero),
                          jnp.where(lane < HEAD_DIM, zero, qs)], axis=0)

    def scores(kt, mx, diagonal):
        start = pl.multiple_of(kt * tq, tq)
        s = _dot_nt(q2, k_ref[0, pl.ds(start, tq), :]) * LOG2_E
        if diagonal:
            s = jnp.where(causal2, s, NEG_BIG)
        s_ref[:, pl.ds(start, tq)] = s
        return _lane_group_max(s, mx)

    mx = lax.fori_loop(0, i, lambda kt, mx: scores(kt, mx, False),
                       jnp.full((2 * tq, LANES), NEG_BIG, F32))
    mx = scores(i, mx, True)
    m = jnp.broadcast_to(jnp.max(mx, axis=1, keepdims=True), (2 * tq, LANES))
    m = jnp.concatenate([m] * (tq // LANES), axis=1)

    ones = jnp.ones((tq, LANES), BF16)

    def weigh(kt, acc):
        start = pl.multiple_of(kt * tq, tq)
        p = jnp.exp2(s_ref[:, pl.ds(start, tq)] - m).astype(BF16)
        vx = jnp.concatenate([v_ref[0, pl.ds(start, tq), :], ones], axis=1)
        return acc + _dot(p, vx)

    acc = lax.fori_loop(0, i + 1, weigh, jnp.zeros((2 * tq, 2 * LANES), F32))
    o = acc[:, :LANES] / acc[:, LANES:]
    lp = lam_ref[...]
    lam = (jnp.exp(jnp.sum(lp[0:1] * lp[1:2], axis=1, keepdims=True))
           - jnp.exp(jnp.sum(lp[2:3] * lp[3:4], axis=1, keepdims=True)) + lam_init)
    od = o[:tq] - lam * o[tq:]
    od = od * lax.rsqrt(jnp.mean(od * od, axis=-1, keepdims=True) + RMS_EPS)
    o_ref[0] = (od * gain_ref[...] * (1.0 - lam_init)).astype(o_ref.dtype)


def _diff_attention(pa, lam_params, gain, *, tq, q_blk, k_blk, v_blk, lam_init):
    b, t, _ = pa.shape
    return pl.pallas_call(
        functools.partial(_diff_kernel, tq=tq, lam_init=lam_init),
        grid=(b, DIFF_HEADS, t // tq),
        in_specs=[
            pl.BlockSpec((4, HEAD_DIM), lambda bi, h, i: (0, 0)),
            pl.BlockSpec((1, LANES), lambda bi, h, i: (0, 0)),
            pl.BlockSpec((1, tq, LANES), lambda bi, h, i: (bi, i, q_blk + h)),
            pl.BlockSpec((1, t, LANES), lambda bi, h, i: (bi, 0, k_blk + h)),
            pl.BlockSpec((1, t, LANES), lambda bi, h, i: (bi, 0, v_blk + h)),
        ],
        out_specs=pl.BlockSpec((1, tq, LANES), lambda bi, h, i: (bi, i, h)),
        out_shape=jax.ShapeDtypeStruct((b, t, DIFF_HEADS * LANES), BF16),
        scratch_shapes=[pltpu.VMEM((2 * tq, t), F32)],
        compiler_params=_params(("parallel", "parallel", "arbitrary")),
        name="diff_attention",
    )(lam_params, gain.reshape(1, LANES), pa, pa, pa)


def _compress_kernel(x_ref, pos_ref, w1_ref, w2_ref, o_ref):
    x = x_ref[0, 0]
    n16, half = x.shape
    xa = (x + pos_ref[0:1]).astype(BF16)
    xb = (x + pos_ref[1:2]).astype(BF16)
    w1 = w1_ref[0]
    first = _dot(xa, w1[:half])
    second = _dot(xb, w1[half:])
    h = first + pltpu.roll(second, n16 - 1, 0)
    g = 0.5 * h * (1.0 + jnp.tanh(math.sqrt(2.0 / math.pi) * (h + 0.044715 * (h * h * h))))
    o_ref[0, 0] = _dot(g.astype(BF16), w2_ref[0])


def _compress(x16, pos2, w1, w2):
    b, n_streams, n16, half = x16.shape
    d = w2.shape[-1]
    return pl.pallas_call(
        _compress_kernel,
        grid=(b, n_streams),
        in_specs=[
            pl.BlockSpec((1, 1, n16, half), lambda bi, s: (bi, s, 0, 0)),
            pl.BlockSpec((2, half), lambda bi, s: (0, 0)),
            pl.BlockSpec((1, 2 * half, w1.shape[-1]), lambda bi, s: (s // NSA_KV_HEADS, 0, 0)),
            pl.BlockSpec((1, w2.shape[1], d), lambda bi, s: (s // NSA_KV_HEADS, 0, 0)),
        ],
        out_specs=pl.BlockSpec((1, 1, n16, d), lambda bi, s: (bi, s, 0, 0)),
        out_shape=jax.ShapeDtypeStruct((b, n_streams, n16, d), F32),
        compiler_params=_params(("parallel", "arbitrary")),
        name="nsa_compress",
    )(x16, pos2, w1, w2)


def _nsa_kernel(q_ref, g_ref, ck_ref, cv_ref, ovt_ref, ks_ref, vs_ref, kw_ref, vw_ref, o_ref,
                kse_ref, kso_ref, vse_ref, vso_ref, kwd_ref, vwe_ref, vwo_ref, ckd_ref, cvd_ref, s_ref,
                *, tq, top_n):
    i = pl.program_id(2)
    grp = NSA_GROUP
    d = HEAD_DIM
    q0 = i * tq
    n_cmp = ck_ref.shape[2]
    n_slc = ovt_ref.shape[0]
    t = ks_ref.shape[2]
    rows = grp * tq
    half = rows // 2

    @pl.when(i == 0)
    def _():
        onehot = jnp.where(lax.broadcasted_iota(jnp.int32, (t, d), 0) // SLC_BLOCK
                           == lax.broadcasted_iota(jnp.int32, (t, d), 1), 1.0, 0.0)
        ones = jnp.ones((t, d), F32)

        def put(ref, left, right):
            ref[...] = jnp.concatenate([left, right], axis=1).astype(BF16)

        ks, vs = ks_ref[0, 0].astype(F32), vs_ref[0, 0].astype(F32)
        kw, vw = kw_ref[0, 0].astype(F32), vw_ref[0, 0].astype(F32)
        put(kse_ref, ks, onehot)
        put(kso_ref, onehot, ks)
        put(vse_ref, vs, ones)
        put(vso_ref, ones, vs)
        put(kwd_ref, kw, kw)
        put(vwe_ref, vw, ones)
        put(vwo_ref, ones, vw)
        put(ckd_ref, ck_ref[0, 0], ck_ref[0, 0])
        put(cvd_ref, cv_ref[0, 0], cv_ref[0, 0])

    lane = lax.broadcasted_iota(jnp.int32, (tq, LANES), 1)
    left = lane < d
    qs = (q_ref[0].astype(F32) * (d ** -0.5)).astype(BF16)
    pairs = [qs[:, :LANES], qs[:, LANES:]]
    zero = jnp.zeros((tq, LANES), BF16)
    qz = jnp.concatenate([jnp.where(left, pairs[0], zero), jnp.where(left, pairs[1], zero),
                          jnp.where(left, zero, pairs[0]), jnp.where(left, zero, pairs[1])], axis=0)
    t4 = q0 + lax.rem(lax.broadcasted_iota(jnp.int32, (rows, 1), 0), tq)

    def rep4(x):
        return jnp.concatenate([x] * grp, axis=0)

    s = _dot_nt(qz, ckd_ref[...])
    cmp_end = CMP_STRIDE * lax.broadcasted_iota(jnp.int32, (1, n_cmp), 1) + (CMP_BLOCK - 1)
    cmask = cmp_end <= t4
    s = jnp.where(cmask, s, NEG_BIG)
    e = jnp.exp(s - jnp.max(s, axis=1, keepdims=True))
    p = e / jnp.sum(e, axis=1, keepdims=True)
    p = jnp.where(cmask, p, 0.0)
    o_cmp = _dot(p.astype(BF16), cvd_ref[...])

    pg = p[0:tq]
    for h in range(1, grp):
        pg = pg + p[h * tq:(h + 1) * tq]
    p_hi = pg.astype(BF16)
    r1 = pg - p_hi.astype(F32)
    p_mid = r1.astype(BF16)
    p_lo = (r1 - p_mid.astype(F32)).astype(BF16)
    ovt = ovt_ref[...]
    imp_t = _dot_nt(ovt, p_hi) + _dot_nt(ovt, p_mid) + _dot_nt(ovt, p_lo)
    blk = lax.broadcasted_iota(jnp.int32, (n_slc, tq), 0)
    cur = (q0 + lax.broadcasted_iota(jnp.int32, (n_slc, tq), 1)) // SLC_BLOCK
    forced = (blk == 0) | (blk == cur) | (blk == cur - 1)
    score = jnp.where(forced, jnp.inf, jnp.where(blk <= cur, imp_t, -jnp.inf))
    rank = jnp.zeros((n_slc, tq), F32)
    for j in range(n_slc):
        sj = score[j:j + 1, :]
        tie = jnp.where(blk > j, 1.0, 0.0)
        rank = rank + jnp.where(sj > score, 1.0, jnp.where(sj == score, tie, 0.0))
    sel_t = jnp.where(rank < top_n, 1.0, 0.0)
    if n_slc < d:
        sel_t = jnp.concatenate([sel_t, jnp.zeros((d - n_slc, tq), F32)], axis=0)
    sel_t2 = jnp.concatenate([sel_t, sel_t], axis=0).astype(BF16)
    eye = jnp.where(lax.broadcasted_iota(jnp.int32, (tq, tq), 0)
                    == lax.broadcasted_iota(jnp.int32, (tq, tq), 1), 1.0, 0.0).astype(BF16)
    sel2 = _dot_nt(eye, sel_t2)
    bias = ((sel2 - 1.0) * (-NEG_BIG)).astype(BF16)
    q_even = jnp.concatenate([jnp.where(left, pairs[0], bias), jnp.where(left, pairs[1], bias)], axis=0)
    q_odd = jnp.concatenate([jnp.where(left, bias, pairs[0]), jnp.where(left, bias, pairs[1])], axis=0)

    row = lax.broadcasted_iota(jnp.int32, (tq, tq), 0)
    col = lax.broadcasted_iota(jnp.int32, (tq, tq), 1)
    causal4 = rep4(col <= row)

    def attend(scores_fn, first_tile, ve_ref, vo_ref):
        def scores(kt, mx, diagonal):
            start = pl.multiple_of(kt * tq, tq)
            s = scores_fn(start, diagonal)
            s_ref[:, pl.ds(start, tq)] = s
            return _lane_group_max(s, mx)

        mx = lax.fori_loop(first_tile, i, lambda kt, mx: scores(kt, mx, False),
                           jnp.full((rows, LANES), NEG_BIG, F32))
        mx = scores(i, mx, True)
        m = jnp.broadcast_to(jnp.max(mx, axis=1, keepdims=True), (rows, LANES))
        m = jnp.concatenate([m] * (tq // LANES), axis=1)

        def weigh(kt, acc):
            start = pl.multiple_of(kt * tq, tq)
            p = jnp.exp2(s_ref[:, pl.ds(start, tq)] - m).astype(BF16)
            return acc + jnp.concatenate([_dot(p[:half], ve_ref[pl.ds(start, tq), :]),
                                          _dot(p[half:], vo_ref[pl.ds(start, tq), :])], axis=0)

        acc = lax.fori_loop(first_tile, i + 1, weigh, jnp.zeros((rows, LANES), F32))
        return acc / pltpu.roll(acc, d, 1)

    def slc_scores(start, diagonal):
        s = jnp.concatenate([_dot_nt(q_even, kse_ref[pl.ds(start, tq), :]),
                             _dot_nt(q_odd, kso_ref[pl.ds(start, tq), :])], axis=0) * LOG2_E
        return jnp.where(causal4, s, NEG_BIG) if diagonal else s

    o_slc = attend(slc_scores, 0, vse_ref, vso_ref)

    def win_scores(start, diagonal):
        s = _dot_nt(qz, kwd_ref[pl.ds(start, tq), :]) * LOG2_E
        if diagonal:
            keep = causal4
        else:
            keep = start + lax.broadcasted_iota(jnp.int32, (1, tq), 1) > t4 - WINDOW
        return jnp.where(keep, s, NEG_BIG)

    o_win = attend(win_scores, jnp.maximum(i - (-(-WINDOW // tq)), 0), vwe_ref, vwo_ref)

    gate = 1.0 / (1.0 + jnp.exp(-g_ref[0]))

    def gated(h, r0):
        r = slice(r0, r0 + tq)
        return (gate[:, 3 * h:3 * h + 1] * o_cmp[r] + gate[:, 3 * h + 1:3 * h + 2] * o_slc[r]
                + gate[:, 3 * h + 2:3 * h + 3] * o_win[r])

    out = [jnp.where(left, gated(2 * pr, pr * tq), gated(2 * pr + 1, half + pr * tq)) for pr in range(2)]
    o_ref[0] = jnp.concatenate(out, axis=1).astype(o_ref.dtype)


def _nsa_attention(pa, pf, ckv, ovt, kv4, *, tq, q_blk, g_blk, top_n):
    b, t, _ = pa.shape
    n16 = ckv.shape[2]
    n_slc = ovt.shape[0]
    qw = NSA_GROUP * HEAD_DIM
    kv_spec = pl.BlockSpec((1, 1, t, HEAD_DIM), lambda bi, j, i: (bi, j, 0, 0))
    return pl.pallas_call(
        functools.partial(_nsa_kernel, tq=tq, top_n=top_n),
        grid=(b, NSA_KV_HEADS, t // tq),
        in_specs=[
            pl.BlockSpec((1, tq, qw), lambda bi, j, i: (bi, i, q_blk + j)),
            pl.BlockSpec((1, tq, LANES), lambda bi, j, i: (bi, i, g_blk + j)),
            pl.BlockSpec((1, 1, n16, HEAD_DIM), lambda bi, j, i: (bi, j, 0, 0)),
            pl.BlockSpec((1, 1, n16, HEAD_DIM), lambda bi, j, i: (bi, NSA_KV_HEADS + j, 0, 0)),
            pl.BlockSpec((n_slc, n16), lambda bi, j, i: (0, 0)),
            kv_spec, kv_spec, kv_spec, kv_spec,
        ],
        out_specs=pl.BlockSpec((1, tq, qw), lambda bi, j, i: (bi, i, j)),
        out_shape=jax.ShapeDtypeStruct((b, t, NSA_HEADS * HEAD_DIM), BF16),
        scratch_shapes=[pltpu.VMEM((t, LANES), BF16)] * 7 + [pltpu.VMEM((n16, LANES), BF16)] * 2
                       + [pltpu.VMEM((NSA_GROUP * tq, t), F32)],
        compiler_params=_params(("arbitrary", "arbitrary", "arbitrary")),
        name="nsa_attention",
    )(pa, pf, ckv, ckv, ovt, kv4[0], kv4[1], kv4[2], kv4[3])


def _conv_kernel(cb_ref, cc_ref, ch_ref, w_ref, o_ref, prev_ref):
    @pl.when(pl.program_id(1) == 0)
    def _():
        prev_ref[...] = jnp.zeros_like(prev_ref)

    u = cc_ref[0] * ch_ref[0]
    tt = u.shape[0]
    row = lax.broadcasted_iota(jnp.int32, u.shape, 0)
    last1 = prev_ref[7:8]
    last2 = prev_ref[6:7]
    u1 = jnp.where(row >= 1, pltpu.roll(u, 1, 0), last1)
    u2 = jnp.where(row >= 2, pltpu.roll(u, 2, 0), jnp.where(row == 1, last1, last2))
    w = w_ref[...]
    o_ref[0] = (cb_ref[0] * (w[0:1] * u2 + w[1:2] * u1 + w[2:3] * u)).astype(o_ref.dtype)
    prev_ref[...] = u[tt - 8:tt]


def _conv_mixer(pf, conv_w, *, tt):
    b, t, _ = pf.shape
    c = CONV_CH
    return pl.pallas_call(
        _conv_kernel,
        grid=(b, t // tt),
        in_specs=[
            pl.BlockSpec((1, tt, c), lambda bi, i: (bi, i, 0)),
            pl.BlockSpec((1, tt, c), lambda bi, i: (bi, i, 1)),
            pl.BlockSpec((1, tt, c), lambda bi, i: (bi, i, 2)),
            pl.BlockSpec((CONV_WIDTH, c), lambda bi, i: (0, 0)),
        ],
        out_specs=pl.BlockSpec((1, tt, c), lambda bi, i: (bi, i, 0)),
        out_shape=jax.ShapeDtypeStruct((b, t, c), BF16),
        scratch_shapes=[pltpu.VMEM((8, c), F32)],
        compiler_params=_params(("parallel", "arbitrary")),
        name="conv_mixer",
    )(pf, pf, pf, conv_w)


def _out_ln_kernel(a_ref, b_ref, c_ref, d_ref, w_ref, x_ref, g_ref, beta_ref, o_ref, ob_ref, *, alpha):
    kw = a_ref.shape[1]
    mix = _dot(a_ref[...], w_ref[0:kw])
    for n, r in enumerate((b_ref, c_ref, d_ref), start=1):
        mix = mix + _dot(r[...], w_ref[n * kw:(n + 1) * kw])
    o = _layer_norm(alpha * x_ref[...] + mix, g_ref[...], beta_ref[...])
    o_ref[...] = o
    ob_ref[...] = o.astype(BF16)


def _out_ln(parts, w, x, g, b, *, alpha, tm):
    n, d = x.shape
    kw = parts[0].shape[1]
    part_spec = pl.BlockSpec((tm, kw), lambda i: (i, 0))
    return pl.pallas_call(
        functools.partial(_out_ln_kernel, alpha=alpha),
        grid=(n // tm,),
        in_specs=[part_spec, part_spec, part_spec, part_spec,
                  pl.BlockSpec(w.shape, lambda i: (0, 0)),
                  pl.BlockSpec((tm, d), lambda i: (i, 0)),
                  pl.BlockSpec((1, d), lambda i: (0, 0)),
                  pl.BlockSpec((1, d), lambda i: (0, 0))],
        out_specs=[pl.BlockSpec((tm, d), lambda i: (i, 0)), pl.BlockSpec((tm, d), lambda i: (i, 0))],
        out_shape=[jax.ShapeDtypeStruct((n, d), F32), jax.ShapeDtypeStruct((n, d), BF16)],
        compiler_params=_params(("parallel",)),
        name="out_ln",
    )(*parts, w, x, g.reshape(1, d), b.reshape(1, d))


def _overlap_t(t):
    n16 = t // CMP_STRIDE
    n_slc = t // SLC_BLOCK
    c_start = CMP_STRIDE * np.arange(n16)
    j_start = SLC_BLOCK * np.arange(n_slc)
    ov = ((c_start[None, :] < j_start[:, None] + SLC_BLOCK)
          & (c_start[None, :] + CMP_BLOCK > j_start[:, None])).astype(np.float32)
    ov[:, n16 - 1] = 0.0
    return jnp.asarray(ov, BF16)


def _mixer(hf, hb, batch, w_att, w_f32, w_out, ln_g, ln_b, diff_lam, diff_gain, pos2, cmp_w1, cmp_w2,
           conv_w, layer_idx, alpha):
    n, _ = hf.shape
    t = n // batch
    pa = _matmul(hb, w_att, BF16, tm=1024, tn=512, name="proj_att").reshape(batch, t, -1)
    pf = _matmul(hb, w_f32, F32, tm=1024, tn=512, name="proj_f32").reshape(batch, t, -1)

    o_sb = _sb_attention(pa, tq=256, q_blk=0, k_blk=4, v_blk=8)
    lam_init = 0.8 - 0.6 * math.exp(-0.3 * layer_idx)
    o_df = _diff_attention(pa, diff_lam, diff_gain, tq=512, q_blk=12, k_blk=16, v_blk=20, lam_init=lam_init)

    n16 = t // CMP_STRIDE
    kvc = pf[:, :, 3 * CONV_CH:3 * CONV_CH + 2 * LANES].reshape(batch, t, 2, NSA_KV_HEADS, HEAD_DIM)
    x16 = kvc.transpose(0, 2, 3, 1, 4).reshape(batch, 2 * NSA_KV_HEADS, n16, CMP_STRIDE * HEAD_DIM)
    ckv = _compress(x16, pos2, cmp_w1, cmp_w2)
    kv4 = pa[:, :, 28 * LANES:32 * LANES].reshape(batch, t, 4, NSA_KV_HEADS, HEAD_DIM).transpose(2, 0, 3, 1, 4)
    n_slc = t // SLC_BLOCK
    o_ns = _nsa_attention(pa, pf, ckv, _overlap_t(t), kv4, tq=256, q_blk=12, g_blk=14,
                          top_n=min(SLC_TOPN, n_slc))

    o_cv = _conv_mixer(pf, conv_w, tt=512)
    parts = [o.reshape(n, -1) for o in (o_sb, o_df, o_ns, o_cv)]
    return _out_ln(parts, w_out, hf, ln_g, ln_b, alpha=alpha, tm=512)


def kernel(x, ln_g, ln_b, ffn_w1, ffn_w3, ffn_w2, w_in, w_out, diff_lam, diff_gain, cmp_pos, cmp_wk1,
           cmp_wk2, cmp_wv1, cmp_wv2, conv_w):
    batch, t, d = x.shape
    depth = ln_g.shape[0]
    alpha = (2 * depth) ** 0.25
    n = batch * t

    w1b, w3b, w2b = ffn_w1.astype(BF16), ffn_w3.astype(BF16), ffn_w2.astype(BF16)
    w_att = w_in[:, :, _ATT_COLS].astype(BF16)
    gate_cols = []
    for j in range(NSA_KV_HEADS):
        real = w_in[:, :, _SEG_OFF["ns_g"] + j * _GATES_PER_KV:_SEG_OFF["ns_g"] + (j + 1) * _GATES_PER_KV]
        gate_cols.append(jnp.pad(real, ((0, 0), (0, 0), (0, LANES - _GATES_PER_KV))))
    w_f32 = jnp.concatenate([w_in[:, :, _F32_COLS]] + gate_cols, axis=-1).astype(BF16)
    w_outb = w_out.astype(BF16)
    half = CMP_STRIDE * HEAD_DIM
    pos2 = cmp_pos.reshape(depth, 2, half)
    cmp_w1 = jnp.stack([cmp_wk1, cmp_wv1], axis=1).astype(BF16)
    cmp_w2 = jnp.stack([cmp_wk2, cmp_wv2], axis=1).astype(BF16)

    hf = x.reshape(n, d)
    for l in range(depth):
        hf, hb = _ffn_ln(hf, w1b[l, 0], w3b[l, 0], w2b[l, 0], ln_g[l, 0], ln_b[l, 0], alpha=alpha, tm=512, tf=512)
        hf, _ = _mixer(hf, hb, batch, w_att[l], w_f32[l], w_outb[l], ln_g[l, 1], ln_b[l, 1], diff_lam[l],
                       diff_gain[l], pos2[l], cmp_w1[l], cmp_w2[l], conv_w[l], l, alpha)
        hf, _ = _ffn_ln(hf, w1b[l, 1], w3b[l, 1], w2b[l, 1], ln_g[l, 2], ln_b[l, 2], alpha=alpha, tm=512, tf=512)
    return hf.reshape(batch, t, d)
```

```python
import functools
import math

import numpy as np
import jax
import jax.numpy as jnp
from jax import lax
from jax.experimental import pallas as pl
from jax.experimental.pallas import tpu as pltpu

F32 = jnp.float32
BF16 = jnp.bfloat16

HEAD_DIM = 64
SB_HEADS = 8
DIFF_HEADS = 4
NSA_HEADS = 8
NSA_KV_HEADS = 2
NSA_GROUP = NSA_HEADS // NSA_KV_HEADS
CMP_BLOCK = 32
CMP_STRIDE = 16
SLC_BLOCK = 64
SLC_TOPN = 16
WINDOW = 512
N_BRANCH = 3
CONV_CH = 512
CONV_WIDTH = 3
LN_EPS = 1e-5
RMS_EPS = 1e-5
NEG_BIG = -1e30
LOG2_E = math.log2(math.e)
EXP_UNDERFLOW = -104.0
LANES = 128
VMEM_LIMIT = 56 * 1024 * 1024

_SEG_NAMES = ("sb_q", "sb_k", "sb_v", "df_q", "df_k", "df_v", "ns_q", "ns_kc", "ns_vc",
              "ns_ks", "ns_vs", "ns_kw", "ns_vw", "ns_g", "cv_b", "cv_c", "cv_h")
_SEG_WIDTHS = (512, 512, 512, 512, 512, 512, 512, 128, 128, 128, 128, 128, 128,
               NSA_HEADS * N_BRANCH, 512, 512, 512)
_SEG_OFF = dict(zip(_SEG_NAMES, np.cumsum((0,) + _SEG_WIDTHS[:-1]).tolist()))
_SEG_W = dict(zip(_SEG_NAMES, _SEG_WIDTHS))


_GATES_PER_KV = NSA_GROUP * N_BRANCH


def _params(sem):
    return pltpu.CompilerParams(dimension_semantics=sem, vmem_limit_bytes=VMEM_LIMIT)


def _layer_norm(y, g, b):
    mu = jnp.mean(y, axis=-1, keepdims=True)
    d = y - mu
    var = jnp.mean(d * d, axis=-1, keepdims=True)
    return d * lax.rsqrt(var + LN_EPS) * g + b


def _dot(a, b):
    return jnp.dot(a, b, preferred_element_type=F32)


def _lane_group_max(s, mx):
    for g in range(s.shape[1] // LANES):
        mx = jnp.maximum(mx, s[:, g * LANES:(g + 1) * LANES])
    return mx


def _dot_nt(a, b):
    return lax.dot_general(a, b, (((1,), (1,)), ((), ())), preferred_element_type=F32)


def _ffn_ln_kernel(x_ref, w1_ref, w3_ref, w2_ref, g_ref, b_ref, o_ref, ob_ref, xb_ref, acc_ref,
                   *, alpha):
    j = pl.program_id(1)

    @pl.when(j == 0)
    def _():
        xb_ref[...] = x_ref[...].astype(BF16)
        acc_ref[...] = jnp.zeros_like(acc_ref)

    xb = xb_ref[...]
    a = _dot(xb, w1_ref[...])
    b = _dot(xb, w3_ref[...])
    hm = (a / (1.0 + jnp.exp(-a))) * b
    acc_ref[...] += _dot(hm.astype(BF16), w2_ref[...])

    @pl.when(j == pl.num_programs(1) - 1)
    def _():
        y = alpha * x_ref[...] + 0.5 * acc_ref[...]
        o = _layer_norm(y, g_ref[...], b_ref[...])
        o_ref[...] = o
        ob_ref[...] = o.astype(BF16)


def _ffn_ln(x, w1, w3, w2, g, b, *, layer, slot, alpha, tm, tf):
    n, d = x.shape
    f = w1.shape[-1]
    return pl.pallas_call(
        functools.partial(_ffn_ln_kernel, alpha=alpha),
        grid=(n // tm, f // tf),
        in_specs=[
            pl.BlockSpec((tm, d), lambda i, j: (i, 0)),
            pl.BlockSpec((None, None, d, tf), lambda i, j: (layer, slot, 0, j)),
            pl.BlockSpec((None, None, d, tf), lambda i, j: (layer, slot, 0, j)),
            pl.BlockSpec((None, None, tf, d), lambda i, j: (layer, slot, j, 0)),
            pl.BlockSpec((1, d), lambda i, j: (0, 0)),
            pl.BlockSpec((1, d), lambda i, j: (0, 0)),
        ],
        out_specs=[pl.BlockSpec((tm, d), lambda i, j: (i, 0)),
                   pl.BlockSpec((tm, d), lambda i, j: (i, 0))],
        out_shape=[jax.ShapeDtypeStruct((n, d), F32), jax.ShapeDtypeStruct((n, d), BF16)],
        scratch_shapes=[pltpu.VMEM((tm, d), BF16), pltpu.VMEM((tm, d), F32)],
        compiler_params=_params(("parallel", "arbitrary")),
        name="ffn_ln",
    )(x, w1, w3, w2, g.reshape(1, d), b.reshape(1, d))


def _matmul_kernel(x_ref, w_ref, o_ref):
    o_ref[...] = _dot(x_ref[...], w_ref[...]).astype(o_ref.dtype)


def _matmul(x, w, out_dtype, *, layer, tm, tn, name):
    n, d = x.shape
    m = w.shape[-1]
    return pl.pallas_call(
        _matmul_kernel,
        grid=(n // tm, m // tn),
        in_specs=[pl.BlockSpec((tm, d), lambda i, j: (i, 0)),
                  pl.BlockSpec((None, d, tn), lambda i, j: (layer, 0, j))],
        out_specs=pl.BlockSpec((tm, tn), lambda i, j: (i, j)),
        out_shape=jax.ShapeDtypeStruct((n, m), out_dtype),
        compiler_params=_params(("parallel", "arbitrary")),
        name=name,
    )(x, w)


def _sb_kernel(q_ref, k_ref, v_ref, o_ref, *, tq):
    i = pl.program_id(2)
    lane = lax.broadcasted_iota(jnp.int32, (tq, LANES), 1)
    row = lax.broadcasted_iota(jnp.int32, (tq, tq), 0)
    col = lax.broadcasted_iota(jnp.int32, (tq, tq), 1)
    strictly_causal = col < row
    causal2 = jnp.concatenate([strictly_causal, strictly_causal], axis=0)
    tri = jnp.where(row >= col, 1.0, 0.0).astype(BF16)
    qs = (q_ref[0].astype(F32) * (HEAD_DIM ** -0.5)).astype(BF16)
    zero = jnp.zeros_like(qs)
    q2 = jnp.concatenate([jnp.where(lane < HEAD_DIM, qs, zero),
                          jnp.where(lane < HEAD_DIM, zero, qs)], axis=0)

    def tile(kt, c, acc, diagonal):
        start = pl.multiple_of(kt * tq, tq)
        ks = k_ref[0, pl.ds(start, tq), :]
        vs = v_ref[0, pl.ds(start, tq), :]
        z = _dot_nt(q2, ks)
        lg = -(jnp.maximum(z, 0.0) + jnp.log1p(jnp.exp(-jnp.abs(z))))
        if diagonal:
            lg = jnp.where(causal2, lg, 0.0)
        hi = lg.astype(BF16)
        lo = (lg - hi.astype(F32)).astype(BF16)
        suffix = _dot(hi, tri) + _dot(lo, tri)
        a = jnp.exp(z + suffix + c)
        if diagonal:
            a = jnp.where(causal2, a, 0.0)
        return c + suffix[:, 0:1], acc + _dot(a.astype(BF16), vs)

    c1, acc1 = tile(i, jnp.zeros((2 * tq, 1), F32), jnp.zeros((2 * tq, LANES), F32), True)

    def live(c):
        return (jnp.max(c) > EXP_UNDERFLOW).astype(jnp.int32)

    def cond(carry):
        return (carry[0] <= i) & (carry[1] > 0)

    def body(carry):
        n, _, c, acc = carry
        c, acc = tile(i - n, c, acc, False)
        return n + 1, live(c), c, acc

    _, _, _, acc = lax.while_loop(cond, body, (jnp.int32(1), live(c1), c1, acc1))
    o_ref[0] = jnp.where(lane < HEAD_DIM, acc[:tq], acc[tq:]).astype(o_ref.dtype)


def _sb_attention(pa, *, tq, q_blk, k_blk, v_blk):
    b, t, _ = pa.shape
    n_pairs = SB_HEADS // 2
    return pl.pallas_call(
        functools.partial(_sb_kernel, tq=tq),
        grid=(b, n_pairs, t // tq),
        in_specs=[
            pl.BlockSpec((1, tq, LANES), lambda bi, p, i: (bi, i, q_blk + p)),
            pl.BlockSpec((1, t, LANES), lambda bi, p, i: (bi, 0, k_blk + p)),
            pl.BlockSpec((1, t, LANES), lambda bi, p, i: (bi, 0, v_blk + p)),
        ],
        out_specs=pl.BlockSpec((1, tq, LANES), lambda bi, p, i: (bi, i, p)),
        out_shape=jax.ShapeDtypeStruct((b, t, n_pairs * LANES), BF16),
        compiler_params=_params(("parallel", "parallel", "arbitrary")),
        name="sb_attention",
    )(pa, pa, pa)


def _diff_kernel(lam_ref, gain_ref, q_ref, k_ref, v_ref, o_ref, s_ref, *, tq, lam_init):
    i = pl.program_id(2)
    lane = lax.broadcasted_iota(jnp.int32, (tq, LANES), 1)
    row = lax.broadcasted_iota(jnp.int32, (tq, tq), 0)
    col = lax.broadcasted_iota(jnp.int32, (tq, tq), 1)
    causal = col <= row
    causal2 = jnp.concatenate([causal, causal], axis=0)
    qs = (q_ref[0].astype(F32) * (HEAD_DIM ** -0.5)).astype(BF16)
    zero = jnp.zeros_like(qs)
    q2 = jnp.concatenate([jnp.where(lane < HEAD_DIM, qs, zero),
                          jnp.where(lane < HEAD_DIM, zero, qs)], axis=0)

    def scores(kt, mx, diagonal):
        start = pl.multiple_of(kt * tq, tq)
        s = _dot_nt(q2, k_ref[0, pl.ds(start, tq), :]) * LOG2_E
        if diagonal:
            s = jnp.where(causal2, s, NEG_BIG)
        s_ref[:, pl.ds(start, tq)] = s
        return _lane_group_max(s, mx)

    mx = lax.fori_loop(0, i, lambda kt, mx: scores(kt, mx, False),
                       jnp.full((2 * tq, LANES), NEG_BIG, F32))
    mx = scores(i, mx, True)
    m = jnp.broadcast_to(jnp.max(mx, axis=1, keepdims=True), (2 * tq, LANES))
    m = jnp.concatenate([m] * (tq // LANES), axis=1)

    ones = jnp.ones((tq, LANES), BF16)

    def weigh(kt, acc):
        start = pl.multiple_of(kt * tq, tq)
        p = jnp.exp2(s_ref[:, pl.ds(start, tq)] - m).astype(BF16)
        vx = jnp.concatenate([v_ref[0, pl.ds(start, tq), :], ones], axis=1)
        return acc + _dot(p, vx)

    acc = lax.fori_loop(0, i + 1, weigh, jnp.zeros((2 * tq, 2 * LANES), F32))
    o = acc[:, :LANES] / acc[:, LANES:]
    lp = lam_ref[...]
    lam = (jnp.exp(jnp.sum(lp[0:1] * lp[1:2], axis=1, keepdims=True))
           - jnp.exp(jnp.sum(lp[2:3] * lp[3:4], axis=1, keepdims=True)) + lam_init)
    od = o[:tq] - lam * o[tq:]
    od = od * lax.rsqrt(jnp.mean(od * od, axis=-1, keepdims=True) + RMS_EPS)
    o_ref[0] = (od * gain_ref[...] * (1.0 - lam_init)).astype(o_ref.dtype)


def _diff_attention(pa, lam_params, gain, *, tq, q_blk, k_blk, v_blk, lam_init):
    b, t, _ = pa.shape
    return pl.pallas_call(
        functools.partial(_diff_kernel, tq=tq, lam_init=lam_init),
        grid=(b, DIFF_HEADS, t // tq),
        in_specs=[
            pl.BlockSpec((4, HEAD_DIM), lambda bi, h, i: (0, 0)),
            pl.BlockSpec((1, LANES), lambda bi, h, i: (0, 0)),
            pl.BlockSpec((1, tq, LANES), lambda bi, h, i: (bi, i, q_blk + h)),
            pl.BlockSpec((1, t, LANES), lambda bi, h, i: (bi, 0, k_blk + h)),
            pl.BlockSpec((1, t, LANES), lambda bi, h, i: (bi, 0, v_blk + h)),
        ],
        out_specs=pl.BlockSpec((1, tq, LANES), lambda bi, h, i: (bi, i, h)),
        out_shape=jax.ShapeDtypeStruct((b, t, DIFF_HEADS * LANES), BF16),
        scratch_shapes=[pltpu.VMEM((2 * tq, t), F32)],
        compiler_params=_params(("parallel", "parallel", "arbitrary")),
        name="diff_attention",
    )(lam_params, gain.reshape(1, LANES), pa, pa, pa)


def _compress_kernel(x_ref, pos_ref, w1_ref, w2_ref, o_ref):
    x = x_ref[0, 0]
    n16, half = x.shape
    xa = (x + pos_ref[0:1]).astype(BF16)
    xb = (x + pos_ref[1:2]).astype(BF16)
    w1 = w1_ref[0]
    first = _dot(xa, w1[:half])
    second = _dot(xb, w1[half:])
    h = first + pltpu.roll(second, n16 - 1, 0)
    g = 0.5 * h * (1.0 + jnp.tanh(math.sqrt(2.0 / math.pi) * (h + 0.044715 * (h * h * h))))
    o_ref[0, 0] = _dot(g.astype(BF16), w2_ref[0])


def _compress(x16, pos2, w1, w2, *, layer):
    b, n_streams, n16, half = x16.shape
    d = w2.shape[-1]
    return pl.pallas_call(
        _compress_kernel,
        grid=(b, n_streams),
        in_specs=[
            pl.BlockSpec((1, 1, n16, half), lambda bi, s: (bi, s, 0, 0)),
            pl.BlockSpec((None, 2, half), lambda bi, s: (layer, 0, 0)),
            pl.BlockSpec((None, 1, 2 * half, w1.shape[-1]), lambda bi, s: (layer, s // NSA_KV_HEADS, 0, 0)),
            pl.BlockSpec((None, 1, w2.shape[2], d), lambda bi, s: (layer, s // NSA_KV_HEADS, 0, 0)),
        ],
        out_specs=pl.BlockSpec((1, 1, n16, d), lambda bi, s: (bi, s, 0, 0)),
        out_shape=jax.ShapeDtypeStruct((b, n_streams, n16, d), F32),
        compiler_params=_params(("parallel", "arbitrary")),
        name="nsa_compress",
    )(x16, pos2, w1, w2)


def _nsa_kernel(q_ref, g_ref, ck_ref, cv_ref, ovt_ref, ks_ref, vs_ref, kw_ref, vw_ref, o_ref,
                kse_ref, kso_ref, vse_ref, vso_ref, kwd_ref, vwe_ref, vwo_ref, ckd_ref, cvd_ref, s_ref,
                *, tq, top_n):
    i = pl.program_id(2)
    grp = NSA_GROUP
    d = HEAD_DIM
    q0 = i * tq
    n_cmp = ck_ref.shape[2]
    n_slc = ovt_ref.shape[0]
    t = ks_ref.shape[2]
    rows = grp * tq
    half = rows // 2

    @pl.when(i == 0)
    def _():
        onehot = jnp.where(lax.broadcasted_iota(jnp.int32, (t, d), 0) // SLC_BLOCK
                           == lax.broadcasted_iota(jnp.int32, (t, d), 1), 1.0, 0.0)
        ones = jnp.ones((t, d), F32)

        def put(ref, left, right):
            ref[...] = jnp.concatenate([left, right], axis=1).astype(BF16)

        ks, vs = ks_ref[0, 0].astype(F32), vs_ref[0, 0].astype(F32)
        kw, vw = kw_ref[0, 0].astype(F32), vw_ref[0, 0].astype(F32)
        put(kse_ref, ks, onehot)
        put(kso_ref, onehot, ks)
        put(vse_ref, vs, ones)
        put(vso_ref, ones, vs)
        put(kwd_ref, kw, kw)
        put(vwe_ref, vw, ones)
        put(vwo_ref, ones, vw)
        put(ckd_ref, ck_ref[0, 0], ck_ref[0, 0])
        put(cvd_ref, cv_ref[0, 0], cv_ref[0, 0])

    lane = lax.broadcasted_iota(jnp.int32, (tq, LANES), 1)
    left = lane < d
    qs = (q_ref[0].astype(F32) * (d ** -0.5)).astype(BF16)
    pairs = [qs[:, :LANES], qs[:, LANES:]]
    zero = jnp.zeros((tq, LANES), BF16)
    qz = jnp.concatenate([jnp.where(left, pairs[0], zero), jnp.where(left, pairs[1], zero),
                          jnp.where(left, zero, pairs[0]), jnp.where(left, zero, pairs[1])], axis=0)
    t4 = q0 + lax.rem(lax.broadcasted_iota(jnp.int32, (rows, 1), 0), tq)

    def rep4(x):
        return jnp.concatenate([x] * grp, axis=0)

    s = _dot_nt(qz, ckd_ref[...])
    cmp_end = CMP_STRIDE * lax.broadcasted_iota(jnp.int32, (1, n_cmp), 1) + (CMP_BLOCK - 1)
    cmask = cmp_end <= t4
    s = jnp.where(cmask, s, NEG_BIG)
    e = jnp.exp(s - jnp.max(s, axis=1, keepdims=True))
    p = e / jnp.sum(e, axis=1, keepdims=True)
    p = jnp.where(cmask, p, 0.0)
    o_cmp = _dot(p.astype(BF16), cvd_ref[...])

    pg = p[0:tq]
    for h in range(1, grp):
        pg = pg + p[h * tq:(h + 1) * tq]
    p_hi = pg.astype(BF16)
    r1 = pg - p_hi.astype(F32)
    p_mid = r1.astype(BF16)
    p_lo = (r1 - p_mid.astype(F32)).astype(BF16)
    ovt = ovt_ref[...]
    imp_t = _dot_nt(ovt, p_hi) + _dot_nt(ovt, p_mid) + _dot_nt(ovt, p_lo)
    blk = lax.broadcasted_iota(jnp.int32, (n_slc, tq), 0)
    cur = (q0 + lax.broadcasted_iota(jnp.int32, (n_slc, tq), 1)) // SLC_BLOCK
    forced = (blk == 0) | (blk == cur) | (blk == cur - 1)
    score = jnp.where(forced, jnp.inf, jnp.where(blk <= cur, imp_t, -jnp.inf))
    rank = jnp.zeros((n_slc, tq), F32)
    for j in range(n_slc):
        sj = score[j:j + 1, :]
        tie = jnp.where(blk > j, 1.0, 0.0)
        rank = rank + jnp.where(sj > score, 1.0, jnp.where(sj == score, tie, 0.0))
    sel_t = jnp.where(rank < top_n, 1.0, 0.0)
    if n_slc < d:
        sel_t = jnp.concatenate([sel_t, jnp.zeros((d - n_slc, tq), F32)], axis=0)
    sel_t2 = jnp.concatenate([sel_t, sel_t], axis=0).astype(BF16)
    eye = jnp.where(lax.broadcasted_iota(jnp.int32, (tq, tq), 0)
                    == lax.broadcasted_iota(jnp.int32, (tq, tq), 1), 1.0, 0.0).astype(BF16)
    sel2 = _dot_nt(eye, sel_t2)
    bias = ((sel2 - 1.0) * (-NEG_BIG)).astype(BF16)
    q_even = jnp.concatenate([jnp.where(left, pairs[0], bias), jnp.where(left, pairs[1], bias)], axis=0)
    q_odd = jnp.concatenate([jnp.where(left, bias, pairs[0]), jnp.where(left, bias, pairs[1])], axis=0)

    row = lax.broadcasted_iota(jnp.int32, (tq, tq), 0)
    col = lax.broadcasted_iota(jnp.int32, (tq, tq), 1)
    causal4 = rep4(col <= row)

    def attend(scores_fn, first_tile, ve_ref, vo_ref):
        def scores(kt, mx, diagonal):
            start = pl.multiple_of(kt * tq, tq)
            s = scores_fn(start, diagonal)
            s_ref[:, pl.ds(start, tq)] = s
            return _lane_group_max(s, mx)

        mx = lax.fori_loop(first_tile, i, lambda kt, mx: scores(kt, mx, False),
                           jnp.full((rows, LANES), NEG_BIG, F32))
        mx = scores(i, mx, True)
        m = jnp.broadcast_to(jnp.max(mx, axis=1, keepdims=True), (rows, LANES))
        m = jnp.concatenate([m] * (tq // LANES), axis=1)

        def weigh(kt, acc):
            start = pl.multiple_of(kt * tq, tq)
            p = jnp.exp2(s_ref[:, pl.ds(start, tq)] - m).astype(BF16)
            return acc + jnp.concatenate([_dot(p[:half], ve_ref[pl.ds(start, tq), :]),
                                          _dot(p[half:], vo_ref[pl.ds(start, tq), :])], axis=0)

        acc = lax.fori_loop(first_tile, i + 1, weigh, jnp.zeros((rows, LANES), F32))
        return acc / pltpu.roll(acc, d, 1)

    def slc_scores(start, diagonal):
        s = jnp.concatenate([_dot_nt(q_even, kse_ref[pl.ds(start, tq), :]),
                             _dot_nt(q_odd, kso_ref[pl.ds(start, tq), :])], axis=0) * LOG2_E
        return jnp.where(causal4, s, NEG_BIG) if diagonal else s

    o_slc = attend(slc_scores, 0, vse_ref, vso_ref)

    def win_scores(start, diagonal):
        s = _dot_nt(qz, kwd_ref[pl.ds(start, tq), :]) * LOG2_E
        if diagonal:
            keep = causal4
        else:
            keep = start + lax.broadcasted_iota(jnp.int32, (1, tq), 1) > t4 - WINDOW
        return jnp.where(keep, s, NEG_BIG)

    o_win = attend(win_scores, jnp.maximum(i - (-(-WINDOW // tq)), 0), vwe_ref, vwo_ref)

    gate = 1.0 / (1.0 + jnp.exp(-g_ref[0]))

    def gated(h, r0):
        r = slice(r0, r0 + tq)
        return (gate[:, 3 * h:3 * h + 1] * o_cmp[r] + gate[:, 3 * h + 1:3 * h + 2] * o_slc[r]
                + gate[:, 3 * h + 2:3 * h + 3] * o_win[r])

    out = [jnp.where(left, gated(2 * pr, pr * tq), gated(2 * pr + 1, half + pr * tq)) for pr in range(2)]
    o_ref[0] = jnp.concatenate(out, axis=1).astype(o_ref.dtype)


def _nsa_attention(pa, pf, ckv, ovt, kv4, *, tq, q_blk, g_blk, top_n):
    b, t, _ = pa.shape
    n16 = ckv.shape[2]
    n_slc = ovt.shape[0]
    qw = NSA_GROUP * HEAD_DIM
    kv_spec = pl.BlockSpec((1, 1, t, HEAD_DIM), lambda bi, j, i: (bi, j, 0, 0))
    return pl.pallas_call(
        functools.partial(_nsa_kernel, tq=tq, top_n=top_n),
        grid=(b, NSA_KV_HEADS, t // tq),
        in_specs=[
            pl.BlockSpec((1, tq, qw), lambda bi, j, i: (bi, i, q_blk + j)),
            pl.BlockSpec((1, tq, LANES), lambda bi, j, i: (bi, i, g_blk + j)),
            pl.BlockSpec((1, 1, n16, HEAD_DIM), lambda bi, j, i: (bi, j, 0, 0)),
            pl.BlockSpec((1, 1, n16, HEAD_DIM), lambda bi, j, i: (bi, NSA_KV_HEADS + j, 0, 0)),
            pl.BlockSpec((n_slc, n16), lambda bi, j, i: (0, 0)),
            kv_spec, kv_spec, kv_spec, kv_spec,
        ],
        out_specs=pl.BlockSpec((1, tq, qw), lambda bi, j, i: (bi, i, j)),
        out_shape=jax.ShapeDtypeStruct((b, t, NSA_HEADS * HEAD_DIM), BF16),
        scratch_shapes=[pltpu.VMEM((t, LANES), BF16)] * 7 + [pltpu.VMEM((n16, LANES), BF16)] * 2
                       + [pltpu.VMEM((NSA_GROUP * tq, t), F32)],
        compiler_params=_params(("arbitrary", "arbitrary", "arbitrary")),
        name="nsa_attention",
    )(pa, pf, ckv, ckv, ovt, kv4[0], kv4[1], kv4[2], kv4[3])


def _conv_kernel(cb_ref, cc_ref, ch_ref, w_ref, o_ref, prev_ref):
    @pl.when(pl.program_id(1) == 0)
    def _():
        prev_ref[...] = jnp.zeros_like(prev_ref)

    u = cc_ref[0] * ch_ref[0]
    tt = u.shape[0]
    row = lax.broadcasted_iota(jnp.int32, u.shape, 0)
    last1 = prev_ref[7:8]
    last2 = prev_ref[6:7]
    u1 = jnp.where(row >= 1, pltpu.roll(u, 1, 0), last1)
    u2 = jnp.where(row >= 2, pltpu.roll(u, 2, 0), jnp.where(row == 1, last1, last2))
    w = w_ref[...]
    o_ref[0] = (cb_ref[0] * (w[0:1] * u2 + w[1:2] * u1 + w[2:3] * u)).astype(o_ref.dtype)
    prev_ref[...] = u[tt - 8:tt]


def _conv_mixer(pf, conv_w, *, tt):
    b, t, _ = pf.shape
    c = CONV_CH
    return pl.pallas_call(
        _conv_kernel,
        grid=(b, t // tt),
        in_specs=[
            pl.BlockSpec((1, tt, c), lambda bi, i: (bi, i, 0)),
            pl.BlockSpec((1, tt, c), lambda bi, i: (bi, i, 1)),
            pl.BlockSpec((1, tt, c), lambda bi, i: (bi, i, 2)),
            pl.BlockSpec((CONV_WIDTH, c), lambda bi, i: (0, 0)),
        ],
        out_specs=pl.BlockSpec((1, tt, c), lambda bi, i: (bi, i, 0)),
        out_shape=jax.ShapeDtypeStruct((b, t, c), BF16),
        scratch_shapes=[pltpu.VMEM((8, c), F32)],
        compiler_params=_params(("parallel", "arbitrary")),
        name="conv_mixer",
    )(pf, pf, pf, conv_w)


def _out_ln_kernel(a_ref, b_ref, c_ref, d_ref, w_ref, x_ref, g_ref, beta_ref, o_ref, ob_ref, *, alpha):
    kw = a_ref.shape[1]
    mix = _dot(a_ref[...], w_ref[0:kw])
    for n, r in enumerate((b_ref, c_ref, d_ref), start=1):
        mix = mix + _dot(r[...], w_ref[n * kw:(n + 1) * kw])
    o = _layer_norm(alpha * x_ref[...] + mix, g_ref[...], beta_ref[...])
    o_ref[...] = o
    ob_ref[...] = o.astype(BF16)


def _out_ln(parts, w, x, g, b, *, layer, alpha, tm):
    n, d = x.shape
    kw = parts[0].shape[1]
    part_spec = pl.BlockSpec((tm, kw), lambda i: (i, 0))
    return pl.pallas_call(
        functools.partial(_out_ln_kernel, alpha=alpha),
        grid=(n // tm,),
        in_specs=[part_spec, part_spec, part_spec, part_spec,
                  pl.BlockSpec((None,) + w.shape[1:], lambda i: (layer, 0, 0)),
                  pl.BlockSpec((tm, d), lambda i: (i, 0)),
                  pl.BlockSpec((1, d), lambda i: (0, 0)),
                  pl.BlockSpec((1, d), lambda i: (0, 0))],
        out_specs=[pl.BlockSpec((tm, d), lambda i: (i, 0)), pl.BlockSpec((tm, d), lambda i: (i, 0))],
        out_shape=[jax.ShapeDtypeStruct((n, d), F32), jax.ShapeDtypeStruct((n, d), BF16)],
        compiler_params=_params(("parallel",)),
        name="out_ln",
    )(*parts, w, x, g.reshape(1, d), b.reshape(1, d))


def _overlap_t(t):
    n16 = t // CMP_STRIDE
    n_slc = t // SLC_BLOCK
    c_start = CMP_STRIDE * np.arange(n16)
    j_start = SLC_BLOCK * np.arange(n_slc)
    ov = ((c_start[None, :] < j_start[:, None] + SLC_BLOCK)
          & (c_start[None, :] + CMP_BLOCK > j_start[:, None])).astype(np.float32)
    ov[:, n16 - 1] = 0.0
    return jnp.asarray(ov, BF16)


def _mixer(hf, hb, batch, w_att, w_f32, w_out, ln_g, ln_b, diff_lam, diff_gain, pos2, cmp_w1, cmp_w2,
           conv_w, layer, alpha):
    n, _ = hf.shape
    t = n // batch
    pa = _matmul(hb, w_att, BF16, layer=layer, tm=1024, tn=512, name="proj_att").reshape(batch, t, -1)
    pf = _matmul(hb, w_f32, F32, layer=layer, tm=1024, tn=512, name="proj_f32").reshape(batch, t, -1)

    o_sb = _sb_attention(pa, tq=256, q_blk=0, k_blk=4, v_blk=8)
    lam_init = 0.8 - 0.6 * math.exp(-0.3 * layer)
    o_df = _diff_attention(pa, diff_lam[layer], diff_gain[layer], tq=512, q_blk=12, k_blk=16, v_blk=20,
                           lam_init=lam_init)

    n16 = t // CMP_STRIDE
    kvc = pf[:, :, 3 * CONV_CH:3 * CONV_CH + 2 * LANES].reshape(batch, t, 2, NSA_KV_HEADS, HEAD_DIM)
    x16 = kvc.transpose(0, 2, 3, 1, 4).reshape(batch, 2 * NSA_KV_HEADS, n16, CMP_STRIDE * HEAD_DIM)
    ckv = _compress(x16, pos2, cmp_w1, cmp_w2, layer=layer)
    kv4 = pa[:, :, 28 * LANES:32 * LANES].reshape(batch, t, 4, NSA_KV_HEADS, HEAD_DIM).transpose(2, 0, 3, 1, 4)
    n_slc = t // SLC_BLOCK
    o_ns = _nsa_attention(pa, pf, ckv, _overlap_t(t), kv4, tq=256, q_blk=12, g_blk=14,
                          top_n=min(SLC_TOPN, n_slc))

    o_cv = _conv_mixer(pf, conv_w[layer], tt=512)
    parts = [o.reshape(n, -1) for o in (o_sb, o_df, o_ns, o_cv)]
    return _out_ln(parts, w_out, hf, ln_g, ln_b, layer=layer, alpha=alpha, tm=512)


def _seg(w_in, name, lo=0, hi=None):
    off = _SEG_OFF[name]
    hi = _SEG_W[name] if hi is None else hi
    return w_in[:, :, off + lo:off + hi]


def kernel(x, ln_g, ln_b, ffn_w1, ffn_w3, ffn_w2, w_in, w_out, diff_lam, diff_gain, cmp_pos, cmp_wk1,
           cmp_wk2, cmp_wv1, cmp_wv2, conv_w):
    batch, t, d = x.shape
    depth = ln_g.shape[0]
    alpha = (2 * depth) ** 0.25
    n = batch * t

    w1b, w3b, w2b = ffn_w1.astype(BF16), ffn_w3.astype(BF16), ffn_w2.astype(BF16)
    w_att = jnp.concatenate([w_in[:, :, :_SEG_OFF["ns_kc"]], w_in[:, :, _SEG_OFF["ns_ks"]:_SEG_OFF["ns_g"]]],
                            axis=-1).astype(BF16)
    gate_pad = jnp.zeros(w_in.shape[:2] + (LANES - _GATES_PER_KV,), w_in.dtype)
    f32_cols = [_seg(w_in, "cv_b"), _seg(w_in, "cv_c"), _seg(w_in, "cv_h"), _seg(w_in, "ns_kc"), _seg(w_in, "ns_vc")]
    for j in range(NSA_KV_HEADS):
        f32_cols += [_seg(w_in, "ns_g", j * _GATES_PER_KV, (j + 1) * _GATES_PER_KV), gate_pad]
    w_f32 = jnp.concatenate(f32_cols, axis=-1).astype(BF16)
    w_outb = w_out.astype(BF16)
    half = CMP_STRIDE * HEAD_DIM
    pos2 = cmp_pos.reshape(depth, 2, half)
    cmp_w1 = jnp.stack([cmp_wk1, cmp_wv1], axis=1).astype(BF16)
    cmp_w2 = jnp.stack([cmp_wk2, cmp_wv2], axis=1).astype(BF16)

    hf = x.reshape(n, d)
    for l in range(depth):
        hf, hb = _ffn_ln(hf, w1b, w3b, w2b, ln_g[l, 0], ln_b[l, 0], layer=l, slot=0, alpha=alpha, tm=512, tf=512)
        hf, _ = _mixer(hf, hb, batch, w_att, w_f32, w_outb, ln_g[l, 1], ln_b[l, 1], diff_lam, diff_gain,
                       pos2, cmp_w1, cmp_w2, conv_w, l, alpha)
        hf, _ = _ffn_ln(hf, w1b, w3b, w2b, ln_g[l, 2], ln_b[l, 2], layer=l, slot=1, alpha=alpha, tm=512, tf=512)
    return hf.reshape(batch, t, d)
```

```python
import functools
import math

import numpy as np
import jax
import jax.numpy as jnp
from jax import lax
from jax.experimental import pallas as pl
from jax.experimental.pallas import tpu as pltpu

F32 = jnp.float32
BF16 = jnp.bfloat16

HEAD_DIM = 64
SB_HEADS = 8
DIFF_HEADS = 4
NSA_HEADS = 8
NSA_KV_HEADS = 2
NSA_GROUP = NSA_HEADS // NSA_KV_HEADS
CMP_BLOCK = 32
CMP_STRIDE = 16
SLC_BLOCK = 64
SLC_TOPN = 16
WINDOW = 512
N_BRANCH = 3
CONV_CH = 512
CONV_WIDTH = 3
LN_EPS = 1e-5
RMS_EPS = 1e-5
NEG_BIG = -1e30
LOG2_E = math.log2(math.e)
EXP_UNDERFLOW = -104.0
LANES = 128
VMEM_LIMIT = 56 * 1024 * 1024

_SEG_NAMES = ("sb_q", "sb_k", "sb_v", "df_q", "df_k", "df_v", "ns_q", "ns_kc", "ns_vc",
              "ns_ks", "ns_vs", "ns_kw", "ns_vw", "ns_g", "cv_b", "cv_c", "cv_h")
_SEG_WIDTHS = (512, 512, 512, 512, 512, 512, 512, 128, 128, 128, 128, 128, 128,
               NSA_HEADS * N_BRANCH, 512, 512, 512)
_SEG_OFF = dict(zip(_SEG_NAMES, np.cumsum((0,) + _SEG_WIDTHS[:-1]).tolist()))
_SEG_W = dict(zip(_SEG_NAMES, _SEG_WIDTHS))


_GATES_PER_KV = NSA_GROUP * N_BRANCH


def _params(sem):
    return pltpu.CompilerParams(dimension_semantics=sem, vmem_limit_bytes=VMEM_LIMIT)


def _layer_norm(y, g, b):
    mu = jnp.mean(y, axis=-1, keepdims=True)
    d = y - mu
    var = jnp.mean(d * d, axis=-1, keepdims=True)
    return d * lax.rsqrt(var + LN_EPS) * g + b


def _dot(a, b):
    return jnp.dot(a, b, preferred_element_type=F32)


def _lane_group_max(s, mx):
    for g in range(s.shape[1] // LANES):
        mx = jnp.maximum(mx, s[:, g * LANES:(g + 1) * LANES])
    return mx


def _softmax_sweep(s_ref, *, rows, acc_lanes, first, q0, tq, big, scores_fn, weigh_fn):
    n_big = (q0 - first) // big if big else 0
    mid = first + n_big * big if big else first
    n_small = (q0 - mid) // tq

    def chunks(n, base, width, fn, init):
        return lax.fori_loop(0, n, lambda c, carry: fn(pl.multiple_of(base + c * width, tq), width, carry), init)

    def scores(start, width, mx, diagonal=False):
        s = scores_fn(start, width, diagonal)
        s_ref[:, pl.ds(start, width)] = s
        return _lane_group_max(s, mx)

    mx = jnp.full((rows, LANES), NEG_BIG, F32)
    if big:
        mx = chunks(n_big, first, big, scores, mx)
    mx = chunks(n_small, mid, tq, scores, mx)
    mx = scores(q0, tq, mx, True)
    m = jnp.broadcast_to(jnp.max(mx, axis=1, keepdims=True), (rows, LANES))

    def weigh(start, width, acc):
        p = jnp.exp2(s_ref[:, pl.ds(start, width)] - jnp.concatenate([m] * (width // LANES), axis=1))
        return acc + weigh_fn(p.astype(BF16), start, width)

    acc = jnp.zeros((rows, acc_lanes), F32)
    if big:
        acc = chunks(n_big, first, big, weigh, acc)
    acc = chunks(n_small, mid, tq, weigh, acc)
    return weigh(q0, tq, acc)


def _dot_nt(a, b):
    return lax.dot_general(a, b, (((1,), (1,)), ((), ())), preferred_element_type=F32)


def _ffn_ln_kernel(x_ref, w1_ref, w3_ref, w2_ref, g_ref, b_ref, o_ref, ob_ref, xb_ref, acc_ref,
                   *, alpha):
    j = pl.program_id(1)

    @pl.when(j == 0)
    def _():
        xb_ref[...] = x_ref[...].astype(BF16)
        acc_ref[...] = jnp.zeros_like(acc_ref)

    xb = xb_ref[...]
    a = _dot(xb, w1_ref[...])
    b = _dot(xb, w3_ref[...])
    hm = (a / (1.0 + jnp.exp(-a))) * b
    acc_ref[...] += _dot(hm.astype(BF16), w2_ref[...])

    @pl.when(j == pl.num_programs(1) - 1)
    def _():
        y = alpha * x_ref[...] + 0.5 * acc_ref[...]
        o = _layer_norm(y, g_ref[...], b_ref[...])
        o_ref[...] = o
        ob_ref[...] = o.astype(BF16)


def _ffn_ln(x, w1, w3, w2, g, b, *, layer, slot, alpha, tm, tf):
    n, d = x.shape
    f = w1.shape[-1]
    return pl.pallas_call(
        functools.partial(_ffn_ln_kernel, alpha=alpha),
        grid=(n // tm, f // tf),
        in_specs=[
            pl.BlockSpec((tm, d), lambda i, j: (i, 0)),
            pl.BlockSpec((None, None, d, tf), lambda i, j: (layer, slot, 0, j)),
            pl.BlockSpec((None, None, d, tf), lambda i, j: (layer, slot, 0, j)),
            pl.BlockSpec((None, None, tf, d), lambda i, j: (layer, slot, j, 0)),
            pl.BlockSpec((1, d), lambda i, j: (0, 0)),
            pl.BlockSpec((1, d), lambda i, j: (0, 0)),
        ],
        out_specs=[pl.BlockSpec((tm, d), lambda i, j: (i, 0)),
                   pl.BlockSpec((tm, d), lambda i, j: (i, 0))],
        out_shape=[jax.ShapeDtypeStruct((n, d), F32), jax.ShapeDtypeStruct((n, d), BF16)],
        scratch_shapes=[pltpu.VMEM((tm, d), BF16), pltpu.VMEM((tm, d), F32)],
        compiler_params=_params(("parallel", "arbitrary")),
        name="ffn_ln",
    )(x, w1, w3, w2, g.reshape(1, d), b.reshape(1, d))


def _matmul_kernel(x_ref, w_ref, o_ref):
    o_ref[...] = _dot(x_ref[...], w_ref[...]).astype(o_ref.dtype)


def _matmul(x, w, out_dtype, *, layer, tm, tn, name):
    n, d = x.shape
    m = w.shape[-1]
    return pl.pallas_call(
        _matmul_kernel,
        grid=(n // tm, m // tn),
        in_specs=[pl.BlockSpec((tm, d), lambda i, j: (i, 0)),
                  pl.BlockSpec((None, d, tn), lambda i, j: (layer, 0, j))],
        out_specs=pl.BlockSpec((tm, tn), lambda i, j: (i, j)),
        out_shape=jax.ShapeDtypeStruct((n, m), out_dtype),
        compiler_params=_params(("parallel", "arbitrary")),
        name=name,
    )(x, w)


def _sb_kernel(q_ref, k_ref, v_ref, o_ref, *, tq):
    i = pl.program_id(2)
    lane = lax.broadcasted_iota(jnp.int32, (tq, LANES), 1)
    row = lax.broadcasted_iota(jnp.int32, (tq, tq), 0)
    col = lax.broadcasted_iota(jnp.int32, (tq, tq), 1)
    strictly_causal = col < row
    causal2 = jnp.concatenate([strictly_causal, strictly_causal], axis=0)
    tri = jnp.where(row >= col, 1.0, 0.0).astype(BF16)
    qs = (q_ref[0].astype(F32) * (HEAD_DIM ** -0.5)).astype(BF16)
    zero = jnp.zeros_like(qs)
    q2 = jnp.concatenate([jnp.where(lane < HEAD_DIM, qs, zero),
                          jnp.where(lane < HEAD_DIM, zero, qs)], axis=0)

    def tile(kt, c, acc, diagonal):
        start = pl.multiple_of(kt * tq, tq)
        ks = k_ref[0, pl.ds(start, tq), :]
        vs = v_ref[0, pl.ds(start, tq), :]
        z = _dot_nt(q2, ks)
        lg = -(jnp.maximum(z, 0.0) + jnp.log1p(jnp.exp(-jnp.abs(z))))
        if diagonal:
            lg = jnp.where(causal2, lg, 0.0)
        hi = lg.astype(BF16)
        lo = (lg - hi.astype(F32)).astype(BF16)
        suffix = _dot(hi, tri) + _dot(lo, tri)
        a = jnp.exp(z + suffix + c)
        if diagonal:
            a = jnp.where(causal2, a, 0.0)
        return c + suffix[:, 0:1], acc + _dot(a.astype(BF16), vs)

    c1, acc1 = tile(i, jnp.zeros((2 * tq, 1), F32), jnp.zeros((2 * tq, LANES), F32), True)

    def live(c):
        return (jnp.max(c) > EXP_UNDERFLOW).astype(jnp.int32)

    def cond(carry):
        return (carry[0] <= i) & (carry[1] > 0)

    def body(carry):
        n, _, c, acc = carry
        c, acc = tile(i - n, c, acc, False)
        return n + 1, live(c), c, acc

    _, _, _, acc = lax.while_loop(cond, body, (jnp.int32(1), live(c1), c1, acc1))
    o_ref[0] = jnp.where(lane < HEAD_DIM, acc[:tq], acc[tq:]).astype(o_ref.dtype)


def _sb_attention(pa, *, tq, q_blk, k_blk, v_blk):
    b, t, _ = pa.shape
    n_pairs = SB_HEADS // 2
    return pl.pallas_call(
        functools.partial(_sb_kernel, tq=tq),
        grid=(b, n_pairs, t // tq),
        in_specs=[
            pl.BlockSpec((1, tq, LANES), lambda bi, p, i: (bi, i, q_blk + p)),
            pl.BlockSpec((1, t, LANES), lambda bi, p, i: (bi, 0, k_blk + p)),
            pl.BlockSpec((1, t, LANES), lambda bi, p, i: (bi, 0, v_blk + p)),
        ],
        out_specs=pl.BlockSpec((1, tq, LANES), lambda bi, p, i: (bi, i, p)),
        out_shape=jax.ShapeDtypeStruct((b, t, n_pairs * LANES), BF16),
        compiler_params=_params(("parallel", "parallel", "arbitrary")),
        name="sb_attention",
    )(pa, pa, pa)


def _diff_kernel(lam_ref, gain_ref, q_ref, k_ref, v_ref, o_ref, s_ref, *, tq, lam_init):
    i = pl.program_id(2)
    lane = lax.broadcasted_iota(jnp.int32, (tq, LANES), 1)
    row = lax.broadcasted_iota(jnp.int32, (tq, tq), 0)
    col = lax.broadcasted_iota(jnp.int32, (tq, tq), 1)
    causal = col <= row
    causal2 = jnp.concatenate([causal, causal], axis=0)
    qs = (q_ref[0].astype(F32) * (HEAD_DIM ** -0.5)).astype(BF16)
    zero = jnp.zeros_like(qs)
    q2 = jnp.concatenate([jnp.where(lane < HEAD_DIM, qs, zero),
                          jnp.where(lane < HEAD_DIM, zero, qs)], axis=0)

    def scores(start, width, diagonal):
        s = _dot_nt(q2, k_ref[0, pl.ds(start, width), :]) * LOG2_E
        return jnp.where(causal2, s, NEG_BIG) if diagonal else s

    def weigh(p, start, width):
        return _dot(p, jnp.concatenate([v_ref[0, pl.ds(start, width), :], jnp.ones((width, LANES), BF16)], axis=1))

    acc = _softmax_sweep(s_ref, rows=2 * tq, acc_lanes=2 * LANES, first=0, q0=pl.multiple_of(i * tq, tq), tq=tq,
                         big=2 * tq, scores_fn=scores, weigh_fn=weigh)
    o = acc[:, :LANES] / acc[:, LANES:]
    lp = lam_ref[...]
    lam = (jnp.exp(jnp.sum(lp[0:1] * lp[1:2], axis=1, keepdims=True))
           - jnp.exp(jnp.sum(lp[2:3] * lp[3:4], axis=1, keepdims=True)) + lam_init)
    od = o[:tq] - lam * o[tq:]
    od = od * lax.rsqrt(jnp.mean(od * od, axis=-1, keepdims=True) + RMS_EPS)
    o_ref[0] = (od * gain_ref[...] * (1.0 - lam_init)).astype(o_ref.dtype)


def _diff_attention(pa, lam_params, gain, *, tq, q_blk, k_blk, v_blk, lam_init):
    b, t, _ = pa.shape
    return pl.pallas_call(
        functools.partial(_diff_kernel, tq=tq, lam_init=lam_init),
        grid=(b, DIFF_HEADS, t // tq),
        in_specs=[
            pl.BlockSpec((4, HEAD_DIM), lambda bi, h, i: (0, 0)),
            pl.BlockSpec((1, LANES), lambda bi, h, i: (0, 0)),
            pl.BlockSpec((1, tq, LANES), lambda bi, h, i: (bi, i, q_blk + h)),
            pl.BlockSpec((1, t, LANES), lambda bi, h, i: (bi, 0, k_blk + h)),
            pl.BlockSpec((1, t, LANES), lambda bi, h, i: (bi, 0, v_blk + h)),
        ],
        out_specs=pl.BlockSpec((1, tq, LANES), lambda bi, h, i: (bi, i, h)),
        out_shape=jax.ShapeDtypeStruct((b, t, DIFF_HEADS * LANES), BF16),
        scratch_shapes=[pltpu.VMEM((2 * tq, t), F32)],
        compiler_params=_params(("parallel", "parallel", "arbitrary")),
        name="diff_attention",
    )(lam_params, gain.reshape(1, LANES), pa, pa, pa)


def _compress_kernel(x_ref, pos_ref, w1_ref, w2_ref, o_ref):
    x = x_ref[0, 0]
    n16, half = x.shape
    xa = (x + pos_ref[0:1]).astype(BF16)
    xb = (x + pos_ref[1:2]).astype(BF16)
    w1 = w1_ref[0]
    first = _dot(xa, w1[:half])
    second = _dot(xb, w1[half:])
    h = first + pltpu.roll(second, n16 - 1, 0)
    g = 0.5 * h * (1.0 + jnp.tanh(math.sqrt(2.0 / math.pi) * (h + 0.044715 * (h * h * h))))
    o_ref[0, 0] = _dot(g.astype(BF16), w2_ref[0])


def _compress(x16, pos2, w1, w2, *, layer):
    b, n_streams, n16, half = x16.shape
    d = w2.shape[-1]
    return pl.pallas_call(
        _compress_kernel,
        grid=(b, n_streams),
        in_specs=[
            pl.BlockSpec((1, 1, n16, half), lambda bi, s: (bi, s, 0, 0)),
            pl.BlockSpec((None, 2, half), lambda bi, s: (layer, 0, 0)),
            pl.BlockSpec((None, 1, 2 * half, w1.shape[-1]), lambda bi, s: (layer, s // NSA_KV_HEADS, 0, 0)),
            pl.BlockSpec((None, 1, w2.shape[2], d), lambda bi, s: (layer, s // NSA_KV_HEADS, 0, 0)),
        ],
        out_specs=pl.BlockSpec((1, 1, n16, d), lambda bi, s: (bi, s, 0, 0)),
        out_shape=jax.ShapeDtypeStruct((b, n_streams, n16, d), F32),
        compiler_params=_params(("parallel", "arbitrary")),
        name="nsa_compress",
    )(x16, pos2, w1, w2)


def _nsa_kernel(q_ref, g_ref, ck_ref, cv_ref, ovt_ref, ks_ref, vs_ref, kw_ref, vw_ref, o_ref,
                kse_ref, kso_ref, vse_ref, vso_ref, kwd_ref, vwe_ref, vwo_ref, ckd_ref, cvd_ref, s_ref,
                *, tq, top_n):
    i = pl.program_id(2)
    grp = NSA_GROUP
    d = HEAD_DIM
    q0 = i * tq
    n_cmp = ck_ref.shape[2]
    n_slc = ovt_ref.shape[0]
    t = ks_ref.shape[2]
    rows = grp * tq
    half = rows // 2

    @pl.when(i == 0)
    def _():
        onehot = jnp.where(lax.broadcasted_iota(jnp.int32, (t, d), 0) // SLC_BLOCK
                           == lax.broadcasted_iota(jnp.int32, (t, d), 1), 1.0, 0.0)
        ones = jnp.ones((t, d), F32)

        def put(ref, left, right):
            ref[...] = jnp.concatenate([left, right], axis=1).astype(BF16)

        ks, vs = ks_ref[0, 0].astype(F32), vs_ref[0, 0].astype(F32)
        kw, vw = kw_ref[0, 0].astype(F32), vw_ref[0, 0].astype(F32)
        put(kse_ref, ks, onehot)
        put(kso_ref, onehot, ks)
        put(vse_ref, vs, ones)
        put(vso_ref, ones, vs)
        put(kwd_ref, kw, kw)
        put(vwe_ref, vw, ones)
        put(vwo_ref, ones, vw)
        put(ckd_ref, ck_ref[0, 0], ck_ref[0, 0])
        put(cvd_ref, cv_ref[0, 0], cv_ref[0, 0])

    lane = lax.broadcasted_iota(jnp.int32, (tq, LANES), 1)
    left = lane < d
    qs = (q_ref[0].astype(F32) * (d ** -0.5)).astype(BF16)
    pairs = [qs[:, :LANES], qs[:, LANES:]]
    zero = jnp.zeros((tq, LANES), BF16)
    qz = jnp.concatenate([jnp.where(left, pairs[0], zero), jnp.where(left, pairs[1], zero),
                          jnp.where(left, zero, pairs[0]), jnp.where(left, zero, pairs[1])], axis=0)
    t4 = q0 + lax.rem(lax.broadcasted_iota(jnp.int32, (rows, 1), 0), tq)

    def rep4(x):
        return jnp.concatenate([x] * grp, axis=0)

    s = _dot_nt(qz, ckd_ref[...])
    cmp_end = CMP_STRIDE * lax.broadcasted_iota(jnp.int32, (1, n_cmp), 1) + (CMP_BLOCK - 1)
    cmask = cmp_end <= t4
    s = jnp.where(cmask, s, NEG_BIG)
    e = jnp.exp(s - jnp.max(s, axis=1, keepdims=True))
    p = e / jnp.sum(e, axis=1, keepdims=True)
    p = jnp.where(cmask, p, 0.0)
    o_cmp = _dot(p.astype(BF16), cvd_ref[...])

    pg = p[0:tq]
    for h in range(1, grp):
        pg = pg + p[h * tq:(h + 1) * tq]
    p_hi = pg.astype(BF16)
    r1 = pg - p_hi.astype(F32)
    p_mid = r1.astype(BF16)
    p_lo = (r1 - p_mid.astype(F32)).astype(BF16)
    ovt = ovt_ref[...]
    imp_t = _dot_nt(ovt, p_hi) + _dot_nt(ovt, p_mid) + _dot_nt(ovt, p_lo)
    blk = lax.broadcasted_iota(jnp.int32, (n_slc, tq), 0)
    cur = (q0 + lax.broadcasted_iota(jnp.int32, (n_slc, tq), 1)) // SLC_BLOCK
    forced = (blk == 0) | (blk == cur) | (blk == cur - 1)
    score = jnp.where(forced, jnp.inf, jnp.where(blk <= cur, imp_t, -jnp.inf))
    rank = jnp.zeros((n_slc, tq), F32)
    for j in range(n_slc):
        sj = score[j:j + 1, :]
        tie = jnp.where(blk > j, 1.0, 0.0)
        rank = rank + jnp.where(sj > score, 1.0, jnp.where(sj == score, tie, 0.0))
    sel_t = jnp.where(rank < top_n, 1.0, 0.0)
    if n_slc < d:
        sel_t = jnp.concatenate([sel_t, jnp.zeros((d - n_slc, tq), F32)], axis=0)
    sel_t2 = jnp.concatenate([sel_t, sel_t], axis=0).astype(BF16)
    eye = jnp.where(lax.broadcasted_iota(jnp.int32, (tq, tq), 0)
                    == lax.broadcasted_iota(jnp.int32, (tq, tq), 1), 1.0, 0.0).astype(BF16)
    sel2 = _dot_nt(eye, sel_t2)
    bias = ((sel2 - 1.0) * (-NEG_BIG)).astype(BF16)
    q_even = jnp.concatenate([jnp.where(left, pairs[0], bias), jnp.where(left, pairs[1], bias)], axis=0)
    q_odd = jnp.concatenate([jnp.where(left, bias, pairs[0]), jnp.where(left, bias, pairs[1])], axis=0)

    row = lax.broadcasted_iota(jnp.int32, (tq, tq), 0)
    col = lax.broadcasted_iota(jnp.int32, (tq, tq), 1)
    causal4 = rep4(col <= row)

    def attend(scores_fn, first, big, ve_ref, vo_ref):
        def weigh(p, start, width):
            return jnp.concatenate([_dot(p[:half], ve_ref[pl.ds(start, width), :]),
                                    _dot(p[half:], vo_ref[pl.ds(start, width), :])], axis=0)

        acc = _softmax_sweep(s_ref, rows=rows, acc_lanes=LANES, first=first, q0=pl.multiple_of(q0, tq), tq=tq,
                             big=big, scores_fn=scores_fn, weigh_fn=weigh)
        return acc / pltpu.roll(acc, d, 1)

    def slc_scores(start, width, diagonal):
        s = jnp.concatenate([_dot_nt(q_even, kse_ref[pl.ds(start, width), :]),
                             _dot_nt(q_odd, kso_ref[pl.ds(start, width), :])], axis=0) * LOG2_E
        return jnp.where(causal4, s, NEG_BIG) if diagonal else s

    o_slc = attend(slc_scores, 0, 4 * tq, vse_ref, vso_ref)

    def win_scores(start, width, diagonal):
        s = _dot_nt(qz, kwd_ref[pl.ds(start, width), :]) * LOG2_E
        if diagonal:
            keep = causal4
        else:
            keep = start + lax.broadcasted_iota(jnp.int32, (1, width), 1) > t4 - WINDOW
        return jnp.where(keep, s, NEG_BIG)

    o_win = attend(win_scores, jnp.maximum(q0 - (-(-WINDOW // tq)) * tq, 0), None, vwe_ref, vwo_ref)

    gate = 1.0 / (1.0 + jnp.exp(-g_ref[0]))

    def gated(h, r0):
        r = slice(r0, r0 + tq)
        return (gate[:, 3 * h:3 * h + 1] * o_cmp[r] + gate[:, 3 * h + 1:3 * h + 2] * o_slc[r]
                + gate[:, 3 * h + 2:3 * h + 3] * o_win[r])

    out = [jnp.where(left, gated(2 * pr, pr * tq), gated(2 * pr + 1, half + pr * tq)) for pr in range(2)]
    o_ref[0] = jnp.concatenate(out, axis=1).astype(o_ref.dtype)


def _nsa_attention(pa, pf, ckv, ovt, kv4, *, tq, q_blk, g_blk, top_n):
    b, t, _ = pa.shape
    n16 = ckv.shape[2]
    n_slc = ovt.shape[0]
    qw = NSA_GROUP * HEAD_DIM
    kv_spec = pl.BlockSpec((1, 1, t, HEAD_DIM), lambda bi, j, i: (bi, j, 0, 0))
    return pl.pallas_call(
        functools.partial(_nsa_kernel, tq=tq, top_n=top_n),
        grid=(b, NSA_KV_HEADS, t // tq),
        in_specs=[
            pl.BlockSpec((1, tq, qw), lambda bi, j, i: (bi, i, q_blk + j)),
            pl.BlockSpec((1, tq, LANES), lambda bi, j, i: (bi, i, g_blk + j)),
            pl.BlockSpec((1, 1, n16, HEAD_DIM), lambda bi, j, i: (bi, j, 0, 0)),
            pl.BlockSpec((1, 1, n16, HEAD_DIM), lambda bi, j, i: (bi, NSA_KV_HEADS + j, 0, 0)),
            pl.BlockSpec((n_slc, n16), lambda bi, j, i: (0, 0)),
            kv_spec, kv_spec, kv_spec, kv_spec,
        ],
        out_specs=pl.BlockSpec((1, tq, qw), lambda bi, j, i: (bi, i, j)),
        out_shape=jax.ShapeDtypeStruct((b, t, NSA_HEADS * HEAD_DIM), BF16),
        scratch_shapes=[pltpu.VMEM((t, LANES), BF16)] * 7 + [pltpu.VMEM((n16, LANES), BF16)] * 2
                       + [pltpu.VMEM((NSA_GROUP * tq, t), F32)],
        compiler_params=_params(("arbitrary", "arbitrary", "arbitrary")),
        name="nsa_attention",
    )(pa, pf, ckv, ckv, ovt, kv4[0], kv4[1], kv4[2], kv4[3])


def _conv_kernel(cb_ref, cc_ref, ch_ref, w_ref, o_ref, prev_ref):
    @pl.when(pl.program_id(1) == 0)
    def _():
        prev_ref[...] = jnp.zeros_like(prev_ref)

    u = cc_ref[0] * ch_ref[0]
    tt = u.shape[0]
    row = lax.broadcasted_iota(jnp.int32, u.shape, 0)
    last1 = prev_ref[7:8]
    last2 = prev_ref[6:7]
    u1 = jnp.where(row >= 1, pltpu.roll(u, 1, 0), last1)
    u2 = jnp.where(row >= 2, pltpu.roll(u, 2, 0), jnp.where(row == 1, last1, last2))
    w = w_ref[...]
    o_ref[0] = (cb_ref[0] * (w[0:1] * u2 + w[1:2] * u1 + w[2:3] * u)).astype(o_ref.dtype)
    prev_ref[...] = u[tt - 8:tt]


def _conv_mixer(pf, conv_w, *, tt):
    b, t, _ = pf.shape
    c = CONV_CH
    return pl.pallas_call(
        _conv_kernel,
        grid=(b, t // tt),
        in_specs=[
            pl.BlockSpec((1, tt, c), lambda bi, i: (bi, i, 0)),
            pl.BlockSpec((1, tt, c), lambda bi, i: (bi, i, 1)),
            pl.BlockSpec((1, tt, c), lambda bi, i: (bi, i, 2)),
            pl.BlockSpec((CONV_WIDTH, c), lambda bi, i: (0, 0)),
        ],
        out_specs=pl.BlockSpec((1, tt, c), lambda bi, i: (bi, i, 0)),
        out_shape=jax.ShapeDtypeStruct((b, t, c), BF16),
        scratch_shapes=[pltpu.VMEM((8, c), F32)],
        compiler_params=_params(("parallel", "arbitrary")),
        name="conv_mixer",
    )(pf, pf, pf, conv_w)


def _out_ln_kernel(a_ref, b_ref, c_ref, d_ref, w_ref, x_ref, g_ref, beta_ref, o_ref, ob_ref, *, alpha):
    kw = a_ref.shape[1]
    mix = _dot(a_ref[...], w_ref[0:kw])
    for n, r in enumerate((b_ref, c_ref, d_ref), start=1):
        mix = mix + _dot(r[...], w_ref[n * kw:(n + 1) * kw])
    o = _layer_norm(alpha * x_ref[...] + mix, g_ref[...], beta_ref[...])
    o_ref[...] = o
    ob_ref[...] = o.astype(BF16)


def _out_ln(parts, w, x, g, b, *, layer, alpha, tm):
    n, d = x.shape
    kw = parts[0].shape[1]
    part_spec = pl.BlockSpec((tm, kw), lambda i: (i, 0))
    return pl.pallas_call(
        functools.partial(_out_ln_kernel, alpha=alpha),
        grid=(n // tm,),
        in_specs=[part_spec, part_spec, part_spec, part_spec,
                  pl.BlockSpec((None,) + w.shape[1:], lambda i: (layer, 0, 0)),
                  pl.BlockSpec((tm, d), lambda i: (i, 0)),
                  pl.BlockSpec((1, d), lambda i: (0, 0)),
                  pl.BlockSpec((1, d), lambda i: (0, 0))],
        out_specs=[pl.BlockSpec((tm, d), lambda i: (i, 0)), pl.BlockSpec((tm, d), lambda i: (i, 0))],
        out_shape=[jax.ShapeDtypeStruct((n, d), F32), jax.ShapeDtypeStruct((n, d), BF16)],
        compiler_params=_params(("parallel",)),
        name="out_ln",
    )(*parts, w, x, g.reshape(1, d), b.reshape(1, d))


def _overlap_t(t):
    n16 = t // CMP_STRIDE
    n_slc = t // SLC_BLOCK
    c_start = CMP_STRIDE * np.arange(n16)
    j_start = SLC_BLOCK * np.arange(n_slc)
    ov = ((c_start[None, :] < j_start[:, None] + SLC_BLOCK)
          & (c_start[None, :] + CMP_BLOCK > j_start[:, None])).astype(np.float32)
    ov[:, n16 - 1] = 0.0
    return jnp.asarray(ov, BF16)


def _mixer(hf, hb, batch, w_att, w_f32, w_out, ln_g, ln_b, diff_lam, diff_gain, pos2, cmp_w1, cmp_w2,
           conv_w, layer, alpha):
    n, _ = hf.shape
    t = n // batch
    pa = _matmul(hb, w_att, BF16, layer=layer, tm=1024, tn=512, name="proj_att").reshape(batch, t, -1)
    pf = _matmul(hb, w_f32, F32, layer=layer, tm=1024, tn=512, name="proj_f32").reshape(batch, t, -1)

    o_sb = _sb_attention(pa, tq=256, q_blk=0, k_blk=4, v_blk=8)
    lam_init = 0.8 - 0.6 * math.exp(-0.3 * layer)
    o_df = _diff_attention(pa, diff_lam[layer], diff_gain[layer], tq=512, q_blk=12, k_blk=16, v_blk=20,
                           lam_init=lam_init)

    n16 = t // CMP_STRIDE
    kvc = pf[:, :, 3 * CONV_CH:3 * CONV_CH + 2 * LANES].reshape(batch, t, 2, NSA_KV_HEADS, HEAD_DIM)
    x16 = kvc.transpose(0, 2, 3, 1, 4).reshape(batch, 2 * NSA_KV_HEADS, n16, CMP_STRIDE * HEAD_DIM)
    ckv = _compress(x16, pos2, cmp_w1, cmp_w2, layer=layer)
    kv4 = pa[:, :, 28 * LANES:32 * LANES].reshape(batch, t, 4, NSA_KV_HEADS, HEAD_DIM).transpose(2, 0, 3, 1, 4)
    n_slc = t // SLC_BLOCK
    o_ns = _nsa_attention(pa, pf, ckv, _overlap_t(t), kv4, tq=256, q_blk=12, g_blk=14,
                          top_n=min(SLC_TOPN, n_slc))

    o_cv = _conv_mixer(pf, conv_w[layer], tt=512)
    parts = [o.reshape(n, -1) for o in (o_sb, o_df, o_ns, o_cv)]
    return _out_ln(parts, w_out, hf, ln_g, ln_b, layer=layer, alpha=alpha, tm=512)


def _seg(w_in, name, lo=0, hi=None):
    off = _SEG_OFF[name]
    hi = _SEG_W[name] if hi is None else hi
    return w_in[:, :, off + lo:off + hi]


def kernel(x, ln_g, ln_b, ffn_w1, ffn_w3, ffn_w2, w_in, w_out, diff_lam, diff_gain, cmp_pos, cmp_wk1,
           cmp_wk2, cmp_wv1, cmp_wv2, conv_w):
    batch, t, d = x.shape
    depth = ln_g.shape[0]
    alpha = (2 * depth) ** 0.25
    n = batch * t

    w1b, w3b, w2b = ffn_w1.astype(BF16), ffn_w3.astype(BF16), ffn_w2.astype(BF16)
    w_att = jnp.concatenate([w_in[:, :, :_SEG_OFF["ns_kc"]], w_in[:, :, _SEG_OFF["ns_ks"]:_SEG_OFF["ns_g"]]],
                            axis=-1).astype(BF16)
    gate_pad = jnp.zeros(w_in.shape[:2] + (LANES - _GATES_PER_KV,), w_in.dtype)
    f32_cols = [_seg(w_in, "cv_b"), _seg(w_in, "cv_c"), _seg(w_in, "cv_h"), _seg(w_in, "ns_kc"), _seg(w_in, "ns_vc")]
    for j in range(NSA_KV_HEADS):
        f32_cols += [_seg(w_in, "ns_g", j * _GATES_PER_KV, (j + 1) * _GATES_PER_KV), gate_pad]
    w_f32 = jnp.concatenate(f32_cols, axis=-1).astype(BF16)
    w_outb = w_out.astype(BF16)
    half = CMP_STRIDE * HEAD_DIM
    pos2 = cmp_pos.reshape(depth, 2, half)
    cmp_w1 = jnp.stack([cmp_wk1, cmp_wv1], axis=1).astype(BF16)
    cmp_w2 = jnp.stack([cmp_wk2, cmp_wv2], axis=1).astype(BF16)

    hf = x.reshape(n, d)
    for l in range(depth):
        hf, hb = _ffn_ln(hf, w1b, w3b, w2b, ln_g[l, 0], ln_b[l, 0], layer=l, slot=0, alpha=alpha, tm=512, tf=512)
        hf, _ = _mixer(hf, hb, batch, w_att, w_f32, w_outb, ln_g[l, 1], ln_b[l, 1], diff_lam, diff_gain,
                       pos2, cmp_w1, cmp_w2, conv_w, l, alpha)
        hf, _ = _ffn_ln(hf, w1b, w3b, w2b, ln_g[l, 2], ln_b[l, 2], layer=l, slot=1, alpha=alpha, tm=512, tf=512)
    return hf.reshape(batch, t, d)
```

```python
import functools
import math

import numpy as np
import jax
import jax.numpy as jnp
from jax import lax
from jax.experimental import pallas as pl
from jax.experimental.pallas import tpu as pltpu

F32 = jnp.float32
BF16 = jnp.bfloat16

HEAD_DIM = 64
SB_HEADS = 8
DIFF_HEADS = 4
NSA_HEADS = 8
NSA_KV_HEADS = 2
NSA_GROUP = NSA_HEADS // NSA_KV_HEADS
CMP_BLOCK = 32
CMP_STRIDE = 16
SLC_BLOCK = 64
SLC_TOPN = 16
WINDOW = 512
N_BRANCH = 3
CONV_CH = 512
CONV_WIDTH = 3
LN_EPS = 1e-5
RMS_EPS = 1e-5
NEG_BIG = -1e30
LOG2_E = math.log2(math.e)
EXP_UNDERFLOW = -104.0
LANES = 128
VMEM_LIMIT = 56 * 1024 * 1024
FFN_TILE = 512

_SEG_NAMES = ("sb_q", "sb_k", "sb_v", "df_q", "df_k", "df_v", "ns_q", "ns_kc", "ns_vc",
              "ns_ks", "ns_vs", "ns_kw", "ns_vw", "ns_g", "cv_b", "cv_c", "cv_h")
_SEG_WIDTHS = (512, 512, 512, 512, 512, 512, 512, 128, 128, 128, 128, 128, 128,
               NSA_HEADS * N_BRANCH, 512, 512, 512)
_SEG_OFF = dict(zip(_SEG_NAMES, np.cumsum((0,) + _SEG_WIDTHS[:-1]).tolist()))
_SEG_W = dict(zip(_SEG_NAMES, _SEG_WIDTHS))


_GATES_PER_KV = NSA_GROUP * N_BRANCH


def _params(sem):
    return pltpu.CompilerParams(dimension_semantics=sem, vmem_limit_bytes=VMEM_LIMIT)


def _layer_norm(y, g, b):
    mu = jnp.mean(y, axis=-1, keepdims=True)
    d = y - mu
    var = jnp.mean(d * d, axis=-1, keepdims=True)
    return d * lax.rsqrt(var + LN_EPS) * g + b


def _dot(a, b):
    return jnp.dot(a, b, preferred_element_type=F32)


def _lane_group_max(s, mx):
    for g in range(s.shape[1] // LANES):
        mx = jnp.maximum(mx, s[:, g * LANES:(g + 1) * LANES])
    return mx


def _softmax_sweep(s_ref, *, rows, acc_lanes, first, q0, tq, big, scores_fn, weigh_fn):
    n_big = (q0 - first) // big if big else 0
    mid = first + n_big * big if big else first
    n_small = (q0 - mid) // tq

    def chunks(n, base, width, fn, init):
        return lax.fori_loop(0, n, lambda c, carry: fn(pl.multiple_of(base + c * width, tq), width, carry), init)

    def scores(start, width, mx, diagonal=False):
        s = scores_fn(start, width, diagonal)
        s_ref[:, pl.ds(start, width)] = s
        return _lane_group_max(s, mx)

    mx = jnp.full((rows, LANES), NEG_BIG, F32)
    if big:
        mx = chunks(n_big, first, big, scores, mx)
    mx = chunks(n_small, mid, tq, scores, mx)
    mx = scores(q0, tq, mx, True)
    m = jnp.broadcast_to(jnp.max(mx, axis=1, keepdims=True), (rows, LANES))

    def weigh(start, width, acc):
        p = jnp.exp2(s_ref[:, pl.ds(start, width)] - jnp.concatenate([m] * (width // LANES), axis=1))
        return acc + weigh_fn(p.astype(BF16), start, width)

    acc = jnp.zeros((rows, acc_lanes), F32)
    if big:
        acc = chunks(n_big, first, big, weigh, acc)
    acc = chunks(n_small, mid, tq, weigh, acc)
    return weigh(q0, tq, acc)


def _dot_nt(a, b):
    return lax.dot_general(a, b, (((1,), (1,)), ((), ())), preferred_element_type=F32)


def _ffn_ln_kernel(x_ref, w13_ref, w2_ref, g_ref, b_ref, o_ref, ob_ref, xb_ref, acc_ref, *, alpha):
    j = pl.program_id(1)

    @pl.when(j == 0)
    def _():
        xb_ref[...] = x_ref[...].astype(BF16)
        acc_ref[...] = jnp.zeros_like(acc_ref)

    tf = w2_ref.shape[0]
    ab = _dot(xb_ref[...], w13_ref[...])
    a, b = ab[:, :tf], ab[:, tf:]
    hm = (a / (1.0 + jnp.exp(-a))) * b
    acc_ref[...] += _dot(hm.astype(BF16), w2_ref[...])

    @pl.when(j == pl.num_programs(1) - 1)
    def _():
        y = alpha * x_ref[...] + 0.5 * acc_ref[...]
        o = _layer_norm(y, g_ref[...], b_ref[...])
        o_ref[...] = o
        ob_ref[...] = o.astype(BF16)


def _interleave_tiles(w1, w3, tf):
    lead, f = w1.shape[:-1], w1.shape[-1]
    both = jnp.stack([w1.reshape(lead + (f // tf, tf)), w3.reshape(lead + (f // tf, tf))], axis=-2)
    return both.reshape(lead + (2 * f,))


def _ffn_ln(x, w13, w2, g, b, *, layer, slot, alpha, tm, tf):
    n, d = x.shape
    f = w2.shape[-2]
    return pl.pallas_call(
        functools.partial(_ffn_ln_kernel, alpha=alpha),
        grid=(n // tm, f // tf),
        in_specs=[
            pl.BlockSpec((tm, d), lambda i, j: (i, 0)),
            pl.BlockSpec((None, None, d, 2 * tf), lambda i, j: (layer, slot, 0, j)),
            pl.BlockSpec((None, None, tf, d), lambda i, j: (layer, slot, j, 0)),
            pl.BlockSpec((1, d), lambda i, j: (0, 0)),
            pl.BlockSpec((1, d), lambda i, j: (0, 0)),
        ],
        out_specs=[pl.BlockSpec((tm, d), lambda i, j: (i, 0)),
                   pl.BlockSpec((tm, d), lambda i, j: (i, 0))],
        out_shape=[jax.ShapeDtypeStruct((n, d), F32), jax.ShapeDtypeStruct((n, d), BF16)],
        scratch_shapes=[pltpu.VMEM((tm, d), BF16), pltpu.VMEM((tm, d), F32)],
        compiler_params=_params(("parallel", "arbitrary")),
        name="ffn_ln",
    )(x, w13, w2, g.reshape(1, d), b.reshape(1, d))


def _matmul_kernel(x_ref, w_ref, o_ref):
    o_ref[...] = _dot(x_ref[...], w_ref[...]).astype(o_ref.dtype)


def _matmul(x, w, out_dtype, *, layer, tm, tn, name):
    n, d = x.shape
    m = w.shape[-1]
    return pl.pallas_call(
        _matmul_kernel,
        grid=(n // tm, m // tn),
        in_specs=[pl.BlockSpec((tm, d), lambda i, j: (i, 0)),
                  pl.BlockSpec((None, d, tn), lambda i, j: (layer, 0, j))],
        out_specs=pl.BlockSpec((tm, tn), lambda i, j: (i, j)),
        out_shape=jax.ShapeDtypeStruct((n, m), out_dtype),
        compiler_params=_params(("parallel", "arbitrary")),
        name=name,
    )(x, w)


def _sb_kernel(q_ref, k_ref, v_ref, o_ref, *, tq):
    i = pl.program_id(2)
    lane = lax.broadcasted_iota(jnp.int32, (tq, LANES), 1)
    row = lax.broadcasted_iota(jnp.int32, (tq, tq), 0)
    col = lax.broadcasted_iota(jnp.int32, (tq, tq), 1)
    strictly_causal = col < row
    causal2 = jnp.concatenate([strictly_causal, strictly_causal], axis=0)
    tri = jnp.where(row >= col, 1.0, 0.0).astype(BF16)
    qs = (q_ref[0].astype(F32) * (HEAD_DIM ** -0.5)).astype(BF16)
    zero = jnp.zeros_like(qs)
    q2 = jnp.concatenate([jnp.where(lane < HEAD_DIM, qs, zero),
                          jnp.where(lane < HEAD_DIM, zero, qs)], axis=0)

    def tile(kt, c, acc, diagonal):
        start = pl.multiple_of(kt * tq, tq)
        ks = k_ref[0, pl.ds(start, tq), :]
        vs = v_ref[0, pl.ds(start, tq), :]
        z = _dot_nt(q2, ks)
        lg = -(jnp.maximum(z, 0.0) + jnp.log1p(jnp.exp(-jnp.abs(z))))
        if diagonal:
            lg = jnp.where(causal2, lg, 0.0)
        hi = lg.astype(BF16)
        lo = (lg - hi.astype(F32)).astype(BF16)
        suffix = _dot(hi, tri) + _dot(lo, tri)
        a = jnp.exp(z + suffix + c)
        if diagonal:
            a = jnp.where(causal2, a, 0.0)
        return c + suffix[:, 0:1], acc + _dot(a.astype(BF16), vs)

    c1, acc1 = tile(i, jnp.zeros((2 * tq, 1), F32), jnp.zeros((2 * tq, LANES), F32), True)

    def live(c):
        return (jnp.max(c) > EXP_UNDERFLOW).astype(jnp.int32)

    def cond(carry):
        return (carry[0] <= i) & (carry[1] > 0)

    def body(carry):
        n, _, c, acc = carry
        c, acc = tile(i - n, c, acc, False)
        return n + 1, live(c), c, acc

    _, _, _, acc = lax.while_loop(cond, body, (jnp.int32(1), live(c1), c1, acc1))
    o_ref[0] = jnp.where(lane < HEAD_DIM, acc[:tq], acc[tq:]).astype(o_ref.dtype)


def _sb_attention(pa, *, tq, q_blk, k_blk, v_blk):
    b, t, _ = pa.shape
    n_pairs = SB_HEADS // 2
    return pl.pallas_call(
        functools.partial(_sb_kernel, tq=tq),
        grid=(b, n_pairs, t // tq),
        in_specs=[
            pl.BlockSpec((1, tq, LANES), lambda bi, p, i: (bi, i, q_blk + p)),
            pl.BlockSpec((1, t, LANES), lambda bi, p, i: (bi, 0, k_blk + p)),
            pl.BlockSpec((1, t, LANES), lambda bi, p, i: (bi, 0, v_blk + p)),
        ],
        out_specs=pl.BlockSpec((1, tq, LANES), lambda bi, p, i: (bi, i, p)),
        out_shape=jax.ShapeDtypeStruct((b, t, n_pairs * LANES), BF16),
        compiler_params=_params(("parallel", "parallel", "arbitrary")),
        name="sb_attention",
    )(pa, pa, pa)


def _diff_kernel(lam_ref, gain_ref, q_ref, k_ref, v_ref, o_ref, s_ref, *, tq, lam_init):
    i = pl.program_id(2)
    lane = lax.broadcasted_iota(jnp.int32, (tq, LANES), 1)
    row = lax.broadcasted_iota(jnp.int32, (tq, tq), 0)
    col = lax.broadcasted_iota(jnp.int32, (tq, tq), 1)
    causal = col <= row
    causal2 = jnp.concatenate([causal, causal], axis=0)
    qs = (q_ref[0].astype(F32) * (HEAD_DIM ** -0.5)).astype(BF16)
    zero = jnp.zeros_like(qs)
    q2 = jnp.concatenate([jnp.where(lane < HEAD_DIM, qs, zero),
                          jnp.where(lane < HEAD_DIM, zero, qs)], axis=0)

    def scores(start, width, diagonal):
        s = _dot_nt(q2, k_ref[0, pl.ds(start, width), :]) * LOG2_E
        return jnp.where(causal2, s, NEG_BIG) if diagonal else s

    def weigh(p, start, width):
        return _dot(p, jnp.concatenate([v_ref[0, pl.ds(start, width), :], jnp.ones((width, LANES), BF16)], axis=1))

    acc = _softmax_sweep(s_ref, rows=2 * tq, acc_lanes=2 * LANES, first=0, q0=pl.multiple_of(i * tq, tq), tq=tq,
                         big=2 * tq, scores_fn=scores, weigh_fn=weigh)
    o = acc[:, :LANES] / acc[:, LANES:]
    lp = lam_ref[...]
    lam = (jnp.exp(jnp.sum(lp[0:1] * lp[1:2], axis=1, keepdims=True))
           - jnp.exp(jnp.sum(lp[2:3] * lp[3:4], axis=1, keepdims=True)) + lam_init)
    od = o[:tq] - lam * o[tq:]
    od = od * lax.rsqrt(jnp.mean(od * od, axis=-1, keepdims=True) + RMS_EPS)
    o_ref[0] = (od * gain_ref[...] * (1.0 - lam_init)).astype(o_ref.dtype)


def _diff_attention(pa, lam_params, gain, *, tq, q_blk, k_blk, v_blk, lam_init):
    b, t, _ = pa.shape
    return pl.pallas_call(
        functools.partial(_diff_kernel, tq=tq, lam_init=lam_init),
        grid=(b, DIFF_HEADS, t // tq),
        in_specs=[
            pl.BlockSpec((4, HEAD_DIM), lambda bi, h, i: (0, 0)),
            pl.BlockSpec((1, LANES), lambda bi, h, i: (0, 0)),
            pl.BlockSpec((1, tq, LANES), lambda bi, h, i: (bi, i, q_blk + h)),
            pl.BlockSpec((1, t, LANES), lambda bi, h, i: (bi, 0, k_blk + h)),
            pl.BlockSpec((1, t, LANES), lambda bi, h, i: (bi, 0, v_blk + h)),
        ],
        out_specs=pl.BlockSpec((1, tq, LANES), lambda bi, h, i: (bi, i, h)),
        out_shape=jax.ShapeDtypeStruct((b, t, DIFF_HEADS * LANES), BF16),
        scratch_shapes=[pltpu.VMEM((2 * tq, t), F32)],
        compiler_params=_params(("parallel", "parallel", "arbitrary")),
        name="diff_attention",
    )(lam_params, gain.reshape(1, LANES), pa, pa, pa)


def _compress_kernel(x_ref, pos_ref, w1_ref, w2_ref, o_ref):
    x = x_ref[0, 0]
    n16, half = x.shape
    xa = (x + pos_ref[0:1]).astype(BF16)
    xb = (x + pos_ref[1:2]).astype(BF16)
    w1 = w1_ref[0]
    first = _dot(xa, w1[:half])
    second = _dot(xb, w1[half:])
    h = first + pltpu.roll(second, n16 - 1, 0)
    g = 0.5 * h * (1.0 + jnp.tanh(math.sqrt(2.0 / math.pi) * (h + 0.044715 * (h * h * h))))
    o_ref[0, 0] = _dot(g.astype(BF16), w2_ref[0])


def _compress(x16, pos2, w1, w2, *, layer):
    b, n_streams, n16, half = x16.shape
    d = w2.shape[-1]
    return pl.pallas_call(
        _compress_kernel,
        grid=(b, n_streams),
        in_specs=[
            pl.BlockSpec((1, 1, n16, half), lambda bi, s: (bi, s, 0, 0)),
            pl.BlockSpec((None, 2, half), lambda bi, s: (layer, 0, 0)),
            pl.BlockSpec((None, 1, 2 * half, w1.shape[-1]), lambda bi, s: (layer, s // NSA_KV_HEADS, 0, 0)),
            pl.BlockSpec((None, 1, w2.shape[2], d), lambda bi, s: (layer, s // NSA_KV_HEADS, 0, 0)),
        ],
        out_specs=pl.BlockSpec((1, 1, n16, d), lambda bi, s: (bi, s, 0, 0)),
        out_shape=jax.ShapeDtypeStruct((b, n_streams, n16, d), F32),
        compiler_params=_params(("parallel", "arbitrary")),
        name="nsa_compress",
    )(x16, pos2, w1, w2)


def _nsa_kernel(q_ref, g_ref, ck_ref, cv_ref, ovt_ref, gx_ref, ks_ref, vs_ref, kw_ref, vw_ref, o_ref,
                kse_ref, kso_ref, vse_ref, vso_ref, kwd_ref, vwe_ref, vwo_ref, ckd_ref, cvd_ref, s_ref,
                *, tq, top_n):
    i = pl.program_id(2)
    grp = NSA_GROUP
    d = HEAD_DIM
    q0 = i * tq
    n_cmp = ck_ref.shape[2]
    n_slc = ovt_ref.shape[0]
    t = ks_ref.shape[2]
    rows = grp * tq
    half = rows // 2

    @pl.when(i == 0)
    def _():
        onehot = jnp.where(lax.broadcasted_iota(jnp.int32, (t, d), 0) // SLC_BLOCK
                           == lax.broadcasted_iota(jnp.int32, (t, d), 1), 1.0, 0.0)
        ones = jnp.ones((t, d), F32)

        def put(ref, left, right):
            ref[...] = jnp.concatenate([left, right], axis=1).astype(BF16)

        ks, vs = ks_ref[0, 0].astype(F32), vs_ref[0, 0].astype(F32)
        kw, vw = kw_ref[0, 0].astype(F32), vw_ref[0, 0].astype(F32)
        put(kse_ref, ks, onehot)
        put(kso_ref, onehot, ks)
        put(vse_ref, vs, ones)
        put(vso_ref, ones, vs)
        put(kwd_ref, kw, kw)
        put(vwe_ref, vw, ones)
        put(vwo_ref, ones, vw)
        put(ckd_ref, ck_ref[0, 0], ck_ref[0, 0])
        put(cvd_ref, cv_ref[0, 0], cv_ref[0, 0])

    lane = lax.broadcasted_iota(jnp.int32, (tq, LANES), 1)
    left = lane < d
    qs = (q_ref[0].astype(F32) * (d ** -0.5)).astype(BF16)
    pairs = [qs[:, :LANES], qs[:, LANES:]]
    zero = jnp.zeros((tq, LANES), BF16)
    qz = jnp.concatenate([jnp.where(left, pairs[0], zero), jnp.where(left, pairs[1], zero),
                          jnp.where(left, zero, pairs[0]), jnp.where(left, zero, pairs[1])], axis=0)
    t4 = q0 + lax.rem(lax.broadcasted_iota(jnp.int32, (rows, 1), 0), tq)

    def rep4(x):
        return jnp.concatenate([x] * grp, axis=0)

    s = _dot_nt(qz, ckd_ref[...])
    cmp_end = CMP_STRIDE * lax.broadcasted_iota(jnp.int32, (1, n_cmp), 1) + (CMP_BLOCK - 1)
    cmask = cmp_end <= t4
    s = jnp.where(cmask, s, NEG_BIG)
    e = jnp.exp(s - jnp.max(s, axis=1, keepdims=True))
    p = e / jnp.sum(e, axis=1, keepdims=True)
    p = jnp.where(cmask, p, 0.0)
    o_cmp = _dot(p.astype(BF16), cvd_ref[...])

    pg = p[0:tq]
    for h in range(1, grp):
        pg = pg + p[h * tq:(h + 1) * tq]
    p_hi = pg.astype(BF16)
    r1 = pg - p_hi.astype(F32)
    p_mid = r1.astype(BF16)
    p_lo = (r1 - p_mid.astype(F32)).astype(BF16)
    ovt = ovt_ref[...]
    imp_t = _dot_nt(ovt, p_hi) + _dot_nt(ovt, p_mid) + _dot_nt(ovt, p_lo)
    blk = lax.broadcasted_iota(jnp.int32, (n_slc, tq), 0)
    cur = (q0 + lax.broadcasted_iota(jnp.int32, (n_slc, tq), 1)) // SLC_BLOCK
    forced = (blk == 0) | (blk == cur) | (blk == cur - 1)
    score = jnp.where(forced, jnp.inf, jnp.where(blk <= cur, imp_t, -jnp.inf))
    rank = jnp.zeros((n_slc, tq), F32)
    for j in range(n_slc):
        sj = score[j:j + 1, :]
        tie = jnp.where(blk > j, 1.0, 0.0)
        rank = rank + jnp.where(sj > score, 1.0, jnp.where(sj == score, tie, 0.0))
    sel_t = jnp.where(rank < top_n, 1.0, 0.0)
    if n_slc < d:
        sel_t = jnp.concatenate([sel_t, jnp.zeros((d - n_slc, tq), F32)], axis=0)
    sel_t2 = jnp.concatenate([sel_t, sel_t], axis=0).astype(BF16)
    eye = jnp.where(lax.broadcasted_iota(jnp.int32, (tq, tq), 0)
                    == lax.broadcasted_iota(jnp.int32, (tq, tq), 1), 1.0, 0.0).astype(BF16)
    sel2 = _dot_nt(eye, sel_t2)
    bias = ((sel2 - 1.0) * (-NEG_BIG)).astype(BF16)
    q_even = jnp.concatenate([jnp.where(left, pairs[0], bias), jnp.where(left, pairs[1], bias)], axis=0)
    q_odd = jnp.concatenate([jnp.where(left, bias, pairs[0]), jnp.where(left, bias, pairs[1])], axis=0)

    row = lax.broadcasted_iota(jnp.int32, (tq, tq), 0)
    col = lax.broadcasted_iota(jnp.int32, (tq, tq), 1)
    causal4 = rep4(col <= row)

    def attend(scores_fn, first, big, ve_ref, vo_ref):
        def weigh(p, start, width):
            return jnp.concatenate([_dot(p[:half], ve_ref[pl.ds(start, width), :]),
                                    _dot(p[half:], vo_ref[pl.ds(start, width), :])], axis=0)

        acc = _softmax_sweep(s_ref, rows=rows, acc_lanes=LANES, first=first, q0=pl.multiple_of(q0, tq), tq=tq,
                             big=big, scores_fn=scores_fn, weigh_fn=weigh)
        out = []
        for pr in range(2):
            even, odd = acc[pr * tq:(pr + 1) * tq], acc[half + pr * tq:half + (pr + 1) * tq]
            den = pltpu.roll(jnp.where(left, odd, even), d, 1)
            out.append(jnp.where(left, even, odd) / den)
        return out

    def slc_scores(start, width, diagonal):
        s = jnp.concatenate([_dot_nt(q_even, kse_ref[pl.ds(start, width), :]),
                             _dot_nt(q_odd, kso_ref[pl.ds(start, width), :])], axis=0) * LOG2_E
        return jnp.where(causal4, s, NEG_BIG) if diagonal else s

    o_slc = attend(slc_scores, 0, 4 * tq, vse_ref, vso_ref)

    def win_scores(start, width, diagonal):
        s = _dot_nt(qz, kwd_ref[pl.ds(start, width), :]) * LOG2_E
        if diagonal:
            keep = causal4
        else:
            keep = start + lax.broadcasted_iota(jnp.int32, (1, width), 1) > t4 - WINDOW
        return jnp.where(keep, s, NEG_BIG)

    o_win = attend(win_scores, jnp.maximum(q0 - (-(-WINDOW // tq)) * tq, 0), None, vwe_ref, vwo_ref)

    gate = 1.0 / (1.0 + jnp.exp(-g_ref[0]))
    g_hi = gate.astype(BF16)
    g_r = gate - g_hi.astype(F32)
    g_mid = g_r.astype(BF16)
    g_lo = (g_r - g_mid.astype(F32)).astype(BF16)
    spread = gx_ref[...]
    gx = _dot(g_hi, spread) + _dot(g_mid, spread) + _dot(g_lo, spread)

    out = []
    for pr in range(2):
        cmp_pr = jnp.where(left, o_cmp[pr * tq:(pr + 1) * tq], o_cmp[half + pr * tq:half + (pr + 1) * tq])
        mixed = 0.0
        for br, o_br in enumerate((cmp_pr, o_slc[pr], o_win[pr])):
            blk_idx = pr * N_BRANCH + br
            mixed = mixed + gx[:, blk_idx * LANES:(blk_idx + 1) * LANES] * o_br
        out.append(mixed)
    o_ref[0] = jnp.concatenate(out, axis=1).astype(o_ref.dtype)


def _nsa_attention(pa, pf, ckv, ovt, kv4, *, tq, q_blk, g_blk, top_n):
    b, t, _ = pa.shape
    n16 = ckv.shape[2]
    n_slc = ovt.shape[0]
    qw = NSA_GROUP * HEAD_DIM
    kv_spec = pl.BlockSpec((1, 1, t, HEAD_DIM), lambda bi, j, i: (bi, j, 0, 0))
    return pl.pallas_call(
        functools.partial(_nsa_kernel, tq=tq, top_n=top_n),
        grid=(b, NSA_KV_HEADS, t // tq),
        in_specs=[
            pl.BlockSpec((1, tq, qw), lambda bi, j, i: (bi, i, q_blk + j)),
            pl.BlockSpec((1, tq, LANES), lambda bi, j, i: (bi, i, g_blk + j)),
            pl.BlockSpec((1, 1, n16, HEAD_DIM), lambda bi, j, i: (bi, j, 0, 0)),
            pl.BlockSpec((1, 1, n16, HEAD_DIM), lambda bi, j, i: (bi, NSA_KV_HEADS + j, 0, 0)),
            pl.BlockSpec((n_slc, n16), lambda bi, j, i: (0, 0)),
            pl.BlockSpec((LANES, 2 * N_BRANCH * LANES), lambda bi, j, i: (0, 0)),
            kv_spec, kv_spec, kv_spec, kv_spec,
        ],
        out_specs=pl.BlockSpec((1, tq, qw), lambda bi, j, i: (bi, i, j)),
        out_shape=jax.ShapeDtypeStruct((b, t, NSA_HEADS * HEAD_DIM), BF16),
        scratch_shapes=[pltpu.VMEM((t, LANES), BF16)] * 7 + [pltpu.VMEM((n16, LANES), BF16)] * 2
                       + [pltpu.VMEM((NSA_GROUP * tq, t), F32)],
        compiler_params=_params(("arbitrary", "arbitrary", "arbitrary")),
        name="nsa_attention",
    )(pa, pf, ckv, ckv, ovt, _gate_spread(), kv4[0], kv4[1], kv4[2], kv4[3])


def _conv_kernel(cb_ref, cc_ref, ch_ref, w_ref, o_ref, prev_ref):
    @pl.when(pl.program_id(1) == 0)
    def _():
        prev_ref[...] = jnp.zeros_like(prev_ref)

    u = cc_ref[0] * ch_ref[0]
    tt = u.shape[0]
    row = lax.broadcasted_iota(jnp.int32, u.shape, 0)
    last1 = prev_ref[7:8]
    last2 = prev_ref[6:7]
    u1 = jnp.where(row >= 1, pltpu.roll(u, 1, 0), last1)
    u2 = jnp.where(row >= 2, pltpu.roll(u, 2, 0), jnp.where(row == 1, last1, last2))
    w = w_ref[...]
    o_ref[0] = (cb_ref[0] * (w[0:1] * u2 + w[1:2] * u1 + w[2:3] * u)).astype(o_ref.dtype)
    prev_ref[...] = u[tt - 8:tt]


def _conv_mixer(pf, conv_w, *, tt):
    b, t, _ = pf.shape
    c = CONV_CH
    return pl.pallas_call(
        _conv_kernel,
        grid=(b, t // tt),
        in_specs=[
            pl.BlockSpec((1, tt, c), lambda bi, i: (bi, i, 0)),
            pl.BlockSpec((1, tt, c), lambda bi, i: (bi, i, 1)),
            pl.BlockSpec((1, tt, c), lambda bi, i: (bi, i, 2)),
            pl.BlockSpec((CONV_WIDTH, c), lambda bi, i: (0, 0)),
        ],
        out_specs=pl.BlockSpec((1, tt, c), lambda bi, i: (bi, i, 0)),
        out_shape=jax.ShapeDtypeStruct((b, t, c), BF16),
        scratch_shapes=[pltpu.VMEM((8, c), F32)],
        compiler_params=_params(("parallel", "arbitrary")),
        name="conv_mixer",
    )(pf, pf, pf, conv_w)


def _out_ln_kernel(a_ref, b_ref, c_ref, d_ref, w_ref, x_ref, g_ref, beta_ref, o_ref, ob_ref, *, alpha):
    kw = a_ref.shape[1]
    mix = _dot(a_ref[...], w_ref[0:kw])
    for n, r in enumerate((b_ref, c_ref, d_ref), start=1):
        mix = mix + _dot(r[...], w_ref[n * kw:(n + 1) * kw])
    o = _layer_norm(alpha * x_ref[...] + mix, g_ref[...], beta_ref[...])
    o_ref[...] = o
    ob_ref[...] = o.astype(BF16)


def _out_ln(parts, w, x, g, b, *, layer, alpha, tm):
    n, d = x.shape
    kw = parts[0].shape[1]
    part_spec = pl.BlockSpec((tm, kw), lambda i: (i, 0))
    return pl.pallas_call(
        functools.partial(_out_ln_kernel, alpha=alpha),
        grid=(n // tm,),
        in_specs=[part_spec, part_spec, part_spec, part_spec,
                  pl.BlockSpec((None,) + w.shape[1:], lambda i: (layer, 0, 0)),
                  pl.BlockSpec((tm, d), lambda i: (i, 0)),
                  pl.BlockSpec((1, d), lambda i: (0, 0)),
                  pl.BlockSpec((1, d), lambda i: (0, 0))],
        out_specs=[pl.BlockSpec((tm, d), lambda i: (i, 0)), pl.BlockSpec((tm, d), lambda i: (i, 0))],
        out_shape=[jax.ShapeDtypeStruct((n, d), F32), jax.ShapeDtypeStruct((n, d), BF16)],
        compiler_params=_params(("parallel",)),
        name="out_ln",
    )(*parts, w, x, g.reshape(1, d), b.reshape(1, d))


def _gate_spread():
    m = np.zeros((LANES, 2 * N_BRANCH * LANES), np.float32)
    for pr in range(2):
        for br in range(N_BRANCH):
            for lane in range(LANES):
                head = 2 * pr + lane // HEAD_DIM
                m[N_BRANCH * head + br, (pr * N_BRANCH + br) * LANES + lane] = 1.0
    return jnp.asarray(m, BF16)


def _overlap_t(t):
    n16 = t // CMP_STRIDE
    n_slc = t // SLC_BLOCK
    c_start = CMP_STRIDE * np.arange(n16)
    j_start = SLC_BLOCK * np.arange(n_slc)
    ov = ((c_start[None, :] < j_start[:, None] + SLC_BLOCK)
          & (c_start[None, :] + CMP_BLOCK > j_start[:, None])).astype(np.float32)
    ov[:, n16 - 1] = 0.0
    return jnp.asarray(ov, BF16)


def _mixer(hf, hb, batch, w_att, w_f32, w_out, ln_g, ln_b, diff_lam, diff_gain, pos2, cmp_w1, cmp_w2,
           conv_w, layer, alpha):
    n, _ = hf.shape
    t = n // batch
    pa = _matmul(hb, w_att, BF16, layer=layer, tm=1024, tn=1024, name="proj_att").reshape(batch, t, -1)
    pf = _matmul(hb, w_f32, F32, layer=layer, tm=1024, tn=1024, name="proj_f32").reshape(batch, t, -1)

    o_sb = _sb_attention(pa, tq=256, q_blk=0, k_blk=4, v_blk=8)
    lam_init = 0.8 - 0.6 * math.exp(-0.3 * layer)
    o_df = _diff_attention(pa, diff_lam[layer], diff_gain[layer], tq=512, q_blk=12, k_blk=16, v_blk=20,
                           lam_init=lam_init)

    n16 = t // CMP_STRIDE
    kvc = pf[:, :, 3 * CONV_CH:3 * CONV_CH + 2 * LANES].reshape(batch, t, 2, NSA_KV_HEADS, HEAD_DIM)
    x16 = kvc.transpose(0, 2, 3, 1, 4).reshape(batch, 2 * NSA_KV_HEADS, n16, CMP_STRIDE * HEAD_DIM)
    ckv = _compress(x16, pos2, cmp_w1, cmp_w2, layer=layer)
    kv4 = pa[:, :, 28 * LANES:32 * LANES].reshape(batch, t, 4, NSA_KV_HEADS, HEAD_DIM).transpose(2, 0, 3, 1, 4)
    n_slc = t // SLC_BLOCK
    o_ns = _nsa_attention(pa, pf, ckv, _overlap_t(t), kv4, tq=256, q_blk=12, g_blk=14,
                          top_n=min(SLC_TOPN, n_slc))

    o_cv = _conv_mixer(pf, conv_w[layer], tt=512)
    parts = [o.reshape(n, -1) for o in (o_sb, o_df, o_ns, o_cv)]
    return _out_ln(parts, w_out, hf, ln_g, ln_b, layer=layer, alpha=alpha, tm=512)


def _seg(w_in, name, lo=0, hi=None):
    off = _SEG_OFF[name]
    hi = _SEG_W[name] if hi is None else hi
    return w_in[:, :, off + lo:off + hi]


def kernel(x, ln_g, ln_b, ffn_w1, ffn_w3, ffn_w2, w_in, w_out, diff_lam, diff_gain, cmp_pos, cmp_wk1,
           cmp_wk2, cmp_wv1, cmp_wv2, conv_w):
    batch, t, d = x.shape
    depth = ln_g.shape[0]
    alpha = (2 * depth) ** 0.25
    n = batch * t

    w13b = _interleave_tiles(ffn_w1, ffn_w3, FFN_TILE).astype(BF16)
    w2b = ffn_w2.astype(BF16)
    w_att = jnp.concatenate([w_in[:, :, :_SEG_OFF["ns_kc"]], w_in[:, :, _SEG_OFF["ns_ks"]:_SEG_OFF["ns_g"]]],
                            axis=-1).astype(BF16)
    gate_pad = jnp.zeros(w_in.shape[:2] + (LANES - _GATES_PER_KV,), w_in.dtype)
    f32_cols = [_seg(w_in, "cv_b"), _seg(w_in, "cv_c"), _seg(w_in, "cv_h"), _seg(w_in, "ns_kc"), _seg(w_in, "ns_vc")]
    for j in range(NSA_KV_HEADS):
        f32_cols += [_seg(w_in, "ns_g", j * _GATES_PER_KV, (j + 1) * _GATES_PER_KV), gate_pad]
    w_f32 = jnp.concatenate(f32_cols, axis=-1).astype(BF16)
    w_outb = w_out.astype(BF16)
    half = CMP_STRIDE * HEAD_DIM
    pos2 = cmp_pos.reshape(depth, 2, half)
    cmp_w1 = jnp.stack([cmp_wk1, cmp_wv1], axis=1).astype(BF16)
    cmp_w2 = jnp.stack([cmp_wk2, cmp_wv2], axis=1).astype(BF16)

    hf = x.reshape(n, d)
    for l in range(depth):
        hf, hb = _ffn_ln(hf, w13b, w2b, ln_g[l, 0], ln_b[l, 0], layer=l, slot=0, alpha=alpha, tm=512, tf=FFN_TILE)
        hf, _ = _mixer(hf, hb, batch, w_att, w_f32, w_outb, ln_g[l, 1], ln_b[l, 1], diff_lam, diff_gain,
                       pos2, cmp_w1, cmp_w2, conv_w, l, alpha)
        hf, _ = _ffn_ln(hf, w13b, w2b, ln_g[l, 2], ln_b[l, 2], layer=l, slot=1, alpha=alpha, tm=512, tf=FFN_TILE)
    return hf.reshape(batch, t, d)
```

```python
import functools
import math

import numpy as np
import jax
import jax.numpy as jnp
from jax import lax
from jax.experimental import pallas as pl
from jax.experimental.pallas import tpu as pltpu

F32 = jnp.float32
BF16 = jnp.bfloat16

HEAD_DIM = 64
SB_HEADS = 8
DIFF_HEADS = 4
NSA_HEADS = 8
NSA_KV_HEADS = 2
NSA_GROUP = NSA_HEADS // NSA_KV_HEADS
CMP_BLOCK = 32
CMP_STRIDE = 16
SLC_BLOCK = 64
SLC_TOPN = 16
WINDOW = 512
N_BRANCH = 3
CONV_CH = 512
CONV_WIDTH = 3
LN_EPS = 1e-5
RMS_EPS = 1e-5
NEG_BIG = -1e30
LOG2_E = math.log2(math.e)
EXP2_UNDERFLOW = -150.0
LANES = 128
VMEM_LIMIT = 56 * 1024 * 1024
FFN_TILE = 512

_SEG_NAMES = ("sb_q", "sb_k", "sb_v", "df_q", "df_k", "df_v", "ns_q", "ns_kc", "ns_vc",
              "ns_ks", "ns_vs", "ns_kw", "ns_vw", "ns_g", "cv_b", "cv_c", "cv_h")
_SEG_WIDTHS = (512, 512, 512, 512, 512, 512, 512, 128, 128, 128, 128, 128, 128,
               NSA_HEADS * N_BRANCH, 512, 512, 512)
_SEG_OFF = dict(zip(_SEG_NAMES, np.cumsum((0,) + _SEG_WIDTHS[:-1]).tolist()))
_SEG_W = dict(zip(_SEG_NAMES, _SEG_WIDTHS))


_GATES_PER_KV = NSA_GROUP * N_BRANCH


def _params(sem):
    return pltpu.CompilerParams(dimension_semantics=sem, vmem_limit_bytes=VMEM_LIMIT)


def _layer_norm(y, g, b):
    mu = jnp.mean(y, axis=-1, keepdims=True)
    d = y - mu
    var = jnp.mean(d * d, axis=-1, keepdims=True)
    return d * lax.rsqrt(var + LN_EPS) * g + b


def _dot(a, b):
    return jnp.dot(a, b, preferred_element_type=F32)


def _lane_group_max(s, mx):
    for g in range(s.shape[1] // LANES):
        mx = jnp.maximum(mx, s[:, g * LANES:(g + 1) * LANES])
    return mx


def _softmax_sweep(s_ref, *, rows, acc_lanes, first, q0, tq, big, scores_fn, weigh_fn):
    n_big = (q0 - first) // big if big else 0
    mid = first + n_big * big if big else first
    n_small = (q0 - mid) // tq

    def chunks(n, base, width, fn, init):
        return lax.fori_loop(0, n, lambda c, carry: fn(pl.multiple_of(base + c * width, tq), width, carry), init)

    def scores(start, width, mx, diagonal=False):
        s = scores_fn(start, width, diagonal)
        s_ref[:, pl.ds(start, width)] = s
        return _lane_group_max(s, mx)

    mx = jnp.full((rows, LANES), NEG_BIG, F32)
    if big:
        mx = chunks(n_big, first, big, scores, mx)
    mx = chunks(n_small, mid, tq, scores, mx)
    mx = scores(q0, tq, mx, True)
    m = jnp.broadcast_to(jnp.max(mx, axis=1, keepdims=True), (rows, LANES))

    def weigh(start, width, acc):
        p = jnp.exp2(s_ref[:, pl.ds(start, width)] - jnp.concatenate([m] * (width // LANES), axis=1))
        return acc + weigh_fn(p.astype(BF16), start, width)

    acc = jnp.zeros((rows, acc_lanes), F32)
    if big:
        acc = chunks(n_big, first, big, weigh, acc)
    acc = chunks(n_small, mid, tq, weigh, acc)
    return weigh(q0, tq, acc)


def _dot_nt(a, b):
    return lax.dot_general(a, b, (((1,), (1,)), ((), ())), preferred_element_type=F32)


def _ffn_ln_kernel(x_ref, w1_ref, w3_ref, w2_ref, g_ref, b_ref, o_ref, ob_ref, xb_ref, acc_ref,
                   *, alpha):
    j = pl.program_id(1)

    @pl.when(j == 0)
    def _():
        xb_ref[...] = x_ref[...].astype(BF16)
        acc_ref[...] = jnp.zeros_like(acc_ref)

    xb = xb_ref[...]
    a = _dot(xb, w1_ref[...])
    b = _dot(xb, w3_ref[...])
    hm = (a / (1.0 + jnp.exp(-a))) * b
    acc_ref[...] += _dot(hm.astype(BF16), w2_ref[...])

    @pl.when(j == pl.num_programs(1) - 1)
    def _():
        y = alpha * x_ref[...] + 0.5 * acc_ref[...]
        o = _layer_norm(y, g_ref[...], b_ref[...])
        o_ref[...] = o
        ob_ref[...] = o.astype(BF16)


def _ffn_ln(x, w1, w3, w2, g, b, *, layer, slot, alpha, tm, tf):
    n, d = x.shape
    f = w1.shape[-1]
    return pl.pallas_call(
        functools.partial(_ffn_ln_kernel, alpha=alpha),
        grid=(n // tm, f // tf),
        in_specs=[
            pl.BlockSpec((tm, d), lambda i, j: (i, 0)),
            pl.BlockSpec((None, None, d, tf), lambda i, j: (layer, slot, 0, j)),
            pl.BlockSpec((None, None, d, tf), lambda i, j: (layer, slot, 0, j)),
            pl.BlockSpec((None, None, tf, d), lambda i, j: (layer, slot, j, 0)),
            pl.BlockSpec((1, d), lambda i, j: (0, 0)),
            pl.BlockSpec((1, d), lambda i, j: (0, 0)),
        ],
        out_specs=[pl.BlockSpec((tm, d), lambda i, j: (i, 0)),
                   pl.BlockSpec((tm, d), lambda i, j: (i, 0))],
        out_shape=[jax.ShapeDtypeStruct((n, d), F32), jax.ShapeDtypeStruct((n, d), BF16)],
        scratch_shapes=[pltpu.VMEM((tm, d), BF16), pltpu.VMEM((tm, d), F32)],
        compiler_params=_params(("parallel", "arbitrary")),
        name="ffn_ln",
    )(x, w1, w3, w2, g.reshape(1, d), b.reshape(1, d))


def _matmul_kernel(x_ref, w_ref, o_ref):
    o_ref[...] = _dot(x_ref[...], w_ref[...]).astype(o_ref.dtype)


def _matmul(x, w, out_dtype, *, layer, tm, tn, name):
    n, d = x.shape
    m = w.shape[-1]
    return pl.pallas_call(
        _matmul_kernel,
        grid=(n // tm, m // tn),
        in_specs=[pl.BlockSpec((tm, d), lambda i, j: (i, 0)),
                  pl.BlockSpec((None, d, tn), lambda i, j: (layer, 0, j))],
        out_specs=pl.BlockSpec((tm, tn), lambda i, j: (i, j)),
        out_shape=jax.ShapeDtypeStruct((n, m), out_dtype),
        compiler_params=_params(("parallel", "arbitrary")),
        name=name,
    )(x, w)


def _sb_kernel(q_ref, k_ref, v_ref, o_ref, *, tq):
    i = pl.program_id(2)
    q0 = i * tq
    wide = 2 * tq
    lane = lax.broadcasted_iota(jnp.int32, (tq, LANES), 1)
    tri = jnp.where(lax.broadcasted_iota(jnp.int32, (wide, wide), 0)
                    >= lax.broadcasted_iota(jnp.int32, (wide, wide), 1), 1.0, 0.0).astype(BF16)
    qs = (q_ref[0].astype(F32) * (HEAD_DIM ** -0.5)).astype(BF16)
    zero = jnp.zeros_like(qs)
    q2 = jnp.concatenate([jnp.where(lane < HEAD_DIM, qs, zero),
                          jnp.where(lane < HEAD_DIM, zero, qs)], axis=0)
    qpos = q0 + lax.rem(lax.broadcasted_iota(jnp.int32, (2 * tq, 1), 0), tq)

    def block(start, width, c, acc, masked):
        ks = k_ref[0, pl.ds(start, width), :]
        vs = v_ref[0, pl.ds(start, width), :]
        z = _dot_nt(q2, ks) * LOG2_E
        nz = -z
        lg = jnp.minimum(nz, 0.0) - jnp.log2(1.0 + jnp.exp2(jnp.minimum(z, nz)))
        if masked:
            keep = start + lax.broadcasted_iota(jnp.int32, (1, width), 1) < qpos
            lg = jnp.where(keep, lg, 0.0)
        hi = lg.astype(BF16)
        lo = (lg - hi.astype(F32)).astype(BF16)
        t_w = tri[:width, :width]
        suffix = _dot(hi, t_w) + _dot(lo, t_w)
        a = jnp.exp2(z + suffix + c)
        if masked:
            a = jnp.where(keep, a, 0.0)
        return c + suffix[:, 0:1], acc + _dot(a.astype(BF16), vs)

    start0 = pl.multiple_of(jnp.maximum(q0 - tq, 0), tq)
    c1, acc1 = block(start0, wide, jnp.zeros((2 * tq, 1), F32), jnp.zeros((2 * tq, LANES), F32), True)

    def live(c):
        return (jnp.max(c) > EXP2_UNDERFLOW).astype(jnp.int32)

    def cond(carry):
        return (carry[0] >= 0) & (carry[1] > 0)

    def body(carry):
        kt, _, c, acc = carry
        c, acc = block(pl.multiple_of(kt * tq, tq), tq, c, acc, False)
        return kt - 1, live(c), c, acc

    _, _, _, acc = lax.while_loop(cond, body, (i - 2, live(c1), c1, acc1))
    o_ref[0] = jnp.where(lane < HEAD_DIM, acc[:tq], acc[tq:]).astype(o_ref.dtype)


def _sb_attention(pa, *, tq, q_blk, k_blk, v_blk):
    b, t, _ = pa.shape
    n_pairs = SB_HEADS // 2
    return pl.pallas_call(
        functools.partial(_sb_kernel, tq=tq),
        grid=(b, n_pairs, t // tq),
        in_specs=[
            pl.BlockSpec((1, tq, LANES), lambda bi, p, i: (bi, i, q_blk + p)),
            pl.BlockSpec((1, t, LANES), lambda bi, p, i: (bi, 0, k_blk + p)),
            pl.BlockSpec((1, t, LANES), lambda bi, p, i: (bi, 0, v_blk + p)),
        ],
        out_specs=pl.BlockSpec((1, tq, LANES), lambda bi, p, i: (bi, i, p)),
        out_shape=jax.ShapeDtypeStruct((b, t, n_pairs * LANES), BF16),
        compiler_params=_params(("parallel", "parallel", "arbitrary")),
        name="sb_attention",
    )(pa, pa, pa)


def _diff_kernel(lam_ref, gain_ref, q_ref, k_ref, v_ref, o_ref, s_ref, *, tq, lam_init):
    i = pl.program_id(2)
    lane = lax.broadcasted_iota(jnp.int32, (tq, LANES), 1)
    row = lax.broadcasted_iota(jnp.int32, (tq, tq), 0)
    col = lax.broadcasted_iota(jnp.int32, (tq, tq), 1)
    causal = col <= row
    causal2 = jnp.concatenate([causal, causal], axis=0)
    qs = (q_ref[0].astype(F32) * (HEAD_DIM ** -0.5)).astype(BF16)
    zero = jnp.zeros_like(qs)
    q2 = jnp.concatenate([jnp.where(lane < HEAD_DIM, qs, zero),
                          jnp.where(lane < HEAD_DIM, zero, qs)], axis=0)

    def scores(start, width, diagonal):
        s = _dot_nt(q2, k_ref[0, pl.ds(start, width), :]) * LOG2_E
        return jnp.where(causal2, s, NEG_BIG) if diagonal else s

    def weigh(p, start, width):
        return _dot(p, jnp.concatenate([v_ref[0, pl.ds(start, width), :], jnp.ones((width, LANES), BF16)], axis=1))

    acc = _softmax_sweep(s_ref, rows=2 * tq, acc_lanes=2 * LANES, first=0, q0=pl.multiple_of(i * tq, tq), tq=tq,
                         big=2 * tq, scores_fn=scores, weigh_fn=weigh)
    o = acc[:, :LANES] / acc[:, LANES:]
    lp = lam_ref[...]
    lam = (jnp.exp(jnp.sum(lp[0:1] * lp[1:2], axis=1, keepdims=True))
           - jnp.exp(jnp.sum(lp[2:3] * lp[3:4], axis=1, keepdims=True)) + lam_init)
    od = o[:tq] - lam * o[tq:]
    od = od * lax.rsqrt(jnp.mean(od * od, axis=-1, keepdims=True) + RMS_EPS)
    o_ref[0] = (od * gain_ref[...] * (1.0 - lam_init)).astype(o_ref.dtype)


def _diff_attention(pa, lam_params, gain, *, tq, q_blk, k_blk, v_blk, lam_init):
    b, t, _ = pa.shape
    return pl.pallas_call(
        functools.partial(_diff_kernel, tq=tq, lam_init=lam_init),
        grid=(b, DIFF_HEADS, t // tq),
        in_specs=[
            pl.BlockSpec((4, HEAD_DIM), lambda bi, h, i: (0, 0)),
            pl.BlockSpec((1, LANES), lambda bi, h, i: (0, 0)),
            pl.BlockSpec((1, tq, LANES), lambda bi, h, i: (bi, i, q_blk + h)),
            pl.BlockSpec((1, t, LANES), lambda bi, h, i: (bi, 0, k_blk + h)),
            pl.BlockSpec((1, t, LANES), lambda bi, h, i: (bi, 0, v_blk + h)),
        ],
        out_specs=pl.BlockSpec((1, tq, LANES), lambda bi, h, i: (bi, i, h)),
        out_shape=jax.ShapeDtypeStruct((b, t, DIFF_HEADS * LANES), BF16),
        scratch_shapes=[pltpu.VMEM((2 * tq, t), F32)],
        compiler_params=_params(("parallel", "parallel", "arbitrary")),
        name="diff_attention",
    )(lam_params, gain.reshape(1, LANES), pa, pa, pa)


def _compress_kernel(x_ref, pos_ref, w1_ref, w2_ref, o_ref):
    x = x_ref[0, 0]
    n16, half = x.shape
    xa = (x + pos_ref[0:1]).astype(BF16)
    xb = (x + pos_ref[1:2]).astype(BF16)
    w1 = w1_ref[0]
    first = _dot(xa, w1[:half])
    second = _dot(xb, w1[half:])
    h = first + pltpu.roll(second, n16 - 1, 0)
    g = 0.5 * h * (1.0 + jnp.tanh(math.sqrt(2.0 / math.pi) * (h + 0.044715 * (h * h * h))))
    o_ref[0, 0] = _dot(g.astype(BF16), w2_ref[0])


def _compress(x16, pos2, w1, w2, *, layer):
    b, n_streams, n16, half = x16.shape
    d = w2.shape[-1]
    return pl.pallas_call(
        _compress_kernel,
        grid=(b, n_streams),
        in_specs=[
            pl.BlockSpec((1, 1, n16, half), lambda bi, s: (bi, s, 0, 0)),
            pl.BlockSpec((None, 2, half), lambda bi, s: (layer, 0, 0)),
            pl.BlockSpec((None, 1, 2 * half, w1.shape[-1]), lambda bi, s: (layer, s // NSA_KV_HEADS, 0, 0)),
            pl.BlockSpec((None, 1, w2.shape[2], d), lambda bi, s: (layer, s // NSA_KV_HEADS, 0, 0)),
        ],
        out_specs=pl.BlockSpec((1, 1, n16, d), lambda bi, s: (bi, s, 0, 0)),
        out_shape=jax.ShapeDtypeStruct((b, n_streams, n16, d), F32),
        compiler_params=_params(("parallel", "arbitrary")),
        name="nsa_compress",
    )(x16, pos2, w1, w2)


def _nsa_kernel(q_ref, g_ref, ck_ref, cv_ref, ovt_ref, gx_ref, ks_ref, vs_ref, kw_ref, vw_ref, o_ref,
                kse_ref, kso_ref, vse_ref, vso_ref, kwd_ref, vwe_ref, vwo_ref, ckd_ref, cvd_ref, s_ref,
                *, tq, top_n):
    i = pl.program_id(2)
    grp = NSA_GROUP
    d = HEAD_DIM
    q0 = i * tq
    n_cmp = ck_ref.shape[2]
    n_slc = ovt_ref.shape[0]
    t = ks_ref.shape[2]
    rows = grp * tq
    half = rows // 2

    @pl.when(i == 0)
    def _():
        onehot = jnp.where(lax.broadcasted_iota(jnp.int32, (t, d), 0) // SLC_BLOCK
                           == lax.broadcasted_iota(jnp.int32, (t, d), 1), 1.0, 0.0)
        ones = jnp.ones((t, d), F32)

        def put(ref, left, right):
            ref[...] = jnp.concatenate([left, right], axis=1).astype(BF16)

        ks, vs = ks_ref[0, 0].astype(F32), vs_ref[0, 0].astype(F32)
        kw, vw = kw_ref[0, 0].astype(F32), vw_ref[0, 0].astype(F32)
        put(kse_ref, ks, onehot)
        put(kso_ref, onehot, ks)
        put(vse_ref, vs, ones)
        put(vso_ref, ones, vs)
        put(kwd_ref, kw, kw)
        put(vwe_ref, vw, ones)
        put(vwo_ref, ones, vw)
        put(ckd_ref, ck_ref[0, 0], ck_ref[0, 0])
        put(cvd_ref, cv_ref[0, 0], cv_ref[0, 0])

    lane = lax.broadcasted_iota(jnp.int32, (tq, LANES), 1)
    left = lane < d
    qs = (q_ref[0].astype(F32) * (d ** -0.5)).astype(BF16)
    pairs = [qs[:, :LANES], qs[:, LANES:]]
    zero = jnp.zeros((tq, LANES), BF16)
    qz = jnp.concatenate([jnp.where(left, pairs[0], zero), jnp.where(left, pairs[1], zero),
                          jnp.where(left, zero, pairs[0]), jnp.where(left, zero, pairs[1])], axis=0)
    t4 = q0 + lax.rem(lax.broadcasted_iota(jnp.int32, (rows, 1), 0), tq)

    def rep4(x):
        return jnp.concatenate([x] * grp, axis=0)

    s = _dot_nt(qz, ckd_ref[...])
    cmp_end = CMP_STRIDE * lax.broadcasted_iota(jnp.int32, (1, n_cmp), 1) + (CMP_BLOCK - 1)
    cmask = cmp_end <= t4
    s = jnp.where(cmask, s, NEG_BIG)
    e = jnp.exp(s - jnp.max(s, axis=1, keepdims=True))
    p = e / jnp.sum(e, axis=1, keepdims=True)
    p = jnp.where(cmask, p, 0.0)
    o_cmp = _dot(p.astype(BF16), cvd_ref[...])

    pg = p[0:tq]
    for h in range(1, grp):
        pg = pg + p[h * tq:(h + 1) * tq]
    p_hi = pg.astype(BF16)
    r1 = pg - p_hi.astype(F32)
    p_mid = r1.astype(BF16)
    p_lo = (r1 - p_mid.astype(F32)).astype(BF16)
    ovt = ovt_ref[...]
    imp_t = _dot_nt(ovt, p_hi) + _dot_nt(ovt, p_mid) + _dot_nt(ovt, p_lo)
    blk = lax.broadcasted_iota(jnp.int32, (n_slc, tq), 0)
    cur = (q0 + lax.broadcasted_iota(jnp.int32, (n_slc, tq), 1)) // SLC_BLOCK
    forced = (blk == 0) | (blk == cur) | (blk == cur - 1)
    score = jnp.where(forced, jnp.inf, jnp.where(blk <= cur, imp_t, -jnp.inf))
    rank = jnp.zeros((n_slc, tq), F32)
    for j in range(n_slc):
        sj = score[j:j + 1, :]
        tie = jnp.where(blk > j, 1.0, 0.0)
        rank = rank + jnp.where(sj > score, 1.0, jnp.where(sj == score, tie, 0.0))
    sel_t = jnp.where(rank < top_n, 1.0, 0.0)
    if n_slc < d:
        sel_t = jnp.concatenate([sel_t, jnp.zeros((d - n_slc, tq), F32)], axis=0)
    sel_t2 = jnp.concatenate([sel_t, sel_t], axis=0).astype(BF16)
    eye = jnp.where(lax.broadcasted_iota(jnp.int32, (tq, tq), 0)
                    == lax.broadcasted_iota(jnp.int32, (tq, tq), 1), 1.0, 0.0).astype(BF16)
    sel2 = _dot_nt(eye, sel_t2)
    bias = ((sel2 - 1.0) * (-NEG_BIG)).astype(BF16)
    q_even = jnp.concatenate([jnp.where(left, pairs[0], bias), jnp.where(left, pairs[1], bias)], axis=0)
    q_odd = jnp.concatenate([jnp.where(left, bias, pairs[0]), jnp.where(left, bias, pairs[1])], axis=0)

    row = lax.broadcasted_iota(jnp.int32, (tq, tq), 0)
    col = lax.broadcasted_iota(jnp.int32, (tq, tq), 1)
    causal4 = rep4(col <= row)

    def attend(scores_fn, first, big, ve_ref, vo_ref):
        def weigh(p, start, width):
            return jnp.concatenate([_dot(p[:half], ve_ref[pl.ds(start, width), :]),
                                    _dot(p[half:], vo_ref[pl.ds(start, width), :])], axis=0)

        acc = _softmax_sweep(s_ref, rows=rows, acc_lanes=LANES, first=first, q0=pl.multiple_of(q0, tq), tq=tq,
                             big=big, scores_fn=scores_fn, weigh_fn=weigh)
        out = []
        for pr in range(2):
            even, odd = acc[pr * tq:(pr + 1) * tq], acc[half + pr * tq:half + (pr + 1) * tq]
            den = pltpu.roll(jnp.where(left, odd, even), d, 1)
            out.append(jnp.where(left, even, odd) / den)
        return out

    def slc_scores(start, width, diagonal):
        s = jnp.concatenate([_dot_nt(q_even, kse_ref[pl.ds(start, width), :]),
                             _dot_nt(q_odd, kso_ref[pl.ds(start, width), :])], axis=0) * LOG2_E
        return jnp.where(causal4, s, NEG_BIG) if diagonal else s

    o_slc = attend(slc_scores, 0, 4 * tq, vse_ref, vso_ref)

    def win_scores(start, width, diagonal):
        s = _dot_nt(qz, kwd_ref[pl.ds(start, width), :]) * LOG2_E
        if diagonal:
            keep = causal4
        else:
            keep = start + lax.broadcasted_iota(jnp.int32, (1, width), 1) > t4 - WINDOW
        return jnp.where(keep, s, NEG_BIG)

    o_win = attend(win_scores, jnp.maximum(q0 - (-(-WINDOW // tq)) * tq, 0), None, vwe_ref, vwo_ref)

    gate = 1.0 / (1.0 + jnp.exp(-g_ref[0]))
    g_hi = gate.astype(BF16)
    g_r = gate - g_hi.astype(F32)
    g_mid = g_r.astype(BF16)
    g_lo = (g_r - g_mid.astype(F32)).astype(BF16)
    spread = gx_ref[...]
    gx = _dot(g_hi, spread) + _dot(g_mid, spread) + _dot(g_lo, spread)

    out = []
    for pr in range(2):
        cmp_pr = jnp.where(left, o_cmp[pr * tq:(pr + 1) * tq], o_cmp[half + pr * tq:half + (pr + 1) * tq])
        mixed = 0.0
        for br, o_br in enumerate((cmp_pr, o_slc[pr], o_win[pr])):
            blk_idx = pr * N_BRANCH + br
            mixed = mixed + gx[:, blk_idx * LANES:(blk_idx + 1) * LANES] * o_br
        out.append(mixed)
    o_ref[0] = jnp.concatenate(out, axis=1).astype(o_ref.dtype)


def _nsa_attention(pa, pf, ckv, ovt, kv4, *, tq, q_blk, g_blk, top_n):
    b, t, _ = pa.shape
    n16 = ckv.shape[2]
    n_slc = ovt.shape[0]
    qw = NSA_GROUP * HEAD_DIM
    kv_spec = pl.BlockSpec((1, 1, t, HEAD_DIM), lambda bi, j, i: (bi, j, 0, 0))
    return pl.pallas_call(
        functools.partial(_nsa_kernel, tq=tq, top_n=top_n),
        grid=(b, NSA_KV_HEADS, t // tq),
        in_specs=[
            pl.BlockSpec((1, tq, qw), lambda bi, j, i: (bi, i, q_blk + j)),
            pl.BlockSpec((1, tq, LANES), lambda bi, j, i: (bi, i, g_blk + j)),
            pl.BlockSpec((1, 1, n16, HEAD_DIM), lambda bi, j, i: (bi, j, 0, 0)),
            pl.BlockSpec((1, 1, n16, HEAD_DIM), lambda bi, j, i: (bi, NSA_KV_HEADS + j, 0, 0)),
            pl.BlockSpec((n_slc, n16), lambda bi, j, i: (0, 0)),
            pl.BlockSpec((LANES, 2 * N_BRANCH * LANES), lambda bi, j, i: (0, 0)),
            kv_spec, kv_spec, kv_spec, kv_spec,
        ],
        out_specs=pl.BlockSpec((1, tq, qw), lambda bi, j, i: (bi, i, j)),
        out_shape=jax.ShapeDtypeStruct((b, t, NSA_HEADS * HEAD_DIM), BF16),
        scratch_shapes=[pltpu.VMEM((t, LANES), BF16)] * 7 + [pltpu.VMEM((n16, LANES), BF16)] * 2
                       + [pltpu.VMEM((NSA_GROUP * tq, t), F32)],
        compiler_params=_params(("arbitrary", "arbitrary", "arbitrary")),
        name="nsa_attention",
    )(pa, pf, ckv, ckv, ovt, _gate_spread(), kv4[0], kv4[1], kv4[2], kv4[3])


def _conv_kernel(cb_ref, cc_ref, ch_ref, w_ref, o_ref, prev_ref):
    @pl.when(pl.program_id(1) == 0)
    def _():
        prev_ref[...] = jnp.zeros_like(prev_ref)

    u = cc_ref[0] * ch_ref[0]
    tt = u.shape[0]
    row = lax.broadcasted_iota(jnp.int32, u.shape, 0)
    last1 = prev_ref[7:8]
    last2 = prev_ref[6:7]
    u1 = jnp.where(row >= 1, pltpu.roll(u, 1, 0), last1)
    u2 = jnp.where(row >= 2, pltpu.roll(u, 2, 0), jnp.where(row == 1, last1, last2))
    w = w_ref[...]
    o_ref[0] = (cb_ref[0] * (w[0:1] * u2 + w[1:2] * u1 + w[2:3] * u)).astype(o_ref.dtype)
    prev_ref[...] = u[tt - 8:tt]


def _conv_mixer(pf, conv_w, *, tt):
    b, t, _ = pf.shape
    c = CONV_CH
    return pl.pallas_call(
        _conv_kernel,
        grid=(b, t // tt),
        in_specs=[
            pl.BlockSpec((1, tt, c), lambda bi, i: (bi, i, 0)),
            pl.BlockSpec((1, tt, c), lambda bi, i: (bi, i, 1)),
            pl.BlockSpec((1, tt, c), lambda bi, i: (bi, i, 2)),
            pl.BlockSpec((CONV_WIDTH, c), lambda bi, i: (0, 0)),
        ],
        out_specs=pl.BlockSpec((1, tt, c), lambda bi, i: (bi, i, 0)),
        out_shape=jax.ShapeDtypeStruct((b, t, c), BF16),
        scratch_shapes=[pltpu.VMEM((8, c), F32)],
        compiler_params=_params(("parallel", "arbitrary")),
        name="conv_mixer",
    )(pf, pf, pf, conv_w)


def _out_ln_kernel(a_ref, b_ref, c_ref, d_ref, w_ref, x_ref, g_ref, beta_ref, o_ref, ob_ref, *, alpha):
    kw = a_ref.shape[1]
    mix = _dot(a_ref[...], w_ref[0:kw])
    for n, r in enumerate((b_ref, c_ref, d_ref), start=1):
        mix = mix + _dot(r[...], w_ref[n * kw:(n + 1) * kw])
    o = _layer_norm(alpha * x_ref[...] + mix, g_ref[...], beta_ref[...])
    o_ref[...] = o
    ob_ref[...] = o.astype(BF16)


def _out_ln(parts, w, x, g, b, *, layer, alpha, tm):
    n, d = x.shape
    kw = parts[0].shape[1]
    part_spec = pl.BlockSpec((tm, kw), lambda i: (i, 0))
    return pl.pallas_call(
        functools.partial(_out_ln_kernel, alpha=alpha),
        grid=(n // tm,),
        in_specs=[part_spec, part_spec, part_spec, part_spec,
                  pl.BlockSpec((None,) + w.shape[1:], lambda i: (layer, 0, 0)),
                  pl.BlockSpec((tm, d), lambda i: (i, 0)),
                  pl.BlockSpec((1, d), lambda i: (0, 0)),
                  pl.BlockSpec((1, d), lambda i: (0, 0))],
        out_specs=[pl.BlockSpec((tm, d), lambda i: (i, 0)), pl.BlockSpec((tm, d), lambda i: (i, 0))],
        out_shape=[jax.ShapeDtypeStruct((n, d), F32), jax.ShapeDtypeStruct((n, d), BF16)],
        compiler_params=_params(("parallel",)),
        name="out_ln",
    )(*parts, w, x, g.reshape(1, d), b.reshape(1, d))


def _gate_spread():
    m = np.zeros((LANES, 2 * N_BRANCH * LANES), np.float32)
    for pr in range(2):
        for br in range(N_BRANCH):
            for lane in range(LANES):
                head = 2 * pr + lane // HEAD_DIM
                m[N_BRANCH * head + br, (pr * N_BRANCH + br) * LANES + lane] = 1.0
    return jnp.asarray(m, BF16)


def _overlap_t(t):
    n16 = t // CMP_STRIDE
    n_slc = t // SLC_BLOCK
    c_start = CMP_STRIDE * np.arange(n16)
    j_start = SLC_BLOCK * np.arange(n_slc)
    ov = ((c_start[None, :] < j_start[:, None] + SLC_BLOCK)
          & (c_start[None, :] + CMP_BLOCK > j_start[:, None])).astype(np.float32)
    ov[:, n16 - 1] = 0.0
    return jnp.asarray(ov, BF16)


def _mixer(hf, hb, batch, w_att, w_f32, w_out, ln_g, ln_b, diff_lam, diff_gain, pos2, cmp_w1, cmp_w2,
           conv_w, layer, alpha):
    n, _ = hf.shape
    t = n // batch
    pa = _matmul(hb, w_att, BF16, layer=layer, tm=1024, tn=1024, name="proj_att").reshape(batch, t, -1)
    pf = _matmul(hb, w_f32, F32, layer=layer, tm=1024, tn=1024, name="proj_f32").reshape(batch, t, -1)

    o_sb = _sb_attention(pa, tq=256, q_blk=0, k_blk=4, v_blk=8)
    lam_init = 0.8 - 0.6 * math.exp(-0.3 * layer)
    o_df = _diff_attention(pa, diff_lam[layer], diff_gain[layer], tq=512, q_blk=12, k_blk=16, v_blk=20,
                           lam_init=lam_init)

    n16 = t // CMP_STRIDE
    kvc = pf[:, :, 3 * CONV_CH:3 * CONV_CH + 2 * LANES].reshape(batch, t, 2, NSA_KV_HEADS, HEAD_DIM)
    x16 = kvc.transpose(0, 2, 3, 1, 4).reshape(batch, 2 * NSA_KV_HEADS, n16, CMP_STRIDE * HEAD_DIM)
    ckv = _compress(x16, pos2, cmp_w1, cmp_w2, layer=layer)
    kv4 = pa[:, :, 28 * LANES:32 * LANES].reshape(batch, t, 4, NSA_KV_HEADS, HEAD_DIM).transpose(2, 0, 3, 1, 4)
    n_slc = t // SLC_BLOCK
    o_ns = _nsa_attention(pa, pf, ckv, _overlap_t(t), kv4, tq=256, q_blk=12, g_blk=14,
                          top_n=min(SLC_TOPN, n_slc))

    o_cv = _conv_mixer(pf, conv_w[layer], tt=512)
    parts = [o.reshape(n, -1) for o in (o_sb, o_df, o_ns, o_cv)]
    return _out_ln(parts, w_out, hf, ln_g, ln_b, layer=layer, alpha=alpha, tm=512)


def _seg(w_in, name, lo=0, hi=None):
    off = _SEG_OFF[name]
    hi = _SEG_W[name] if hi is None else hi
    return w_in[:, :, off + lo:off + hi]


def kernel(x, ln_g, ln_b, ffn_w1, ffn_w3, ffn_w2, w_in, w_out, diff_lam, diff_gain, cmp_pos, cmp_wk1,
           cmp_wk2, cmp_wv1, cmp_wv2, conv_w):
    batch, t, d = x.shape
    depth = ln_g.shape[0]
    alpha = (2 * depth) ** 0.25
    n = batch * t

    w1b, w3b, w2b = ffn_w1.astype(BF16), ffn_w3.astype(BF16), ffn_w2.astype(BF16)
    w_att = jnp.concatenate([w_in[:, :, :_SEG_OFF["ns_kc"]], w_in[:, :, _SEG_OFF["ns_ks"]:_SEG_OFF["ns_g"]]],
                            axis=-1).astype(BF16)
    gate_pad = jnp.zeros(w_in.shape[:2] + (LANES - _GATES_PER_KV,), w_in.dtype)
    f32_cols = [_seg(w_in, "cv_b"), _seg(w_in, "cv_c"), _seg(w_in, "cv_h"), _seg(w_in, "ns_kc"), _seg(w_in, "ns_vc")]
    for j in range(NSA_KV_HEADS):
        f32_cols += [_seg(w_in, "ns_g", j * _GATES_PER_KV, (j + 1) * _GATES_PER_KV), gate_pad]
    w_f32 = jnp.concatenate(f32_cols, axis=-1).astype(BF16)
    w_outb = w_out.astype(BF16)
    half = CMP_STRIDE * HEAD_DIM
    pos2 = cmp_pos.reshape(depth, 2, half)
    cmp_w1 = jnp.stack([cmp_wk1, cmp_wv1], axis=1).astype(BF16)
    cmp_w2 = jnp.stack([cmp_wk2, cmp_wv2], axis=1).astype(BF16)

    hf = x.reshape(n, d)
    for l in range(depth):
        hf, hb = _ffn_ln(hf, w1b, w3b, w2b, ln_g[l, 0], ln_b[l, 0], layer=l, slot=0, alpha=alpha, tm=512,
                         tf=FFN_TILE)
        hf, _ = _mixer(hf, hb, batch, w_att, w_f32, w_outb, ln_g[l, 1], ln_b[l, 1], diff_lam, diff_gain,
                       pos2, cmp_w1, cmp_w2, conv_w, l, alpha)
        hf, _ = _ffn_ln(hf, w1b, w3b, w2b, ln_g[l, 2], ln_b[l, 2], layer=l, slot=1, alpha=alpha, tm=512,
                        tf=FFN_TILE)
    return hf.reshape(batch, t, d)
```

```python
import functools
import math

import numpy as np
import jax
import jax.numpy as jnp
from jax import lax
from jax.experimental import pallas as pl
from jax.experimental.pallas import tpu as pltpu

F32 = jnp.float32
BF16 = jnp.bfloat16

HEAD_DIM = 64
SB_HEADS = 8
DIFF_HEADS = 4
NSA_HEADS = 8
NSA_KV_HEADS = 2
NSA_GROUP = NSA_HEADS // NSA_KV_HEADS
CMP_BLOCK = 32
CMP_STRIDE = 16
SLC_BLOCK = 64
SLC_TOPN = 16
WINDOW = 512
N_BRANCH = 3
CONV_CH = 512
CONV_WIDTH = 3
LN_EPS = 1e-5
RMS_EPS = 1e-5
NEG_BIG = -1e30
LOG2_E = math.log2(math.e)
EXP2_UNDERFLOW = -150.0
LANES = 128
VMEM_LIMIT = 56 * 1024 * 1024
FFN_TILE = 512

_SEG_NAMES = ("sb_q", "sb_k", "sb_v", "df_q", "df_k", "df_v", "ns_q", "ns_kc", "ns_vc",
              "ns_ks", "ns_vs", "ns_kw", "ns_vw", "ns_g", "cv_b", "cv_c", "cv_h")
_SEG_WIDTHS = (512, 512, 512, 512, 512, 512, 512, 128, 128, 128, 128, 128, 128,
               NSA_HEADS * N_BRANCH, 512, 512, 512)
_SEG_OFF = dict(zip(_SEG_NAMES, np.cumsum((0,) + _SEG_WIDTHS[:-1]).tolist()))
_SEG_W = dict(zip(_SEG_NAMES, _SEG_WIDTHS))


_GATES_PER_KV = NSA_GROUP * N_BRANCH


def _params(sem):
    return pltpu.CompilerParams(dimension_semantics=sem, vmem_limit_bytes=VMEM_LIMIT)


def _layer_norm(y, g, b):
    mu = jnp.mean(y, axis=-1, keepdims=True)
    d = y - mu
    var = jnp.mean(d * d, axis=-1, keepdims=True)
    return d * lax.rsqrt(var + LN_EPS) * g + b


def _dot(a, b):
    return jnp.dot(a, b, preferred_element_type=F32)


def _lane_group_max(s, mx):
    for g in range(s.shape[1] // LANES):
        mx = jnp.maximum(mx, s[:, g * LANES:(g + 1) * LANES])
    return mx


def _softmax_sweep(s_ref, *, rows, acc_lanes, first, q0, tq, big, scores_fn, weigh_fn):
    n_big = (q0 - first) // big if big else 0
    mid = first + n_big * big if big else first
    n_small = (q0 - mid) // tq

    def chunks(n, base, width, fn, init):
        return lax.fori_loop(0, n, lambda c, carry: fn(pl.multiple_of(base + c * width, tq), width, carry), init)

    def scores(start, width, mx, diagonal=False):
        s = scores_fn(start, width, diagonal)
        s_ref[:, pl.ds(start, width)] = s
        return _lane_group_max(s, mx)

    mx = jnp.full((rows, LANES), NEG_BIG, F32)
    if big:
        mx = chunks(n_big, first, big, scores, mx)
    mx = chunks(n_small, mid, tq, scores, mx)
    mx = scores(q0, tq, mx, True)
    m = jnp.broadcast_to(jnp.max(mx, axis=1, keepdims=True), (rows, LANES))

    def weigh(start, width, acc):
        p = jnp.exp2(s_ref[:, pl.ds(start, width)] - jnp.concatenate([m] * (width // LANES), axis=1))
        return acc + weigh_fn(p.astype(BF16), start, width)

    acc = jnp.zeros((rows, acc_lanes), F32)
    if big:
        acc = chunks(n_big, first, big, weigh, acc)
    acc = chunks(n_small, mid, tq, weigh, acc)
    return weigh(q0, tq, acc)


def _dot_nt(a, b):
    return lax.dot_general(a, b, (((1,), (1,)), ((), ())), preferred_element_type=F32)


def _ffn_ln_kernel(x_ref, w1_ref, w3_ref, w2_ref, g_ref, b_ref, o_ref, ob_ref, xb_ref, acc_ref,
                   *, alpha):
    j = pl.program_id(1)

    @pl.when(j == 0)
    def _():
        xb_ref[...] = x_ref[...].astype(BF16)
        acc_ref[...] = jnp.zeros_like(acc_ref)

    xb = xb_ref[...]
    a = _dot(xb, w1_ref[...])
    b = _dot(xb, w3_ref[...])
    hm = (a / (1.0 + jnp.exp(-a))) * b
    acc_ref[...] += _dot(hm.astype(BF16), w2_ref[...])

    @pl.when(j == pl.num_programs(1) - 1)
    def _():
        y = alpha * x_ref[...] + 0.5 * acc_ref[...]
        o = _layer_norm(y, g_ref[...], b_ref[...])
        o_ref[...] = o
        ob_ref[...] = o.astype(BF16)


def _ffn_ln(x, w1, w3, w2, g, b, *, layer, slot, alpha, tm, tf):
    n, d = x.shape
    f = w1.shape[-1]
    return pl.pallas_call(
        functools.partial(_ffn_ln_kernel, alpha=alpha),
        grid=(n // tm, f // tf),
        in_specs=[
            pl.BlockSpec((tm, d), lambda i, j: (i, 0)),
            pl.BlockSpec((None, None, d, tf), lambda i, j: (layer, slot, 0, j)),
            pl.BlockSpec((None, None, d, tf), lambda i, j: (layer, slot, 0, j)),
            pl.BlockSpec((None, None, tf, d), lambda i, j: (layer, slot, j, 0)),
            pl.BlockSpec((1, d), lambda i, j: (0, 0)),
            pl.BlockSpec((1, d), lambda i, j: (0, 0)),
        ],
        out_specs=[pl.BlockSpec((tm, d), lambda i, j: (i, 0)),
                   pl.BlockSpec((tm, d), lambda i, j: (i, 0))],
        out_shape=[jax.ShapeDtypeStruct((n, d), F32), jax.ShapeDtypeStruct((n, d), BF16)],
        scratch_shapes=[pltpu.VMEM((tm, d), BF16), pltpu.VMEM((tm, d), F32)],
        compiler_params=_params(("parallel", "arbitrary")),
        name="ffn_ln",
    )(x, w1, w3, w2, g.reshape(1, d), b.reshape(1, d))


def _matmul_kernel(x_ref, w_ref, o_ref):
    o_ref[...] = _dot(x_ref[...], w_ref[...]).astype(o_ref.dtype)


def _matmul(x, w, out_dtype, *, layer, tm, tn, name):
    n, d = x.shape
    m = w.shape[-1]
    return pl.pallas_call(
        _matmul_kernel,
        grid=(n // tm, m // tn),
        in_specs=[pl.BlockSpec((tm, d), lambda i, j: (i, 0)),
                  pl.BlockSpec((None, d, tn), lambda i, j: (layer, 0, j))],
        out_specs=pl.BlockSpec((tm, tn), lambda i, j: (i, j)),
        out_shape=jax.ShapeDtypeStruct((n, m), out_dtype),
        compiler_params=_params(("parallel", "arbitrary")),
        name=name,
    )(x, w)


def _sb_kernel(q_ref, k_ref, v_ref, o_ref, *, tq, pairs):
    i = pl.program_id(2)
    q0 = i * tq
    lane = lax.broadcasted_iota(jnp.int32, (tq, LANES), 1)
    tri = jnp.where(lax.broadcasted_iota(jnp.int32, (tq, tq), 0)
                    >= lax.broadcasted_iota(jnp.int32, (tq, tq), 1), 1.0, 0.0).astype(BF16)
    qpos = q0 + lax.rem(lax.broadcasted_iota(jnp.int32, (2 * tq, 1), 0), tq)
    q2 = []
    for p in range(pairs):
        qs = (q_ref[0, :, p * LANES:(p + 1) * LANES].astype(F32) * (HEAD_DIM ** -0.5)).astype(BF16)
        zero = jnp.zeros_like(qs)
        q2.append(jnp.concatenate([jnp.where(lane < HEAD_DIM, qs, zero),
                                   jnp.where(lane < HEAD_DIM, zero, qs)], axis=0))

    def block(start, state, keep=None):
        width = tq
        masked = keep is not None
        out = []
        for p, (c, acc) in enumerate(state):
            ks = k_ref[0, pl.ds(start, width), p * LANES:(p + 1) * LANES]
            vs = v_ref[0, pl.ds(start, width), p * LANES:(p + 1) * LANES]
            z = _dot_nt(q2[p], ks) * LOG2_E
            nz = -z
            lg = jnp.minimum(nz, 0.0) - jnp.log2(1.0 + jnp.exp2(jnp.minimum(z, nz)))
            if masked:
                lg = jnp.where(keep, lg, 0.0)
            hi = lg.astype(BF16)
            lo = (lg - hi.astype(F32)).astype(BF16)
            suffix = _dot(hi, tri) + _dot(lo, tri)
            a = jnp.exp2(z + suffix + c)
            if masked:
                a = jnp.where(keep, a, 0.0)
            out.append((c + suffix[:, 0:1], acc + _dot(a.astype(BF16), vs)))
        return tuple(out)

    state = ((jnp.zeros((2 * tq, 1), F32), jnp.zeros((2 * tq, LANES), F32)),) * pairs
    key = lax.broadcasted_iota(jnp.int32, (1, tq), 1)
    state = block(pl.multiple_of(q0, tq), state, keep=q0 + key < qpos)
    state = block(pl.multiple_of(jnp.maximum(q0 - tq, 0), tq), state, keep=key < q0)

    def live(state):
        top = functools.reduce(jnp.maximum, [c for c, _ in state])
        return (jnp.max(top) > EXP2_UNDERFLOW).astype(jnp.int32)

    def cond(carry):
        return (carry[0] >= 0) & (carry[1] > 0)

    def body(carry):
        kt, _, state = carry
        state = block(pl.multiple_of(kt * tq, tq), state)
        return kt - 1, live(state), state

    _, _, state = lax.while_loop(cond, body, (i - 2, live(state), state))
    o_ref[0] = jnp.concatenate([jnp.where(lane < HEAD_DIM, acc[:tq], acc[tq:]) for _, acc in state],
                               axis=1).astype(o_ref.dtype)


def _sb_attention(pa, *, tq, pairs, q_blk, k_blk, v_blk):
    b, t, _ = pa.shape
    w = pairs * LANES
    n_groups = SB_HEADS // (2 * pairs)
    return pl.pallas_call(
        functools.partial(_sb_kernel, tq=tq, pairs=pairs),
        grid=(b, n_groups, t // tq),
        in_specs=[
            pl.BlockSpec((1, tq, w), lambda bi, p, i: (bi, i, q_blk + p)),
            pl.BlockSpec((1, t, w), lambda bi, p, i: (bi, 0, k_blk + p)),
            pl.BlockSpec((1, t, w), lambda bi, p, i: (bi, 0, v_blk + p)),
        ],
        out_specs=pl.BlockSpec((1, tq, w), lambda bi, p, i: (bi, i, p)),
        out_shape=jax.ShapeDtypeStruct((b, t, SB_HEADS * HEAD_DIM), BF16),
        compiler_params=_params(("parallel", "parallel", "arbitrary")),
        name="sb_attention",
    )(pa, pa, pa)


def _diff_kernel(lam_ref, gain_ref, q_ref, k_ref, v_ref, o_ref, s_ref, *, tq, lam_init):
    i = pl.program_id(2)
    lane = lax.broadcasted_iota(jnp.int32, (tq, LANES), 1)
    row = lax.broadcasted_iota(jnp.int32, (tq, tq), 0)
    col = lax.broadcasted_iota(jnp.int32, (tq, tq), 1)
    causal = col <= row
    causal2 = jnp.concatenate([causal, causal], axis=0)
    qs = (q_ref[0].astype(F32) * (HEAD_DIM ** -0.5)).astype(BF16)
    zero = jnp.zeros_like(qs)
    q2 = jnp.concatenate([jnp.where(lane < HEAD_DIM, qs, zero),
                          jnp.where(lane < HEAD_DIM, zero, qs)], axis=0)

    def scores(start, width, diagonal):
        s = _dot_nt(q2, k_ref[0, pl.ds(start, width), :]) * LOG2_E
        return jnp.where(causal2, s, NEG_BIG) if diagonal else s

    def weigh(p, start, width):
        return _dot(p, jnp.concatenate([v_ref[0, pl.ds(start, width), :], jnp.ones((width, LANES), BF16)], axis=1))

    acc = _softmax_sweep(s_ref, rows=2 * tq, acc_lanes=2 * LANES, first=0, q0=pl.multiple_of(i * tq, tq), tq=tq,
                         big=2 * tq, scores_fn=scores, weigh_fn=weigh)
    o = acc[:, :LANES] / acc[:, LANES:]
    lp = lam_ref[...]
    lam = (jnp.exp(jnp.sum(lp[0:1] * lp[1:2], axis=1, keepdims=True))
           - jnp.exp(jnp.sum(lp[2:3] * lp[3:4], axis=1, keepdims=True)) + lam_init)
    od = o[:tq] - lam * o[tq:]
    od = od * lax.rsqrt(jnp.mean(od * od, axis=-1, keepdims=True) + RMS_EPS)
    o_ref[0] = (od * gain_ref[...] * (1.0 - lam_init)).astype(o_ref.dtype)


def _diff_attention(pa, lam_params, gain, *, tq, q_blk, k_blk, v_blk, lam_init):
    b, t, _ = pa.shape
    return pl.pallas_call(
        functools.partial(_diff_kernel, tq=tq, lam_init=lam_init),
        grid=(b, DIFF_HEADS, t // tq),
        in_specs=[
            pl.BlockSpec((4, HEAD_DIM), lambda bi, h, i: (0, 0)),
            pl.BlockSpec((1, LANES), lambda bi, h, i: (0, 0)),
            pl.BlockSpec((1, tq, LANES), lambda bi, h, i: (bi, i, q_blk + h)),
            pl.BlockSpec((1, t, LANES), lambda bi, h, i: (bi, 0, k_blk + h)),
            pl.BlockSpec((1, t, LANES), lambda bi, h, i: (bi, 0, v_blk + h)),
        ],
        out_specs=pl.BlockSpec((1, tq, LANES), lambda bi, h, i: (bi, i, h)),
        out_shape=jax.ShapeDtypeStruct((b, t, DIFF_HEADS * LANES), BF16),
        scratch_shapes=[pltpu.VMEM((2 * tq, t), F32)],
        compiler_params=_params(("parallel", "parallel", "arbitrary")),
        name="diff_attention",
    )(lam_params, gain.reshape(1, LANES), pa, pa, pa)


def _compress_kernel(x_ref, pos_ref, w1_ref, w2_ref, o_ref):
    x = x_ref[0, 0]
    n16, half = x.shape
    xa = (x + pos_ref[0:1]).astype(BF16)
    xb = (x + pos_ref[1:2]).astype(BF16)
    w1 = w1_ref[0]
    first = _dot(xa, w1[:half])
    second = _dot(xb, w1[half:])
    h = first + pltpu.roll(second, n16 - 1, 0)
    g = 0.5 * h * (1.0 + jnp.tanh(math.sqrt(2.0 / math.pi) * (h + 0.044715 * (h * h * h))))
    o_ref[0, 0] = _dot(g.astype(BF16), w2_ref[0])


def _compress(x16, pos2, w1, w2, *, layer):
    b, n_streams, n16, half = x16.shape
    d = w2.shape[-1]
    return pl.pallas_call(
        _compress_kernel,
        grid=(b, n_streams),
        in_specs=[
            pl.BlockSpec((1, 1, n16, half), lambda bi, s: (bi, s, 0, 0)),
            pl.BlockSpec((None, 2, half), lambda bi, s: (layer, 0, 0)),
            pl.BlockSpec((None, 1, 2 * half, w1.shape[-1]), lambda bi, s: (layer, s // NSA_KV_HEADS, 0, 0)),
            pl.BlockSpec((None, 1, w2.shape[2], d), lambda bi, s: (layer, s // NSA_KV_HEADS, 0, 0)),
        ],
        out_specs=pl.BlockSpec((1, 1, n16, d), lambda bi, s: (bi, s, 0, 0)),
        out_shape=jax.ShapeDtypeStruct((b, n_streams, n16, d), F32),
        compiler_params=_params(("parallel", "arbitrary")),
        name="nsa_compress",
    )(x16, pos2, w1, w2)


def _nsa_kernel(q_ref, g_ref, ck_ref, cv_ref, ovt_ref, gx_ref, ks_ref, vs_ref, kw_ref, vw_ref, o_ref,
                kse_ref, kso_ref, vse_ref, vso_ref, kwd_ref, vwe_ref, vwo_ref, ckd_ref, cvd_ref, s_ref,
                *, tq, top_n):
    i = pl.program_id(2)
    grp = NSA_GROUP
    d = HEAD_DIM
    q0 = i * tq
    n_cmp = ck_ref.shape[2]
    n_slc = ovt_ref.shape[0]
    t = ks_ref.shape[2]
    rows = grp * tq
    half = rows // 2

    @pl.when(i == 0)
    def _():
        onehot = jnp.where(lax.broadcasted_iota(jnp.int32, (t, d), 0) // SLC_BLOCK
                           == lax.broadcasted_iota(jnp.int32, (t, d), 1), 1.0, 0.0)
        ones = jnp.ones((t, d), F32)

        def put(ref, left, right):
            ref[...] = jnp.concatenate([left, right], axis=1).astype(BF16)

        ks, vs = ks_ref[0, 0].astype(F32), vs_ref[0, 0].astype(F32)
        kw, vw = kw_ref[0, 0].astype(F32), vw_ref[0, 0].astype(F32)
        put(kse_ref, ks, onehot)
        put(kso_ref, onehot, ks)
        put(vse_ref, vs, ones)
        put(vso_ref, ones, vs)
        put(kwd_ref, kw, kw)
        put(vwe_ref, vw, ones)
        put(vwo_ref, ones, vw)
        put(ckd_ref, ck_ref[0, 0], ck_ref[0, 0])
        put(cvd_ref, cv_ref[0, 0], cv_ref[0, 0])

    lane = lax.broadcasted_iota(jnp.int32, (tq, LANES), 1)
    left = lane < d
    qs = (q_ref[0].astype(F32) * (d ** -0.5)).astype(BF16)
    pairs = [qs[:, :LANES], qs[:, LANES:]]
    zero = jnp.zeros((tq, LANES), BF16)
    qz = jnp.concatenate([jnp.where(left, pairs[0], zero), jnp.where(left, pairs[1], zero),
                          jnp.where(left, zero, pairs[0]), jnp.where(left, zero, pairs[1])], axis=0)
    t4 = q0 + lax.rem(lax.broadcasted_iota(jnp.int32, (rows, 1), 0), tq)

    def rep4(x):
        return jnp.concatenate([x] * grp, axis=0)

    s = _dot_nt(qz, ckd_ref[...])
    cmp_end = CMP_STRIDE * lax.broadcasted_iota(jnp.int32, (1, n_cmp), 1) + (CMP_BLOCK - 1)
    cmask = cmp_end <= t4
    s = jnp.where(cmask, s, NEG_BIG)
    e = jnp.exp(s - jnp.max(s, axis=1, keepdims=True))
    p = e / jnp.sum(e, axis=1, keepdims=True)
    p = jnp.where(cmask, p, 0.0)
    o_cmp = _dot(p.astype(BF16), cvd_ref[...])

    pg = p[0:tq]
    for h in range(1, grp):
        pg = pg + p[h * tq:(h + 1) * tq]
    p_hi = pg.astype(BF16)
    r1 = pg - p_hi.astype(F32)
    p_mid = r1.astype(BF16)
    p_lo = (r1 - p_mid.astype(F32)).astype(BF16)
    ovt = ovt_ref[...]
    imp_t = _dot_nt(ovt, p_hi) + _dot_nt(ovt, p_mid) + _dot_nt(ovt, p_lo)
    blk = lax.broadcasted_iota(jnp.int32, (n_slc, tq), 0)
    cur = (q0 + lax.broadcasted_iota(jnp.int32, (n_slc, tq), 1)) // SLC_BLOCK
    forced = (blk == 0) | (blk == cur) | (blk == cur - 1)
    score = jnp.where(forced, jnp.inf, jnp.where(blk <= cur, imp_t, -jnp.inf))
    rank = jnp.zeros((n_slc, tq), F32)
    for j in range(n_slc):
        sj = score[j:j + 1, :]
        tie = jnp.where(blk > j, 1.0, 0.0)
        rank = rank + jnp.where(sj > score, 1.0, jnp.where(sj == score, tie, 0.0))
    sel_t = jnp.where(rank < top_n, 1.0, 0.0)
    if n_slc < d:
        sel_t = jnp.concatenate([sel_t, jnp.zeros((d - n_slc, tq), F32)], axis=0)
    sel_t2 = jnp.concatenate([sel_t, sel_t], axis=0).astype(BF16)
    eye = jnp.where(lax.broadcasted_iota(jnp.int32, (tq, tq), 0)
                    == lax.broadcasted_iota(jnp.int32, (tq, tq), 1), 1.0, 0.0).astype(BF16)
    sel2 = _dot_nt(eye, sel_t2)
    bias = ((sel2 - 1.0) * (-NEG_BIG)).astype(BF16)
    q_even = jnp.concatenate([jnp.where(left, pairs[0], bias), jnp.where(left, pairs[1], bias)], axis=0)
    q_odd = jnp.concatenate([jnp.where(left, bias, pairs[0]), jnp.where(left, bias, pairs[1])], axis=0)

    row = lax.broadcasted_iota(jnp.int32, (tq, tq), 0)
    col = lax.broadcasted_iota(jnp.int32, (tq, tq), 1)
    causal4 = rep4(col <= row)

    def attend(scores_fn, first, big, ve_ref, vo_ref):
        def weigh(p, start, width):
            return jnp.concatenate([_dot(p[:half], ve_ref[pl.ds(start, width), :]),
                                    _dot(p[half:], vo_ref[pl.ds(start, width), :])], axis=0)

        acc = _softmax_sweep(s_ref, rows=rows, acc_lanes=LANES, first=first, q0=pl.multiple_of(q0, tq), tq=tq,
                             big=big, scores_fn=scores_fn, weigh_fn=weigh)
        out = []
        for pr in range(2):
            even, odd = acc[pr * tq:(pr + 1) * tq], acc[half + pr * tq:half + (pr + 1) * tq]
            den = pltpu.roll(jnp.where(left, odd, even), d, 1)
            out.append(jnp.where(left, even, odd) / den)
        return out

    def slc_scores(start, width, diagonal):
        s = jnp.concatenate([_dot_nt(q_even, kse_ref[pl.ds(start, width), :]),
                             _dot_nt(q_odd, kso_ref[pl.ds(start, width), :])], axis=0) * LOG2_E
        return jnp.where(causal4, s, NEG_BIG) if diagonal else s

    o_slc = attend(slc_scores, 0, 4 * tq, vse_ref, vso_ref)

    def win_scores(start, width, diagonal):
        s = _dot_nt(qz, kwd_ref[pl.ds(start, width), :]) * LOG2_E
        if diagonal:
            keep = causal4
        else:
            keep = start + lax.broadcasted_iota(jnp.int32, (1, width), 1) > t4 - WINDOW
        return jnp.where(keep, s, NEG_BIG)

    o_win = attend(win_scores, jnp.maximum(q0 - (-(-WINDOW // tq)) * tq, 0), None, vwe_ref, vwo_ref)

    gate = 1.0 / (1.0 + jnp.exp(-g_ref[0]))
    g_hi = gate.astype(BF16)
    g_r = gate - g_hi.astype(F32)
    g_mid = g_r.astype(BF16)
    g_lo = (g_r - g_mid.astype(F32)).astype(BF16)
    spread = gx_ref[...]
    gx = _dot(g_hi, spread) + _dot(g_mid, spread) + _dot(g_lo, spread)

    out = []
    for pr in range(2):
        cmp_pr = jnp.where(left, o_cmp[pr * tq:(pr + 1) * tq], o_cmp[half + pr * tq:half + (pr + 1) * tq])
        mixed = 0.0
        for br, o_br in enumerate((cmp_pr, o_slc[pr], o_win[pr])):
            blk_idx = pr * N_BRANCH + br
            mixed = mixed + gx[:, blk_idx * LANES:(blk_idx + 1) * LANES] * o_br
        out.append(mixed)
    o_ref[0] = jnp.concatenate(out, axis=1).astype(o_ref.dtype)


def _nsa_attention(pa, pf, ckv, ovt, kv4, *, tq, q_blk, g_blk, top_n):
    b, t, _ = pa.shape
    n16 = ckv.shape[2]
    n_slc = ovt.shape[0]
    qw = NSA_GROUP * HEAD_DIM
    kv_spec = pl.BlockSpec((1, 1, t, HEAD_DIM), lambda bi, j, i: (bi, j, 0, 0))
    return pl.pallas_call(
        functools.partial(_nsa_kernel, tq=tq, top_n=top_n),
        grid=(b, NSA_KV_HEADS, t // tq),
        in_specs=[
            pl.BlockSpec((1, tq, qw), lambda bi, j, i: (bi, i, q_blk + j)),
            pl.BlockSpec((1, tq, LANES), lambda bi, j, i: (bi, i, g_blk + j)),
            pl.BlockSpec((1, 1, n16, HEAD_DIM), lambda bi, j, i: (bi, j, 0, 0)),
            pl.BlockSpec((1, 1, n16, HEAD_DIM), lambda bi, j, i: (bi, NSA_KV_HEADS + j, 0, 0)),
            pl.BlockSpec((n_slc, n16), lambda bi, j, i: (0, 0)),
            pl.BlockSpec((LANES, 2 * N_BRANCH * LANES), lambda bi, j, i: (0, 0)),
            kv_spec, kv_spec, kv_spec, kv_spec,
        ],
        out_specs=pl.BlockSpec((1, tq, qw), lambda bi, j, i: (bi, i, j)),
        out_shape=jax.ShapeDtypeStruct((b, t, NSA_HEADS * HEAD_DIM), BF16),
        scratch_shapes=[pltpu.VMEM((t, LANES), BF16)] * 7 + [pltpu.VMEM((n16, LANES), BF16)] * 2
                       + [pltpu.VMEM((NSA_GROUP * tq, t), F32)],
        compiler_params=_params(("arbitrary", "arbitrary", "arbitrary")),
        name="nsa_attention",
    )(pa, pf, ckv, ckv, ovt, _gate_spread(), kv4[0], kv4[1], kv4[2], kv4[3])


def _conv_kernel(cb_ref, cc_ref, ch_ref, w_ref, o_ref, prev_ref):
    @pl.when(pl.program_id(1) == 0)
    def _():
        prev_ref[...] = jnp.zeros_like(prev_ref)

    u = cc_ref[0] * ch_ref[0]
    tt = u.shape[0]
    row = lax.broadcasted_iota(jnp.int32, u.shape, 0)
    last1 = prev_ref[7:8]
    last2 = prev_ref[6:7]
    u1 = jnp.where(row >= 1, pltpu.roll(u, 1, 0), last1)
    u2 = jnp.where(row >= 2, pltpu.roll(u, 2, 0), jnp.where(row == 1, last1, last2))
    w = w_ref[...]
    o_ref[0] = (cb_ref[0] * (w[0:1] * u2 + w[1:2] * u1 + w[2:3] * u)).astype(o_ref.dtype)
    prev_ref[...] = u[tt - 8:tt]


def _conv_mixer(pf, conv_w, *, tt):
    b, t, _ = pf.shape
    c = CONV_CH
    return pl.pallas_call(
        _conv_kernel,
        grid=(b, t // tt),
        in_specs=[
            pl.BlockSpec((1, tt, c), lambda bi, i: (bi, i, 0)),
            pl.BlockSpec((1, tt, c), lambda bi, i: (bi, i, 1)),
            pl.BlockSpec((1, tt, c), lambda bi, i: (bi, i, 2)),
            pl.BlockSpec((CONV_WIDTH, c), lambda bi, i: (0, 0)),
        ],
        out_specs=pl.BlockSpec((1, tt, c), lambda bi, i: (bi, i, 0)),
        out_shape=jax.ShapeDtypeStruct((b, t, c), BF16),
        scratch_shapes=[pltpu.VMEM((8, c), F32)],
        compiler_params=_params(("parallel", "arbitrary")),
        name="conv_mixer",
    )(pf, pf, pf, conv_w)


def _out_ln_kernel(a_ref, b_ref, c_ref, d_ref, w_ref, x_ref, g_ref, beta_ref, o_ref, ob_ref, *, alpha):
    kw = a_ref.shape[1]
    mix = _dot(a_ref[...], w_ref[0:kw])
    for n, r in enumerate((b_ref, c_ref, d_ref), start=1):
        mix = mix + _dot(r[...], w_ref[n * kw:(n + 1) * kw])
    o = _layer_norm(alpha * x_ref[...] + mix, g_ref[...], beta_ref[...])
    o_ref[...] = o
    ob_ref[...] = o.astype(BF16)


def _out_ln(parts, w, x, g, b, *, layer, alpha, tm):
    n, d = x.shape
    kw = parts[0].shape[1]
    part_spec = pl.BlockSpec((tm, kw), lambda i: (i, 0))
    return pl.pallas_call(
        functools.partial(_out_ln_kernel, alpha=alpha),
        grid=(n // tm,),
        in_specs=[part_spec, part_spec, part_spec, part_spec,
                  pl.BlockSpec((None,) + w.shape[1:], lambda i: (layer, 0, 0)),
                  pl.BlockSpec((tm, d), lambda i: (i, 0)),
                  pl.BlockSpec((1, d), lambda i: (0, 0)),
                  pl.BlockSpec((1, d), lambda i: (0, 0))],
        out_specs=[pl.BlockSpec((tm, d), lambda i: (i, 0)), pl.BlockSpec((tm, d), lambda i: (i, 0))],
        out_shape=[jax.ShapeDtypeStruct((n, d), F32), jax.ShapeDtypeStruct((n, d), BF16)],
        compiler_params=_params(("parallel",)),
        name="out_ln",
    )(*parts, w, x, g.reshape(1, d), b.reshape(1, d))


def _gate_spread():
    m = np.zeros((LANES, 2 * N_BRANCH * LANES), np.float32)
    for pr in range(2):
        for br in range(N_BRANCH):
            for lane in range(LANES):
                head = 2 * pr + lane // HEAD_DIM
                m[N_BRANCH * head + br, (pr * N_BRANCH + br) * LANES + lane] = 1.0
    return jnp.asarray(m, BF16)


def _overlap_t(t):
    n16 = t // CMP_STRIDE
    n_slc = t // SLC_BLOCK
    c_start = CMP_STRIDE * np.arange(n16)
    j_start = SLC_BLOCK * np.arange(n_slc)
    ov = ((c_start[None, :] < j_start[:, None] + SLC_BLOCK)
          & (c_start[None, :] + CMP_BLOCK > j_start[:, None])).astype(np.float32)
    ov[:, n16 - 1] = 0.0
    return jnp.asarray(ov, BF16)


def _mixer(hf, hb, batch, w_att, w_f32, w_out, ln_g, ln_b, diff_lam, diff_gain, pos2, cmp_w1, cmp_w2,
           conv_w, layer, alpha):
    n, _ = hf.shape
    t = n // batch
    pa = _matmul(hb, w_att, BF16, layer=layer, tm=1024, tn=1024, name="proj_att").reshape(batch, t, -1)
    pf = _matmul(hb, w_f32, F32, layer=layer, tm=1024, tn=1024, name="proj_f32").reshape(batch, t, -1)

    o_sb = _sb_attention(pa, tq=256, pairs=4, q_blk=0, k_blk=1, v_blk=2)
    lam_init = 0.8 - 0.6 * math.exp(-0.3 * layer)
    o_df = _diff_attention(pa, diff_lam[layer], diff_gain[layer], tq=512, q_blk=12, k_blk=16, v_blk=20,
                           lam_init=lam_init)

    n16 = t // CMP_STRIDE
    kvc = pf[:, :, 3 * CONV_CH:3 * CONV_CH + 2 * LANES].reshape(batch, t, 2, NSA_KV_HEADS, HEAD_DIM)
    x16 = kvc.transpose(0, 2, 3, 1, 4).reshape(batch, 2 * NSA_KV_HEADS, n16, CMP_STRIDE * HEAD_DIM)
    ckv = _compress(x16, pos2, cmp_w1, cmp_w2, layer=layer)
    kv4 = pa[:, :, 28 * LANES:32 * LANES].reshape(batch, t, 4, NSA_KV_HEADS, HEAD_DIM).transpose(2, 0, 3, 1, 4)
    n_slc = t // SLC_BLOCK
    o_ns = _nsa_attention(pa, pf, ckv, _overlap_t(t), kv4, tq=256, q_blk=12, g_blk=14,
                          top_n=min(SLC_TOPN, n_slc))

    o_cv = _conv_mixer(pf, conv_w[layer], tt=512)
    parts = [o.reshape(n, -1) for o in (o_sb, o_df, o_ns, o_cv)]
    return _out_ln(parts, w_out, hf, ln_g, ln_b, layer=layer, alpha=alpha, tm=512)


def _seg(w_in, name, lo=0, hi=None):
    off = _SEG_OFF[name]
    hi = _SEG_W[name] if hi is None else hi
    return w_in[:, :, off + lo:off + hi]


def kernel(x, ln_g, ln_b, ffn_w1, ffn_w3, ffn_w2, w_in, w_out, diff_lam, diff_gain, cmp_pos, cmp_wk1,
           cmp_wk2, cmp_wv1, cmp_wv2, conv_w):
    batch, t, d = x.shape
    depth = ln_g.shape[0]
    alpha = (2 * depth) ** 0.25
    n = batch * t

    w1b, w3b, w2b = ffn_w1.astype(BF16), ffn_w3.astype(BF16), ffn_w2.astype(BF16)
    w_att = jnp.concatenate([w_in[:, :, :_SEG_OFF["ns_kc"]], w_in[:, :, _SEG_OFF["ns_ks"]:_SEG_OFF["ns_g"]]],
                            axis=-1).astype(BF16)
    gate_pad = jnp.zeros(w_in.shape[:2] + (LANES - _GATES_PER_KV,), w_in.dtype)
    f32_cols = [_seg(w_in, "cv_b"), _seg(w_in, "cv_c"), _seg(w_in, "cv_h"), _seg(w_in, "ns_kc"), _seg(w_in, "ns_vc")]
    for j in range(NSA_KV_HEADS):
        f32_cols += [_seg(w_in, "ns_g", j * _GATES_PER_KV, (j + 1) * _GATES_PER_KV), gate_pad]
    w_f32 = jnp.concatenate(f32_cols, axis=-1).astype(BF16)
    w_outb = w_out.astype(BF16)
    half = CMP_STRIDE * HEAD_DIM
    pos2 = cmp_pos.reshape(depth, 2, half)
    cmp_w1 = jnp.stack([cmp_wk1, cmp_wv1], axis=1).astype(BF16)
    cmp_w2 = jnp.stack([cmp_wk2, cmp_wv2], axis=1).astype(BF16)

    hf = x.reshape(n, d)
    for l in range(depth):
        hf, hb = _ffn_ln(hf, w1b, w3b, w2b, ln_g[l, 0], ln_b[l, 0], layer=l, slot=0, alpha=alpha, tm=512,
                         tf=FFN_TILE)
        hf, _ = _mixer(hf, hb, batch, w_att, w_f32, w_outb, ln_g[l, 1], ln_b[l, 1], diff_lam, diff_gain,
                       pos2, cmp_w1, cmp_w2, conv_w, l, alpha)
        hf, _ = _ffn_ln(hf, w1b, w3b, w2b, ln_g[l, 2], ln_b[l, 2], layer=l, slot=1, alpha=alpha, tm=512,
                        tf=FFN_TILE)
    return hf.reshape(batch, t, d)
```

```python
import functools
import math

import numpy as np
import jax
import jax.numpy as jnp
from jax import lax
from jax.experimental import pallas as pl
from jax.experimental.pallas import tpu as pltpu

F32 = jnp.float32
BF16 = jnp.bfloat16

HEAD_DIM = 64
SB_HEADS = 8
DIFF_HEADS = 4
NSA_HEADS = 8
NSA_KV_HEADS = 2
NSA_GROUP = NSA_HEADS // NSA_KV_HEADS
CMP_BLOCK = 32
CMP_STRIDE = 16
SLC_BLOCK = 64
SLC_TOPN = 16
WINDOW = 512
N_BRANCH = 3
CONV_CH = 512
CONV_WIDTH = 3
LN_EPS = 1e-5
RMS_EPS = 1e-5
NEG_BIG = -1e30
LOG2_E = math.log2(math.e)
EXP2_UNDERFLOW = -150.0
LANES = 128
VMEM_LIMIT = 56 * 1024 * 1024
FFN_TILE = 512

_SEG_NAMES = ("sb_q", "sb_k", "sb_v", "df_q", "df_k", "df_v", "ns_q", "ns_kc", "ns_vc",
              "ns_ks", "ns_vs", "ns_kw", "ns_vw", "ns_g", "cv_b", "cv_c", "cv_h")
_SEG_WIDTHS = (512, 512, 512, 512, 512, 512, 512, 128, 128, 128, 128, 128, 128,
               NSA_HEADS * N_BRANCH, 512, 512, 512)
_SEG_OFF = dict(zip(_SEG_NAMES, np.cumsum((0,) + _SEG_WIDTHS[:-1]).tolist()))
_SEG_W = dict(zip(_SEG_NAMES, _SEG_WIDTHS))


_GATES_PER_KV = NSA_GROUP * N_BRANCH


def _params(sem):
    return pltpu.CompilerParams(dimension_semantics=sem, vmem_limit_bytes=VMEM_LIMIT)


def _layer_norm(y, g, b):
    mu = jnp.mean(y, axis=-1, keepdims=True)
    d = y - mu
    var = jnp.mean(d * d, axis=-1, keepdims=True)
    return d * lax.rsqrt(var + LN_EPS) * g + b


def _dot(a, b):
    return jnp.dot(a, b, preferred_element_type=F32)


def _lane_group_max(s, mx):
    for g in range(s.shape[1] // LANES):
        mx = jnp.maximum(mx, s[:, g * LANES:(g + 1) * LANES])
    return mx


def _softmax_sweep(s_refs, *, rows, acc_lanes, first, q0, tq, big, scores_fn, weigh_fn):
    n_big = (q0 - first) // big if big else 0
    mid = first + n_big * big if big else first
    n_small = (q0 - mid) // tq

    def chunks(n, base, width, fn, init):
        return lax.fori_loop(0, n, lambda c, carry: fn(pl.multiple_of(base + c * width, tq), width, carry), init)

    def scores(start, width, mxs, diagonal=False):
        out = []
        for ref, s, mx in zip(s_refs, scores_fn(start, width, diagonal), mxs):
            ref[:, pl.ds(start, width)] = s
            out.append(_lane_group_max(s, mx))
        return tuple(out)

    mxs = (jnp.full((rows, LANES), NEG_BIG, F32),) * len(s_refs)
    if big:
        mxs = chunks(n_big, first, big, scores, mxs)
    mxs = chunks(n_small, mid, tq, scores, mxs)
    mxs = scores(q0, tq, mxs, True)
    ms = [jnp.broadcast_to(jnp.max(mx, axis=1, keepdims=True), (rows, LANES)) for mx in mxs]

    def weigh(start, width, accs):
        ps = [jnp.exp2(ref[:, pl.ds(start, width)] - jnp.concatenate([m] * (width // LANES), axis=1)).astype(BF16)
              for ref, m in zip(s_refs, ms)]
        return tuple(acc + w for acc, w in zip(accs, weigh_fn(ps, start, width)))

    accs = (jnp.zeros((rows, acc_lanes), F32),) * len(s_refs)
    if big:
        accs = chunks(n_big, first, big, weigh, accs)
    accs = chunks(n_small, mid, tq, weigh, accs)
    return weigh(q0, tq, accs)


def _dot_nt(a, b):
    return lax.dot_general(a, b, (((1,), (1,)), ((), ())), preferred_element_type=F32)


def _ffn_ln_kernel(x_ref, w1_ref, w3_ref, w2_ref, g_ref, b_ref, o_ref, ob_ref, xb_ref, acc_ref,
                   *, alpha):
    j = pl.program_id(1)

    @pl.when(j == 0)
    def _():
        xb_ref[...] = x_ref[...].astype(BF16)
        acc_ref[...] = jnp.zeros_like(acc_ref)

    xb = xb_ref[...]
    a = _dot(xb, w1_ref[...])
    b = _dot(xb, w3_ref[...])
    hm = (a / (1.0 + jnp.exp(-a))) * b
    acc_ref[...] += _dot(hm.astype(BF16), w2_ref[...])

    @pl.when(j == pl.num_programs(1) - 1)
    def _():
        y = alpha * x_ref[...] + 0.5 * acc_ref[...]
        o = _layer_norm(y, g_ref[...], b_ref[...])
        o_ref[...] = o
        ob_ref[...] = o.astype(BF16)


def _ffn_ln(x, w1, w3, w2, g, b, *, layer, slot, alpha, tm, tf):
    n, d = x.shape
    f = w1.shape[-1]
    return pl.pallas_call(
        functools.partial(_ffn_ln_kernel, alpha=alpha),
        grid=(n // tm, f // tf),
        in_specs=[
            pl.BlockSpec((tm, d), lambda i, j: (i, 0)),
            pl.BlockSpec((None, None, d, tf), lambda i, j: (layer, slot, 0, j)),
            pl.BlockSpec((None, None, d, tf), lambda i, j: (layer, slot, 0, j)),
            pl.BlockSpec((None, None, tf, d), lambda i, j: (layer, slot, j, 0)),
            pl.BlockSpec((1, d), lambda i, j: (0, 0)),
            pl.BlockSpec((1, d), lambda i, j: (0, 0)),
        ],
        out_specs=[pl.BlockSpec((tm, d), lambda i, j: (i, 0)),
                   pl.BlockSpec((tm, d), lambda i, j: (i, 0))],
        out_shape=[jax.ShapeDtypeStruct((n, d), F32), jax.ShapeDtypeStruct((n, d), BF16)],
        scratch_shapes=[pltpu.VMEM((tm, d), BF16), pltpu.VMEM((tm, d), F32)],
        compiler_params=_params(("parallel", "arbitrary")),
        name="ffn_ln",
    )(x, w1, w3, w2, g.reshape(1, d), b.reshape(1, d))


def _matmul_kernel(x_ref, w_ref, o_ref):
    o_ref[...] = _dot(x_ref[...], w_ref[...]).astype(o_ref.dtype)


def _matmul(x, w, out_dtype, *, layer, tm, tn, name):
    n, d = x.shape
    m = w.shape[-1]
    return pl.pallas_call(
        _matmul_kernel,
        grid=(n // tm, m // tn),
        in_specs=[pl.BlockSpec((tm, d), lambda i, j: (i, 0)),
                  pl.BlockSpec((None, d, tn), lambda i, j: (layer, 0, j))],
        out_specs=pl.BlockSpec((tm, tn), lambda i, j: (i, j)),
        out_shape=jax.ShapeDtypeStruct((n, m), out_dtype),
        compiler_params=_params(("parallel", "arbitrary")),
        name=name,
    )(x, w)


def _sb_kernel(q_ref, k_ref, v_ref, o_ref, *, tq, pairs):
    i = pl.program_id(2)
    q0 = i * tq
    lane = lax.broadcasted_iota(jnp.int32, (tq, LANES), 1)
    tri = jnp.where(lax.broadcasted_iota(jnp.int32, (tq, tq), 0)
                    >= lax.broadcasted_iota(jnp.int32, (tq, tq), 1), 1.0, 0.0).astype(BF16)
    qpos = q0 + lax.rem(lax.broadcasted_iota(jnp.int32, (2 * tq, 1), 0), tq)
    q2 = []
    for p in range(pairs):
        qs = (q_ref[0, :, p * LANES:(p + 1) * LANES].astype(F32) * (HEAD_DIM ** -0.5)).astype(BF16)
        zero = jnp.zeros_like(qs)
        q2.append(jnp.concatenate([jnp.where(lane < HEAD_DIM, qs, zero),
                                   jnp.where(lane < HEAD_DIM, zero, qs)], axis=0))

    def block(start, state, keep=None):
        width = tq
        masked = keep is not None
        out = []
        for p, (c, acc) in enumerate(state):
            ks = k_ref[0, pl.ds(start, width), p * LANES:(p + 1) * LANES]
            vs = v_ref[0, pl.ds(start, width), p * LANES:(p + 1) * LANES]
            z = _dot_nt(q2[p], ks) * LOG2_E
            nz = -z
            lg = jnp.minimum(nz, 0.0) - jnp.log2(1.0 + jnp.exp2(jnp.minimum(z, nz)))
            if masked:
                lg = jnp.where(keep, lg, 0.0)
            hi = lg.astype(BF16)
            lo = (lg - hi.astype(F32)).astype(BF16)
            suffix = _dot(hi, tri) + _dot(lo, tri)
            a = jnp.exp2(z + suffix + c)
            if masked:
                a = jnp.where(keep, a, 0.0)
            out.append((c + suffix[:, 0:1], acc + _dot(a.astype(BF16), vs)))
        return tuple(out)

    state = ((jnp.zeros((2 * tq, 1), F32), jnp.zeros((2 * tq, LANES), F32)),) * pairs
    key = lax.broadcasted_iota(jnp.int32, (1, tq), 1)
    state = block(pl.multiple_of(q0, tq), state, keep=q0 + key < qpos)
    state = block(pl.multiple_of(jnp.maximum(q0 - tq, 0), tq), state, keep=key < q0)

    def live(state):
        top = functools.reduce(jnp.maximum, [c for c, _ in state])
        return (jnp.max(top) > EXP2_UNDERFLOW).astype(jnp.int32)

    def cond(carry):
        return (carry[0] >= 0) & (carry[1] > 0)

    def body(carry):
        kt, _, state = carry
        state = block(pl.multiple_of(kt * tq, tq), state)
        return kt - 1, live(state), state

    _, _, state = lax.while_loop(cond, body, (i - 2, live(state), state))
    o_ref[0] = jnp.concatenate([jnp.where(lane < HEAD_DIM, acc[:tq], acc[tq:]) for _, acc in state],
                               axis=1).astype(o_ref.dtype)


def _sb_attention(pa, *, tq, pairs, q_blk, k_blk, v_blk):
    b, t, _ = pa.shape
    w = pairs * LANES
    n_groups = SB_HEADS // (2 * pairs)
    return pl.pallas_call(
        functools.partial(_sb_kernel, tq=tq, pairs=pairs),
        grid=(b, n_groups, t // tq),
        in_specs=[
            pl.BlockSpec((1, tq, w), lambda bi, p, i: (bi, i, q_blk + p)),
            pl.BlockSpec((1, t, w), lambda bi, p, i: (bi, 0, k_blk + p)),
            pl.BlockSpec((1, t, w), lambda bi, p, i: (bi, 0, v_blk + p)),
        ],
        out_specs=pl.BlockSpec((1, tq, w), lambda bi, p, i: (bi, i, p)),
        out_shape=jax.ShapeDtypeStruct((b, t, SB_HEADS * HEAD_DIM), BF16),
        compiler_params=_params(("parallel", "parallel", "arbitrary")),
        name="sb_attention",
    )(pa, pa, pa)


def _diff_kernel(lam_ref, gain_ref, q_ref, k_ref, v_ref, o_ref, s_ref, *, tq, heads, lam_init):
    i = pl.program_id(2)
    lane = lax.broadcasted_iota(jnp.int32, (tq, LANES), 1)
    row = lax.broadcasted_iota(jnp.int32, (tq, tq), 0)
    col = lax.broadcasted_iota(jnp.int32, (tq, tq), 1)
    causal = col <= row
    causal2 = jnp.concatenate([causal, causal], axis=0)
    q2 = []
    for h in range(heads):
        qs = (q_ref[0, :, h * LANES:(h + 1) * LANES].astype(F32) * (HEAD_DIM ** -0.5)).astype(BF16)
        zero = jnp.zeros_like(qs)
        q2.append(jnp.concatenate([jnp.where(lane < HEAD_DIM, qs, zero),
                                   jnp.where(lane < HEAD_DIM, zero, qs)], axis=0))

    def scores(start, width, diagonal):
        out = []
        for h in range(heads):
            s = _dot_nt(q2[h], k_ref[0, pl.ds(start, width), h * LANES:(h + 1) * LANES]) * LOG2_E
            out.append(jnp.where(causal2, s, NEG_BIG) if diagonal else s)
        return out

    def weigh(ps, start, width):
        ones = jnp.ones((width, LANES), BF16)
        return [_dot(p, jnp.concatenate([v_ref[0, pl.ds(start, width), h * LANES:(h + 1) * LANES], ones], axis=1))
                for h, p in enumerate(ps)]

    s_refs = [s_ref.at[pl.ds(h * 2 * tq, 2 * tq)] for h in range(heads)]
    accs = _softmax_sweep(s_refs, rows=2 * tq, acc_lanes=2 * LANES, first=0, q0=pl.multiple_of(i * tq, tq), tq=tq,
                          big=2 * tq, scores_fn=scores, weigh_fn=weigh)
    lp = lam_ref[...]
    lam = (jnp.exp(jnp.sum(lp[0:1] * lp[1:2], axis=1, keepdims=True))
           - jnp.exp(jnp.sum(lp[2:3] * lp[3:4], axis=1, keepdims=True)) + lam_init)
    out = []
    for acc in accs:
        o = acc[:, :LANES] / acc[:, LANES:]
        od = o[:tq] - lam * o[tq:]
        od = od * lax.rsqrt(jnp.mean(od * od, axis=-1, keepdims=True) + RMS_EPS)
        out.append(od * gain_ref[...] * (1.0 - lam_init))
    o_ref[0] = jnp.concatenate(out, axis=1).astype(o_ref.dtype)


def _diff_attention(pa, lam_params, gain, *, tq, heads, q_blk, k_blk, v_blk, lam_init):
    b, t, _ = pa.shape
    w = heads * LANES
    return pl.pallas_call(
        functools.partial(_diff_kernel, tq=tq, heads=heads, lam_init=lam_init),
        grid=(b, DIFF_HEADS // heads, t // tq),
        in_specs=[
            pl.BlockSpec((4, HEAD_DIM), lambda bi, h, i: (0, 0)),
            pl.BlockSpec((1, LANES), lambda bi, h, i: (0, 0)),
            pl.BlockSpec((1, tq, w), lambda bi, h, i: (bi, i, q_blk + h)),
            pl.BlockSpec((1, t, w), lambda bi, h, i: (bi, 0, k_blk + h)),
            pl.BlockSpec((1, t, w), lambda bi, h, i: (bi, 0, v_blk + h)),
        ],
        out_specs=pl.BlockSpec((1, tq, w), lambda bi, h, i: (bi, i, h)),
        out_shape=jax.ShapeDtypeStruct((b, t, DIFF_HEADS * LANES), BF16),
        scratch_shapes=[pltpu.VMEM((heads * 2 * tq, t), F32)],
        compiler_params=_params(("parallel", "parallel", "arbitrary")),
        name="diff_attention",
    )(lam_params, gain.reshape(1, LANES), pa, pa, pa)


def _compress_kernel(x_ref, pos_ref, w1_ref, w2_ref, o_ref):
    x = x_ref[0, 0]
    n16, half = x.shape
    xa = (x + pos_ref[0:1]).astype(BF16)
    xb = (x + pos_ref[1:2]).astype(BF16)
    w1 = w1_ref[0]
    first = _dot(xa, w1[:half])
    second = _dot(xb, w1[half:])
    h = first + pltpu.roll(second, n16 - 1, 0)
    g = 0.5 * h * (1.0 + jnp.tanh(math.sqrt(2.0 / math.pi) * (h + 0.044715 * (h * h * h))))
    o_ref[0, 0] = _dot(g.astype(BF16), w2_ref[0])


def _compress(x16, pos2, w1, w2, *, layer):
    b, n_streams, n16, half = x16.shape
    d = w2.shape[-1]
    return pl.pallas_call(
        _compress_kernel,
        grid=(b, n_streams),
        in_specs=[
            pl.BlockSpec((1, 1, n16, half), lambda bi, s: (bi, s, 0, 0)),
            pl.BlockSpec((None, 2, half), lambda bi, s: (layer, 0, 0)),
            pl.BlockSpec((None, 1, 2 * half, w1.shape[-1]), lambda bi, s: (layer, s // NSA_KV_HEADS, 0, 0)),
            pl.BlockSpec((None, 1, w2.shape[2], d), lambda bi, s: (layer, s // NSA_KV_HEADS, 0, 0)),
        ],
        out_specs=pl.BlockSpec((1, 1, n16, d), lambda bi, s: (bi, s, 0, 0)),
        out_shape=jax.ShapeDtypeStruct((b, n_streams, n16, d), F32),
        compiler_params=_params(("parallel", "arbitrary")),
        name="nsa_compress",
    )(x16, pos2, w1, w2)


def _nsa_kernel(q_ref, g_ref, ck_ref, cv_ref, ovt_ref, gx_ref, ks_ref, vs_ref, kw_ref, vw_ref, o_ref,
                kse_ref, kso_ref, vse_ref, vso_ref, kwd_ref, vwe_ref, vwo_ref, ckd_ref, cvd_ref, s_ref,
                *, tq, top_n):
    i = pl.program_id(2)
    grp = NSA_GROUP
    d = HEAD_DIM
    q0 = i * tq
    n_cmp = ck_ref.shape[2]
    n_slc = ovt_ref.shape[0]
    t = ks_ref.shape[2]
    rows = grp * tq
    half = rows // 2

    @pl.when(i == 0)
    def _():
        onehot = jnp.where(lax.broadcasted_iota(jnp.int32, (t, d), 0) // SLC_BLOCK
                           == lax.broadcasted_iota(jnp.int32, (t, d), 1), 1.0, 0.0)
        ones = jnp.ones((t, d), F32)

        def put(ref, left, right):
            ref[...] = jnp.concatenate([left, right], axis=1).astype(BF16)

        ks, vs = ks_ref[0, 0].astype(F32), vs_ref[0, 0].astype(F32)
        kw, vw = kw_ref[0, 0].astype(F32), vw_ref[0, 0].astype(F32)
        put(kse_ref, ks, onehot)
        put(kso_ref, onehot, ks)
        put(vse_ref, vs, ones)
        put(vso_ref, ones, vs)
        put(kwd_ref, kw, kw)
        put(vwe_ref, vw, ones)
        put(vwo_ref, ones, vw)
        put(ckd_ref, ck_ref[0, 0], ck_ref[0, 0])
        put(cvd_ref, cv_ref[0, 0], cv_ref[0, 0])

    lane = lax.broadcasted_iota(jnp.int32, (tq, LANES), 1)
    left = lane < d
    qs = (q_ref[0].astype(F32) * (d ** -0.5)).astype(BF16)
    pairs = [qs[:, :LANES], qs[:, LANES:]]
    zero = jnp.zeros((tq, LANES), BF16)
    qz = jnp.concatenate([jnp.where(left, pairs[0], zero), jnp.where(left, pairs[1], zero),
                          jnp.where(left, zero, pairs[0]), jnp.where(left, zero, pairs[1])], axis=0)
    t4 = q0 + lax.rem(lax.broadcasted_iota(jnp.int32, (rows, 1), 0), tq)

    def rep4(x):
        return jnp.concatenate([x] * grp, axis=0)

    s = _dot_nt(qz, ckd_ref[...])
    cmp_end = CMP_STRIDE * lax.broadcasted_iota(jnp.int32, (1, n_cmp), 1) + (CMP_BLOCK - 1)
    cmask = cmp_end <= t4
    s = jnp.where(cmask, s, NEG_BIG)
    e = jnp.exp(s - jnp.max(s, axis=1, keepdims=True))
    p = e / jnp.sum(e, axis=1, keepdims=True)
    p = jnp.where(cmask, p, 0.0)
    o_cmp = _dot(p.astype(BF16), cvd_ref[...])

    pg = p[0:tq]
    for h in range(1, grp):
        pg = pg + p[h * tq:(h + 1) * tq]
    p_hi = pg.astype(BF16)
    r1 = pg - p_hi.astype(F32)
    p_mid = r1.astype(BF16)
    p_lo = (r1 - p_mid.astype(F32)).astype(BF16)
    ovt = ovt_ref[...]
    imp_t = _dot_nt(ovt, p_hi) + _dot_nt(ovt, p_mid) + _dot_nt(ovt, p_lo)
    blk = lax.broadcasted_iota(jnp.int32, (n_slc, tq), 0)
    cur = (q0 + lax.broadcasted_iota(jnp.int32, (n_slc, tq), 1)) // SLC_BLOCK
    forced = (blk == 0) | (blk == cur) | (blk == cur - 1)
    score = jnp.where(forced, jnp.inf, jnp.where(blk <= cur, imp_t, -jnp.inf))
    rank = jnp.zeros((n_slc, tq), F32)
    for j in range(n_slc):
        sj = score[j:j + 1, :]
        tie = jnp.where(blk > j, 1.0, 0.0)
        rank = rank + jnp.where(sj > score, 1.0, jnp.where(sj == score, tie, 0.0))
    sel_t = jnp.where(rank < top_n, 1.0, 0.0)
    if n_slc < d:
        sel_t = jnp.concatenate([sel_t, jnp.zeros((d - n_slc, tq), F32)], axis=0)
    sel_t2 = jnp.concatenate([sel_t, sel_t], axis=0).astype(BF16)
    eye = jnp.where(lax.broadcasted_iota(jnp.int32, (tq, tq), 0)
                    == lax.broadcasted_iota(jnp.int32, (tq, tq), 1), 1.0, 0.0).astype(BF16)
    sel2 = _dot_nt(eye, sel_t2)
    bias = ((sel2 - 1.0) * (-NEG_BIG)).astype(BF16)
    q_even = jnp.concatenate([jnp.where(left, pairs[0], bias), jnp.where(left, pairs[1], bias)], axis=0)
    q_odd = jnp.concatenate([jnp.where(left, bias, pairs[0]), jnp.where(left, bias, pairs[1])], axis=0)

    row = lax.broadcasted_iota(jnp.int32, (tq, tq), 0)
    col = lax.broadcasted_iota(jnp.int32, (tq, tq), 1)
    causal4 = rep4(col <= row)

    def attend(scores_fn, first, big, ve_ref, vo_ref):
        def weigh(p, start, width):
            return jnp.concatenate([_dot(p[:half], ve_ref[pl.ds(start, width), :]),
                                    _dot(p[half:], vo_ref[pl.ds(start, width), :])], axis=0)

        (acc,) = _softmax_sweep((s_ref,), rows=rows, acc_lanes=LANES, first=first, q0=pl.multiple_of(q0, tq),
                                tq=tq, big=big, scores_fn=lambda *a: (scores_fn(*a),),
                                weigh_fn=lambda ps, start, width: (weigh(ps[0], start, width),))
        out = []
        for pr in range(2):
            even, odd = acc[pr * tq:(pr + 1) * tq], acc[half + pr * tq:half + (pr + 1) * tq]
            den = pltpu.roll(jnp.where(left, odd, even), d, 1)
            out.append(jnp.where(left, even, odd) / den)
        return out

    def slc_scores(start, width, diagonal):
        s = jnp.concatenate([_dot_nt(q_even, kse_ref[pl.ds(start, width), :]),
                             _dot_nt(q_odd, kso_ref[pl.ds(start, width), :])], axis=0) * LOG2_E
        return jnp.where(causal4, s, NEG_BIG) if diagonal else s

    o_slc = attend(slc_scores, 0, 4 * tq, vse_ref, vso_ref)

    def win_scores(start, width, diagonal):
        s = _dot_nt(qz, kwd_ref[pl.ds(start, width), :]) * LOG2_E
        if diagonal:
            keep = causal4
        else:
            keep = start + lax.broadcasted_iota(jnp.int32, (1, width), 1) > t4 - WINDOW
        return jnp.where(keep, s, NEG_BIG)

    o_win = attend(win_scores, jnp.maximum(q0 - (-(-WINDOW // tq)) * tq, 0), None, vwe_ref, vwo_ref)

    gate = 1.0 / (1.0 + jnp.exp(-g_ref[0]))
    g_hi = gate.astype(BF16)
    g_r = gate - g_hi.astype(F32)
    g_mid = g_r.astype(BF16)
    g_lo = (g_r - g_mid.astype(F32)).astype(BF16)
    spread = gx_ref[...]
    gx = _dot(g_hi, spread) + _dot(g_mid, spread) + _dot(g_lo, spread)

    out = []
    for pr in range(2):
        cmp_pr = jnp.where(left, o_cmp[pr * tq:(pr + 1) * tq], o_cmp[half + pr * tq:half + (pr + 1) * tq])
        mixed = 0.0
        for br, o_br in enumerate((cmp_pr, o_slc[pr], o_win[pr])):
            blk_idx = pr * N_BRANCH + br
            mixed = mixed + gx[:, blk_idx * LANES:(blk_idx + 1) * LANES] * o_br
        out.append(mixed)
    o_ref[0] = jnp.concatenate(out, axis=1).astype(o_ref.dtype)


def _nsa_attention(pa, pf, ckv, ovt, kv4, *, tq, q_blk, g_blk, top_n):
    b, t, _ = pa.shape
    n16 = ckv.shape[2]
    n_slc = ovt.shape[0]
    qw = NSA_GROUP * HEAD_DIM
    kv_spec = pl.BlockSpec((1, 1, t, HEAD_DIM), lambda bi, j, i: (bi, j, 0, 0))
    return pl.pallas_call(
        functools.partial(_nsa_kernel, tq=tq, top_n=top_n),
        grid=(b, NSA_KV_HEADS, t // tq),
        in_specs=[
            pl.BlockSpec((1, tq, qw), lambda bi, j, i: (bi, i, q_blk + j)),
            pl.BlockSpec((1, tq, LANES), lambda bi, j, i: (bi, i, g_blk + j)),
            pl.BlockSpec((1, 1, n16, HEAD_DIM), lambda bi, j, i: (bi, j, 0, 0)),
            pl.BlockSpec((1, 1, n16, HEAD_DIM), lambda bi, j, i: (bi, NSA_KV_HEADS + j, 0, 0)),
            pl.BlockSpec((n_slc, n16), lambda bi, j, i: (0, 0)),
            pl.BlockSpec((LANES, 2 * N_BRANCH * LANES), lambda bi, j, i: (0, 0)),
            kv_spec, kv_spec, kv_spec, kv_spec,
        ],
        out_specs=pl.BlockSpec((1, tq, qw), lambda bi, j, i: (bi, i, j)),
        out_shape=jax.ShapeDtypeStruct((b, t, NSA_HEADS * HEAD_DIM), BF16),
        scratch_shapes=[pltpu.VMEM((t, LANES), BF16)] * 7 + [pltpu.VMEM((n16, LANES), BF16)] * 2
                       + [pltpu.VMEM((NSA_GROUP * tq, t), F32)],
        compiler_params=_params(("arbitrary", "arbitrary", "arbitrary")),
        name="nsa_attention",
    )(pa, pf, ckv, ckv, ovt, _gate_spread(), kv4[0], kv4[1], kv4[2], kv4[3])


def _conv_kernel(cb_ref, cc_ref, ch_ref, w_ref, o_ref, prev_ref):
    @pl.when(pl.program_id(1) == 0)
    def _():
        prev_ref[...] = jnp.zeros_like(prev_ref)

    u = cc_ref[0] * ch_ref[0]
    tt = u.shape[0]
    row = lax.broadcasted_iota(jnp.int32, u.shape, 0)
    last1 = prev_ref[7:8]
    last2 = prev_ref[6:7]
    u1 = jnp.where(row >= 1, pltpu.roll(u, 1, 0), last1)
    u2 = jnp.where(row >= 2, pltpu.roll(u, 2, 0), jnp.where(row == 1, last1, last2))
    w = w_ref[...]
    o_ref[0] = (cb_ref[0] * (w[0:1] * u2 + w[1:2] * u1 + w[2:3] * u)).astype(o_ref.dtype)
    prev_ref[...] = u[tt - 8:tt]


def _conv_mixer(pf, conv_w, *, tt):
    b, t, _ = pf.shape
    c = CONV_CH
    return pl.pallas_call(
        _conv_kernel,
        grid=(b, t // tt),
        in_specs=[
            pl.BlockSpec((1, tt, c), lambda bi, i: (bi, i, 0)),
            pl.BlockSpec((1, tt, c), lambda bi, i: (bi, i, 1)),
            pl.BlockSpec((1, tt, c), lambda bi, i: (bi, i, 2)),
            pl.BlockSpec((CONV_WIDTH, c), lambda bi, i: (0, 0)),
        ],
        out_specs=pl.BlockSpec((1, tt, c), lambda bi, i: (bi, i, 0)),
        out_shape=jax.ShapeDtypeStruct((b, t, c), BF16),
        scratch_shapes=[pltpu.VMEM((8, c), F32)],
        compiler_params=_params(("parallel", "arbitrary")),
        name="conv_mixer",
    )(pf, pf, pf, conv_w)


def _out_ln_kernel(a_ref, b_ref, c_ref, d_ref, w_ref, x_ref, g_ref, beta_ref, o_ref, ob_ref, *, alpha):
    kw = a_ref.shape[1]
    mix = _dot(a_ref[...], w_ref[0:kw])
    for n, r in enumerate((b_ref, c_ref, d_ref), start=1):
        mix = mix + _dot(r[...], w_ref[n * kw:(n + 1) * kw])
    o = _layer_norm(alpha * x_ref[...] + mix, g_ref[...], beta_ref[...])
    o_ref[...] = o
    ob_ref[...] = o.astype(BF16)


def _out_ln(parts, w, x, g, b, *, layer, alpha, tm):
    n, d = x.shape
    kw = parts[0].shape[1]
    part_spec = pl.BlockSpec((tm, kw), lambda i: (i, 0))
    return pl.pallas_call(
        functools.partial(_out_ln_kernel, alpha=alpha),
        grid=(n // tm,),
        in_specs=[part_spec, part_spec, part_spec, part_spec,
                  pl.BlockSpec((None,) + w.shape[1:], lambda i: (layer, 0, 0)),
                  pl.BlockSpec((tm, d), lambda i: (i, 0)),
                  pl.BlockSpec((1, d), lambda i: (0, 0)),
                  pl.BlockSpec((1, d), lambda i: (0, 0))],
        out_specs=[pl.BlockSpec((tm, d), lambda i: (i, 0)), pl.BlockSpec((tm, d), lambda i: (i, 0))],
        out_shape=[jax.ShapeDtypeStruct((n, d), F32), jax.ShapeDtypeStruct((n, d), BF16)],
        compiler_params=_params(("parallel",)),
        name="out_ln",
    )(*parts, w, x, g.reshape(1, d), b.reshape(1, d))


def _gate_spread():
    m = np.zeros((LANES, 2 * N_BRANCH * LANES), np.float32)
    for pr in range(2):
        for br in range(N_BRANCH):
            for lane in range(LANES):
                head = 2 * pr + lane // HEAD_DIM
                m[N_BRANCH * head + br, (pr * N_BRANCH + br) * LANES + lane] = 1.0
    return jnp.asarray(m, BF16)


def _overlap_t(t):
    n16 = t // CMP_STRIDE
    n_slc = t // SLC_BLOCK
    c_start = CMP_STRIDE * np.arange(n16)
    j_start = SLC_BLOCK * np.arange(n_slc)
    ov = ((c_start[None, :] < j_start[:, None] + SLC_BLOCK)
          & (c_start[None, :] + CMP_BLOCK > j_start[:, None])).astype(np.float32)
    ov[:, n16 - 1] = 0.0
    return jnp.asarray(ov, BF16)


def _mixer(hf, hb, batch, w_att, w_f32, w_out, ln_g, ln_b, diff_lam, diff_gain, pos2, cmp_w1, cmp_w2,
           conv_w, layer, alpha):
    n, _ = hf.shape
    t = n // batch
    pa = _matmul(hb, w_att, BF16, layer=layer, tm=1024, tn=1024, name="proj_att").reshape(batch, t, -1)
    pf = _matmul(hb, w_f32, F32, layer=layer, tm=1024, tn=1024, name="proj_f32").reshape(batch, t, -1)

    o_sb = _sb_attention(pa, tq=256, pairs=4, q_blk=0, k_blk=1, v_blk=2)
    lam_init = 0.8 - 0.6 * math.exp(-0.3 * layer)
    o_df = _diff_attention(pa, diff_lam[layer], diff_gain[layer], tq=512, heads=2, q_blk=6, k_blk=8, v_blk=10,
                           lam_init=lam_init)

    n16 = t // CMP_STRIDE
    kvc = pf[:, :, 3 * CONV_CH:3 * CONV_CH + 2 * LANES].reshape(batch, t, 2, NSA_KV_HEADS, HEAD_DIM)
    x16 = kvc.transpose(0, 2, 3, 1, 4).reshape(batch, 2 * NSA_KV_HEADS, n16, CMP_STRIDE * HEAD_DIM)
    ckv = _compress(x16, pos2, cmp_w1, cmp_w2, layer=layer)
    kv4 = pa[:, :, 28 * LANES:32 * LANES].reshape(batch, t, 4, NSA_KV_HEADS, HEAD_DIM).transpose(2, 0, 3, 1, 4)
    n_slc = t // SLC_BLOCK
    o_ns = _nsa_attention(pa, pf, ckv, _overlap_t(t), kv4, tq=256, q_blk=12, g_blk=14,
                          top_n=min(SLC_TOPN, n_slc))

    o_cv = _conv_mixer(pf, conv_w[layer], tt=512)
    parts = [o.reshape(n, -1) for o in (o_sb, o_df, o_ns, o_cv)]
    return _out_ln(parts, w_out, hf, ln_g, ln_b, layer=layer, alpha=alpha, tm=512)


def _seg(w_in, name, lo=0, hi=None):
    off = _SEG_OFF[name]
    hi = _SEG_W[name] if hi is None else hi
    return w_in[:, :, off + lo:off + hi]


def kernel(x, ln_g, ln_b, ffn_w1, ffn_w3, ffn_w2, w_in, w_out, diff_lam, diff_gain, cmp_pos, cmp_wk1,
           cmp_wk2, cmp_wv1, cmp_wv2, conv_w):
    batch, t, d = x.shape
    depth = ln_g.shape[0]
    alpha = (2 * depth) ** 0.25
    n = batch * t

    w1b, w3b, w2b = ffn_w1.astype(BF16), ffn_w3.astype(BF16), ffn_w2.astype(BF16)
    w_att = jnp.concatenate([w_in[:, :, :_SEG_OFF["ns_kc"]], w_in[:, :, _SEG_OFF["ns_ks"]:_SEG_OFF["ns_g"]]],
                            axis=-1).astype(BF16)
    gate_pad = jnp.zeros(w_in.shape[:2] + (LANES - _GATES_PER_KV,), w_in.dtype)
    f32_cols = [_seg(w_in, "cv_b"), _seg(w_in, "cv_c"), _seg(w_in, "cv_h"), _seg(w_in, "ns_kc"), _seg(w_in, "ns_vc")]
    for j in range(NSA_KV_HEADS):
        f32_cols += [_seg(w_in, "ns_g", j * _GATES_PER_KV, (j + 1) * _GATES_PER_KV), gate_pad]
    w_f32 = jnp.concatenate(f32_cols, axis=-1).astype(BF16)
    w_outb = w_out.astype(BF16)
    half = CMP_STRIDE * HEAD_DIM
    pos2 = cmp_pos.reshape(depth, 2, half)
    cmp_w1 = jnp.stack([cmp_wk1, cmp_wv1], axis=1).astype(BF16)
    cmp_w2 = jnp.stack([cmp_wk2, cmp_wv2], axis=1).astype(BF16)

    hf = x.reshape(n, d)
    for l in range(depth):
        hf, hb = _ffn_ln(hf, w1b, w3b, w2b, ln_g[l, 0], ln_b[l, 0], layer=l, slot=0, alpha=alpha, tm=512,
                         tf=FFN_TILE)
        hf, _ = _mixer(hf, hb, batch, w_att, w_f32, w_outb, ln_g[l, 1], ln_b[l, 1], diff_lam, diff_gain,
                       pos2, cmp_w1, cmp_w2, conv_w, l, alpha)
        hf, _ = _ffn_ln(hf, w1b, w3b, w2b, ln_g[l, 2], ln_b[l, 2], layer=l, slot=1, alpha=alpha, tm=512,
                        tf=FFN_TILE)
    return hf.reshape(batch, t, d)
```

```python
import functools
import math

import numpy as np
import jax
import jax.numpy as jnp
from jax import lax
from jax.experimental import pallas as pl
from jax.experimental.pallas import tpu as pltpu

F32 = jnp.float32
BF16 = jnp.bfloat16

HEAD_DIM = 64
SB_HEADS = 8
DIFF_HEADS = 4
NSA_HEADS = 8
NSA_KV_HEADS = 2
NSA_GROUP = NSA_HEADS // NSA_KV_HEADS
CMP_BLOCK = 32
CMP_STRIDE = 16
SLC_BLOCK = 64
SLC_TOPN = 16
WINDOW = 512
N_BRANCH = 3
CONV_CH = 512
CONV_WIDTH = 3
LN_EPS = 1e-5
RMS_EPS = 1e-5
NEG_BIG = -1e30
LOG2_E = math.log2(math.e)
EXP2_UNDERFLOW = -150.0
LANES = 128
VMEM_LIMIT = 56 * 1024 * 1024
FFN_TILE = 256
FFN_ROWS = 1024

_SEG_NAMES = ("sb_q", "sb_k", "sb_v", "df_q", "df_k", "df_v", "ns_q", "ns_kc", "ns_vc",
              "ns_ks", "ns_vs", "ns_kw", "ns_vw", "ns_g", "cv_b", "cv_c", "cv_h")
_SEG_WIDTHS = (512, 512, 512, 512, 512, 512, 512, 128, 128, 128, 128, 128, 128,
               NSA_HEADS * N_BRANCH, 512, 512, 512)
_SEG_OFF = dict(zip(_SEG_NAMES, np.cumsum((0,) + _SEG_WIDTHS[:-1]).tolist()))
_SEG_W = dict(zip(_SEG_NAMES, _SEG_WIDTHS))


_GATES_PER_KV = NSA_GROUP * N_BRANCH
ATT_WIDTH = _SEG_OFF["ns_kc"] + 4 * _SEG_W["ns_ks"]


def _params(sem):
    return pltpu.CompilerParams(dimension_semantics=sem, vmem_limit_bytes=VMEM_LIMIT)


def _layer_norm(y, g, b):
    mu = jnp.mean(y, axis=-1, keepdims=True)
    d = y - mu
    var = jnp.mean(d * d, axis=-1, keepdims=True)
    return d * lax.rsqrt(var + LN_EPS) * g + b


def _dot(a, b):
    return jnp.dot(a, b, preferred_element_type=F32)


def _lane_group_max(s, mx):
    for g in range(s.shape[1] // LANES):
        mx = jnp.maximum(mx, s[:, g * LANES:(g + 1) * LANES])
    return mx


def _softmax_sweep(s_refs, *, rows, acc_lanes, first, q0, tq, big, scores_fn, weigh_fn):
    n_big = (q0 - first) // big if big else 0
    mid = first + n_big * big if big else first
    n_small = (q0 - mid) // tq

    def chunks(n, base, width, fn, init):
        return lax.fori_loop(0, n, lambda c, carry: fn(pl.multiple_of(base + c * width, tq), width, carry), init)

    def scores(start, width, mxs, diagonal=False):
        out = []
        for ref, s, mx in zip(s_refs, scores_fn(start, width, diagonal), mxs):
            ref[:, pl.ds(start, width)] = s
            out.append(_lane_group_max(s, mx))
        return tuple(out)

    mxs = (jnp.full((rows, LANES), NEG_BIG, F32),) * len(s_refs)
    if big:
        mxs = chunks(n_big, first, big, scores, mxs)
    mxs = chunks(n_small, mid, tq, scores, mxs)
    mxs = scores(q0, tq, mxs, True)
    ms = [jnp.broadcast_to(jnp.max(mx, axis=1, keepdims=True), (rows, LANES)) for mx in mxs]

    def weigh(start, width, accs):
        ps = [jnp.exp2(ref[:, pl.ds(start, width)] - jnp.concatenate([m] * (width // LANES), axis=1)).astype(BF16)
              for ref, m in zip(s_refs, ms)]
        return tuple(acc + w for acc, w in zip(accs, weigh_fn(ps, start, width)))

    accs = (jnp.zeros((rows, acc_lanes), F32),) * len(s_refs)
    if big:
        accs = chunks(n_big, first, big, weigh, accs)
    accs = chunks(n_small, mid, tq, weigh, accs)
    return weigh(q0, tq, accs)


def _dot_nt(a, b):
    return lax.dot_general(a, b, (((1,), (1,)), ((), ())), preferred_element_type=F32)


def _ffn_ln_kernel(x_ref, w1_ref, w3_ref, w2_ref, g_ref, b_ref, o_ref, xb_ref, *, alpha):
    j = pl.program_id(1)

    @pl.when(j == 0)
    def _():
        xb_ref[...] = x_ref[...].astype(BF16)
        o_ref[...] = jnp.zeros_like(o_ref)

    xb = xb_ref[...]
    a = _dot(xb, w1_ref[...])
    b = _dot(xb, w3_ref[...])
    hm = (a / (1.0 + jnp.exp(-a))) * b
    o_ref[...] += _dot(hm.astype(BF16), w2_ref[...])

    @pl.when(j == pl.num_programs(1) - 1)
    def _():
        y = alpha * x_ref[...] + 0.5 * o_ref[...]
        o_ref[...] = _layer_norm(y, g_ref[...], b_ref[...])


def _ffn_ln(x, w1, w3, w2, g, b, *, layer, slot, alpha, tm, tf):
    n, d = x.shape
    f = w1.shape[-1]
    return pl.pallas_call(
        functools.partial(_ffn_ln_kernel, alpha=alpha),
        grid=(n // tm, f // tf),
        in_specs=[
            pl.BlockSpec((tm, d), lambda i, j: (i, 0)),
            pl.BlockSpec((None, None, d, tf), lambda i, j: (layer, slot, 0, j)),
            pl.BlockSpec((None, None, d, tf), lambda i, j: (layer, slot, 0, j)),
            pl.BlockSpec((None, None, tf, d), lambda i, j: (layer, slot, j, 0)),
            pl.BlockSpec((1, d), lambda i, j: (0, 0)),
            pl.BlockSpec((1, d), lambda i, j: (0, 0)),
        ],
        out_specs=pl.BlockSpec((tm, d), lambda i, j: (i, 0)),
        out_shape=jax.ShapeDtypeStruct((n, d), F32),
        scratch_shapes=[pltpu.VMEM((tm, d), BF16)],
        compiler_params=_params(("parallel", "arbitrary")),
        name="ffn_ln",
    )(x, w1, w3, w2, g.reshape(1, d), b.reshape(1, d))


def _proj_kernel(x_ref, w_ref, oa_ref, of_ref, xb_ref, *, n_a):
    j = pl.program_id(1)

    @pl.when(j == 0)
    def _():
        xb_ref[...] = x_ref[...].astype(BF16)

    y = _dot(xb_ref[...], w_ref[...])

    @pl.when(j < n_a)
    def _():
        oa_ref[...] = y.astype(oa_ref.dtype)

    @pl.when(j >= n_a)
    def _():
        of_ref[...] = y


def _proj(x, w, *, layer, width_a, tm, tn):
    n, d = x.shape
    m = w.shape[-1]
    n_a = width_a // tn
    return pl.pallas_call(
        functools.partial(_proj_kernel, n_a=n_a),
        grid=(n // tm, m // tn),
        in_specs=[pl.BlockSpec((tm, d), lambda i, j: (i, 0)),
                  pl.BlockSpec((None, d, tn), lambda i, j: (layer, 0, j))],
        out_specs=[pl.BlockSpec((tm, tn), lambda i, j: (i, jnp.minimum(j, n_a - 1))),
                   pl.BlockSpec((tm, tn), lambda i, j: (i, jnp.maximum(j - n_a, 0)))],
        out_shape=[jax.ShapeDtypeStruct((n, width_a), BF16), jax.ShapeDtypeStruct((n, m - width_a), F32)],
        scratch_shapes=[pltpu.VMEM((tm, d), BF16)],
        compiler_params=_params(("parallel", "arbitrary")),
        name="proj",
    )(x, w)


def _sb_kernel(q_ref, k_ref, v_ref, o_ref, *, tq, pairs):
    i = pl.program_id(2)
    q0 = i * tq
    lane = lax.broadcasted_iota(jnp.int32, (tq, LANES), 1)
    tri = jnp.where(lax.broadcasted_iota(jnp.int32, (tq, tq), 0)
                    >= lax.broadcasted_iota(jnp.int32, (tq, tq), 1), 1.0, 0.0).astype(BF16)
    qpos = q0 + lax.rem(lax.broadcasted_iota(jnp.int32, (2 * tq, 1), 0), tq)
    q2 = []
    for p in range(pairs):
        qs = (q_ref[0, :, p * LANES:(p + 1) * LANES].astype(F32) * (HEAD_DIM ** -0.5)).astype(BF16)
        zero = jnp.zeros_like(qs)
        q2.append(jnp.concatenate([jnp.where(lane < HEAD_DIM, qs, zero),
                                   jnp.where(lane < HEAD_DIM, zero, qs)], axis=0))

    def block(start, state, keep=None):
        width = tq
        masked = keep is not None
        out = []
        for p, (c, acc) in enumerate(state):
            ks = k_ref[0, pl.ds(start, width), p * LANES:(p + 1) * LANES]
            vs = v_ref[0, pl.ds(start, width), p * LANES:(p + 1) * LANES]
            z = _dot_nt(q2[p], ks) * LOG2_E
            nz = -z
            lg = jnp.minimum(nz, 0.0) - jnp.log2(1.0 + jnp.exp2(jnp.minimum(z, nz)))
            if masked:
                lg = jnp.where(keep, lg, 0.0)
            hi = lg.astype(BF16)
            lo = (lg - hi.astype(F32)).astype(BF16)
            suffix = _dot(hi, tri) + _dot(lo, tri)
            a = jnp.exp2(z + suffix + c)
            if masked:
                a = jnp.where(keep, a, 0.0)
            out.append((c + suffix[:, 0:1], acc + _dot(a.astype(BF16), vs)))
        return tuple(out)

    state = ((jnp.zeros((2 * tq, 1), F32), jnp.zeros((2 * tq, LANES), F32)),) * pairs
    key = lax.broadcasted_iota(jnp.int32, (1, tq), 1)
    state = block(pl.multiple_of(q0, tq), state, keep=q0 + key < qpos)
    state = block(pl.multiple_of(jnp.maximum(q0 - tq, 0), tq), state, keep=key < q0)

    def live(state):
        top = functools.reduce(jnp.maximum, [c for c, _ in state])
        return (jnp.max(top) > EXP2_UNDERFLOW).astype(jnp.int32)

    def cond(carry):
        return (carry[0] >= 0) & (carry[1] > 0)

    def body(carry):
        kt, _, state = carry
        state = block(pl.multiple_of(kt * tq, tq), state)
        return kt - 1, live(state), state

    _, _, state = lax.while_loop(cond, body, (i - 2, live(state), state))
    o_ref[0] = jnp.concatenate([jnp.where(lane < HEAD_DIM, acc[:tq], acc[tq:]) for _, acc in state],
                               axis=1).astype(o_ref.dtype)


def _sb_attention(pa, *, tq, pairs, q_blk, k_blk, v_blk):
    b, t, _ = pa.shape
    w = pairs * LANES
    n_groups = SB_HEADS // (2 * pairs)
    return pl.pallas_call(
        functools.partial(_sb_kernel, tq=tq, pairs=pairs),
        grid=(b, n_groups, t // tq),
        in_specs=[
            pl.BlockSpec((1, tq, w), lambda bi, p, i: (bi, i, q_blk + p)),
            pl.BlockSpec((1, t, w), lambda bi, p, i: (bi, 0, k_blk + p)),
            pl.BlockSpec((1, t, w), lambda bi, p, i: (bi, 0, v_blk + p)),
        ],
        out_specs=pl.BlockSpec((1, tq, w), lambda bi, p, i: (bi, i, p)),
        out_shape=jax.ShapeDtypeStruct((b, t, SB_HEADS * HEAD_DIM), BF16),
        compiler_params=_params(("parallel", "parallel", "arbitrary")),
        name="sb_attention",
    )(pa, pa, pa)


def _diff_kernel(lam_ref, gain_ref, q_ref, k_ref, v_ref, o_ref, s_ref, *, tq, heads, lam_init):
    i = pl.program_id(2)
    lane = lax.broadcasted_iota(jnp.int32, (tq, LANES), 1)
    row = lax.broadcasted_iota(jnp.int32, (tq, tq), 0)
    col = lax.broadcasted_iota(jnp.int32, (tq, tq), 1)
    causal = col <= row
    causal2 = jnp.concatenate([causal, causal], axis=0)
    q2 = []
    for h in range(heads):
        qs = (q_ref[0, :, h * LANES:(h + 1) * LANES].astype(F32) * (HEAD_DIM ** -0.5)).astype(BF16)
        zero = jnp.zeros_like(qs)
        q2.append(jnp.concatenate([jnp.where(lane < HEAD_DIM, qs, zero),
                                   jnp.where(lane < HEAD_DIM, zero, qs)], axis=0))

    def scores(start, width, diagonal):
        out = []
        for h in range(heads):
            s = _dot_nt(q2[h], k_ref[0, pl.ds(start, width), h * LANES:(h + 1) * LANES]) * LOG2_E
            out.append(jnp.where(causal2, s, NEG_BIG) if diagonal else s)
        return out

    def weigh(ps, start, width):
        ones = jnp.ones((width, LANES), BF16)
        return [_dot(p, jnp.concatenate([v_ref[0, pl.ds(start, width), h * LANES:(h + 1) * LANES], ones], axis=1))
                for h, p in enumerate(ps)]

    s_refs = [s_ref.at[pl.ds(h * 2 * tq, 2 * tq)] for h in range(heads)]
    accs = _softmax_sweep(s_refs, rows=2 * tq, acc_lanes=2 * LANES, first=0, q0=pl.multiple_of(i * tq, tq), tq=tq,
                          big=2 * tq, scores_fn=scores, weigh_fn=weigh)
    lp = lam_ref[...]
    lam = (jnp.exp(jnp.sum(lp[0:1] * lp[1:2], axis=1, keepdims=True))
           - jnp.exp(jnp.sum(lp[2:3] * lp[3:4], axis=1, keepdims=True)) + lam_init)
    out = []
    for acc in accs:
        o = acc[:, :LANES] / acc[:, LANES:]
        od = o[:tq] - lam * o[tq:]
        od = od * lax.rsqrt(jnp.mean(od * od, axis=-1, keepdims=True) + RMS_EPS)
        out.append(od * gain_ref[...] * (1.0 - lam_init))
    o_ref[0] = jnp.concatenate(out, axis=1).astype(o_ref.dtype)


def _diff_attention(pa, lam_params, gain, *, tq, heads, q_blk, k_blk, v_blk, lam_init):
    b, t, _ = pa.shape
    w = heads * LANES
    return pl.pallas_call(
        functools.partial(_diff_kernel, tq=tq, heads=heads, lam_init=lam_init),
        grid=(b, DIFF_HEADS // heads, t // tq),
        in_specs=[
            pl.BlockSpec((4, HEAD_DIM), lambda bi, h, i: (0, 0)),
            pl.BlockSpec((1, LANES), lambda bi, h, i: (0, 0)),
            pl.BlockSpec((1, tq, w), lambda bi, h, i: (bi, i, q_blk + h)),
            pl.BlockSpec((1, t, w), lambda bi, h, i: (bi, 0, k_blk + h)),
            pl.BlockSpec((1, t, w), lambda bi, h, i: (bi, 0, v_blk + h)),
        ],
        out_specs=pl.BlockSpec((1, tq, w), lambda bi, h, i: (bi, i, h)),
        out_shape=jax.ShapeDtypeStruct((b, t, DIFF_HEADS * LANES), BF16),
        scratch_shapes=[pltpu.VMEM((heads * 2 * tq, t), F32)],
        compiler_params=_params(("parallel", "parallel", "arbitrary")),
        name="diff_attention",
    )(lam_params, gain.reshape(1, LANES), pa, pa, pa)


def _compress_kernel(x_ref, pos_ref, w1_ref, w2_ref, o_ref):
    x = x_ref[0, 0]
    n16, half = x.shape
    xa = (x + pos_ref[0:1]).astype(BF16)
    xb = (x + pos_ref[1:2]).astype(BF16)
    w1 = w1_ref[0]
    first = _dot(xa, w1[:half])
    second = _dot(xb, w1[half:])
    h = first + pltpu.roll(second, n16 - 1, 0)
    g = 0.5 * h * (1.0 + jnp.tanh(math.sqrt(2.0 / math.pi) * (h + 0.044715 * (h * h * h))))
    o_ref[0, 0] = _dot(g.astype(BF16), w2_ref[0])


def _compress(x16, pos2, w1, w2, *, layer):
    b, n_streams, n16, half = x16.shape
    d = w2.shape[-1]
    return pl.pallas_call(
        _compress_kernel,
        grid=(b, n_streams),
        in_specs=[
            pl.BlockSpec((1, 1, n16, half), lambda bi, s: (bi, s, 0, 0)),
            pl.BlockSpec((None, 2, half), lambda bi, s: (layer, 0, 0)),
            pl.BlockSpec((None, 1, 2 * half, w1.shape[-1]), lambda bi, s: (layer, s // NSA_KV_HEADS, 0, 0)),
            pl.BlockSpec((None, 1, w2.shape[2], d), lambda bi, s: (layer, s // NSA_KV_HEADS, 0, 0)),
        ],
        out_specs=pl.BlockSpec((1, 1, n16, d), lambda bi, s: (bi, s, 0, 0)),
        out_shape=jax.ShapeDtypeStruct((b, n_streams, n16, d), F32),
        compiler_params=_params(("parallel", "arbitrary")),
        name="nsa_compress",
    )(x16, pos2, w1, w2)


def _nsa_kernel(q_ref, g_ref, ck_ref, cv_ref, ovt_ref, gx_ref, ks_ref, vs_ref, kw_ref, vw_ref, o_ref,
                kse_ref, kso_ref, vse_ref, vso_ref, kwd_ref, vwe_ref, vwo_ref, ckd_ref, cvd_ref, s_ref,
                *, tq, top_n):
    i = pl.program_id(2)
    grp = NSA_GROUP
    d = HEAD_DIM
    q0 = i * tq
    n_cmp = ck_ref.shape[2]
    n_slc = ovt_ref.shape[0]
    t = ks_ref.shape[2]
    rows = grp * tq
    half = rows // 2

    @pl.when(i == 0)
    def _():
        onehot = jnp.where(lax.broadcasted_iota(jnp.int32, (t, d), 0) // SLC_BLOCK
                           == lax.broadcasted_iota(jnp.int32, (t, d), 1), 1.0, 0.0)
        ones = jnp.ones((t, d), F32)

        def put(ref, left, right):
            ref[...] = jnp.concatenate([left, right], axis=1).astype(BF16)

        ks, vs = ks_ref[0, 0].astype(F32), vs_ref[0, 0].astype(F32)
        kw, vw = kw_ref[0, 0].astype(F32), vw_ref[0, 0].astype(F32)
        put(kse_ref, ks, onehot)
        put(kso_ref, onehot, ks)
        put(vse_ref, vs, ones)
        put(vso_ref, ones, vs)
        put(kwd_ref, kw, kw)
        put(vwe_ref, vw, ones)
        put(vwo_ref, ones, vw)
        put(ckd_ref, ck_ref[0, 0], ck_ref[0, 0])
        put(cvd_ref, cv_ref[0, 0], cv_ref[0, 0])

    lane = lax.broadcasted_iota(jnp.int32, (tq, LANES), 1)
    left = lane < d
    qs = (q_ref[0].astype(F32) * (d ** -0.5)).astype(BF16)
    pairs = [qs[:, :LANES], qs[:, LANES:]]
    zero = jnp.zeros((tq, LANES), BF16)
    qz = jnp.concatenate([jnp.where(left, pairs[0], zero), jnp.where(left, pairs[1], zero),
                          jnp.where(left, zero, pairs[0]), jnp.where(left, zero, pairs[1])], axis=0)
    t4 = q0 + lax.rem(lax.broadcasted_iota(jnp.int32, (rows, 1), 0), tq)

    def rep4(x):
        return jnp.concatenate([x] * grp, axis=0)

    s = _dot_nt(qz, ckd_ref[...])
    cmp_end = CMP_STRIDE * lax.broadcasted_iota(jnp.int32, (1, n_cmp), 1) + (CMP_BLOCK - 1)
    cmask = cmp_end <= t4
    s = jnp.where(cmask, s, NEG_BIG)
    e = jnp.exp(s - jnp.max(s, axis=1, keepdims=True))
    p = e / jnp.sum(e, axis=1, keepdims=True)
    p = jnp.where(cmask, p, 0.0)
    o_cmp = _dot(p.astype(BF16), cvd_ref[...])

    pg = p[0:tq]
    for h in range(1, grp):
        pg = pg + p[h * tq:(h + 1) * tq]
    p_hi = pg.astype(BF16)
    r1 = pg - p_hi.astype(F32)
    p_mid = r1.astype(BF16)
    p_lo = (r1 - p_mid.astype(F32)).astype(BF16)
    ovt = ovt_ref[...]
    imp_t = _dot_nt(ovt, p_hi) + _dot_nt(ovt, p_mid) + _dot_nt(ovt, p_lo)
    blk = lax.broadcasted_iota(jnp.int32, (n_slc, tq), 0)
    cur = (q0 + lax.broadcasted_iota(jnp.int32, (n_slc, tq), 1)) // SLC_BLOCK
    forced = (blk == 0) | (blk == cur) | (blk == cur - 1)
    score = jnp.where(forced, jnp.inf, jnp.where(blk <= cur, imp_t, -jnp.inf))
    rank = jnp.zeros((n_slc, tq), F32)
    for j in range(n_slc):
        sj = score[j:j + 1, :]
        tie = jnp.where(blk > j, 1.0, 0.0)
        rank = rank + jnp.where(sj > score, 1.0, jnp.where(sj == score, tie, 0.0))
    sel_t = jnp.where(rank < top_n, 1.0, 0.0)
    if n_slc < d:
        sel_t = jnp.concatenate([sel_t, jnp.zeros((d - n_slc, tq), F32)], axis=0)
    sel_t2 = jnp.concatenate([sel_t, sel_t], axis=0).astype(BF16)
    eye = jnp.where(lax.broadcasted_iota(jnp.int32, (tq, tq), 0)
                    == lax.broadcasted_iota(jnp.int32, (tq, tq), 1), 1.0, 0.0).astype(BF16)
    sel2 = _dot_nt(eye, sel_t2)
    bias = ((sel2 - 1.0) * (-NEG_BIG)).astype(BF16)
    q_even = jnp.concatenate([jnp.where(left, pairs[0], bias), jnp.where(left, pairs[1], bias)], axis=0)
    q_odd = jnp.concatenate([jnp.where(left, bias, pairs[0]), jnp.where(left, bias, pairs[1])], axis=0)

    row = lax.broadcasted_iota(jnp.int32, (tq, tq), 0)
    col = lax.broadcasted_iota(jnp.int32, (tq, tq), 1)
    causal4 = rep4(col <= row)

    def attend(scores_fn, first, big, ve_ref, vo_ref):
        def weigh(p, start, width):
            return jnp.concatenate([_dot(p[:half], ve_ref[pl.ds(start, width), :]),
                                    _dot(p[half:], vo_ref[pl.ds(start, width), :])], axis=0)

        (acc,) = _softmax_sweep((s_ref,), rows=rows, acc_lanes=LANES, first=first, q0=pl.multiple_of(q0, tq),
                                tq=tq, big=big, scores_fn=lambda *a: (scores_fn(*a),),
                                weigh_fn=lambda ps, start, width: (weigh(ps[0], start, width),))
        out = []
        for pr in range(2):
            even, odd = acc[pr * tq:(pr + 1) * tq], acc[half + pr * tq:half + (pr + 1) * tq]
            den = pltpu.roll(jnp.where(left, odd, even), d, 1)
            out.append(jnp.where(left, even, odd) / den)
        return out

    def slc_scores(start, width, diagonal):
        s = jnp.concatenate([_dot_nt(q_even, kse_ref[pl.ds(start, width), :]),
                             _dot_nt(q_odd, kso_ref[pl.ds(start, width), :])], axis=0) * LOG2_E
        return jnp.where(causal4, s, NEG_BIG) if diagonal else s

    o_slc = attend(slc_scores, 0, 4 * tq, vse_ref, vso_ref)

    def win_scores(start, width, diagonal):
        s = _dot_nt(qz, kwd_ref[pl.ds(start, width), :]) * LOG2_E
        if diagonal:
            keep = causal4
        else:
            keep = start + lax.broadcasted_iota(jnp.int32, (1, width), 1) > t4 - WINDOW
        return jnp.where(keep, s, NEG_BIG)

    o_win = attend(win_scores, jnp.maximum(q0 - (-(-WINDOW // tq)) * tq, 0), None, vwe_ref, vwo_ref)

    gate = 1.0 / (1.0 + jnp.exp(-g_ref[0]))
    g_hi = gate.astype(BF16)
    g_r = gate - g_hi.astype(F32)
    g_mid = g_r.astype(BF16)
    g_lo = (g_r - g_mid.astype(F32)).astype(BF16)
    spread = gx_ref[...]
    gx = _dot(g_hi, spread) + _dot(g_mid, spread) + _dot(g_lo, spread)

    out = []
    for pr in range(2):
        cmp_pr = jnp.where(left, o_cmp[pr * tq:(pr + 1) * tq], o_cmp[half + pr * tq:half + (pr + 1) * tq])
        mixed = 0.0
        for br, o_br in enumerate((cmp_pr, o_slc[pr], o_win[pr])):
            blk_idx = pr * N_BRANCH + br
            mixed = mixed + gx[:, blk_idx * LANES:(blk_idx + 1) * LANES] * o_br
        out.append(mixed)
    o_ref[0] = jnp.concatenate(out, axis=1).astype(o_ref.dtype)


def _nsa_attention(pa, pf, ckv, ovt, kv4, *, tq, q_blk, g_blk, top_n):
    b, t, _ = pa.shape
    n16 = ckv.shape[2]
    n_slc = ovt.shape[0]
    qw = NSA_GROUP * HEAD_DIM
    kv_spec = pl.BlockSpec((1, 1, t, HEAD_DIM), lambda bi, j, i: (bi, j, 0, 0))
    return pl.pallas_call(
        functools.partial(_nsa_kernel, tq=tq, top_n=top_n),
        grid=(b, NSA_KV_HEADS, t // tq),
        in_specs=[
            pl.BlockSpec((1, tq, qw), lambda bi, j, i: (bi, i, q_blk + j)),
            pl.BlockSpec((1, tq, LANES), lambda bi, j, i: (bi, i, g_blk + j)),
            pl.BlockSpec((1, 1, n16, HEAD_DIM), lambda bi, j, i: (bi, j, 0, 0)),
            pl.BlockSpec((1, 1, n16, HEAD_DIM), lambda bi, j, i: (bi, NSA_KV_HEADS + j, 0, 0)),
            pl.BlockSpec((n_slc, n16), lambda bi, j, i: (0, 0)),
            pl.BlockSpec((LANES, 2 * N_BRANCH * LANES), lambda bi, j, i: (0, 0)),
            kv_spec, kv_spec, kv_spec, kv_spec,
        ],
        out_specs=pl.BlockSpec((1, tq, qw), lambda bi, j, i: (bi, i, j)),
        out_shape=jax.ShapeDtypeStruct((b, t, NSA_HEADS * HEAD_DIM), BF16),
        scratch_shapes=[pltpu.VMEM((t, LANES), BF16)] * 7 + [pltpu.VMEM((n16, LANES), BF16)] * 2
                       + [pltpu.VMEM((NSA_GROUP * tq, t), F32)],
        compiler_params=_params(("arbitrary", "arbitrary", "arbitrary")),
        name="nsa_attention",
    )(pa, pf, ckv, ckv, ovt, _gate_spread(), kv4[0], kv4[1], kv4[2], kv4[3])


def _conv_kernel(cb_ref, cc_ref, ch_ref, w_ref, o_ref, prev_ref):
    @pl.when(pl.program_id(1) == 0)
    def _():
        prev_ref[...] = jnp.zeros_like(prev_ref)

    u = cc_ref[0] * ch_ref[0]
    tt = u.shape[0]
    row = lax.broadcasted_iota(jnp.int32, u.shape, 0)
    last1 = prev_ref[7:8]
    last2 = prev_ref[6:7]
    u1 = jnp.where(row >= 1, pltpu.roll(u, 1, 0), last1)
    u2 = jnp.where(row >= 2, pltpu.roll(u, 2, 0), jnp.where(row == 1, last1, last2))
    w = w_ref[...]
    o_ref[0] = (cb_ref[0] * (w[0:1] * u2 + w[1:2] * u1 + w[2:3] * u)).astype(o_ref.dtype)
    prev_ref[...] = u[tt - 8:tt]


def _conv_mixer(pf, conv_w, *, tt):
    b, t, _ = pf.shape
    c = CONV_CH
    return pl.pallas_call(
        _conv_kernel,
        grid=(b, t // tt),
        in_specs=[
            pl.BlockSpec((1, tt, c), lambda bi, i: (bi, i, 0)),
            pl.BlockSpec((1, tt, c), lambda bi, i: (bi, i, 1)),
            pl.BlockSpec((1, tt, c), lambda bi, i: (bi, i, 2)),
            pl.BlockSpec((CONV_WIDTH, c), lambda bi, i: (0, 0)),
        ],
        out_specs=pl.BlockSpec((1, tt, c), lambda bi, i: (bi, i, 0)),
        out_shape=jax.ShapeDtypeStruct((b, t, c), BF16),
        scratch_shapes=[pltpu.VMEM((8, c), F32)],
        compiler_params=_params(("parallel", "arbitrary")),
        name="conv_mixer",
    )(pf, pf, pf, conv_w)


def _out_ln_kernel(a_ref, b_ref, c_ref, d_ref, w_ref, x_ref, g_ref, beta_ref, o_ref, *, alpha):
    kw = a_ref.shape[1]
    mix = _dot(a_ref[...], w_ref[0:kw])
    for n, r in enumerate((b_ref, c_ref, d_ref), start=1):
        mix = mix + _dot(r[...], w_ref[n * kw:(n + 1) * kw])
    o_ref[...] = _layer_norm(alpha * x_ref[...] + mix, g_ref[...], beta_ref[...])


def _out_ln(parts, w, x, g, b, *, layer, alpha, tm):
    n, d = x.shape
    kw = parts[0].shape[1]
    part_spec = pl.BlockSpec((tm, kw), lambda i: (i, 0))
    return pl.pallas_call(
        functools.partial(_out_ln_kernel, alpha=alpha),
        grid=(n // tm,),
        in_specs=[part_spec, part_spec, part_spec, part_spec,
                  pl.BlockSpec((None,) + w.shape[1:], lambda i: (layer, 0, 0)),
                  pl.BlockSpec((tm, d), lambda i: (i, 0)),
                  pl.BlockSpec((1, d), lambda i: (0, 0)),
                  pl.BlockSpec((1, d), lambda i: (0, 0))],
        out_specs=pl.BlockSpec((tm, d), lambda i: (i, 0)),
        out_shape=jax.ShapeDtypeStruct((n, d), F32),
        compiler_params=_params(("parallel",)),
        name="out_ln",
    )(*parts, w, x, g.reshape(1, d), b.reshape(1, d))


def _gate_spread():
    m = np.zeros((LANES, 2 * N_BRANCH * LANES), np.float32)
    for pr in range(2):
        for br in range(N_BRANCH):
            for lane in range(LANES):
                head = 2 * pr + lane // HEAD_DIM
                m[N_BRANCH * head + br, (pr * N_BRANCH + br) * LANES + lane] = 1.0
    return jnp.asarray(m, BF16)


def _overlap_t(t):
    n16 = t // CMP_STRIDE
    n_slc = t // SLC_BLOCK
    c_start = CMP_STRIDE * np.arange(n16)
    j_start = SLC_BLOCK * np.arange(n_slc)
    ov = ((c_start[None, :] < j_start[:, None] + SLC_BLOCK)
          & (c_start[None, :] + CMP_BLOCK > j_start[:, None])).astype(np.float32)
    ov[:, n16 - 1] = 0.0
    return jnp.asarray(ov, BF16)


def _mixer(hf, batch, w_proj, w_out, ln_g, ln_b, diff_lam, diff_gain, pos2, cmp_w1, cmp_w2,
           conv_w, layer, alpha):
    n, _ = hf.shape
    t = n // batch
    pa, pf = _proj(hf, w_proj, layer=layer, width_a=ATT_WIDTH, tm=1024, tn=1024)
    pa, pf = pa.reshape(batch, t, -1), pf.reshape(batch, t, -1)

    o_sb = _sb_attention(pa, tq=256, pairs=4, q_blk=0, k_blk=1, v_blk=2)
    lam_init = 0.8 - 0.6 * math.exp(-0.3 * layer)
    o_df = _diff_attention(pa, diff_lam[layer], diff_gain[layer], tq=512, heads=2, q_blk=6, k_blk=8, v_blk=10,
                           lam_init=lam_init)

    n16 = t // CMP_STRIDE
    kvc = pf[:, :, 3 * CONV_CH:3 * CONV_CH + 2 * LANES].reshape(batch, t, 2, NSA_KV_HEADS, HEAD_DIM)
    x16 = kvc.transpose(0, 2, 3, 1, 4).reshape(batch, 2 * NSA_KV_HEADS, n16, CMP_STRIDE * HEAD_DIM)
    ckv = _compress(x16, pos2, cmp_w1, cmp_w2, layer=layer)
    kv4 = pa[:, :, 28 * LANES:32 * LANES].reshape(batch, t, 4, NSA_KV_HEADS, HEAD_DIM).transpose(2, 0, 3, 1, 4)
    n_slc = t // SLC_BLOCK
    o_ns = _nsa_attention(pa, pf, ckv, _overlap_t(t), kv4, tq=256, q_blk=12, g_blk=14,
                          top_n=min(SLC_TOPN, n_slc))

    o_cv = _conv_mixer(pf, conv_w[layer], tt=512)
    parts = [o.reshape(n, -1) for o in (o_sb, o_df, o_ns, o_cv)]
    return _out_ln(parts, w_out, hf, ln_g, ln_b, layer=layer, alpha=alpha, tm=512)


def _seg(w_in, name, lo=0, hi=None):
    off = _SEG_OFF[name]
    hi = _SEG_W[name] if hi is None else hi
    return w_in[:, :, off + lo:off + hi]


def kernel(x, ln_g, ln_b, ffn_w1, ffn_w3, ffn_w2, w_in, w_out, diff_lam, diff_gain, cmp_pos, cmp_wk1,
           cmp_wk2, cmp_wv1, cmp_wv2, conv_w):
    batch, t, d = x.shape
    depth = ln_g.shape[0]
    alpha = (2 * depth) ** 0.25
    n = batch * t

    w1b, w3b, w2b = ffn_w1.astype(BF16), ffn_w3.astype(BF16), ffn_w2.astype(BF16)
    gate_pad = jnp.zeros(w_in.shape[:2] + (LANES - _GATES_PER_KV,), w_in.dtype)
    cols = [w_in[:, :, :_SEG_OFF["ns_kc"]], w_in[:, :, _SEG_OFF["ns_ks"]:_SEG_OFF["ns_g"]],
            _seg(w_in, "cv_b"), _seg(w_in, "cv_c"), _seg(w_in, "cv_h"), _seg(w_in, "ns_kc"), _seg(w_in, "ns_vc")]
    for j in range(NSA_KV_HEADS):
        cols += [_seg(w_in, "ns_g", j * _GATES_PER_KV, (j + 1) * _GATES_PER_KV), gate_pad]
    w_proj = jnp.concatenate(cols, axis=-1).astype(BF16)
    w_outb = w_out.astype(BF16)
    half = CMP_STRIDE * HEAD_DIM
    pos2 = cmp_pos.reshape(depth, 2, half)
    cmp_w1 = jnp.stack([cmp_wk1, cmp_wv1], axis=1).astype(BF16)
    cmp_w2 = jnp.stack([cmp_wk2, cmp_wv2], axis=1).astype(BF16)

    hf = x.reshape(n, d)
    for l in range(depth):
        hf = _ffn_ln(hf, w1b, w3b, w2b, ln_g[l, 0], ln_b[l, 0], layer=l, slot=0, alpha=alpha, tm=FFN_ROWS,
                     tf=FFN_TILE)
        hf = _mixer(hf, batch, w_proj, w_outb, ln_g[l, 1], ln_b[l, 1], diff_lam, diff_gain,
                    pos2, cmp_w1, cmp_w2, conv_w, l, alpha)
        hf = _ffn_ln(hf, w1b, w3b, w2b, ln_g[l, 2], ln_b[l, 2], layer=l, slot=1, alpha=alpha, tm=FFN_ROWS,
                     tf=FFN_TILE)
    return hf.reshape(batch, t, d)
```

```python
import functools
import math

import numpy as np
import jax
import jax.numpy as jnp
from jax import lax
from jax.experimental import pallas as pl
from jax.experimental.pallas import tpu as pltpu

F32 = jnp.float32
BF16 = jnp.bfloat16

HEAD_DIM = 64
SB_HEADS = 8
DIFF_HEADS = 4
NSA_HEADS = 8
NSA_KV_HEADS = 2
NSA_GROUP = NSA_HEADS // NSA_KV_HEADS
CMP_BLOCK = 32
CMP_STRIDE = 16
SLC_BLOCK = 64
SLC_TOPN = 16
WINDOW = 512
N_BRANCH = 3
CONV_CH = 512
CONV_WIDTH = 3
LN_EPS = 1e-5
RMS_EPS = 1e-5
NEG_BIG = -1e30
LOG2_E = math.log2(math.e)
EXP2_UNDERFLOW = -150.0
LANES = 128
VMEM_LIMIT = 56 * 1024 * 1024
FFN_TILE = 256
FFN_ROWS = 1024

_SEG_NAMES = ("sb_q", "sb_k", "sb_v", "df_q", "df_k", "df_v", "ns_q", "ns_kc", "ns_vc",
              "ns_ks", "ns_vs", "ns_kw", "ns_vw", "ns_g", "cv_b", "cv_c", "cv_h")
_SEG_WIDTHS = (512, 512, 512, 512, 512, 512, 512, 128, 128, 128, 128, 128, 128,
               NSA_HEADS * N_BRANCH, 512, 512, 512)
_SEG_OFF = dict(zip(_SEG_NAMES, np.cumsum((0,) + _SEG_WIDTHS[:-1]).tolist()))
_SEG_W = dict(zip(_SEG_NAMES, _SEG_WIDTHS))


_GATES_PER_KV = NSA_GROUP * N_BRANCH
ATT_WIDTH = _SEG_OFF["ns_kc"] + 4 * _SEG_W["ns_ks"]


def _params(sem):
    return pltpu.CompilerParams(dimension_semantics=sem, vmem_limit_bytes=VMEM_LIMIT)


def _layer_norm(y, g, b):
    mu = jnp.mean(y, axis=-1, keepdims=True)
    d = y - mu
    var = jnp.mean(d * d, axis=-1, keepdims=True)
    return d * lax.rsqrt(var + LN_EPS) * g + b


def _dot(a, b):
    return jnp.dot(a, b, preferred_element_type=F32)


def _lane_group_max(s, mx):
    for g in range(s.shape[1] // LANES):
        mx = jnp.maximum(mx, s[:, g * LANES:(g + 1) * LANES])
    return mx


def _softmax_sweep(s_refs, *, rows, acc_lanes, first, q0, tq, big, scores_fn, weigh_fn):
    n_big = (q0 - first) // big if big else 0
    mid = first + n_big * big if big else first
    n_small = (q0 - mid) // tq

    def chunks(n, base, width, fn, init):
        return lax.fori_loop(0, n, lambda c, carry: fn(pl.multiple_of(base + c * width, tq), width, carry), init)

    def scores(start, width, mxs, diagonal=False):
        out = []
        for ref, s, mx in zip(s_refs, scores_fn(start, width, diagonal), mxs):
            ref[:, pl.ds(start, width)] = s
            out.append(_lane_group_max(s, mx))
        return tuple(out)

    mxs = (jnp.full((rows, LANES), NEG_BIG, F32),) * len(s_refs)
    if big:
        mxs = chunks(n_big, first, big, scores, mxs)
    mxs = chunks(n_small, mid, tq, scores, mxs)
    mxs = scores(q0, tq, mxs, True)
    ms = [jnp.broadcast_to(jnp.max(mx, axis=1, keepdims=True), (rows, LANES)) for mx in mxs]

    def weigh(start, width, accs):
        ps = [jnp.exp2(ref[:, pl.ds(start, width)] - jnp.concatenate([m] * (width // LANES), axis=1)).astype(BF16)
              for ref, m in zip(s_refs, ms)]
        return tuple(acc + w for acc, w in zip(accs, weigh_fn(ps, start, width)))

    accs = (jnp.zeros((rows, acc_lanes), F32),) * len(s_refs)
    if big:
        accs = chunks(n_big, first, big, weigh, accs)
    accs = chunks(n_small, mid, tq, weigh, accs)
    return weigh(q0, tq, accs)


def _dot_nt(a, b):
    return lax.dot_general(a, b, (((1,), (1,)), ((), ())), preferred_element_type=F32)


def _ffn_ln_kernel(x_ref, w1_ref, w3_ref, w2_ref, g_ref, b_ref, o_ref, xb_ref, *, alpha):
    j = pl.program_id(1)

    @pl.when(j == 0)
    def _():
        xb_ref[...] = x_ref[...].astype(BF16)
        o_ref[...] = jnp.zeros_like(o_ref)

    xb = xb_ref[...]
    a = _dot(xb, w1_ref[...])
    b = _dot(xb, w3_ref[...])
    hm = (a / (1.0 + jnp.exp(-a))) * b
    o_ref[...] += _dot(hm.astype(BF16), w2_ref[...])

    @pl.when(j == pl.num_programs(1) - 1)
    def _():
        y = alpha * x_ref[...] + 0.5 * o_ref[...]
        o_ref[...] = _layer_norm(y, g_ref[...], b_ref[...])


def _ffn_ln(x, w1, w3, w2, g, b, *, layer, slot, alpha, tm, tf):
    n, d = x.shape
    f = w1.shape[-1]
    return pl.pallas_call(
        functools.partial(_ffn_ln_kernel, alpha=alpha),
        grid=(n // tm, f // tf),
        in_specs=[
            pl.BlockSpec((tm, d), lambda i, j: (i, 0)),
            pl.BlockSpec((None, None, d, tf), lambda i, j: (layer, slot, 0, j)),
            pl.BlockSpec((None, None, d, tf), lambda i, j: (layer, slot, 0, j)),
            pl.BlockSpec((None, None, tf, d), lambda i, j: (layer, slot, j, 0)),
            pl.BlockSpec((1, d), lambda i, j: (0, 0)),
            pl.BlockSpec((1, d), lambda i, j: (0, 0)),
        ],
        out_specs=pl.BlockSpec((tm, d), lambda i, j: (i, 0)),
        out_shape=jax.ShapeDtypeStruct((n, d), F32),
        scratch_shapes=[pltpu.VMEM((tm, d), BF16)],
        compiler_params=_params(("parallel", "arbitrary")),
        name="ffn_ln",
    )(x, w1, w3, w2, g.reshape(1, d), b.reshape(1, d))


def _proj_kernel(x_ref, w_ref, oa_ref, of_ref, xb_ref, *, n_a):
    j = pl.program_id(1)

    @pl.when(j == 0)
    def _():
        xb_ref[...] = x_ref[...].astype(BF16)

    @pl.when(j < n_a)
    def _():
        oa_ref[...] = _dot(xb_ref[...], w_ref[...]).astype(oa_ref.dtype)

    @pl.when(j >= n_a)
    def _():
        of_ref[...] = _dot(xb_ref[...], w_ref[...])


def _proj(x, w, *, layer, width_a, tm, tn):
    n, d = x.shape
    m = w.shape[-1]
    n_a = width_a // tn
    return pl.pallas_call(
        functools.partial(_proj_kernel, n_a=n_a),
        grid=(n // tm, m // tn),
        in_specs=[pl.BlockSpec((tm, d), lambda i, j: (i, 0)),
                  pl.BlockSpec((None, d, tn), lambda i, j: (layer, 0, j))],
        out_specs=[pl.BlockSpec((tm, tn), lambda i, j: (i, jnp.minimum(j, n_a - 1))),
                   pl.BlockSpec((tm, tn), lambda i, j: (i, jnp.maximum(j - n_a, 0)))],
        out_shape=[jax.ShapeDtypeStruct((n, width_a), BF16), jax.ShapeDtypeStruct((n, m - width_a), F32)],
        scratch_shapes=[pltpu.VMEM((tm, d), BF16)],
        compiler_params=_params(("parallel", "arbitrary")),
        name="proj",
    )(x, w)


def _sb_kernel(q_ref, k_ref, v_ref, o_ref, *, tq, pairs):
    i = pl.program_id(2)
    q0 = i * tq
    lane = lax.broadcasted_iota(jnp.int32, (tq, LANES), 1)
    tri = jnp.where(lax.broadcasted_iota(jnp.int32, (tq, tq), 0)
                    >= lax.broadcasted_iota(jnp.int32, (tq, tq), 1), 1.0, 0.0).astype(BF16)
    qpos = q0 + lax.rem(lax.broadcasted_iota(jnp.int32, (2 * tq, 1), 0), tq)
    q2 = []
    for p in range(pairs):
        qs = (q_ref[0, :, p * LANES:(p + 1) * LANES].astype(F32) * (HEAD_DIM ** -0.5)).astype(BF16)
        zero = jnp.zeros_like(qs)
        q2.append(jnp.concatenate([jnp.where(lane < HEAD_DIM, qs, zero),
                                   jnp.where(lane < HEAD_DIM, zero, qs)], axis=0))

    def block(start, state, keep=None):
        width = tq
        masked = keep is not None
        out = []
        for p, (c, acc) in enumerate(state):
            ks = k_ref[0, pl.ds(start, width), p * LANES:(p + 1) * LANES]
            vs = v_ref[0, pl.ds(start, width), p * LANES:(p + 1) * LANES]
            z = _dot_nt(q2[p], ks) * LOG2_E
            nz = -z
            lg = jnp.minimum(nz, 0.0) - jnp.log2(1.0 + jnp.exp2(jnp.minimum(z, nz)))
            if masked:
                lg = jnp.where(keep, lg, 0.0)
            hi = lg.astype(BF16)
            lo = (lg - hi.astype(F32)).astype(BF16)
            suffix = _dot(hi, tri) + _dot(lo, tri)
            a = jnp.exp2(z + suffix + c)
            if masked:
                a = jnp.where(keep, a, 0.0)
            out.append((c + suffix[:, 0:1], acc + _dot(a.astype(BF16), vs)))
        return tuple(out)

    state = ((jnp.zeros((2 * tq, 1), F32), jnp.zeros((2 * tq, LANES), F32)),) * pairs
    key = lax.broadcasted_iota(jnp.int32, (1, tq), 1)
    state = block(pl.multiple_of(q0, tq), state, keep=q0 + key < qpos)
    state = block(pl.multiple_of(jnp.maximum(q0 - tq, 0), tq), state, keep=key < q0)

    def live(state):
        top = functools.reduce(jnp.maximum, [c for c, _ in state])
        return (jnp.max(top) > EXP2_UNDERFLOW).astype(jnp.int32)

    def cond(carry):
        return (carry[0] >= 0) & (carry[1] > 0)

    def body(carry):
        kt, _, state = carry
        state = block(pl.multiple_of(kt * tq, tq), state)
        return kt - 1, live(state), state

    _, _, state = lax.while_loop(cond, body, (i - 2, live(state), state))
    o_ref[0] = jnp.concatenate([jnp.where(lane < HEAD_DIM, acc[:tq], acc[tq:]) for _, acc in state],
                               axis=1).astype(o_ref.dtype)


def _sb_attention(pa, *, tq, pairs, q_blk, k_blk, v_blk):
    b, t, _ = pa.shape
    w = pairs * LANES
    n_groups = SB_HEADS // (2 * pairs)
    return pl.pallas_call(
        functools.partial(_sb_kernel, tq=tq, pairs=pairs),
        grid=(b, n_groups, t // tq),
        in_specs=[
            pl.BlockSpec((1, tq, w), lambda bi, p, i: (bi, i, q_blk + p)),
            pl.BlockSpec((1, t, w), lambda bi, p, i: (bi, 0, k_blk + p)),
            pl.BlockSpec((1, t, w), lambda bi, p, i: (bi, 0, v_blk + p)),
        ],
        out_specs=pl.BlockSpec((1, tq, w), lambda bi, p, i: (bi, i, p)),
        out_shape=jax.ShapeDtypeStruct((b, t, SB_HEADS * HEAD_DIM), BF16),
        compiler_params=_params(("parallel", "parallel", "arbitrary")),
        name="sb_attention",
    )(pa, pa, pa)


def _diff_kernel(lam_ref, gain_ref, q_ref, k_ref, v_ref, o_ref, s_ref, *, tq, heads, lam_init):
    i = pl.program_id(2)
    lane = lax.broadcasted_iota(jnp.int32, (tq, LANES), 1)
    row = lax.broadcasted_iota(jnp.int32, (tq, tq), 0)
    col = lax.broadcasted_iota(jnp.int32, (tq, tq), 1)
    causal = col <= row
    causal2 = jnp.concatenate([causal, causal], axis=0)
    q2 = []
    for h in range(heads):
        qs = (q_ref[0, :, h * LANES:(h + 1) * LANES].astype(F32) * (HEAD_DIM ** -0.5)).astype(BF16)
        zero = jnp.zeros_like(qs)
        q2.append(jnp.concatenate([jnp.where(lane < HEAD_DIM, qs, zero),
                                   jnp.where(lane < HEAD_DIM, zero, qs)], axis=0))

    def scores(start, width, diagonal):
        out = []
        for h in range(heads):
            s = _dot_nt(q2[h], k_ref[0, pl.ds(start, width), h * LANES:(h + 1) * LANES]) * LOG2_E
            out.append(jnp.where(causal2, s, NEG_BIG) if diagonal else s)
        return out

    def weigh(ps, start, width):
        ones = jnp.ones((width, LANES), BF16)
        return [_dot(p, jnp.concatenate([v_ref[0, pl.ds(start, width), h * LANES:(h + 1) * LANES], ones], axis=1))
                for h, p in enumerate(ps)]

    s_refs = [s_ref.at[pl.ds(h * 2 * tq, 2 * tq)] for h in range(heads)]
    accs = _softmax_sweep(s_refs, rows=2 * tq, acc_lanes=2 * LANES, first=0, q0=pl.multiple_of(i * tq, tq), tq=tq,
                          big=2 * tq, scores_fn=scores, weigh_fn=weigh)
    lp = lam_ref[...]
    lam = (jnp.exp(jnp.sum(lp[0:1] * lp[1:2], axis=1, keepdims=True))
           - jnp.exp(jnp.sum(lp[2:3] * lp[3:4], axis=1, keepdims=True)) + lam_init)
    out = []
    for acc in accs:
        o = acc[:, :LANES] / acc[:, LANES:]
        od = o[:tq] - lam * o[tq:]
        od = od * lax.rsqrt(jnp.mean(od * od, axis=-1, keepdims=True) + RMS_EPS)
        out.append(od * gain_ref[...] * (1.0 - lam_init))
    o_ref[0] = jnp.concatenate(out, axis=1).astype(o_ref.dtype)


def _diff_attention(pa, lam_params, gain, *, tq, heads, q_blk, k_blk, v_blk, lam_init):
    b, t, _ = pa.shape
    w = heads * LANES
    return pl.pallas_call(
        functools.partial(_diff_kernel, tq=tq, heads=heads, lam_init=lam_init),
        grid=(b, DIFF_HEADS // heads, t // tq),
        in_specs=[
            pl.BlockSpec((4, HEAD_DIM), lambda bi, h, i: (0, 0)),
            pl.BlockSpec((1, LANES), lambda bi, h, i: (0, 0)),
            pl.BlockSpec((1, tq, w), lambda bi, h, i: (bi, i, q_blk + h)),
            pl.BlockSpec((1, t, w), lambda bi, h, i: (bi, 0, k_blk + h)),
            pl.BlockSpec((1, t, w), lambda bi, h, i: (bi, 0, v_blk + h)),
        ],
        out_specs=pl.BlockSpec((1, tq, w), lambda bi, h, i: (bi, i, h)),
        out_shape=jax.ShapeDtypeStruct((b, t, DIFF_HEADS * LANES), BF16),
        scratch_shapes=[pltpu.VMEM((heads * 2 * tq, t), F32)],
        compiler_params=_params(("parallel", "parallel", "arbitrary")),
        name="diff_attention",
    )(lam_params, gain.reshape(1, LANES), pa, pa, pa)


def _compress_kernel(x_ref, pos_ref, w1_ref, w2_ref, o_ref):
    x = x_ref[0, 0]
    n16, half = x.shape
    xa = (x + pos_ref[0:1]).astype(BF16)
    xb = (x + pos_ref[1:2]).astype(BF16)
    w1 = w1_ref[0]
    first = _dot(xa, w1[:half])
    second = _dot(xb, w1[half:])
    h = first + pltpu.roll(second, n16 - 1, 0)
    g = 0.5 * h * (1.0 + jnp.tanh(math.sqrt(2.0 / math.pi) * (h + 0.044715 * (h * h * h))))
    o_ref[0, 0] = _dot(g.astype(BF16), w2_ref[0])


def _compress(x16, pos2, w1, w2, *, layer):
    b, n_streams, n16, half = x16.shape
    d = w2.shape[-1]
    return pl.pallas_call(
        _compress_kernel,
        grid=(b, n_streams),
        in_specs=[
            pl.BlockSpec((1, 1, n16, half), lambda bi, s: (bi, s, 0, 0)),
            pl.BlockSpec((None, 2, half), lambda bi, s: (layer, 0, 0)),
            pl.BlockSpec((None, 1, 2 * half, w1.shape[-1]), lambda bi, s: (layer, s // NSA_KV_HEADS, 0, 0)),
            pl.BlockSpec((None, 1, w2.shape[2], d), lambda bi, s: (layer, s // NSA_KV_HEADS, 0, 0)),
        ],
        out_specs=pl.BlockSpec((1, 1, n16, d), lambda bi, s: (bi, s, 0, 0)),
        out_shape=jax.ShapeDtypeStruct((b, n_streams, n16, d), F32),
        compiler_params=_params(("parallel", "arbitrary")),
        name="nsa_compress",
    )(x16, pos2, w1, w2)


def _nsa_kernel(q_ref, g_ref, ck_ref, cv_ref, ovt_ref, gx_ref, ks_ref, vs_ref, kw_ref, vw_ref, o_ref,
                kse_ref, kso_ref, vse_ref, vso_ref, kwd_ref, vwe_ref, vwo_ref, ckd_ref, cvd_ref, s_ref,
                *, tq, top_n):
    i = pl.program_id(2)
    grp = NSA_GROUP
    d = HEAD_DIM
    q0 = i * tq
    n_cmp = ck_ref.shape[2]
    n_slc = ovt_ref.shape[0]
    t = ks_ref.shape[1]
    rows = grp * tq
    half = rows // 2

    @pl.when(i == 0)
    def _():
        onehot = jnp.where(lax.broadcasted_iota(jnp.int32, (t, d), 0) // SLC_BLOCK
                           == lax.broadcasted_iota(jnp.int32, (t, d), 1), 1.0, 0.0)
        ones = jnp.ones((t, d), F32)

        def put(ref, left, right):
            ref[...] = jnp.concatenate([left, right], axis=1).astype(BF16)

        first_kv = pl.program_id(1) == 0

        def own_head(ref):
            both = ref[0].astype(F32)
            return jnp.where(first_kv, both[:, :d], both[:, d:])

        ks, vs, kw, vw = own_head(ks_ref), own_head(vs_ref), own_head(kw_ref), own_head(vw_ref)
        put(kse_ref, ks, onehot)
        put(kso_ref, onehot, ks)
        put(vse_ref, vs, ones)
        put(vso_ref, ones, vs)
        put(kwd_ref, kw, kw)
        put(vwe_ref, vw, ones)
        put(vwo_ref, ones, vw)
        put(ckd_ref, ck_ref[0, 0], ck_ref[0, 0])
        put(cvd_ref, cv_ref[0, 0], cv_ref[0, 0])

    lane = lax.broadcasted_iota(jnp.int32, (tq, LANES), 1)
    left = lane < d
    qs = (q_ref[0].astype(F32) * (d ** -0.5)).astype(BF16)
    pairs = [qs[:, :LANES], qs[:, LANES:]]
    zero = jnp.zeros((tq, LANES), BF16)
    qz = jnp.concatenate([jnp.where(left, pairs[0], zero), jnp.where(left, pairs[1], zero),
                          jnp.where(left, zero, pairs[0]), jnp.where(left, zero, pairs[1])], axis=0)
    t4 = q0 + lax.rem(lax.broadcasted_iota(jnp.int32, (rows, 1), 0), tq)

    def rep4(x):
        return jnp.concatenate([x] * grp, axis=0)

    s = _dot_nt(qz, ckd_ref[...])
    cmp_end = CMP_STRIDE * lax.broadcasted_iota(jnp.int32, (1, n_cmp), 1) + (CMP_BLOCK - 1)
    cmask = cmp_end <= t4
    s = jnp.where(cmask, s, NEG_BIG)
    e = jnp.exp(s - jnp.max(s, axis=1, keepdims=True))
    p = e / jnp.sum(e, axis=1, keepdims=True)
    p = jnp.where(cmask, p, 0.0)
    o_cmp = _dot(p.astype(BF16), cvd_ref[...])

    pg = p[0:tq]
    for h in range(1, grp):
        pg = pg + p[h * tq:(h + 1) * tq]
    p_hi = pg.astype(BF16)
    r1 = pg - p_hi.astype(F32)
    p_mid = r1.astype(BF16)
    p_lo = (r1 - p_mid.astype(F32)).astype(BF16)
    ovt = ovt_ref[...]
    imp_t = _dot_nt(ovt, p_hi) + _dot_nt(ovt, p_mid) + _dot_nt(ovt, p_lo)
    blk = lax.broadcasted_iota(jnp.int32, (n_slc, tq), 0)
    cur = (q0 + lax.broadcasted_iota(jnp.int32, (n_slc, tq), 1)) // SLC_BLOCK
    forced = (blk == 0) | (blk == cur) | (blk == cur - 1)
    score = jnp.where(forced, jnp.inf, jnp.where(blk <= cur, imp_t, -jnp.inf))
    rank = jnp.zeros((n_slc, tq), F32)
    for j in range(n_slc):
        sj = score[j:j + 1, :]
        tie = jnp.where(blk > j, 1.0, 0.0)
        rank = rank + jnp.where(sj > score, 1.0, jnp.where(sj == score, tie, 0.0))
    sel_t = jnp.where(rank < top_n, 1.0, 0.0)
    if n_slc < d:
        sel_t = jnp.concatenate([sel_t, jnp.zeros((d - n_slc, tq), F32)], axis=0)
    sel_t2 = jnp.concatenate([sel_t, sel_t], axis=0).astype(BF16)
    eye = jnp.where(lax.broadcasted_iota(jnp.int32, (tq, tq), 0)
                    == lax.broadcasted_iota(jnp.int32, (tq, tq), 1), 1.0, 0.0).astype(BF16)
    sel2 = _dot_nt(eye, sel_t2)
    bias = ((sel2 - 1.0) * (-NEG_BIG)).astype(BF16)
    q_even = jnp.concatenate([jnp.where(left, pairs[0], bias), jnp.where(left, pairs[1], bias)], axis=0)
    q_odd = jnp.concatenate([jnp.where(left, bias, pairs[0]), jnp.where(left, bias, pairs[1])], axis=0)

    row = lax.broadcasted_iota(jnp.int32, (tq, tq), 0)
    col = lax.broadcasted_iota(jnp.int32, (tq, tq), 1)
    causal4 = rep4(col <= row)

    def attend(scores_fn, first, big, ve_ref, vo_ref):
        def weigh(p, start, width):
            return jnp.concatenate([_dot(p[:half], ve_ref[pl.ds(start, width), :]),
                                    _dot(p[half:], vo_ref[pl.ds(start, width), :])], axis=0)

        (acc,) = _softmax_sweep((s_ref,), rows=rows, acc_lanes=LANES, first=first, q0=pl.multiple_of(q0, tq),
                                tq=tq, big=big, scores_fn=lambda *a: (scores_fn(*a),),
                                weigh_fn=lambda ps, start, width: (weigh(ps[0], start, width),))
        out = []
        for pr in range(2):
            even, odd = acc[pr * tq:(pr + 1) * tq], acc[half + pr * tq:half + (pr + 1) * tq]
            den = pltpu.roll(jnp.where(left, odd, even), d, 1)
            out.append(jnp.where(left, even, odd) / den)
        return out

    def slc_scores(start, width, diagonal):
        s = jnp.concatenate([_dot_nt(q_even, kse_ref[pl.ds(start, width), :]),
                             _dot_nt(q_odd, kso_ref[pl.ds(start, width), :])], axis=0) * LOG2_E
        return jnp.where(causal4, s, NEG_BIG) if diagonal else s

    o_slc = attend(slc_scores, 0, 4 * tq, vse_ref, vso_ref)

    def win_scores(start, width, diagonal):
        s = _dot_nt(qz, kwd_ref[pl.ds(start, width), :]) * LOG2_E
        if diagonal:
            keep = causal4
        else:
            keep = start + lax.broadcasted_iota(jnp.int32, (1, width), 1) > t4 - WINDOW
        return jnp.where(keep, s, NEG_BIG)

    o_win = attend(win_scores, jnp.maximum(q0 - (-(-WINDOW // tq)) * tq, 0), None, vwe_ref, vwo_ref)

    gate = 1.0 / (1.0 + jnp.exp(-g_ref[0]))
    g_hi = gate.astype(BF16)
    g_r = gate - g_hi.astype(F32)
    g_mid = g_r.astype(BF16)
    g_lo = (g_r - g_mid.astype(F32)).astype(BF16)
    spread = gx_ref[...]
    gx = _dot(g_hi, spread) + _dot(g_mid, spread) + _dot(g_lo, spread)

    out = []
    for pr in range(2):
        cmp_pr = jnp.where(left, o_cmp[pr * tq:(pr + 1) * tq], o_cmp[half + pr * tq:half + (pr + 1) * tq])
        mixed = 0.0
        for br, o_br in enumerate((cmp_pr, o_slc[pr], o_win[pr])):
            blk_idx = pr * N_BRANCH + br
            mixed = mixed + gx[:, blk_idx * LANES:(blk_idx + 1) * LANES] * o_br
        out.append(mixed)
    o_ref[0] = jnp.concatenate(out, axis=1).astype(o_ref.dtype)


def _nsa_attention(pa, pf, ckv, ovt, *, tq, q_blk, kv_blk, g_blk, top_n):
    b, t, _ = pa.shape
    n16 = ckv.shape[2]
    n_slc = ovt.shape[0]
    qw = NSA_GROUP * HEAD_DIM
    kv_specs = [pl.BlockSpec((1, t, LANES), functools.partial(lambda bi, j, i, c: (bi, 0, c), c=kv_blk + n))
                for n in range(4)]
    return pl.pallas_call(
        functools.partial(_nsa_kernel, tq=tq, top_n=top_n),
        grid=(b, NSA_KV_HEADS, t // tq),
        in_specs=[
            pl.BlockSpec((1, tq, qw), lambda bi, j, i: (bi, i, q_blk + j)),
            pl.BlockSpec((1, tq, LANES), lambda bi, j, i: (bi, i, g_blk + j)),
            pl.BlockSpec((1, 1, n16, HEAD_DIM), lambda bi, j, i: (bi, j, 0, 0)),
            pl.BlockSpec((1, 1, n16, HEAD_DIM), lambda bi, j, i: (bi, NSA_KV_HEADS + j, 0, 0)),
            pl.BlockSpec((n_slc, n16), lambda bi, j, i: (0, 0)),
            pl.BlockSpec((LANES, 2 * N_BRANCH * LANES), lambda bi, j, i: (0, 0)),
            *kv_specs,
        ],
        out_specs=pl.BlockSpec((1, tq, qw), lambda bi, j, i: (bi, i, j)),
        out_shape=jax.ShapeDtypeStruct((b, t, NSA_HEADS * HEAD_DIM), BF16),
        scratch_shapes=[pltpu.VMEM((t, LANES), BF16)] * 7 + [pltpu.VMEM((n16, LANES), BF16)] * 2
                       + [pltpu.VMEM((NSA_GROUP * tq, t), F32)],
        compiler_params=_params(("arbitrary", "arbitrary", "arbitrary")),
        name="nsa_attention",
    )(pa, pf, ckv, ckv, ovt, _gate_spread(), pa, pa, pa, pa)


def _conv_kernel(cb_ref, cc_ref, ch_ref, w_ref, o_ref, prev_ref):
    @pl.when(pl.program_id(1) == 0)
    def _():
        prev_ref[...] = jnp.zeros_like(prev_ref)

    u = cc_ref[0] * ch_ref[0]
    tt = u.shape[0]
    row = lax.broadcasted_iota(jnp.int32, u.shape, 0)
    last1 = prev_ref[7:8]
    last2 = prev_ref[6:7]
    u1 = jnp.where(row >= 1, pltpu.roll(u, 1, 0), last1)
    u2 = jnp.where(row >= 2, pltpu.roll(u, 2, 0), jnp.where(row == 1, last1, last2))
    w = w_ref[...]
    o_ref[0] = (cb_ref[0] * (w[0:1] * u2 + w[1:2] * u1 + w[2:3] * u)).astype(o_ref.dtype)
    prev_ref[...] = u[tt - 8:tt]


def _conv_mixer(pf, conv_w, *, tt):
    b, t, _ = pf.shape
    c = CONV_CH
    return pl.pallas_call(
        _conv_kernel,
        grid=(b, t // tt),
        in_specs=[
            pl.BlockSpec((1, tt, c), lambda bi, i: (bi, i, 0)),
            pl.BlockSpec((1, tt, c), lambda bi, i: (bi, i, 1)),
            pl.BlockSpec((1, tt, c), lambda bi, i: (bi, i, 2)),
            pl.BlockSpec((CONV_WIDTH, c), lambda bi, i: (0, 0)),
        ],
        out_specs=pl.BlockSpec((1, tt, c), lambda bi, i: (bi, i, 0)),
        out_shape=jax.ShapeDtypeStruct((b, t, c), BF16),
        scratch_shapes=[pltpu.VMEM((8, c), F32)],
        compiler_params=_params(("parallel", "arbitrary")),
        name="conv_mixer",
    )(pf, pf, pf, conv_w)


def _out_ln_kernel(a_ref, b_ref, c_ref, d_ref, w_ref, x_ref, g_ref, beta_ref, o_ref, *, alpha):
    kw = a_ref.shape[1]
    mix = _dot(a_ref[...], w_ref[0:kw])
    for n, r in enumerate((b_ref, c_ref, d_ref), start=1):
        mix = mix + _dot(r[...], w_ref[n * kw:(n + 1) * kw])
    o_ref[...] = _layer_norm(alpha * x_ref[...] + mix, g_ref[...], beta_ref[...])


def _out_ln(parts, w, x, g, b, *, layer, alpha, tm):
    n, d = x.shape
    kw = parts[0].shape[1]
    part_spec = pl.BlockSpec((tm, kw), lambda i: (i, 0))
    return pl.pallas_call(
        functools.partial(_out_ln_kernel, alpha=alpha),
        grid=(n // tm,),
        in_specs=[part_spec, part_spec, part_spec, part_spec,
                  pl.BlockSpec((None,) + w.shape[1:], lambda i: (layer, 0, 0)),
                  pl.BlockSpec((tm, d), lambda i: (i, 0)),
                  pl.BlockSpec((1, d), lambda i: (0, 0)),
                  pl.BlockSpec((1, d), lambda i: (0, 0))],
        out_specs=pl.BlockSpec((tm, d), lambda i: (i, 0)),
        out_shape=jax.ShapeDtypeStruct((n, d), F32),
        compiler_params=_params(("parallel",)),
        name="out_ln",
    )(*parts, w, x, g.reshape(1, d), b.reshape(1, d))


def _gate_spread():
    m = np.zeros((LANES, 2 * N_BRANCH * LANES), np.float32)
    for pr in range(2):
        for br in range(N_BRANCH):
            for lane in range(LANES):
                head = 2 * pr + lane // HEAD_DIM
                m[N_BRANCH * head + br, (pr * N_BRANCH + br) * LANES + lane] = 1.0
    return jnp.asarray(m, BF16)


def _overlap_t(t):
    n16 = t // CMP_STRIDE
    n_slc = t // SLC_BLOCK
    c_start = CMP_STRIDE * np.arange(n16)
    j_start = SLC_BLOCK * np.arange(n_slc)
    ov = ((c_start[None, :] < j_start[:, None] + SLC_BLOCK)
          & (c_start[None, :] + CMP_BLOCK > j_start[:, None])).astype(np.float32)
    ov[:, n16 - 1] = 0.0
    return jnp.asarray(ov, BF16)


def _mixer(hf, batch, w_proj, w_out, ln_g, ln_b, diff_lam, diff_gain, pos2, cmp_w1, cmp_w2,
           conv_w, layer, alpha):
    n, _ = hf.shape
    t = n // batch
    pa, pf = _proj(hf, w_proj, layer=layer, width_a=ATT_WIDTH, tm=1024, tn=1024)
    pa, pf = pa.reshape(batch, t, -1), pf.reshape(batch, t, -1)

    o_sb = _sb_attention(pa, tq=256, pairs=4, q_blk=0, k_blk=1, v_blk=2)
    lam_init = 0.8 - 0.6 * math.exp(-0.3 * layer)
    o_df = _diff_attention(pa, diff_lam[layer], diff_gain[layer], tq=512, heads=2, q_blk=6, k_blk=8, v_blk=10,
                           lam_init=lam_init)

    n16 = t // CMP_STRIDE
    kvc = pf[:, :, 3 * CONV_CH:3 * CONV_CH + 2 * LANES].reshape(batch, t, 2, NSA_KV_HEADS, HEAD_DIM)
    x16 = kvc.transpose(0, 2, 3, 1, 4).reshape(batch, 2 * NSA_KV_HEADS, n16, CMP_STRIDE * HEAD_DIM)
    ckv = _compress(x16, pos2, cmp_w1, cmp_w2, layer=layer)
    n_slc = t // SLC_BLOCK
    o_ns = _nsa_attention(pa, pf, ckv, _overlap_t(t), tq=256, q_blk=12, kv_blk=28, g_blk=14,
                          top_n=min(SLC_TOPN, n_slc))

    o_cv = _conv_mixer(pf, conv_w[layer], tt=512)
    parts = [o.reshape(n, -1) for o in (o_sb, o_df, o_ns, o_cv)]
    return _out_ln(parts, w_out, hf, ln_g, ln_b, layer=layer, alpha=alpha, tm=512)


def _seg(w_in, name, lo=0, hi=None):
    off = _SEG_OFF[name]
    hi = _SEG_W[name] if hi is None else hi
    return w_in[:, :, off + lo:off + hi]


def kernel(x, ln_g, ln_b, ffn_w1, ffn_w3, ffn_w2, w_in, w_out, diff_lam, diff_gain, cmp_pos, cmp_wk1,
           cmp_wk2, cmp_wv1, cmp_wv2, conv_w):
    batch, t, d = x.shape
    depth = ln_g.shape[0]
    alpha = (2 * depth) ** 0.25
    n = batch * t

    w1b, w3b, w2b = ffn_w1.astype(BF16), ffn_w3.astype(BF16), ffn_w2.astype(BF16)
    gate_pad = jnp.zeros(w_in.shape[:2] + (LANES - _GATES_PER_KV,), w_in.dtype)
    cols = [w_in[:, :, :_SEG_OFF["ns_kc"]], w_in[:, :, _SEG_OFF["ns_ks"]:_SEG_OFF["ns_g"]],
            _seg(w_in, "cv_b"), _seg(w_in, "cv_c"), _seg(w_in, "cv_h"), _seg(w_in, "ns_kc"), _seg(w_in, "ns_vc")]
    for j in range(NSA_KV_HEADS):
        cols += [_seg(w_in, "ns_g", j * _GATES_PER_KV, (j + 1) * _GATES_PER_KV), gate_pad]
    w_proj = jnp.concatenate(cols, axis=-1).astype(BF16)
    w_outb = w_out.astype(BF16)
    half = CMP_STRIDE * HEAD_DIM
    pos2 = cmp_pos.reshape(depth, 2, half)
    cmp_w1 = jnp.stack([cmp_wk1, cmp_wv1], axis=1).astype(BF16)
    cmp_w2 = jnp.stack([cmp_wk2, cmp_wv2], axis=1).astype(BF16)

    hf = x.reshape(n, d)
    for l in range(depth):
        hf = _ffn_ln(hf, w1b, w3b, w2b, ln_g[l, 0], ln_b[l, 0], layer=l, slot=0, alpha=alpha, tm=FFN_ROWS,
                     tf=FFN_TILE)
        hf = _mixer(hf, batch, w_proj, w_outb, ln_g[l, 1], ln_b[l, 1], diff_lam, diff_gain,
                    pos2, cmp_w1, cmp_w2, conv_w, l, alpha)
        hf = _ffn_ln(hf, w1b, w3b, w2b, ln_g[l, 2], ln_b[l, 2], layer=l, slot=1, alpha=alpha, tm=FFN_ROWS,
                     tf=FFN_TILE)
    return hf.reshape(batch, t, d)
```

```python
import functools
import math

import numpy as np
import jax
import jax.numpy as jnp
from jax import lax
from jax.experimental import pallas as pl
from jax.experimental.pallas import tpu as pltpu

F32 = jnp.float32
BF16 = jnp.bfloat16

HEAD_DIM = 64
SB_HEADS = 8
DIFF_HEADS = 4
NSA_HEADS = 8
NSA_KV_HEADS = 2
NSA_GROUP = NSA_HEADS // NSA_KV_HEADS
CMP_BLOCK = 32
CMP_STRIDE = 16
SLC_BLOCK = 64
SLC_TOPN = 16
WINDOW = 512
N_BRANCH = 3
CONV_CH = 512
CONV_WIDTH = 3
LN_EPS = 1e-5
RMS_EPS = 1e-5
NEG_BIG = -1e30
LOG2_E = math.log2(math.e)
EXP2_UNDERFLOW = -150.0
LANES = 128
VMEM_LIMIT = 56 * 1024 * 1024
FFN_TILE = 256
FFN_ROWS = 1024

_SEG_NAMES = ("sb_q", "sb_k", "sb_v", "df_q", "df_k", "df_v", "ns_q", "ns_kc", "ns_vc",
              "ns_ks", "ns_vs", "ns_kw", "ns_vw", "ns_g", "cv_b", "cv_c", "cv_h")
_SEG_WIDTHS = (512, 512, 512, 512, 512, 512, 512, 128, 128, 128, 128, 128, 128,
               NSA_HEADS * N_BRANCH, 512, 512, 512)
_SEG_OFF = dict(zip(_SEG_NAMES, np.cumsum((0,) + _SEG_WIDTHS[:-1]).tolist()))
_SEG_W = dict(zip(_SEG_NAMES, _SEG_WIDTHS))


_GATES_PER_KV = NSA_GROUP * N_BRANCH
ATT_WIDTH = _SEG_OFF["ns_kc"] + 4 * _SEG_W["ns_ks"]


def _params(sem):
    return pltpu.CompilerParams(dimension_semantics=sem, vmem_limit_bytes=VMEM_LIMIT)


def _layer_norm(y, g, b):
    mu = jnp.mean(y, axis=-1, keepdims=True)
    d = y - mu
    var = jnp.mean(d * d, axis=-1, keepdims=True)
    return d * lax.rsqrt(var + LN_EPS) * g + b


def _dot(a, b):
    return jnp.dot(a, b, preferred_element_type=F32)


def _lane_group_max(s, mx):
    for g in range(s.shape[1] // LANES):
        mx = jnp.maximum(mx, s[:, g * LANES:(g + 1) * LANES])
    return mx


def _softmax_sweep(s_refs, *, rows, acc_lanes, first, q0, tq, big, scores_fn, weigh_fn):
    n_big = (q0 - first) // big if big else 0
    mid = first + n_big * big if big else first
    n_small = (q0 - mid) // tq

    def chunks(n, base, width, fn, init):
        return lax.fori_loop(0, n, lambda c, carry: fn(pl.multiple_of(base + c * width, tq), width, carry), init)

    def scores(start, width, mxs, diagonal=False):
        out = []
        for ref, s, mx in zip(s_refs, scores_fn(start, width, diagonal), mxs):
            ref[:, pl.ds(start, width)] = s
            out.append(_lane_group_max(s, mx))
        return tuple(out)

    mxs = (jnp.full((rows, LANES), NEG_BIG, F32),) * len(s_refs)
    if big:
        mxs = chunks(n_big, first, big, scores, mxs)
    mxs = chunks(n_small, mid, tq, scores, mxs)
    mxs = scores(q0, tq, mxs, True)
    ms = [jnp.broadcast_to(jnp.max(mx, axis=1, keepdims=True), (rows, LANES)) for mx in mxs]

    def weigh(start, width, accs):
        ps = [jnp.exp2(ref[:, pl.ds(start, width)] - jnp.concatenate([m] * (width // LANES), axis=1)).astype(BF16)
              for ref, m in zip(s_refs, ms)]
        return tuple(acc + w for acc, w in zip(accs, weigh_fn(ps, start, width)))

    accs = (jnp.zeros((rows, acc_lanes), F32),) * len(s_refs)
    if big:
        accs = chunks(n_big, first, big, weigh, accs)
    accs = chunks(n_small, mid, tq, weigh, accs)
    return weigh(q0, tq, accs)


def _dot_nt(a, b):
    return lax.dot_general(a, b, (((1,), (1,)), ((), ())), preferred_element_type=F32)


def _ffn_ln_kernel(x_ref, w1_ref, w3_ref, w2_ref, g_ref, b_ref, o_ref, xb_ref, *, alpha):
    j = pl.program_id(1)

    @pl.when(j == 0)
    def _():
        xb_ref[...] = x_ref[...].astype(BF16)
        o_ref[...] = jnp.zeros_like(o_ref)

    xb = xb_ref[...]
    a = _dot(xb, w1_ref[...])
    b = _dot(xb, w3_ref[...])
    hm = (a / (1.0 + jnp.exp(-a))) * b
    o_ref[...] += _dot(hm.astype(BF16), w2_ref[...])

    @pl.when(j == pl.num_programs(1) - 1)
    def _():
        y = alpha * x_ref[...] + 0.5 * o_ref[...]
        o_ref[...] = _layer_norm(y, g_ref[...], b_ref[...])


def _ffn_ln(x, w1, w3, w2, g, b, *, layer, slot, alpha, tm, tf):
    n, d = x.shape
    f = w1.shape[-1]
    return pl.pallas_call(
        functools.partial(_ffn_ln_kernel, alpha=alpha),
        grid=(n // tm, f // tf),
        in_specs=[
            pl.BlockSpec((tm, d), lambda i, j: (i, 0)),
            pl.BlockSpec((None, None, d, tf), lambda i, j: (layer, slot, 0, j)),
            pl.BlockSpec((None, None, d, tf), lambda i, j: (layer, slot, 0, j)),
            pl.BlockSpec((None, None, tf, d), lambda i, j: (layer, slot, j, 0)),
            pl.BlockSpec((1, d), lambda i, j: (0, 0)),
            pl.BlockSpec((1, d), lambda i, j: (0, 0)),
        ],
        out_specs=pl.BlockSpec((tm, d), lambda i, j: (i, 0)),
        out_shape=jax.ShapeDtypeStruct((n, d), F32),
        scratch_shapes=[pltpu.VMEM((tm, d), BF16)],
        compiler_params=_params(("parallel", "arbitrary")),
        name="ffn_ln",
    )(x, w1, w3, w2, g.reshape(1, d), b.reshape(1, d))


def _proj_kernel(x_ref, w_ref, oa_ref, of_ref, xb_ref, *, n_a):
    j = pl.program_id(1)

    @pl.when(j == 0)
    def _():
        xb_ref[...] = x_ref[...].astype(BF16)

    @pl.when(j < n_a)
    def _():
        oa_ref[...] = _dot(xb_ref[...], w_ref[...]).astype(oa_ref.dtype)

    @pl.when(j >= n_a)
    def _():
        of_ref[...] = _dot(xb_ref[...], w_ref[...])


def _proj(x, w, *, layer, width_a, tm, tn):
    n, d = x.shape
    m = w.shape[-1]
    n_a = width_a // tn
    return pl.pallas_call(
        functools.partial(_proj_kernel, n_a=n_a),
        grid=(n // tm, m // tn),
        in_specs=[pl.BlockSpec((tm, d), lambda i, j: (i, 0)),
                  pl.BlockSpec((None, d, tn), lambda i, j: (layer, 0, j))],
        out_specs=[pl.BlockSpec((tm, tn), lambda i, j: (i, jnp.minimum(j, n_a - 1))),
                   pl.BlockSpec((tm, tn), lambda i, j: (i, jnp.maximum(j - n_a, 0)))],
        out_shape=[jax.ShapeDtypeStruct((n, width_a), BF16), jax.ShapeDtypeStruct((n, m - width_a), F32)],
        scratch_shapes=[pltpu.VMEM((tm, d), BF16)],
        compiler_params=_params(("parallel", "arbitrary")),
        name="proj",
    )(x, w)


def _sb_kernel(q_ref, k_ref, v_ref, o_ref, *, tq, pairs):
    i = pl.program_id(2)
    q0 = i * tq
    lane = lax.broadcasted_iota(jnp.int32, (tq, LANES), 1)
    tri = jnp.where(lax.broadcasted_iota(jnp.int32, (tq, tq), 0)
                    >= lax.broadcasted_iota(jnp.int32, (tq, tq), 1), 1.0, 0.0).astype(BF16)
    qpos = q0 + lax.rem(lax.broadcasted_iota(jnp.int32, (2 * tq, 1), 0), tq)
    q2 = []
    for p in range(pairs):
        qs = (q_ref[0, :, p * LANES:(p + 1) * LANES].astype(F32) * (HEAD_DIM ** -0.5)).astype(BF16)
        zero = jnp.zeros_like(qs)
        q2.append(jnp.concatenate([jnp.where(lane < HEAD_DIM, qs, zero),
                                   jnp.where(lane < HEAD_DIM, zero, qs)], axis=0))

    def block(start, state, keep=None):
        width = tq
        masked = keep is not None
        out = []
        for p, (c, acc) in enumerate(state):
            ks = k_ref[0, pl.ds(start, width), p * LANES:(p + 1) * LANES]
            vs = v_ref[0, pl.ds(start, width), p * LANES:(p + 1) * LANES]
            z = _dot_nt(q2[p], ks) * LOG2_E
            nz = -z
            lg = jnp.minimum(nz, 0.0) - jnp.log2(1.0 + jnp.exp2(jnp.minimum(z, nz)))
            if masked:
                lg = jnp.where(keep, lg, 0.0)
            hi = lg.astype(BF16)
            lo = (lg - hi.astype(F32)).astype(BF16)
            suffix = _dot(hi, tri) + _dot(lo, tri)
            a = jnp.exp2(z + suffix + c)
            if masked:
                a = jnp.where(keep, a, 0.0)
            out.append((c + suffix[:, 0:1], acc + _dot(a.astype(BF16), vs)))
        return tuple(out)

    state = ((jnp.zeros((2 * tq, 1), F32), jnp.zeros((2 * tq, LANES), F32)),) * pairs
    key = lax.broadcasted_iota(jnp.int32, (1, tq), 1)
    state = block(pl.multiple_of(q0, tq), state, keep=q0 + key < qpos)
    state = block(pl.multiple_of(jnp.maximum(q0 - tq, 0), tq), state, keep=key < q0)

    def live(state):
        top = functools.reduce(jnp.maximum, [c for c, _ in state])
        return (jnp.max(top) > EXP2_UNDERFLOW).astype(jnp.int32)

    def cond(carry):
        return (carry[0] >= 0) & (carry[1] > 0)

    def body(carry):
        kt, _, state = carry
        state = block(pl.multiple_of(kt * tq, tq), state)
        return kt - 1, live(state), state

    _, _, state = lax.while_loop(cond, body, (i - 2, live(state), state))
    o_ref[0] = jnp.concatenate([jnp.where(lane < HEAD_DIM, acc[:tq], acc[tq:]) for _, acc in state],
                               axis=1).astype(o_ref.dtype)


def _sb_attention(pa, *, tq, pairs, q_blk, k_blk, v_blk):
    b, t, _ = pa.shape
    w = pairs * LANES
    n_groups = SB_HEADS // (2 * pairs)
    return pl.pallas_call(
        functools.partial(_sb_kernel, tq=tq, pairs=pairs),
        grid=(b, n_groups, t // tq),
        in_specs=[
            pl.BlockSpec((1, tq, w), lambda bi, p, i: (bi, i, q_blk + p)),
            pl.BlockSpec((1, t, w), lambda bi, p, i: (bi, 0, k_blk + p)),
            pl.BlockSpec((1, t, w), lambda bi, p, i: (bi, 0, v_blk + p)),
        ],
        out_specs=pl.BlockSpec((1, tq, w), lambda bi, p, i: (bi, i, p)),
        out_shape=jax.ShapeDtypeStruct((b, t, SB_HEADS * HEAD_DIM), BF16),
        compiler_params=_params(("parallel", "parallel", "arbitrary")),
        name="sb_attention",
    )(pa, pa, pa)


def _diff_kernel(lam_ref, gain_ref, q_ref, k_ref, v_ref, o_ref, s_ref, *, tq, heads, lam_init):
    i = pl.program_id(2)
    lane = lax.broadcasted_iota(jnp.int32, (tq, LANES), 1)
    row = lax.broadcasted_iota(jnp.int32, (tq, tq), 0)
    col = lax.broadcasted_iota(jnp.int32, (tq, tq), 1)
    causal = col <= row
    causal2 = jnp.concatenate([causal, causal], axis=0)
    q2 = []
    for h in range(heads):
        qs = (q_ref[0, :, h * LANES:(h + 1) * LANES].astype(F32) * (HEAD_DIM ** -0.5)).astype(BF16)
        zero = jnp.zeros_like(qs)
        q2.append(jnp.concatenate([jnp.where(lane < HEAD_DIM, qs, zero),
                                   jnp.where(lane < HEAD_DIM, zero, qs)], axis=0))

    def scores(start, width, diagonal):
        out = []
        for h in range(heads):
            s = _dot_nt(q2[h], k_ref[0, pl.ds(start, width), h * LANES:(h + 1) * LANES]) * LOG2_E
            out.append(jnp.where(causal2, s, NEG_BIG) if diagonal else s)
        return out

    def weigh(ps, start, width):
        ones = jnp.ones((width, LANES), BF16)
        return [_dot(p, jnp.concatenate([v_ref[0, pl.ds(start, width), h * LANES:(h + 1) * LANES], ones], axis=1))
                for h, p in enumerate(ps)]

    s_refs = [s_ref.at[pl.ds(h * 2 * tq, 2 * tq)] for h in range(heads)]
    accs = _softmax_sweep(s_refs, rows=2 * tq, acc_lanes=2 * LANES, first=0, q0=pl.multiple_of(i * tq, tq), tq=tq,
                          big=2 * tq, scores_fn=scores, weigh_fn=weigh)
    lp = lam_ref[...]
    lam = (jnp.exp(jnp.sum(lp[0:1] * lp[1:2], axis=1, keepdims=True))
           - jnp.exp(jnp.sum(lp[2:3] * lp[3:4], axis=1, keepdims=True)) + lam_init)
    out = []
    for acc in accs:
        o = acc[:, :LANES] / acc[:, LANES:]
        od = o[:tq] - lam * o[tq:]
        od = od * lax.rsqrt(jnp.mean(od * od, axis=-1, keepdims=True) + RMS_EPS)
        out.append(od * gain_ref[...] * (1.0 - lam_init))
    o_ref[0] = jnp.concatenate(out, axis=1).astype(o_ref.dtype)


def _diff_attention(pa, lam_params, gain, *, tq, heads, q_blk, k_blk, v_blk, lam_init):
    b, t, _ = pa.shape
    w = heads * LANES
    return pl.pallas_call(
        functools.partial(_diff_kernel, tq=tq, heads=heads, lam_init=lam_init),
        grid=(b, DIFF_HEADS // heads, t // tq),
        in_specs=[
            pl.BlockSpec((4, HEAD_DIM), lambda bi, h, i: (0, 0)),
            pl.BlockSpec((1, LANES), lambda bi, h, i: (0, 0)),
            pl.BlockSpec((1, tq, w), lambda bi, h, i: (bi, i, q_blk + h)),
            pl.BlockSpec((1, t, w), lambda bi, h, i: (bi, 0, k_blk + h)),
            pl.BlockSpec((1, t, w), lambda bi, h, i: (bi, 0, v_blk + h)),
        ],
        out_specs=pl.BlockSpec((1, tq, w), lambda bi, h, i: (bi, i, h)),
        out_shape=jax.ShapeDtypeStruct((b, t, DIFF_HEADS * LANES), BF16),
        scratch_shapes=[pltpu.VMEM((heads * 2 * tq, t), F32)],
        compiler_params=_params(("parallel", "parallel", "arbitrary")),
        name="diff_attention",
    )(lam_params, gain.reshape(1, LANES), pa, pa, pa)


def _compress_kernel(x_ref, pos_ref, w1_ref, w2_ref, o_ref):
    x = x_ref[0, 0]
    n16, half = x.shape
    xa = (x + pos_ref[0:1]).astype(BF16)
    xb = (x + pos_ref[1:2]).astype(BF16)
    w1 = w1_ref[0]
    first = _dot(xa, w1[:half])
    second = _dot(xb, w1[half:])
    h = first + pltpu.roll(second, n16 - 1, 0)
    g = 0.5 * h * (1.0 + jnp.tanh(math.sqrt(2.0 / math.pi) * (h + 0.044715 * (h * h * h))))
    o_ref[0, 0] = _dot(g.astype(BF16), w2_ref[0])


def _compress(x16, pos2, w1, w2, *, layer):
    b, n_streams, n16, half = x16.shape
    d = w2.shape[-1]
    return pl.pallas_call(
        _compress_kernel,
        grid=(b, n_streams),
        in_specs=[
            pl.BlockSpec((1, 1, n16, half), lambda bi, s: (bi, s, 0, 0)),
            pl.BlockSpec((None, 2, half), lambda bi, s: (layer, 0, 0)),
            pl.BlockSpec((None, 1, 2 * half, w1.shape[-1]), lambda bi, s: (layer, s // NSA_KV_HEADS, 0, 0)),
            pl.BlockSpec((None, 1, w2.shape[2], d), lambda bi, s: (layer, s // NSA_KV_HEADS, 0, 0)),
        ],
        out_specs=pl.BlockSpec((1, 1, n16, d), lambda bi, s: (bi, s, 0, 0)),
        out_shape=jax.ShapeDtypeStruct((b, n_streams, n16, d), F32),
        compiler_params=_params(("parallel", "arbitrary")),
        name="nsa_compress",
    )(x16, pos2, w1, w2)


def _nsa_kernel(q_ref, g_ref, ck_ref, cv_ref, ovt_ref, gx_ref, ks_ref, vs_ref, kw_ref, vw_ref, o_ref,
                kse_ref, kso_ref, vse_ref, vso_ref, kwd_ref, vwe_ref, vwo_ref, ckd_ref, cvd_ref, s_ref,
                *, tq, top_n):
    i = pl.program_id(2)
    grp = NSA_GROUP
    d = HEAD_DIM
    q0 = i * tq
    n_cmp = ck_ref.shape[2]
    n_slc = ovt_ref.shape[0]
    t = ks_ref.shape[1]
    rows = grp * tq
    half = rows // 2

    @pl.when(i == 0)
    def _():
        onehot = jnp.where(lax.broadcasted_iota(jnp.int32, (t, d), 0) // SLC_BLOCK
                           == lax.broadcasted_iota(jnp.int32, (t, d), 1), 1.0, 0.0)
        ones = jnp.ones((t, d), F32)

        def put(ref, left, right):
            ref[...] = jnp.concatenate([left, right], axis=1).astype(BF16)

        first_kv = pl.program_id(1) == 0

        def own_head(ref):
            both = ref[0].astype(F32)
            return jnp.where(first_kv, both[:, :d], both[:, d:])

        ks, vs, kw, vw = own_head(ks_ref), own_head(vs_ref), own_head(kw_ref), own_head(vw_ref)
        put(kse_ref, ks, onehot)
        put(kso_ref, onehot, ks)
        put(vse_ref, vs, ones)
        put(vso_ref, ones, vs)
        put(kwd_ref, kw, kw)
        put(vwe_ref, vw, ones)
        put(vwo_ref, ones, vw)
        put(ckd_ref, ck_ref[0, 0], ck_ref[0, 0])
        put(cvd_ref, cv_ref[0, 0], cv_ref[0, 0])

    lane = lax.broadcasted_iota(jnp.int32, (tq, LANES), 1)
    left = lane < d
    qs = (q_ref[0].astype(F32) * (d ** -0.5)).astype(BF16)
    pairs = [qs[:, :LANES], qs[:, LANES:]]
    zero = jnp.zeros((tq, LANES), BF16)
    qz = jnp.concatenate([jnp.where(left, pairs[0], zero), jnp.where(left, pairs[1], zero),
                          jnp.where(left, zero, pairs[0]), jnp.where(left, zero, pairs[1])], axis=0)
    t4 = q0 + lax.rem(lax.broadcasted_iota(jnp.int32, (rows, 1), 0), tq)

    def rep4(x):
        return jnp.concatenate([x] * grp, axis=0)

    s = _dot_nt(qz, ckd_ref[...])
    cmp_end = CMP_STRIDE * lax.broadcasted_iota(jnp.int32, (1, n_cmp), 1) + (CMP_BLOCK - 1)
    cmask = cmp_end <= t4
    s = jnp.where(cmask, s, NEG_BIG)
    e = jnp.exp(s - jnp.max(s, axis=1, keepdims=True))
    p = e / jnp.sum(e, axis=1, keepdims=True)
    p = jnp.where(cmask, p, 0.0)
    o_cmp = _dot(p.astype(BF16), cvd_ref[...])

    pg = p[0:tq]
    for h in range(1, grp):
        pg = pg + p[h * tq:(h + 1) * tq]
    p_hi = pg.astype(BF16)
    r1 = pg - p_hi.astype(F32)
    p_mid = r1.astype(BF16)
    p_lo = (r1 - p_mid.astype(F32)).astype(BF16)
    ovt = ovt_ref[...]
    imp_t = _dot_nt(ovt, p_hi) + _dot_nt(ovt, p_mid) + _dot_nt(ovt, p_lo)
    blk = lax.broadcasted_iota(jnp.int32, (n_slc, tq), 0)
    cur = (q0 + lax.broadcasted_iota(jnp.int32, (n_slc, tq), 1)) // SLC_BLOCK
    forced = (blk == 0) | (blk == cur) | (blk == cur - 1)
    score = jnp.where(forced, jnp.inf, jnp.where(blk <= cur, imp_t, -jnp.inf))
    rank = jnp.zeros((n_slc, tq), F32)
    for j in range(n_slc):
        sj = score[j:j + 1, :]
        tie = jnp.where(blk > j, 1.0, 0.0)
        rank = rank + jnp.where(sj > score, 1.0, jnp.where(sj == score, tie, 0.0))
    sel_t = jnp.where(rank < top_n, 1.0, 0.0)
    if n_slc < d:
        sel_t = jnp.concatenate([sel_t, jnp.zeros((d - n_slc, tq), F32)], axis=0)
    sel_t2 = jnp.concatenate([sel_t, sel_t], axis=0).astype(BF16)
    eye = jnp.where(lax.broadcasted_iota(jnp.int32, (tq, tq), 0)
                    == lax.broadcasted_iota(jnp.int32, (tq, tq), 1), 1.0, 0.0).astype(BF16)
    sel2 = _dot_nt(eye, sel_t2)
    bias = ((sel2 - 1.0) * (-NEG_BIG)).astype(BF16)
    q_even = jnp.concatenate([jnp.where(left, pairs[0], bias), jnp.where(left, pairs[1], bias)], axis=0)
    q_odd = jnp.concatenate([jnp.where(left, bias, pairs[0]), jnp.where(left, bias, pairs[1])], axis=0)

    row = lax.broadcasted_iota(jnp.int32, (tq, tq), 0)
    col = lax.broadcasted_iota(jnp.int32, (tq, tq), 1)
    causal4 = rep4(col <= row)

    def weigh(p, start, width, ve_ref, vo_ref):
        return jnp.concatenate([_dot(p[:half], ve_ref[pl.ds(start, width), :]),
                                _dot(p[half:], vo_ref[pl.ds(start, width), :])], axis=0)

    def normalise(acc):
        out = []
        for pr in range(2):
            even, odd = acc[pr * tq:(pr + 1) * tq], acc[half + pr * tq:half + (pr + 1) * tq]
            den = pltpu.roll(jnp.where(left, odd, even), d, 1)
            out.append(jnp.where(left, even, odd) / den)
        return out

    def slc_scores(start, width, diagonal):
        s = jnp.concatenate([_dot_nt(q_even, kse_ref[pl.ds(start, width), :]),
                             _dot_nt(q_odd, kso_ref[pl.ds(start, width), :])], axis=0) * LOG2_E
        return jnp.where(causal4, s, NEG_BIG) if diagonal else s

    (acc,) = _softmax_sweep((s_ref,), rows=rows, acc_lanes=LANES, first=0, q0=pl.multiple_of(q0, tq), tq=tq,
                            big=4 * tq, scores_fn=lambda *a: (slc_scores(*a),),
                            weigh_fn=lambda ps, start, width: (weigh(ps[0], start, width, vse_ref, vso_ref),))
    o_slc = normalise(acc)

    w_width = (-(-WINDOW // tq) + 1) * tq
    w_start = pl.multiple_of(jnp.maximum(q0 + tq - w_width, 0), tq)
    s = _dot_nt(qz, kwd_ref[pl.ds(w_start, w_width), :])
    kpos = w_start + lax.broadcasted_iota(jnp.int32, (1, w_width), 1)
    s = jnp.where((kpos <= t4) & (kpos > t4 - WINDOW), s, NEG_BIG)
    pw = jnp.exp(s - jnp.max(s, axis=1, keepdims=True))
    o_win = normalise(weigh(pw.astype(BF16), w_start, w_width, vwe_ref, vwo_ref))

    gate = 1.0 / (1.0 + jnp.exp(-g_ref[0]))
    g_hi = gate.astype(BF16)
    g_r = gate - g_hi.astype(F32)
    g_mid = g_r.astype(BF16)
    g_lo = (g_r - g_mid.astype(F32)).astype(BF16)
    spread = gx_ref[...]
    gx = _dot(g_hi, spread) + _dot(g_mid, spread) + _dot(g_lo, spread)

    out = []
    for pr in range(2):
        cmp_pr = jnp.where(left, o_cmp[pr * tq:(pr + 1) * tq], o_cmp[half + pr * tq:half + (pr + 1) * tq])
        mixed = 0.0
        for br, o_br in enumerate((cmp_pr, o_slc[pr], o_win[pr])):
            blk_idx = pr * N_BRANCH + br
            mixed = mixed + gx[:, blk_idx * LANES:(blk_idx + 1) * LANES] * o_br
        out.append(mixed)
    o_ref[0] = jnp.concatenate(out, axis=1).astype(o_ref.dtype)


def _nsa_attention(pa, pf, ckv, ovt, *, tq, q_blk, kv_blk, g_blk, top_n):
    b, t, _ = pa.shape
    n16 = ckv.shape[2]
    n_slc = ovt.shape[0]
    qw = NSA_GROUP * HEAD_DIM
    kv_specs = [pl.BlockSpec((1, t, LANES), functools.partial(lambda bi, j, i, c: (bi, 0, c), c=kv_blk + n))
                for n in range(4)]
    return pl.pallas_call(
        functools.partial(_nsa_kernel, tq=tq, top_n=top_n),
        grid=(b, NSA_KV_HEADS, t // tq),
        in_specs=[
            pl.BlockSpec((1, tq, qw), lambda bi, j, i: (bi, i, q_blk + j)),
            pl.BlockSpec((1, tq, LANES), lambda bi, j, i: (bi, i, g_blk + j)),
            pl.BlockSpec((1, 1, n16, HEAD_DIM), lambda bi, j, i: (bi, j, 0, 0)),
            pl.BlockSpec((1, 1, n16, HEAD_DIM), lambda bi, j, i: (bi, NSA_KV_HEADS + j, 0, 0)),
            pl.BlockSpec((n_slc, n16), lambda bi, j, i: (0, 0)),
            pl.BlockSpec((LANES, 2 * N_BRANCH * LANES), lambda bi, j, i: (0, 0)),
            *kv_specs,
        ],
        out_specs=pl.BlockSpec((1, tq, qw), lambda bi, j, i: (bi, i, j)),
        out_shape=jax.ShapeDtypeStruct((b, t, NSA_HEADS * HEAD_DIM), BF16),
        scratch_shapes=[pltpu.VMEM((t, LANES), BF16)] * 7 + [pltpu.VMEM((n16, LANES), BF16)] * 2
                       + [pltpu.VMEM((NSA_GROUP * tq, t), F32)],
        compiler_params=_params(("arbitrary", "arbitrary", "arbitrary")),
        name="nsa_attention",
    )(pa, pf, ckv, ckv, ovt, _gate_spread(), pa, pa, pa, pa)


def _conv_kernel(cb_ref, cc_ref, ch_ref, w_ref, o_ref, prev_ref):
    @pl.when(pl.program_id(1) == 0)
    def _():
        prev_ref[...] = jnp.zeros_like(prev_ref)

    u = cc_ref[0] * ch_ref[0]
    tt = u.shape[0]
    row = lax.broadcasted_iota(jnp.int32, u.shape, 0)
    last1 = prev_ref[7:8]
    last2 = prev_ref[6:7]
    u1 = jnp.where(row >= 1, pltpu.roll(u, 1, 0), last1)
    u2 = jnp.where(row >= 2, pltpu.roll(u, 2, 0), jnp.where(row == 1, last1, last2))
    w = w_ref[...]
    o_ref[0] = (cb_ref[0] * (w[0:1] * u2 + w[1:2] * u1 + w[2:3] * u)).astype(o_ref.dtype)
    prev_ref[...] = u[tt - 8:tt]


def _conv_mixer(pf, conv_w, *, tt):
    b, t, _ = pf.shape
    c = CONV_CH
    return pl.pallas_call(
        _conv_kernel,
        grid=(b, t // tt),
        in_specs=[
            pl.BlockSpec((1, tt, c), lambda bi, i: (bi, i, 0)),
            pl.BlockSpec((1, tt, c), lambda bi, i: (bi, i, 1)),
            pl.BlockSpec((1, tt, c), lambda bi, i: (bi, i, 2)),
            pl.BlockSpec((CONV_WIDTH, c), lambda bi, i: (0, 0)),
        ],
        out_specs=pl.BlockSpec((1, tt, c), lambda bi, i: (bi, i, 0)),
        out_shape=jax.ShapeDtypeStruct((b, t, c), BF16),
        scratch_shapes=[pltpu.VMEM((8, c), F32)],
        compiler_params=_params(("parallel", "arbitrary")),
        name="conv_mixer",
    )(pf, pf, pf, conv_w)


def _out_ln_kernel(a_ref, b_ref, c_ref, d_ref, w_ref, x_ref, g_ref, beta_ref, o_ref, *, alpha):
    kw = a_ref.shape[1]
    mix = _dot(a_ref[...], w_ref[0:kw])
    for n, r in enumerate((b_ref, c_ref, d_ref), start=1):
        mix = mix + _dot(r[...], w_ref[n * kw:(n + 1) * kw])
    o_ref[...] = _layer_norm(alpha * x_ref[...] + mix, g_ref[...], beta_ref[...])


def _out_ln(parts, w, x, g, b, *, layer, alpha, tm):
    n, d = x.shape
    kw = parts[0].shape[1]
    part_spec = pl.BlockSpec((tm, kw), lambda i: (i, 0))
    return pl.pallas_call(
        functools.partial(_out_ln_kernel, alpha=alpha),
        grid=(n // tm,),
        in_specs=[part_spec, part_spec, part_spec, part_spec,
                  pl.BlockSpec((None,) + w.shape[1:], lambda i: (layer, 0, 0)),
                  pl.BlockSpec((tm, d), lambda i: (i, 0)),
                  pl.BlockSpec((1, d), lambda i: (0, 0)),
                  pl.BlockSpec((1, d), lambda i: (0, 0))],
        out_specs=pl.BlockSpec((tm, d), lambda i: (i, 0)),
        out_shape=jax.ShapeDtypeStruct((n, d), F32),
        compiler_params=_params(("parallel",)),
        name="out_ln",
    )(*parts, w, x, g.reshape(1, d), b.reshape(1, d))


def _gate_spread():
    m = np.zeros((LANES, 2 * N_BRANCH * LANES), np.float32)
    for pr in range(2):
        for br in range(N_BRANCH):
            for lane in range(LANES):
                head = 2 * pr + lane // HEAD_DIM
                m[N_BRANCH * head + br, (pr * N_BRANCH + br) * LANES + lane] = 1.0
    return jnp.asarray(m, BF16)


def _overlap_t(t):
    n16 = t // CMP_STRIDE
    n_slc = t // SLC_BLOCK
    c_start = CMP_STRIDE * np.arange(n16)
    j_start = SLC_BLOCK * np.arange(n_slc)
    ov = ((c_start[None, :] < j_start[:, None] + SLC_BLOCK)
          & (c_start[None, :] + CMP_BLOCK > j_start[:, None])).astype(np.float32)
    ov[:, n16 - 1] = 0.0
    return jnp.asarray(ov, BF16)


def _mixer(hf, batch, w_proj, w_out, ln_g, ln_b, diff_lam, diff_gain, pos2, cmp_w1, cmp_w2,
           conv_w, layer, alpha):
    n, _ = hf.shape
    t = n // batch
    pa, pf = _proj(hf, w_proj, layer=layer, width_a=ATT_WIDTH, tm=1024, tn=1024)
    pa, pf = pa.reshape(batch, t, -1), pf.reshape(batch, t, -1)

    o_sb = _sb_attention(pa, tq=256, pairs=4, q_blk=0, k_blk=1, v_blk=2)
    lam_init = 0.8 - 0.6 * math.exp(-0.3 * layer)
    o_df = _diff_attention(pa, diff_lam[layer], diff_gain[layer], tq=512, heads=2, q_blk=6, k_blk=8, v_blk=10,
                           lam_init=lam_init)

    n16 = t // CMP_STRIDE
    kvc = pf[:, :, 3 * CONV_CH:3 * CONV_CH + 2 * LANES].reshape(batch, t, 2, NSA_KV_HEADS, HEAD_DIM)
    x16 = kvc.transpose(0, 2, 3, 1, 4).reshape(batch, 2 * NSA_KV_HEADS, n16, CMP_STRIDE * HEAD_DIM)
    ckv = _compress(x16, pos2, cmp_w1, cmp_w2, layer=layer)
    n_slc = t // SLC_BLOCK
    o_ns = _nsa_attention(pa, pf, ckv, _overlap_t(t), tq=256, q_blk=12, kv_blk=28, g_blk=14,
                          top_n=min(SLC_TOPN, n_slc))

    o_cv = _conv_mixer(pf, conv_w[layer], tt=512)
    parts = [o.reshape(n, -1) for o in (o_sb, o_df, o_ns, o_cv)]
    return _out_ln(parts, w_out, hf, ln_g, ln_b, layer=layer, alpha=alpha, tm=512)


def _seg(w_in, name, lo=0, hi=None):
    off = _SEG_OFF[name]
    hi = _SEG_W[name] if hi is None else hi
    return w_in[:, :, off + lo:off + hi]


def kernel(x, ln_g, ln_b, ffn_w1, ffn_w3, ffn_w2, w_in, w_out, diff_lam, diff_gain, cmp_pos, cmp_wk1,
           cmp_wk2, cmp_wv1, cmp_wv2, conv_w):
    batch, t, d = x.shape
    depth = ln_g.shape[0]
    alpha = (2 * depth) ** 0.25
    n = batch * t

    w1b, w3b, w2b = ffn_w1.astype(BF16), ffn_w3.astype(BF16), ffn_w2.astype(BF16)
    gate_pad = jnp.zeros(w_in.shape[:2] + (LANES - _GATES_PER_KV,), w_in.dtype)
    cols = [w_in[:, :, :_SEG_OFF["ns_kc"]], w_in[:, :, _SEG_OFF["ns_ks"]:_SEG_OFF["ns_g"]],
            _seg(w_in, "cv_b"), _seg(w_in, "cv_c"), _seg(w_in, "cv_h"), _seg(w_in, "ns_kc"), _seg(w_in, "ns_vc")]
    for j in range(NSA_KV_HEADS):
        cols += [_seg(w_in, "ns_g", j * _GATES_PER_KV, (j + 1) * _GATES_PER_KV), gate_pad]
    w_proj = jnp.concatenate(cols, axis=-1).astype(BF16)
    w_outb = w_out.astype(BF16)
    half = CMP_STRIDE * HEAD_DIM
    pos2 = cmp_pos.reshape(depth, 2, half)
    cmp_w1 = jnp.stack([cmp_wk1, cmp_wv1], axis=1).astype(BF16)
    cmp_w2 = jnp.stack([cmp_wk2, cmp_wv2], axis=1).astype(BF16)

    hf = x.reshape(n, d)
    for l in range(depth):
        hf = _ffn_ln(hf, w1b, w3b, w2b, ln_g[l, 0], ln_b[l, 0], layer=l, slot=0, alpha=alpha, tm=FFN_ROWS,
                     tf=FFN_TILE)
        hf = _mixer(hf, batch, w_proj, w_outb, ln_g[l, 1], ln_b[l, 1], diff_lam, diff_gain,
                    pos2, cmp_w1, cmp_w2, conv_w, l, alpha)
        hf = _ffn_ln(hf, w1b, w3b, w2b, ln_g[l, 2], ln_b[l, 2], layer=l, slot=1, alpha=alpha, tm=FFN_ROWS,
                     tf=FFN_TILE)
    return hf.reshape(batch, t, d)
```

```python
import functools
import math

import numpy as np
import jax
import jax.numpy as jnp
from jax import lax
from jax.experimental import pallas as pl
from jax.experimental.pallas import tpu as pltpu

F32 = jnp.float32
BF16 = jnp.bfloat16

HEAD_DIM = 64
SB_HEADS = 8
DIFF_HEADS = 4
NSA_HEADS = 8
NSA_KV_HEADS = 2
NSA_GROUP = NSA_HEADS // NSA_KV_HEADS
CMP_BLOCK = 32
CMP_STRIDE = 16
SLC_BLOCK = 64
SLC_TOPN = 16
WINDOW = 512
N_BRANCH = 3
CONV_CH = 512
CONV_WIDTH = 3
LN_EPS = 1e-5
RMS_EPS = 1e-5
NEG_BIG = -1e30
LOG2_E = math.log2(math.e)
EXP2_UNDERFLOW = -150.0
LANES = 128
VMEM_LIMIT = 56 * 1024 * 1024
FFN_TILE = 256
FFN_ROWS = 1024

_SEG_NAMES = ("sb_q", "sb_k", "sb_v", "df_q", "df_k", "df_v", "ns_q", "ns_kc", "ns_vc",
              "ns_ks", "ns_vs", "ns_kw", "ns_vw", "ns_g", "cv_b", "cv_c", "cv_h")
_SEG_WIDTHS = (512, 512, 512, 512, 512, 512, 512, 128, 128, 128, 128, 128, 128,
               NSA_HEADS * N_BRANCH, 512, 512, 512)
_SEG_OFF = dict(zip(_SEG_NAMES, np.cumsum((0,) + _SEG_WIDTHS[:-1]).tolist()))
_SEG_W = dict(zip(_SEG_NAMES, _SEG_WIDTHS))


_GATES_PER_KV = NSA_GROUP * N_BRANCH
ATT_WIDTH = _SEG_OFF["ns_kc"] + 4 * _SEG_W["ns_ks"]


def _params(sem):
    return pltpu.CompilerParams(dimension_semantics=sem, vmem_limit_bytes=VMEM_LIMIT)


def _layer_norm(y, g, b):
    mu = jnp.mean(y, axis=-1, keepdims=True)
    d = y - mu
    var = jnp.mean(d * d, axis=-1, keepdims=True)
    return d * lax.rsqrt(var + LN_EPS) * g + b


def _dot(a, b):
    return jnp.dot(a, b, preferred_element_type=F32)


def _lane_group_max(s, mx):
    for g in range(s.shape[1] // LANES):
        mx = jnp.maximum(mx, s[:, g * LANES:(g + 1) * LANES])
    return mx


def _softmax_sweep(s_refs, *, rows, acc_lanes, first, q0, tq, big, scores_fn, weigh_fn):
    n_big = (q0 - first) // big if big else 0
    mid = first + n_big * big if big else first
    n_small = (q0 - mid) // tq

    def chunks(n, base, width, fn, init):
        return lax.fori_loop(0, n, lambda c, carry: fn(pl.multiple_of(base + c * width, tq), width, carry), init)

    def scores(start, width, mxs, diagonal=False):
        out = []
        for ref, s, mx in zip(s_refs, scores_fn(start, width, diagonal), mxs):
            ref[:, pl.ds(start, width)] = s
            out.append(_lane_group_max(s, mx))
        return tuple(out)

    mxs = (jnp.full((rows, LANES), NEG_BIG, F32),) * len(s_refs)
    if big:
        mxs = chunks(n_big, first, big, scores, mxs)
    mxs = chunks(n_small, mid, tq, scores, mxs)
    mxs = scores(q0, tq, mxs, True)
    ms = [jnp.broadcast_to(jnp.max(mx, axis=1, keepdims=True), (rows, LANES)) for mx in mxs]

    def weigh(start, width, accs):
        ps = [jnp.exp2(ref[:, pl.ds(start, width)] - jnp.concatenate([m] * (width // LANES), axis=1)).astype(BF16)
              for ref, m in zip(s_refs, ms)]
        return tuple(acc + w for acc, w in zip(accs, weigh_fn(ps, start, width)))

    accs = (jnp.zeros((rows, acc_lanes), F32),) * len(s_refs)
    if big:
        accs = chunks(n_big, first, big, weigh, accs)
    accs = chunks(n_small, mid, tq, weigh, accs)
    return weigh(q0, tq, accs)


def _dot_nt(a, b):
    return lax.dot_general(a, b, (((1,), (1,)), ((), ())), preferred_element_type=F32)


def _ffn_ln_kernel(x_ref, w1_ref, w3_ref, w2_ref, g_ref, b_ref, o_ref, xb_ref, *, alpha):
    j = pl.program_id(1)

    @pl.when(j == 0)
    def _():
        xb_ref[...] = x_ref[...].astype(BF16)
        o_ref[...] = jnp.zeros_like(o_ref)

    xb = xb_ref[...]
    a = _dot(xb, w1_ref[...])
    b = _dot(xb, w3_ref[...])
    hm = (a / (1.0 + jnp.exp(-a))) * b
    o_ref[...] += _dot(hm.astype(BF16), w2_ref[...])

    @pl.when(j == pl.num_programs(1) - 1)
    def _():
        y = alpha * x_ref[...] + 0.5 * o_ref[...]
        o_ref[...] = _layer_norm(y, g_ref[...], b_ref[...])


def _ffn_ln(x, w1, w3, w2, g, b, *, layer, slot, alpha, tm, tf):
    n, d = x.shape
    f = w1.shape[-1]
    return pl.pallas_call(
        functools.partial(_ffn_ln_kernel, alpha=alpha),
        grid=(n // tm, f // tf),
        in_specs=[
            pl.BlockSpec((tm, d), lambda i, j: (i, 0)),
            pl.BlockSpec((None, None, d, tf), lambda i, j: (layer, slot, 0, j)),
            pl.BlockSpec((None, None, d, tf), lambda i, j: (layer, slot, 0, j)),
            pl.BlockSpec((None, None, tf, d), lambda i, j: (layer, slot, j, 0)),
            pl.BlockSpec((1, d), lambda i, j: (0, 0)),
            pl.BlockSpec((1, d), lambda i, j: (0, 0)),
        ],
        out_specs=pl.BlockSpec((tm, d), lambda i, j: (i, 0)),
        out_shape=jax.ShapeDtypeStruct((n, d), F32),
        scratch_shapes=[pltpu.VMEM((tm, d), BF16)],
        compiler_params=_params(("parallel", "arbitrary")),
        name="ffn_ln",
    )(x, w1, w3, w2, g.reshape(1, d), b.reshape(1, d))


def _proj_kernel(x_ref, w_ref, oa_ref, of_ref, xb_ref, *, n_a):
    j = pl.program_id(1)

    @pl.when(j == 0)
    def _():
        xb_ref[...] = x_ref[...].astype(BF16)

    @pl.when(j < n_a)
    def _():
        oa_ref[...] = _dot(xb_ref[...], w_ref[...]).astype(oa_ref.dtype)

    @pl.when(j >= n_a)
    def _():
        of_ref[...] = _dot(xb_ref[...], w_ref[...])


def _proj(x, w, *, layer, width_a, tm, tn):
    n, d = x.shape
    m = w.shape[-1]
    n_a = width_a // tn
    return pl.pallas_call(
        functools.partial(_proj_kernel, n_a=n_a),
        grid=(n // tm, m // tn),
        in_specs=[pl.BlockSpec((tm, d), lambda i, j: (i, 0)),
                  pl.BlockSpec((None, d, tn), lambda i, j: (layer, 0, j))],
        out_specs=[pl.BlockSpec((tm, tn), lambda i, j: (i, jnp.minimum(j, n_a - 1))),
                   pl.BlockSpec((tm, tn), lambda i, j: (i, jnp.maximum(j - n_a, 0)))],
        out_shape=[jax.ShapeDtypeStruct((n, width_a), BF16), jax.ShapeDtypeStruct((n, m - width_a), F32)],
        scratch_shapes=[pltpu.VMEM((tm, d), BF16)],
        compiler_params=_params(("parallel", "arbitrary")),
        name="proj",
    )(x, w)


def _sb_kernel(q_ref, k_ref, v_ref, o_ref, *, tq, pairs):
    i = pl.program_id(2)
    q0 = i * tq
    lane = lax.broadcasted_iota(jnp.int32, (tq, LANES), 1)
    tri = jnp.where(lax.broadcasted_iota(jnp.int32, (tq, tq), 0)
                    >= lax.broadcasted_iota(jnp.int32, (tq, tq), 1), 1.0, 0.0).astype(BF16)
    qpos = q0 + lax.rem(lax.broadcasted_iota(jnp.int32, (2 * tq, 1), 0), tq)
    q2 = []
    for p in range(pairs):
        qs = (q_ref[0, :, p * LANES:(p + 1) * LANES].astype(F32) * (HEAD_DIM ** -0.5)).astype(BF16)
        zero = jnp.zeros_like(qs)
        q2.append(jnp.concatenate([jnp.where(lane < HEAD_DIM, qs, zero),
                                   jnp.where(lane < HEAD_DIM, zero, qs)], axis=0))

    def block(start, state, keep=None):
        width = tq
        masked = keep is not None
        out = []
        for p, (c, acc) in enumerate(state):
            ks = k_ref[0, pl.ds(start, width), p * LANES:(p + 1) * LANES]
            vs = v_ref[0, pl.ds(start, width), p * LANES:(p + 1) * LANES]
            z = _dot_nt(q2[p], ks) * LOG2_E
            nz = -z
            lg = jnp.minimum(nz, 0.0) - jnp.log2(1.0 + jnp.exp2(jnp.minimum(z, nz)))
            if masked:
                lg = jnp.where(keep, lg, 0.0)
            hi = lg.astype(BF16)
            lo = (lg - hi.astype(F32)).astype(BF16)
            suffix = _dot(hi, tri) + _dot(lo, tri)
            a = jnp.exp2(z + suffix + c)
            if masked:
                a = jnp.where(keep, a, 0.0)
            out.append((c + suffix[:, 0:1], acc + _dot(a.astype(BF16), vs)))
        return tuple(out)

    state = ((jnp.zeros((2 * tq, 1), F32), jnp.zeros((2 * tq, LANES), F32)),) * pairs
    key = lax.broadcasted_iota(jnp.int32, (1, tq), 1)
    state = block(pl.multiple_of(q0, tq), state, keep=q0 + key < qpos)
    state = block(pl.multiple_of(jnp.maximum(q0 - tq, 0), tq), state, keep=key < q0)

    def live(state):
        top = functools.reduce(jnp.maximum, [c for c, _ in state])
        return (jnp.max(top) > EXP2_UNDERFLOW).astype(jnp.int32)

    def cond(carry):
        return (carry[0] >= 0) & (carry[1] > 0)

    def body(carry):
        kt, _, state = carry
        state = block(pl.multiple_of(kt * tq, tq), state)
        return kt - 1, live(state), state

    _, _, state = lax.while_loop(cond, body, (i - 2, live(state), state))
    o_ref[0] = jnp.concatenate([jnp.where(lane < HEAD_DIM, acc[:tq], acc[tq:]) for _, acc in state],
                               axis=1).astype(o_ref.dtype)


def _sb_attention(pa, *, tq, pairs, q_blk, k_blk, v_blk):
    b, t, _ = pa.shape
    w = pairs * LANES
    n_groups = SB_HEADS // (2 * pairs)
    return pl.pallas_call(
        functools.partial(_sb_kernel, tq=tq, pairs=pairs),
        grid=(b, n_groups, t // tq),
        in_specs=[
            pl.BlockSpec((1, tq, w), lambda bi, p, i: (bi, i, q_blk + p)),
            pl.BlockSpec((1, t, w), lambda bi, p, i: (bi, 0, k_blk + p)),
            pl.BlockSpec((1, t, w), lambda bi, p, i: (bi, 0, v_blk + p)),
        ],
        out_specs=pl.BlockSpec((1, tq, w), lambda bi, p, i: (bi, i, p)),
        out_shape=jax.ShapeDtypeStruct((b, t, SB_HEADS * HEAD_DIM), BF16),
        compiler_params=_params(("parallel", "parallel", "arbitrary")),
        name="sb_attention",
    )(pa, pa, pa)


def _diff_kernel(lam_ref, gain_ref, q_ref, k_ref, v_ref, o_ref, s_ref, *, tq, heads, lam_init):
    i = pl.program_id(2)
    lane = lax.broadcasted_iota(jnp.int32, (tq, LANES), 1)
    row = lax.broadcasted_iota(jnp.int32, (tq, tq), 0)
    col = lax.broadcasted_iota(jnp.int32, (tq, tq), 1)
    causal = col <= row
    causal2 = jnp.concatenate([causal, causal], axis=0)
    q2 = []
    for h in range(heads):
        qs = (q_ref[0, :, h * LANES:(h + 1) * LANES].astype(F32) * (HEAD_DIM ** -0.5)).astype(BF16)
        zero = jnp.zeros_like(qs)
        q2.append(jnp.concatenate([jnp.where(lane < HEAD_DIM, qs, zero),
                                   jnp.where(lane < HEAD_DIM, zero, qs)], axis=0))

    def scores(start, width, diagonal):
        out = []
        for h in range(heads):
            s = _dot_nt(q2[h], k_ref[0, pl.ds(start, width), h * LANES:(h + 1) * LANES]) * LOG2_E
            out.append(jnp.where(causal2, s, NEG_BIG) if diagonal else s)
        return out

    def weigh(ps, start, width):
        ones = jnp.ones((width, LANES), BF16)
        return [_dot(p, jnp.concatenate([v_ref[0, pl.ds(start, width), h * LANES:(h + 1) * LANES], ones], axis=1))
                for h, p in enumerate(ps)]

    s_refs = [s_ref.at[pl.ds(h * 2 * tq, 2 * tq)] for h in range(heads)]
    accs = _softmax_sweep(s_refs, rows=2 * tq, acc_lanes=2 * LANES, first=0, q0=pl.multiple_of(i * tq, tq), tq=tq,
                          big=2 * tq, scores_fn=scores, weigh_fn=weigh)
    lp = lam_ref[...]
    lam = (jnp.exp(jnp.sum(lp[0:1] * lp[1:2], axis=1, keepdims=True))
           - jnp.exp(jnp.sum(lp[2:3] * lp[3:4], axis=1, keepdims=True)) + lam_init)
    out = []
    for acc in accs:
        o = acc[:, :LANES] / acc[:, LANES:]
        od = o[:tq] - lam * o[tq:]
        od = od * lax.rsqrt(jnp.mean(od * od, axis=-1, keepdims=True) + RMS_EPS)
        out.append(od * gain_ref[...] * (1.0 - lam_init))
    o_ref[0] = jnp.concatenate(out, axis=1).astype(o_ref.dtype)


def _diff_attention(pa, lam_params, gain, *, tq, heads, q_blk, k_blk, v_blk, lam_init):
    b, t, _ = pa.shape
    w = heads * LANES
    return pl.pallas_call(
        functools.partial(_diff_kernel, tq=tq, heads=heads, lam_init=lam_init),
        grid=(b, DIFF_HEADS // heads, t // tq),
        in_specs=[
            pl.BlockSpec((4, HEAD_DIM), lambda bi, h, i: (0, 0)),
            pl.BlockSpec((1, LANES), lambda bi, h, i: (0, 0)),
            pl.BlockSpec((1, tq, w), lambda bi, h, i: (bi, i, q_blk + h)),
            pl.BlockSpec((1, t, w), lambda bi, h, i: (bi, 0, k_blk + h)),
            pl.BlockSpec((1, t, w), lambda bi, h, i: (bi, 0, v_blk + h)),
        ],
        out_specs=pl.BlockSpec((1, tq, w), lambda bi, h, i: (bi, i, h)),
        out_shape=jax.ShapeDtypeStruct((b, t, DIFF_HEADS * LANES), BF16),
        scratch_shapes=[pltpu.VMEM((heads * 2 * tq, t), F32)],
        compiler_params=_params(("parallel", "parallel", "arbitrary")),
        name="diff_attention",
    )(lam_params, gain.reshape(1, LANES), pa, pa, pa)


def _compress_kernel(x_ref, pos_ref, w1_ref, w2_ref, o_ref):
    x = x_ref[0, 0]
    n16, half = x.shape
    xa = (x + pos_ref[0:1]).astype(BF16)
    xb = (x + pos_ref[1:2]).astype(BF16)
    w1 = w1_ref[0]
    first = _dot(xa, w1[:half])
    second = _dot(xb, w1[half:])
    h = first + pltpu.roll(second, n16 - 1, 0)
    g = 0.5 * h * (1.0 + jnp.tanh(math.sqrt(2.0 / math.pi) * (h + 0.044715 * (h * h * h))))
    o_ref[0, 0] = _dot(g.astype(BF16), w2_ref[0])


def _compress(x16, pos2, w1, w2, *, layer):
    b, n_streams, n16, half = x16.shape
    d = w2.shape[-1]
    return pl.pallas_call(
        _compress_kernel,
        grid=(b, n_streams),
        in_specs=[
            pl.BlockSpec((1, 1, n16, half), lambda bi, s: (bi, s, 0, 0)),
            pl.BlockSpec((None, 2, half), lambda bi, s: (layer, 0, 0)),
            pl.BlockSpec((None, 1, 2 * half, w1.shape[-1]), lambda bi, s: (layer, s // NSA_KV_HEADS, 0, 0)),
            pl.BlockSpec((None, 1, w2.shape[2], d), lambda bi, s: (layer, s // NSA_KV_HEADS, 0, 0)),
        ],
        out_specs=pl.BlockSpec((1, 1, n16, d), lambda bi, s: (bi, s, 0, 0)),
        out_shape=jax.ShapeDtypeStruct((b, n_streams, n16, d), F32),
        compiler_params=_params(("parallel", "arbitrary")),
        name="nsa_compress",
    )(x16, pos2, w1, w2)


def _nsa_kernel(q_ref, g_ref, ck_ref, cv_ref, ovt_ref, gx_ref, ks_ref, vs_ref, kw_ref, vw_ref, o_ref,
                kse_ref, kso_ref, vse_ref, vso_ref, kwd_ref, vwe_ref, vwo_ref, ckd_ref, cvd_ref, s_ref,
                *, tq, top_n):
    i = pl.program_id(2)
    grp = NSA_GROUP
    d = HEAD_DIM
    q0 = i * tq
    n_cmp = ck_ref.shape[2]
    n_slc = ovt_ref.shape[0]
    t = ks_ref.shape[1]
    rows = grp * tq
    half = rows // 2

    @pl.when(i == 0)
    def _():
        onehot = jnp.where(lax.broadcasted_iota(jnp.int32, (t, d), 0) // SLC_BLOCK
                           == lax.broadcasted_iota(jnp.int32, (t, d), 1), 1.0, 0.0)
        ones = jnp.ones((t, d), F32)

        def put(ref, left, right):
            ref[...] = jnp.concatenate([left, right], axis=1).astype(BF16)

        first_kv = pl.program_id(1) == 0

        def own_head(ref):
            both = ref[0].astype(F32)
            return jnp.where(first_kv, both[:, :d], both[:, d:])

        ks, vs, kw, vw = own_head(ks_ref), own_head(vs_ref), own_head(kw_ref), own_head(vw_ref)
        put(kse_ref, ks, onehot)
        put(kso_ref, onehot, ks)
        put(vse_ref, vs, ones)
        put(vso_ref, ones, vs)
        put(kwd_ref, kw, kw)
        put(vwe_ref, vw, ones)
        put(vwo_ref, ones, vw)
        put(ckd_ref, ck_ref[0, 0], ck_ref[0, 0])
        put(cvd_ref, cv_ref[0, 0], cv_ref[0, 0])

    lane = lax.broadcasted_iota(jnp.int32, (tq, LANES), 1)
    left = lane < d
    qs = (q_ref[0].astype(F32) * (d ** -0.5)).astype(BF16)
    pairs = [qs[:, :LANES], qs[:, LANES:]]
    zero = jnp.zeros((tq, LANES), BF16)
    qz = jnp.concatenate([jnp.where(left, pairs[0], zero), jnp.where(left, pairs[1], zero),
                          jnp.where(left, zero, pairs[0]), jnp.where(left, zero, pairs[1])], axis=0)
    t4 = q0 + lax.rem(lax.broadcasted_iota(jnp.int32, (rows, 1), 0), tq)

    def rep4(x):
        return jnp.concatenate([x] * grp, axis=0)

    def weigh(p, start, width, ve_ref, vo_ref):
        return jnp.concatenate([_dot(p[:half], ve_ref[pl.ds(start, width), :]),
                                _dot(p[half:], vo_ref[pl.ds(start, width), :])], axis=0)

    def normalise(acc):
        out = []
        for pr in range(2):
            even, odd = acc[pr * tq:(pr + 1) * tq], acc[half + pr * tq:half + (pr + 1) * tq]
            den = pltpu.roll(jnp.where(left, odd, even), d, 1)
            out.append(jnp.where(left, even, odd) / den)
        return out

    w_width = (-(-WINDOW // tq) + 1) * tq
    w_start = pl.multiple_of(jnp.maximum(q0 + tq - w_width, 0), tq)
    sw = _dot_nt(qz, kwd_ref[pl.ds(w_start, w_width), :])
    kpos = w_start + lax.broadcasted_iota(jnp.int32, (1, w_width), 1)
    sw = jnp.where((kpos <= t4) & (kpos > t4 - WINDOW), sw, NEG_BIG)
    pw = jnp.exp(sw - jnp.max(sw, axis=1, keepdims=True))
    o_win = normalise(weigh(pw.astype(BF16), w_start, w_width, vwe_ref, vwo_ref))

    s = _dot_nt(qz, ckd_ref[...])
    cmp_end = CMP_STRIDE * lax.broadcasted_iota(jnp.int32, (1, n_cmp), 1) + (CMP_BLOCK - 1)
    cmask = cmp_end <= t4
    s = jnp.where(cmask, s, NEG_BIG)
    e = jnp.exp(s - jnp.max(s, axis=1, keepdims=True))
    p = e / jnp.sum(e, axis=1, keepdims=True)
    p = jnp.where(cmask, p, 0.0)
    o_cmp = _dot(p.astype(BF16), cvd_ref[...])

    pg = p[0:tq]
    for h in range(1, grp):
        pg = pg + p[h * tq:(h + 1) * tq]
    p_hi = pg.astype(BF16)
    r1 = pg - p_hi.astype(F32)
    p_mid = r1.astype(BF16)
    p_lo = (r1 - p_mid.astype(F32)).astype(BF16)
    ovt = ovt_ref[...]
    imp_t = _dot_nt(ovt, p_hi) + _dot_nt(ovt, p_mid) + _dot_nt(ovt, p_lo)
    blk = lax.broadcasted_iota(jnp.int32, (n_slc, tq), 0)
    cur = (q0 + lax.broadcasted_iota(jnp.int32, (n_slc, tq), 1)) // SLC_BLOCK
    forced = (blk == 0) | (blk == cur) | (blk == cur - 1)
    score = jnp.where(forced, jnp.inf, jnp.where(blk <= cur, imp_t, -jnp.inf))
    rank = jnp.zeros((n_slc, tq), F32)
    for j in range(n_slc):
        sj = score[j:j + 1, :]
        tie = jnp.where(blk > j, 1.0, 0.0)
        rank = rank + jnp.where(sj > score, 1.0, jnp.where(sj == score, tie, 0.0))
    sel_t = jnp.where(rank < top_n, 1.0, 0.0)
    if n_slc < d:
        sel_t = jnp.concatenate([sel_t, jnp.zeros((d - n_slc, tq), F32)], axis=0)
    sel_t2 = jnp.concatenate([sel_t, sel_t], axis=0).astype(BF16)
    eye = jnp.where(lax.broadcasted_iota(jnp.int32, (tq, tq), 0)
                    == lax.broadcasted_iota(jnp.int32, (tq, tq), 1), 1.0, 0.0).astype(BF16)
    sel2 = _dot_nt(eye, sel_t2)
    bias = ((sel2 - 1.0) * (-NEG_BIG)).astype(BF16)
    q_even = jnp.concatenate([jnp.where(left, pairs[0], bias), jnp.where(left, pairs[1], bias)], axis=0)
    q_odd = jnp.concatenate([jnp.where(left, bias, pairs[0]), jnp.where(left, bias, pairs[1])], axis=0)

    row = lax.broadcasted_iota(jnp.int32, (tq, tq), 0)
    col = lax.broadcasted_iota(jnp.int32, (tq, tq), 1)
    causal4 = rep4(col <= row)

    def slc_scores(start, width, diagonal):
        s = jnp.concatenate([_dot_nt(q_even, kse_ref[pl.ds(start, width), :]),
                             _dot_nt(q_odd, kso_ref[pl.ds(start, width), :])], axis=0) * LOG2_E
        return jnp.where(causal4, s, NEG_BIG) if diagonal else s

    (acc,) = _softmax_sweep((s_ref,), rows=rows, acc_lanes=LANES, first=0, q0=pl.multiple_of(q0, tq), tq=tq,
                            big=4 * tq, scores_fn=lambda *a: (slc_scores(*a),),
                            weigh_fn=lambda ps, start, width: (weigh(ps[0], start, width, vse_ref, vso_ref),))
    o_slc = normalise(acc)

    gate = 1.0 / (1.0 + jnp.exp(-g_ref[0]))
    g_hi = gate.astype(BF16)
    g_r = gate - g_hi.astype(F32)
    g_mid = g_r.astype(BF16)
    g_lo = (g_r - g_mid.astype(F32)).astype(BF16)
    spread = gx_ref[...]
    gx = _dot(g_hi, spread) + _dot(g_mid, spread) + _dot(g_lo, spread)

    out = []
    for pr in range(2):
        cmp_pr = jnp.where(left, o_cmp[pr * tq:(pr + 1) * tq], o_cmp[half + pr * tq:half + (pr + 1) * tq])
        mixed = 0.0
        for br, o_br in enumerate((cmp_pr, o_slc[pr], o_win[pr])):
            blk_idx = pr * N_BRANCH + br
            mixed = mixed + gx[:, blk_idx * LANES:(blk_idx + 1) * LANES] * o_br
        out.append(mixed)
    o_ref[0] = jnp.concatenate(out, axis=1).astype(o_ref.dtype)


def _nsa_attention(pa, pf, ckv, ovt, *, tq, q_blk, kv_blk, g_blk, top_n):
    b, t, _ = pa.shape
    n16 = ckv.shape[2]
    n_slc = ovt.shape[0]
    qw = NSA_GROUP * HEAD_DIM
    kv_specs = [pl.BlockSpec((1, t, LANES), functools.partial(lambda bi, j, i, c: (bi, 0, c), c=kv_blk + n))
                for n in range(4)]
    return pl.pallas_call(
        functools.partial(_nsa_kernel, tq=tq, top_n=top_n),
        grid=(b, NSA_KV_HEADS, t // tq),
        in_specs=[
            pl.BlockSpec((1, tq, qw), lambda bi, j, i: (bi, i, q_blk + j)),
            pl.BlockSpec((1, tq, LANES), lambda bi, j, i: (bi, i, g_blk + j)),
            pl.BlockSpec((1, 1, n16, HEAD_DIM), lambda bi, j, i: (bi, j, 0, 0)),
            pl.BlockSpec((1, 1, n16, HEAD_DIM), lambda bi, j, i: (bi, NSA_KV_HEADS + j, 0, 0)),
            pl.BlockSpec((n_slc, n16), lambda bi, j, i: (0, 0)),
            pl.BlockSpec((LANES, 2 * N_BRANCH * LANES), lambda bi, j, i: (0, 0)),
            *kv_specs,
        ],
        out_specs=pl.BlockSpec((1, tq, qw), lambda bi, j, i: (bi, i, j)),
        out_shape=jax.ShapeDtypeStruct((b, t, NSA_HEADS * HEAD_DIM), BF16),
        scratch_shapes=[pltpu.VMEM((t, LANES), BF16)] * 7 + [pltpu.VMEM((n16, LANES), BF16)] * 2
                       + [pltpu.VMEM((NSA_GROUP * tq, t), F32)],
        compiler_params=_params(("arbitrary", "arbitrary", "arbitrary")),
        name="nsa_attention",
    )(pa, pf, ckv, ckv, ovt, _gate_spread(), pa, pa, pa, pa)


def _conv_kernel(cb_ref, cc_ref, ch_ref, w_ref, o_ref, prev_ref):
    @pl.when(pl.program_id(1) == 0)
    def _():
        prev_ref[...] = jnp.zeros_like(prev_ref)

    u = cc_ref[0] * ch_ref[0]
    tt = u.shape[0]
    row = lax.broadcasted_iota(jnp.int32, u.shape, 0)
    last1 = prev_ref[7:8]
    last2 = prev_ref[6:7]
    u1 = jnp.where(row >= 1, pltpu.roll(u, 1, 0), last1)
    u2 = jnp.where(row >= 2, pltpu.roll(u, 2, 0), jnp.where(row == 1, last1, last2))
    w = w_ref[...]
    o_ref[0] = (cb_ref[0] * (w[0:1] * u2 + w[1:2] * u1 + w[2:3] * u)).astype(o_ref.dtype)
    prev_ref[...] = u[tt - 8:tt]


def _conv_mixer(pf, conv_w, *, tt):
    b, t, _ = pf.shape
    c = CONV_CH
    return pl.pallas_call(
        _conv_kernel,
        grid=(b, t // tt),
        in_specs=[
            pl.BlockSpec((1, tt, c), lambda bi, i: (bi, i, 0)),
            pl.BlockSpec((1, tt, c), lambda bi, i: (bi, i, 1)),
            pl.BlockSpec((1, tt, c), lambda bi, i: (bi, i, 2)),
            pl.BlockSpec((CONV_WIDTH, c), lambda bi, i: (0, 0)),
        ],
        out_specs=pl.BlockSpec((1, tt, c), lambda bi, i: (bi, i, 0)),
        out_shape=jax.ShapeDtypeStruct((b, t, c), BF16),
        scratch_shapes=[pltpu.VMEM((8, c), F32)],
        compiler_params=_params(("parallel", "arbitrary")),
        name="conv_mixer",
    )(pf, pf, pf, conv_w)


def _out_ln_kernel(a_ref, b_ref, c_ref, d_ref, w_ref, x_ref, g_ref, beta_ref, o_ref, *, alpha):
    kw = a_ref.shape[1]
    mix = _dot(a_ref[...], w_ref[0:kw])
    for n, r in enumerate((b_ref, c_ref, d_ref), start=1):
        mix = mix + _dot(r[...], w_ref[n * kw:(n + 1) * kw])
    o_ref[...] = _layer_norm(alpha * x_ref[...] + mix, g_ref[...], beta_ref[...])


def _out_ln(parts, w, x, g, b, *, layer, alpha, tm):
    n, d = x.shape
    kw = parts[0].shape[1]
    part_spec = pl.BlockSpec((tm, kw), lambda i: (i, 0))
    return pl.pallas_call(
        functools.partial(_out_ln_kernel, alpha=alpha),
        grid=(n // tm,),
        in_specs=[part_spec, part_spec, part_spec, part_spec,
                  pl.BlockSpec((None,) + w.shape[1:], lambda i: (layer, 0, 0)),
                  pl.BlockSpec((tm, d), lambda i: (i, 0)),
                  pl.BlockSpec((1, d), lambda i: (0, 0)),
                  pl.BlockSpec((1, d), lambda i: (0, 0))],
        out_specs=pl.BlockSpec((tm, d), lambda i: (i, 0)),
        out_shape=jax.ShapeDtypeStruct((n, d), F32),
        compiler_params=_params(("parallel",)),
        name="out_ln",
    )(*parts, w, x, g.reshape(1, d), b.reshape(1, d))


def _gate_spread():
    m = np.zeros((LANES, 2 * N_BRANCH * LANES), np.float32)
    for pr in range(2):
        for br in range(N_BRANCH):
            for lane in range(LANES):
                head = 2 * pr + lane // HEAD_DIM
                m[N_BRANCH * head + br, (pr * N_BRANCH + br) * LANES + lane] = 1.0
    return jnp.asarray(m, BF16)


def _overlap_t(t):
    n16 = t // CMP_STRIDE
    n_slc = t // SLC_BLOCK
    c_start = CMP_STRIDE * np.arange(n16)
    j_start = SLC_BLOCK * np.arange(n_slc)
    ov = ((c_start[None, :] < j_start[:, None] + SLC_BLOCK)
          & (c_start[None, :] + CMP_BLOCK > j_start[:, None])).astype(np.float32)
    ov[:, n16 - 1] = 0.0
    return jnp.asarray(ov, BF16)


def _mixer(hf, batch, w_proj, w_out, ln_g, ln_b, diff_lam, diff_gain, pos2, cmp_w1, cmp_w2,
           conv_w, layer, alpha):
    n, _ = hf.shape
    t = n // batch
    pa, pf = _proj(hf, w_proj, layer=layer, width_a=ATT_WIDTH, tm=1024, tn=1024)
    pa, pf = pa.reshape(batch, t, -1), pf.reshape(batch, t, -1)

    o_sb = _sb_attention(pa, tq=256, pairs=4, q_blk=0, k_blk=1, v_blk=2)
    lam_init = 0.8 - 0.6 * math.exp(-0.3 * layer)
    o_df = _diff_attention(pa, diff_lam[layer], diff_gain[layer], tq=512, heads=2, q_blk=6, k_blk=8, v_blk=10,
                           lam_init=lam_init)

    n16 = t // CMP_STRIDE
    kvc = pf[:, :, 3 * CONV_CH:3 * CONV_CH + 2 * LANES].reshape(batch, t, 2, NSA_KV_HEADS, HEAD_DIM)
    x16 = kvc.transpose(0, 2, 3, 1, 4).reshape(batch, 2 * NSA_KV_HEADS, n16, CMP_STRIDE * HEAD_DIM)
    ckv = _compress(x16, pos2, cmp_w1, cmp_w2, layer=layer)
    n_slc = t // SLC_BLOCK
    o_ns = _nsa_attention(pa, pf, ckv, _overlap_t(t), tq=256, q_blk=12, kv_blk=28, g_blk=14,
                          top_n=min(SLC_TOPN, n_slc))

    o_cv = _conv_mixer(pf, conv_w[layer], tt=512)
    parts = [o.reshape(n, -1) for o in (o_sb, o_df, o_ns, o_cv)]
    return _out_ln(parts, w_out, hf, ln_g, ln_b, layer=layer, alpha=alpha, tm=512)


def _seg(w_in, name, lo=0, hi=None):
    off = _SEG_OFF[name]
    hi = _SEG_W[name] if hi is None else hi
    return w_in[:, :, off + lo:off + hi]


def kernel(x, ln_g, ln_b, ffn_w1, ffn_w3, ffn_w2, w_in, w_out, diff_lam, diff_gain, cmp_pos, cmp_wk1,
           cmp_wk2, cmp_wv1, cmp_wv2, conv_w):
    batch, t, d = x.shape
    depth = ln_g.shape[0]
    alpha = (2 * depth) ** 0.25
    n = batch * t

    w1b, w3b, w2b = ffn_w1.astype(BF16), ffn_w3.astype(BF16), ffn_w2.astype(BF16)
    gate_pad = jnp.zeros(w_in.shape[:2] + (LANES - _GATES_PER_KV,), w_in.dtype)
    cols = [w_in[:, :, :_SEG_OFF["ns_kc"]], w_in[:, :, _SEG_OFF["ns_ks"]:_SEG_OFF["ns_g"]],
            _seg(w_in, "cv_b"), _seg(w_in, "cv_c"), _seg(w_in, "cv_h"), _seg(w_in, "ns_kc"), _seg(w_in, "ns_vc")]
    for j in range(NSA_KV_HEADS):
        cols += [_seg(w_in, "ns_g", j * _GATES_PER_KV, (j + 1) * _GATES_PER_KV), gate_pad]
    w_proj = jnp.concatenate(cols, axis=-1).astype(BF16)
    w_outb = w_out.astype(BF16)
    half = CMP_STRIDE * HEAD_DIM
    pos2 = cmp_pos.reshape(depth, 2, half)
    cmp_w1 = jnp.stack([cmp_wk1, cmp_wv1], axis=1).astype(BF16)
    cmp_w2 = jnp.stack([cmp_wk2, cmp_wv2], axis=1).astype(BF16)

    hf = x.reshape(n, d)
    for l in range(depth):
        hf = _ffn_ln(hf, w1b, w3b, w2b, ln_g[l, 0], ln_b[l, 0], layer=l, slot=0, alpha=alpha, tm=FFN_ROWS,
                     tf=FFN_TILE)
        hf = _mixer(hf, batch, w_proj, w_outb, ln_g[l, 1], ln_b[l, 1], diff_lam, diff_gain,
                    pos2, cmp_w1, cmp_w2, conv_w, l, alpha)
        hf = _ffn_ln(hf, w1b, w3b, w2b, ln_g[l, 2], ln_b[l, 2], layer=l, slot=1, alpha=alpha, tm=FFN_ROWS,
                     tf=FFN_TILE)
    return hf.reshape(batch, t, d)
```

```python
import functools
import math

import numpy as np
import jax
import jax.numpy as jnp
from jax import lax
from jax.experimental import pallas as pl
from jax.experimental.pallas import tpu as pltpu

F32 = jnp.float32
BF16 = jnp.bfloat16

HEAD_DIM = 64
SB_HEADS = 8
DIFF_HEADS = 4
NSA_HEADS = 8
NSA_KV_HEADS = 2
NSA_GROUP = NSA_HEADS // NSA_KV_HEADS
CMP_BLOCK = 32
CMP_STRIDE = 16
SLC_BLOCK = 64
SLC_TOPN = 16
WINDOW = 512
N_BRANCH = 3
CONV_CH = 512
CONV_WIDTH = 3
LN_EPS = 1e-5
RMS_EPS = 1e-5
NEG_BIG = -1e30
LOG2_E = math.log2(math.e)
EXP2_UNDERFLOW = -150.0
LANES = 128
VMEM_LIMIT = 56 * 1024 * 1024
FFN_TILE = 256
FFN_ROWS = 1024

_SEG_NAMES = ("sb_q", "sb_k", "sb_v", "df_q", "df_k", "df_v", "ns_q", "ns_kc", "ns_vc",
              "ns_ks", "ns_vs", "ns_kw", "ns_vw", "ns_g", "cv_b", "cv_c", "cv_h")
_SEG_WIDTHS = (512, 512, 512, 512, 512, 512, 512, 128, 128, 128, 128, 128, 128,
               NSA_HEADS * N_BRANCH, 512, 512, 512)
_SEG_OFF = dict(zip(_SEG_NAMES, np.cumsum((0,) + _SEG_WIDTHS[:-1]).tolist()))
_SEG_W = dict(zip(_SEG_NAMES, _SEG_WIDTHS))


_GATES_PER_KV = NSA_GROUP * N_BRANCH
ATT_WIDTH = _SEG_OFF["ns_kc"] + 4 * _SEG_W["ns_ks"]


def _params(sem):
    return pltpu.CompilerParams(dimension_semantics=sem, vmem_limit_bytes=VMEM_LIMIT)


def _layer_norm(y, g, b):
    mu = jnp.mean(y, axis=-1, keepdims=True)
    d = y - mu
    var = jnp.mean(d * d, axis=-1, keepdims=True)
    return d * lax.rsqrt(var + LN_EPS) * g + b


def _dot(a, b):
    return jnp.dot(a, b, preferred_element_type=F32)


def _lane_group_max(s, mx):
    for g in range(s.shape[1] // LANES):
        mx = jnp.maximum(mx, s[:, g * LANES:(g + 1) * LANES])
    return mx


def _softmax_sweep(s_refs, *, rows, acc_lanes, q0, tq, big, scores_fn, weigh_fn):
    n_big = q0 // big
    mid = pl.multiple_of(n_big * big, tq)
    tail = (q0 - mid) // tq
    tail_widths = [(k + 1) * tq for k in range(big // tq)]

    def chunks(fn, init):
        return lax.fori_loop(0, n_big, lambda c, carry: fn(pl.multiple_of(c * big, big), big, carry), init)

    def scores(start, width, mxs, is_tail=False):
        out = []
        for ref, s, mx in zip(s_refs, scores_fn(start, width, is_tail), mxs):
            ref[:, pl.ds(start, width)] = s
            out.append(_lane_group_max(s, mx))
        return tuple(out)

    mxs = chunks(scores, (jnp.full((rows, LANES), NEG_BIG, F32),) * len(s_refs))
    mxs = lax.switch(tail, [functools.partial(scores, mid, w, is_tail=True) for w in tail_widths], mxs)
    ms = [jnp.broadcast_to(jnp.max(mx, axis=1, keepdims=True), (rows, LANES)) for mx in mxs]

    def weigh(start, width, accs):
        ps = [jnp.exp2(ref[:, pl.ds(start, width)] - jnp.concatenate([m] * (width // LANES), axis=1)).astype(BF16)
              for ref, m in zip(s_refs, ms)]
        return tuple(acc + w for acc, w in zip(accs, weigh_fn(ps, start, width)))

    accs = chunks(weigh, (jnp.zeros((rows, acc_lanes), F32),) * len(s_refs))
    return lax.switch(tail, [functools.partial(weigh, mid, w) for w in tail_widths], accs)


def _dot_nt(a, b):
    return lax.dot_general(a, b, (((1,), (1,)), ((), ())), preferred_element_type=F32)


def _ffn_ln_kernel(x_ref, w1_ref, w3_ref, w2_ref, g_ref, b_ref, o_ref, xb_ref, *, alpha):
    j = pl.program_id(1)

    @pl.when(j == 0)
    def _():
        xb_ref[...] = x_ref[...].astype(BF16)
        o_ref[...] = jnp.zeros_like(o_ref)

    xb = xb_ref[...]
    a = _dot(xb, w1_ref[...])
    b = _dot(xb, w3_ref[...])
    hm = (a / (1.0 + jnp.exp(-a))) * b
    o_ref[...] += _dot(hm.astype(BF16), w2_ref[...])

    @pl.when(j == pl.num_programs(1) - 1)
    def _():
        y = alpha * x_ref[...] + 0.5 * o_ref[...]
        o_ref[...] = _layer_norm(y, g_ref[...], b_ref[...])


def _ffn_ln(x, w1, w3, w2, g, b, *, layer, slot, alpha, tm, tf):
    n, d = x.shape
    f = w1.shape[-1]
    return pl.pallas_call(
        functools.partial(_ffn_ln_kernel, alpha=alpha),
        grid=(n // tm, f // tf),
        in_specs=[
            pl.BlockSpec((tm, d), lambda i, j: (i, 0)),
            pl.BlockSpec((None, None, d, tf), lambda i, j: (layer, slot, 0, j)),
            pl.BlockSpec((None, None, d, tf), lambda i, j: (layer, slot, 0, j)),
            pl.BlockSpec((None, None, tf, d), lambda i, j: (layer, slot, j, 0)),
            pl.BlockSpec((1, d), lambda i, j: (0, 0)),
            pl.BlockSpec((1, d), lambda i, j: (0, 0)),
        ],
        out_specs=pl.BlockSpec((tm, d), lambda i, j: (i, 0)),
        out_shape=jax.ShapeDtypeStruct((n, d), F32),
        scratch_shapes=[pltpu.VMEM((tm, d), BF16)],
        compiler_params=_params(("parallel", "arbitrary")),
        name="ffn_ln",
    )(x, w1, w3, w2, g.reshape(1, d), b.reshape(1, d))


def _proj_kernel(x_ref, w_ref, oa_ref, of_ref, xb_ref, *, n_a):
    j = pl.program_id(1)

    @pl.when(j == 0)
    def _():
        xb_ref[...] = x_ref[...].astype(BF16)

    @pl.when(j < n_a)
    def _():
        oa_ref[...] = _dot(xb_ref[...], w_ref[...]).astype(oa_ref.dtype)

    @pl.when(j >= n_a)
    def _():
        of_ref[...] = _dot(xb_ref[...], w_ref[...])


def _proj(x, w, *, layer, width_a, tm, tn):
    n, d = x.shape
    m = w.shape[-1]
    n_a = width_a // tn
    return pl.pallas_call(
        functools.partial(_proj_kernel, n_a=n_a),
        grid=(n // tm, m // tn),
        in_specs=[pl.BlockSpec((tm, d), lambda i, j: (i, 0)),
                  pl.BlockSpec((None, d, tn), lambda i, j: (layer, 0, j))],
        out_specs=[pl.BlockSpec((tm, tn), lambda i, j: (i, jnp.minimum(j, n_a - 1))),
                   pl.BlockSpec((tm, tn), lambda i, j: (i, jnp.maximum(j - n_a, 0)))],
        out_shape=[jax.ShapeDtypeStruct((n, width_a), BF16), jax.ShapeDtypeStruct((n, m - width_a), F32)],
        scratch_shapes=[pltpu.VMEM((tm, d), BF16)],
        compiler_params=_params(("parallel", "arbitrary")),
        name="proj",
    )(x, w)


def _sb_kernel(q_ref, k_ref, v_ref, o_ref, *, tq, pairs):
    i = pl.program_id(2)
    q0 = i * tq
    lane = lax.broadcasted_iota(jnp.int32, (tq, LANES), 1)
    tri = jnp.where(lax.broadcasted_iota(jnp.int32, (tq, tq), 0)
                    >= lax.broadcasted_iota(jnp.int32, (tq, tq), 1), 1.0, 0.0).astype(BF16)
    qpos = q0 + lax.rem(lax.broadcasted_iota(jnp.int32, (2 * tq, 1), 0), tq)
    q2 = []
    for p in range(pairs):
        qs = (q_ref[0, :, p * LANES:(p + 1) * LANES].astype(F32) * (HEAD_DIM ** -0.5)).astype(BF16)
        zero = jnp.zeros_like(qs)
        q2.append(jnp.concatenate([jnp.where(lane < HEAD_DIM, qs, zero),
                                   jnp.where(lane < HEAD_DIM, zero, qs)], axis=0))

    def block(start, state, keep=None):
        width = tq
        masked = keep is not None
        out = []
        for p, (c, acc) in enumerate(state):
            ks = k_ref[0, pl.ds(start, width), p * LANES:(p + 1) * LANES]
            vs = v_ref[0, pl.ds(start, width), p * LANES:(p + 1) * LANES]
            z = _dot_nt(q2[p], ks) * LOG2_E
            nz = -z
            lg = jnp.minimum(nz, 0.0) - jnp.log2(1.0 + jnp.exp2(jnp.minimum(z, nz)))
            if masked:
                lg = jnp.where(keep, lg, 0.0)
            hi = lg.astype(BF16)
            lo = (lg - hi.astype(F32)).astype(BF16)
            suffix = _dot(hi, tri) + _dot(lo, tri)
            a = jnp.exp2(z + suffix + c)
            if masked:
                a = jnp.where(keep, a, 0.0)
            out.append((c + suffix[:, 0:1], acc + _dot(a.astype(BF16), vs)))
        return tuple(out)

    state = ((jnp.zeros((2 * tq, 1), F32), jnp.zeros((2 * tq, LANES), F32)),) * pairs
    key = lax.broadcasted_iota(jnp.int32, (1, tq), 1)
    state = block(pl.multiple_of(q0, tq), state, keep=q0 + key < qpos)
    state = block(pl.multiple_of(jnp.maximum(q0 - tq, 0), tq), state, keep=key < q0)

    def live(state):
        top = functools.reduce(jnp.maximum, [c for c, _ in state])
        return (jnp.max(top) > EXP2_UNDERFLOW).astype(jnp.int32)

    def cond(carry):
        return (carry[0] >= 0) & (carry[1] > 0)

    def body(carry):
        kt, _, state = carry
        state = block(pl.multiple_of(kt * tq, tq), state)
        return kt - 1, live(state), state

    _, _, state = lax.while_loop(cond, body, (i - 2, live(state), state))
    o_ref[0] = jnp.concatenate([jnp.where(lane < HEAD_DIM, acc[:tq], acc[tq:]) for _, acc in state],
                               axis=1).astype(o_ref.dtype)


def _sb_attention(pa, *, tq, pairs, q_blk, k_blk, v_blk):
    b, t, _ = pa.shape
    w = pairs * LANES
    n_groups = SB_HEADS // (2 * pairs)
    return pl.pallas_call(
        functools.partial(_sb_kernel, tq=tq, pairs=pairs),
        grid=(b, n_groups, t // tq),
        in_specs=[
            pl.BlockSpec((1, tq, w), lambda bi, p, i: (bi, i, q_blk + p)),
            pl.BlockSpec((1, t, w), lambda bi, p, i: (bi, 0, k_blk + p)),
            pl.BlockSpec((1, t, w), lambda bi, p, i: (bi, 0, v_blk + p)),
        ],
        out_specs=pl.BlockSpec((1, tq, w), lambda bi, p, i: (bi, i, p)),
        out_shape=jax.ShapeDtypeStruct((b, t, SB_HEADS * HEAD_DIM), BF16),
        compiler_params=_params(("parallel", "parallel", "arbitrary")),
        name="sb_attention",
    )(pa, pa, pa)


def _diff_kernel(lam_ref, gain_ref, q_ref, k_ref, v_ref, o_ref, s_ref, *, tq, heads, lam_init):
    i = pl.program_id(2)
    q0 = pl.multiple_of(i * tq, tq)
    lane = lax.broadcasted_iota(jnp.int32, (tq, LANES), 1)
    qpos = q0 + lax.rem(lax.broadcasted_iota(jnp.int32, (2 * tq, 1), 0), tq)
    q2 = []
    for h in range(heads):
        qs = (q_ref[0, :, h * LANES:(h + 1) * LANES].astype(F32) * (HEAD_DIM ** -0.5)).astype(BF16)
        zero = jnp.zeros_like(qs)
        q2.append(jnp.concatenate([jnp.where(lane < HEAD_DIM, qs, zero),
                                   jnp.where(lane < HEAD_DIM, zero, qs)], axis=0))

    def scores(start, width, tail):
        out = []
        for h in range(heads):
            s = _dot_nt(q2[h], k_ref[0, pl.ds(start, width), h * LANES:(h + 1) * LANES]) * LOG2_E
            if tail:
                s = jnp.where(start + lax.broadcasted_iota(jnp.int32, (1, width), 1) <= qpos, s, NEG_BIG)
            out.append(s)
        return out

    def weigh(ps, start, width):
        ones = jnp.ones((width, LANES), BF16)
        return [_dot(p, jnp.concatenate([v_ref[0, pl.ds(start, width), h * LANES:(h + 1) * LANES], ones], axis=1))
                for h, p in enumerate(ps)]

    s_refs = [s_ref.at[pl.ds(h * 2 * tq, 2 * tq)] for h in range(heads)]
    accs = _softmax_sweep(s_refs, rows=2 * tq, acc_lanes=2 * LANES, q0=q0, tq=tq, big=2 * tq,
                          scores_fn=scores, weigh_fn=weigh)
    lp = lam_ref[...]
    lam = (jnp.exp(jnp.sum(lp[0:1] * lp[1:2], axis=1, keepdims=True))
           - jnp.exp(jnp.sum(lp[2:3] * lp[3:4], axis=1, keepdims=True)) + lam_init)
    out = []
    for acc in accs:
        o = acc[:, :LANES] / acc[:, LANES:]
        od = o[:tq] - lam * o[tq:]
        od = od * lax.rsqrt(jnp.mean(od * od, axis=-1, keepdims=True) + RMS_EPS)
        out.append(od * gain_ref[...] * (1.0 - lam_init))
    o_ref[0] = jnp.concatenate(out, axis=1).astype(o_ref.dtype)


def _diff_attention(pa, lam_params, gain, *, tq, heads, q_blk, k_blk, v_blk, lam_init):
    b, t, _ = pa.shape
    w = heads * LANES
    return pl.pallas_call(
        functools.partial(_diff_kernel, tq=tq, heads=heads, lam_init=lam_init),
        grid=(b, DIFF_HEADS // heads, t // tq),
        in_specs=[
            pl.BlockSpec((4, HEAD_DIM), lambda bi, h, i: (0, 0)),
            pl.BlockSpec((1, LANES), lambda bi, h, i: (0, 0)),
            pl.BlockSpec((1, tq, w), lambda bi, h, i: (bi, i, q_blk + h)),
            pl.BlockSpec((1, t, w), lambda bi, h, i: (bi, 0, k_blk + h)),
            pl.BlockSpec((1, t, w), lambda bi, h, i: (bi, 0, v_blk + h)),
        ],
        out_specs=pl.BlockSpec((1, tq, w), lambda bi, h, i: (bi, i, h)),
        out_shape=jax.ShapeDtypeStruct((b, t, DIFF_HEADS * LANES), BF16),
        scratch_shapes=[pltpu.VMEM((heads * 2 * tq, t), F32)],
        compiler_params=_params(("parallel", "parallel", "arbitrary")),
        name="diff_attention",
    )(lam_params, gain.reshape(1, LANES), pa, pa, pa)


def _compress_kernel(x_ref, pos_ref, w1_ref, w2_ref, o_ref):
    x = x_ref[0, 0]
    n16, half = x.shape
    xa = (x + pos_ref[0:1]).astype(BF16)
    xb = (x + pos_ref[1:2]).astype(BF16)
    w1 = w1_ref[0]
    first = _dot(xa, w1[:half])
    second = _dot(xb, w1[half:])
    h = first + pltpu.roll(second, n16 - 1, 0)
    g = 0.5 * h * (1.0 + jnp.tanh(math.sqrt(2.0 / math.pi) * (h + 0.044715 * (h * h * h))))
    o_ref[0, 0] = _dot(g.astype(BF16), w2_ref[0])


def _compress(x16, pos2, w1, w2, *, layer):
    b, n_streams, n16, half = x16.shape
    d = w2.shape[-1]
    return pl.pallas_call(
        _compress_kernel,
        grid=(b, n_streams),
        in_specs=[
            pl.BlockSpec((1, 1, n16, half), lambda bi, s: (bi, s, 0, 0)),
            pl.BlockSpec((None, 2, half), lambda bi, s: (layer, 0, 0)),
            pl.BlockSpec((None, 1, 2 * half, w1.shape[-1]), lambda bi, s: (layer, s // NSA_KV_HEADS, 0, 0)),
            pl.BlockSpec((None, 1, w2.shape[2], d), lambda bi, s: (layer, s // NSA_KV_HEADS, 0, 0)),
        ],
        out_specs=pl.BlockSpec((1, 1, n16, d), lambda bi, s: (bi, s, 0, 0)),
        out_shape=jax.ShapeDtypeStruct((b, n_streams, n16, d), F32),
        compiler_params=_params(("parallel", "arbitrary")),
        name="nsa_compress",
    )(x16, pos2, w1, w2)


def _nsa_kernel(q_ref, g_ref, ck_ref, cv_ref, ovt_ref, gx_ref, ks_ref, vs_ref, kw_ref, vw_ref, o_ref,
                kse_ref, kso_ref, vse_ref, vso_ref, kwd_ref, vwe_ref, vwo_ref, ckd_ref, cvd_ref, s_ref,
                *, tq, top_n):
    i = pl.program_id(2)
    grp = NSA_GROUP
    d = HEAD_DIM
    q0 = i * tq
    n_cmp = ck_ref.shape[2]
    n_slc = ovt_ref.shape[0]
    t = ks_ref.shape[1]
    rows = grp * tq
    half = rows // 2

    @pl.when(i == 0)
    def _():
        onehot = jnp.where(lax.broadcasted_iota(jnp.int32, (t, d), 0) // SLC_BLOCK
                           == lax.broadcasted_iota(jnp.int32, (t, d), 1), 1.0, 0.0)
        ones = jnp.ones((t, d), F32)

        def put(ref, left, right):
            ref[...] = jnp.concatenate([left, right], axis=1).astype(BF16)

        first_kv = pl.program_id(1) == 0

        def own_head(ref):
            both = ref[0].astype(F32)
            return jnp.where(first_kv, both[:, :d], both[:, d:])

        ks, vs, kw, vw = own_head(ks_ref), own_head(vs_ref), own_head(kw_ref), own_head(vw_ref)
        put(kse_ref, ks, onehot)
        put(kso_ref, onehot, ks)
        put(vse_ref, vs, ones)
        put(vso_ref, ones, vs)
        put(kwd_ref, kw, kw)
        put(vwe_ref, vw, ones)
        put(vwo_ref, ones, vw)
        put(ckd_ref, ck_ref[0, 0], ck_ref[0, 0])
        put(cvd_ref, cv_ref[0, 0], cv_ref[0, 0])

    lane = lax.broadcasted_iota(jnp.int32, (tq, LANES), 1)
    left = lane < d
    qs = (q_ref[0].astype(F32) * (d ** -0.5)).astype(BF16)
    pairs = [qs[:, :LANES], qs[:, LANES:]]
    zero = jnp.zeros((tq, LANES), BF16)
    qz = jnp.concatenate([jnp.where(left, pairs[0], zero), jnp.where(left, pairs[1], zero),
                          jnp.where(left, zero, pairs[0]), jnp.where(left, zero, pairs[1])], axis=0)
    t4 = q0 + lax.rem(lax.broadcasted_iota(jnp.int32, (rows, 1), 0), tq)

    def rep4(x):
        return jnp.concatenate([x] * grp, axis=0)

    def weigh(p, start, width, ve_ref, vo_ref):
        return jnp.concatenate([_dot(p[:half], ve_ref[pl.ds(start, width), :]),
                                _dot(p[half:], vo_ref[pl.ds(start, width), :])], axis=0)

    def normalise(acc):
        out = []
        for pr in range(2):
            even, odd = acc[pr * tq:(pr + 1) * tq], acc[half + pr * tq:half + (pr + 1) * tq]
            den = pltpu.roll(jnp.where(left, odd, even), d, 1)
            out.append(jnp.where(left, even, odd) / den)
        return out

    w_width = (-(-WINDOW // tq) + 1) * tq
    w_start = pl.multiple_of(jnp.maximum(q0 + tq - w_width, 0), tq)
    sw = _dot_nt(qz, kwd_ref[pl.ds(w_start, w_width), :])
    kpos = w_start + lax.broadcasted_iota(jnp.int32, (1, w_width), 1)
    sw = jnp.where((kpos <= t4) & (kpos > t4 - WINDOW), sw, NEG_BIG)
    pw = jnp.exp(sw - jnp.max(sw, axis=1, keepdims=True))
    o_win = normalise(weigh(pw.astype(BF16), w_start, w_width, vwe_ref, vwo_ref))

    s = _dot_nt(qz, ckd_ref[...])
    cmp_end = CMP_STRIDE * lax.broadcasted_iota(jnp.int32, (1, n_cmp), 1) + (CMP_BLOCK - 1)
    cmask = cmp_end <= t4
    s = jnp.where(cmask, s, NEG_BIG)
    e = jnp.exp(s - jnp.max(s, axis=1, keepdims=True))
    p = e / jnp.sum(e, axis=1, keepdims=True)
    p = jnp.where(cmask, p, 0.0)
    o_cmp = _dot(p.astype(BF16), cvd_ref[...])

    pg = p[0:tq]
    for h in range(1, grp):
        pg = pg + p[h * tq:(h + 1) * tq]
    p_hi = pg.astype(BF16)
    r1 = pg - p_hi.astype(F32)
    p_mid = r1.astype(BF16)
    p_lo = (r1 - p_mid.astype(F32)).astype(BF16)
    ovt = ovt_ref[...]
    imp_t = _dot_nt(ovt, p_hi) + _dot_nt(ovt, p_mid) + _dot_nt(ovt, p_lo)
    blk = lax.broadcasted_iota(jnp.int32, (n_slc, tq), 0)
    cur = (q0 + lax.broadcasted_iota(jnp.int32, (n_slc, tq), 1)) // SLC_BLOCK
    forced = (blk == 0) | (blk == cur) | (blk == cur - 1)
    score = jnp.where(forced, jnp.inf, jnp.where(blk <= cur, imp_t, -jnp.inf))
    rank = jnp.zeros((n_slc, tq), F32)
    for j in range(n_slc):
        sj = score[j:j + 1, :]
        tie = jnp.where(blk > j, 1.0, 0.0)
        rank = rank + jnp.where(sj > score, 1.0, jnp.where(sj == score, tie, 0.0))
    sel_t = jnp.where(rank < top_n, 1.0, 0.0)
    if n_slc < d:
        sel_t = jnp.concatenate([sel_t, jnp.zeros((d - n_slc, tq), F32)], axis=0)
    sel_t2 = jnp.concatenate([sel_t, sel_t], axis=0).astype(BF16)
    eye = jnp.where(lax.broadcasted_iota(jnp.int32, (tq, tq), 0)
                    == lax.broadcasted_iota(jnp.int32, (tq, tq), 1), 1.0, 0.0).astype(BF16)
    sel2 = _dot_nt(eye, sel_t2)
    bias = ((sel2 - 1.0) * (-NEG_BIG)).astype(BF16)
    q_even = jnp.concatenate([jnp.where(left, pairs[0], bias), jnp.where(left, pairs[1], bias)], axis=0)
    q_odd = jnp.concatenate([jnp.where(left, bias, pairs[0]), jnp.where(left, bias, pairs[1])], axis=0)

    def slc_scores(start, width, tail):
        s = jnp.concatenate([_dot_nt(q_even, kse_ref[pl.ds(start, width), :]),
                             _dot_nt(q_odd, kso_ref[pl.ds(start, width), :])], axis=0) * LOG2_E
        if tail:
            s = jnp.where(start + lax.broadcasted_iota(jnp.int32, (1, width), 1) <= t4, s, NEG_BIG)
        return s

    (acc,) = _softmax_sweep((s_ref,), rows=rows, acc_lanes=LANES, q0=pl.multiple_of(q0, tq), tq=tq,
                            big=4 * tq, scores_fn=lambda *a: (slc_scores(*a),),
                            weigh_fn=lambda ps, start, width: (weigh(ps[0], start, width, vse_ref, vso_ref),))
    o_slc = normalise(acc)

    gate = 1.0 / (1.0 + jnp.exp(-g_ref[0]))
    g_hi = gate.astype(BF16)
    g_r = gate - g_hi.astype(F32)
    g_mid = g_r.astype(BF16)
    g_lo = (g_r - g_mid.astype(F32)).astype(BF16)
    spread = gx_ref[...]
    gx = _dot(g_hi, spread) + _dot(g_mid, spread) + _dot(g_lo, spread)

    out = []
    for pr in range(2):
        cmp_pr = jnp.where(left, o_cmp[pr * tq:(pr + 1) * tq], o_cmp[half + pr * tq:half + (pr + 1) * tq])
        mixed = 0.0
        for br, o_br in enumerate((cmp_pr, o_slc[pr], o_win[pr])):
            blk_idx = pr * N_BRANCH + br
            mixed = mixed + gx[:, blk_idx * LANES:(blk_idx + 1) * LANES] * o_br
        out.append(mixed)
    o_ref[0] = jnp.concatenate(out, axis=1).astype(o_ref.dtype)


def _nsa_attention(pa, pf, ckv, ovt, *, tq, q_blk, kv_blk, g_blk, top_n):
    b, t, _ = pa.shape
    n16 = ckv.shape[2]
    n_slc = ovt.shape[0]
    qw = NSA_GROUP * HEAD_DIM
    kv_specs = [pl.BlockSpec((1, t, LANES), functools.partial(lambda bi, j, i, c: (bi, 0, c), c=kv_blk + n))
                for n in range(4)]
    return pl.pallas_call(
        functools.partial(_nsa_kernel, tq=tq, top_n=top_n),
        grid=(b, NSA_KV_HEADS, t // tq),
        in_specs=[
            pl.BlockSpec((1, tq, qw), lambda bi, j, i: (bi, i, q_blk + j)),
            pl.BlockSpec((1, tq, LANES), lambda bi, j, i: (bi, i, g_blk + j)),
            pl.BlockSpec((1, 1, n16, HEAD_DIM), lambda bi, j, i: (bi, j, 0, 0)),
            pl.BlockSpec((1, 1, n16, HEAD_DIM), lambda bi, j, i: (bi, NSA_KV_HEADS + j, 0, 0)),
            pl.BlockSpec((n_slc, n16), lambda bi, j, i: (0, 0)),
            pl.BlockSpec((LANES, 2 * N_BRANCH * LANES), lambda bi, j, i: (0, 0)),
            *kv_specs,
        ],
        out_specs=pl.BlockSpec((1, tq, qw), lambda bi, j, i: (bi, i, j)),
        out_shape=jax.ShapeDtypeStruct((b, t, NSA_HEADS * HEAD_DIM), BF16),
        scratch_shapes=[pltpu.VMEM((t, LANES), BF16)] * 7 + [pltpu.VMEM((n16, LANES), BF16)] * 2
                       + [pltpu.VMEM((NSA_GROUP * tq, t), F32)],
        compiler_params=_params(("arbitrary", "arbitrary", "arbitrary")),
        name="nsa_attention",
    )(pa, pf, ckv, ckv, ovt, _gate_spread(), pa, pa, pa, pa)


def _conv_kernel(cb_ref, cc_ref, ch_ref, w_ref, o_ref, prev_ref):
    @pl.when(pl.program_id(1) == 0)
    def _():
        prev_ref[...] = jnp.zeros_like(prev_ref)

    u = cc_ref[0] * ch_ref[0]
    tt = u.shape[0]
    row = lax.broadcasted_iota(jnp.int32, u.shape, 0)
    last1 = prev_ref[7:8]
    last2 = prev_ref[6:7]
    u1 = jnp.where(row >= 1, pltpu.roll(u, 1, 0), last1)
    u2 = jnp.where(row >= 2, pltpu.roll(u, 2, 0), jnp.where(row == 1, last1, last2))
    w = w_ref[...]
    o_ref[0] = (cb_ref[0] * (w[0:1] * u2 + w[1:2] * u1 + w[2:3] * u)).astype(o_ref.dtype)
    prev_ref[...] = u[tt - 8:tt]


def _conv_mixer(pf, conv_w, *, tt):
    b, t, _ = pf.shape
    c = CONV_CH
    return pl.pallas_call(
        _conv_kernel,
        grid=(b, t // tt),
        in_specs=[
            pl.BlockSpec((1, tt, c), lambda bi, i: (bi, i, 0)),
            pl.BlockSpec((1, tt, c), lambda bi, i: (bi, i, 1)),
            pl.BlockSpec((1, tt, c), lambda bi, i: (bi, i, 2)),
            pl.BlockSpec((CONV_WIDTH, c), lambda bi, i: (0, 0)),
        ],
        out_specs=pl.BlockSpec((1, tt, c), lambda bi, i: (bi, i, 0)),
        out_shape=jax.ShapeDtypeStruct((b, t, c), BF16),
        scratch_shapes=[pltpu.VMEM((8, c), F32)],
        compiler_params=_params(("parallel", "arbitrary")),
        name="conv_mixer",
    )(pf, pf, pf, conv_w)


def _out_ln_kernel(a_ref, b_ref, c_ref, d_ref, w_ref, x_ref, g_ref, beta_ref, o_ref, *, alpha):
    kw = a_ref.shape[1]
    mix = _dot(a_ref[...], w_ref[0:kw])
    for n, r in enumerate((b_ref, c_ref, d_ref), start=1):
        mix = mix + _dot(r[...], w_ref[n * kw:(n + 1) * kw])
    o_ref[...] = _layer_norm(alpha * x_ref[...] + mix, g_ref[...], beta_ref[...])


def _out_ln(parts, w, x, g, b, *, layer, alpha, tm):
    n, d = x.shape
    kw = parts[0].shape[1]
    part_spec = pl.BlockSpec((tm, kw), lambda i: (i, 0))
    return pl.pallas_call(
        functools.partial(_out_ln_kernel, alpha=alpha),
        grid=(n // tm,),
        in_specs=[part_spec, part_spec, part_spec, part_spec,
                  pl.BlockSpec((None,) + w.shape[1:], lambda i: (layer, 0, 0)),
                  pl.BlockSpec((tm, d), lambda i: (i, 0)),
                  pl.BlockSpec((1, d), lambda i: (0, 0)),
                  pl.BlockSpec((1, d), lambda i: (0, 0))],
        out_specs=pl.BlockSpec((tm, d), lambda i: (i, 0)),
        out_shape=jax.ShapeDtypeStruct((n, d), F32),
        compiler_params=_params(("parallel",)),
        name="out_ln",
    )(*parts, w, x, g.reshape(1, d), b.reshape(1, d))


def _gate_spread():
    m = np.zeros((LANES, 2 * N_BRANCH * LANES), np.float32)
    for pr in range(2):
        for br in range(N_BRANCH):
            for lane in range(LANES):
                head = 2 * pr + lane // HEAD_DIM
                m[N_BRANCH * head + br, (pr * N_BRANCH + br) * LANES + lane] = 1.0
    return jnp.asarray(m, BF16)


def _overlap_t(t):
    n16 = t // CMP_STRIDE
    n_slc = t // SLC_BLOCK
    c_start = CMP_STRIDE * np.arange(n16)
    j_start = SLC_BLOCK * np.arange(n_slc)
    ov = ((c_start[None, :] < j_start[:, None] + SLC_BLOCK)
          & (c_start[None, :] + CMP_BLOCK > j_start[:, None])).astype(np.float32)
    ov[:, n16 - 1] = 0.0
    return jnp.asarray(ov, BF16)


def _mixer(hf, batch, w_proj, w_out, ln_g, ln_b, diff_lam, diff_gain, pos2, cmp_w1, cmp_w2,
           conv_w, layer, alpha):
    n, _ = hf.shape
    t = n // batch
    pa, pf = _proj(hf, w_proj, layer=layer, width_a=ATT_WIDTH, tm=1024, tn=1024)
    pa, pf = pa.reshape(batch, t, -1), pf.reshape(batch, t, -1)

    o_sb = _sb_attention(pa, tq=256, pairs=4, q_blk=0, k_blk=1, v_blk=2)
    lam_init = 0.8 - 0.6 * math.exp(-0.3 * layer)
    o_df = _diff_attention(pa, diff_lam[layer], diff_gain[layer], tq=512, heads=2, q_blk=6, k_blk=8, v_blk=10,
                           lam_init=lam_init)

    n16 = t // CMP_STRIDE
    kvc = pf[:, :, 3 * CONV_CH:3 * CONV_CH + 2 * LANES].reshape(batch, t, 2, NSA_KV_HEADS, HEAD_DIM)
    x16 = kvc.transpose(0, 2, 3, 1, 4).reshape(batch, 2 * NSA_KV_HEADS, n16, CMP_STRIDE * HEAD_DIM)
    ckv = _compress(x16, pos2, cmp_w1, cmp_w2, layer=layer)
    n_slc = t // SLC_BLOCK
    o_ns = _nsa_attention(pa, pf, ckv, _overlap_t(t), tq=256, q_blk=12, kv_blk=28, g_blk=14,
                          top_n=min(SLC_TOPN, n_slc))

    o_cv = _conv_mixer(pf, conv_w[layer], tt=512)
    parts = [o.reshape(n, -1) for o in (o_sb, o_df, o_ns, o_cv)]
    return _out_ln(parts, w_out, hf, ln_g, ln_b, layer=layer, alpha=alpha, tm=512)


def _seg(w_in, name, lo=0, hi=None):
    off = _SEG_OFF[name]
    hi = _SEG_W[name] if hi is None else hi
    return w_in[:, :, off + lo:off + hi]


def kernel(x, ln_g, ln_b, ffn_w1, ffn_w3, ffn_w2, w_in, w_out, diff_lam, diff_gain, cmp_pos, cmp_wk1,
           cmp_wk2, cmp_wv1, cmp_wv2, conv_w):
    batch, t, d = x.shape
    depth = ln_g.shape[0]
    alpha = (2 * depth) ** 0.25
    n = batch * t

    w1b, w3b, w2b = ffn_w1.astype(BF16), ffn_w3.astype(BF16), ffn_w2.astype(BF16)
    gate_pad = jnp.zeros(w_in.shape[:2] + (LANES - _GATES_PER_KV,), w_in.dtype)
    cols = [w_in[:, :, :_SEG_OFF["ns_kc"]], w_in[:, :, _SEG_OFF["ns_ks"]:_SEG_OFF["ns_g"]],
            _seg(w_in, "cv_b"), _seg(w_in, "cv_c"), _seg(w_in, "cv_h"), _seg(w_in, "ns_kc"), _seg(w_in, "ns_vc")]
    for j in range(NSA_KV_HEADS):
        cols += [_seg(w_in, "ns_g", j * _GATES_PER_KV, (j + 1) * _GATES_PER_KV), gate_pad]
    w_proj = jnp.concatenate(cols, axis=-1).astype(BF16)
    w_outb = w_out.astype(BF16)
    half = CMP_STRIDE * HEAD_DIM
    pos2 = cmp_pos.reshape(depth, 2, half)
    cmp_w1 = jnp.stack([cmp_wk1, cmp_wv1], axis=1).astype(BF16)
    cmp_w2 = jnp.stack([cmp_wk2, cmp_wv2], axis=1).astype(BF16)

    hf = x.reshape(n, d)
    for l in range(depth):
        hf = _ffn_ln(hf, w1b, w3b, w2b, ln_g[l, 0], ln_b[l, 0], layer=l, slot=0, alpha=alpha, tm=FFN_ROWS,
                     tf=FFN_TILE)
        hf = _mixer(hf, batch, w_proj, w_outb, ln_g[l, 1], ln_b[l, 1], diff_lam, diff_gain,
                    pos2, cmp_w1, cmp_w2, conv_w, l, alpha)
        hf = _ffn_ln(hf, w1b, w3b, w2b, ln_g[l, 2], ln_b[l, 2], layer=l, slot=1, alpha=alpha, tm=FFN_ROWS,
                     tf=FFN_TILE)
    return hf.reshape(batch, t, d)
```

```python
import functools
import math

import numpy as np
import jax
import jax.numpy as jnp
from jax import lax
from jax.experimental import pallas as pl
from jax.experimental.pallas import tpu as pltpu

F32 = jnp.float32
BF16 = jnp.bfloat16

HEAD_DIM = 64
SB_HEADS = 8
DIFF_HEADS = 4
NSA_HEADS = 8
NSA_KV_HEADS = 2
NSA_GROUP = NSA_HEADS // NSA_KV_HEADS
CMP_BLOCK = 32
CMP_STRIDE = 16
SLC_BLOCK = 64
SLC_TOPN = 16
WINDOW = 512
N_BRANCH = 3
CONV_CH = 512
CONV_WIDTH = 3
LN_EPS = 1e-5
RMS_EPS = 1e-5
NEG_BIG = -1e30
LOG2_E = math.log2(math.e)
EXP2_UNDERFLOW = -150.0
LANES = 128
VMEM_LIMIT = 56 * 1024 * 1024
FFN_TILE = 256
FFN_ROWS = 1024

_SEG_NAMES = ("sb_q", "sb_k", "sb_v", "df_q", "df_k", "df_v", "ns_q", "ns_kc", "ns_vc",
              "ns_ks", "ns_vs", "ns_kw", "ns_vw", "ns_g", "cv_b", "cv_c", "cv_h")
_SEG_WIDTHS = (512, 512, 512, 512, 512, 512, 512, 128, 128, 128, 128, 128, 128,
               NSA_HEADS * N_BRANCH, 512, 512, 512)
_SEG_OFF = dict(zip(_SEG_NAMES, np.cumsum((0,) + _SEG_WIDTHS[:-1]).tolist()))
_SEG_W = dict(zip(_SEG_NAMES, _SEG_WIDTHS))


_GATES_PER_KV = NSA_GROUP * N_BRANCH
ATT_WIDTH = _SEG_OFF["ns_kc"] + 4 * _SEG_W["ns_ks"]


def _params(sem):
    return pltpu.CompilerParams(dimension_semantics=sem, vmem_limit_bytes=VMEM_LIMIT)


def _layer_norm(y, g, b):
    mu = jnp.mean(y, axis=-1, keepdims=True)
    d = y - mu
    var = jnp.mean(d * d, axis=-1, keepdims=True)
    return d * lax.rsqrt(var + LN_EPS) * g + b


def _dot(a, b):
    return jnp.dot(a, b, preferred_element_type=F32)


def _lane_group_max(s, mx):
    for g in range(s.shape[1] // LANES):
        mx = jnp.maximum(mx, s[:, g * LANES:(g + 1) * LANES])
    return mx


def _softmax_sweep(s_refs, *, rows, acc_lanes, q0, tq, big, merge_tail, scores_fn, weigh_fn):
    n_big = q0 // big
    mid = pl.multiple_of(n_big * big, tq)
    tail = (q0 - mid) // tq
    tail_widths = [(k + 1) * tq for k in range(big // tq)]

    def chunks(fn, init):
        return lax.fori_loop(0, n_big, lambda c, carry: fn(pl.multiple_of(c * big, big), big, carry), init)

    def scores(start, width, mxs, is_tail=False):
        out = []
        for ref, s, mx in zip(s_refs, scores_fn(start, width, is_tail), mxs):
            ref[:, pl.ds(start, width)] = s
            out.append(_lane_group_max(s, mx))
        return tuple(out)

    def tiles(fn, init):
        return lax.fori_loop(0, tail, lambda c, carry: fn(pl.multiple_of(mid + c * tq, tq), tq, carry), init)

    mxs = chunks(scores, (jnp.full((rows, LANES), NEG_BIG, F32),) * len(s_refs))
    if merge_tail:
        mxs = lax.switch(tail, [functools.partial(scores, mid, w, is_tail=True) for w in tail_widths], mxs)
    else:
        mxs = scores(q0, tq, tiles(scores, mxs), is_tail=True)
    ms = [jnp.broadcast_to(jnp.max(mx, axis=1, keepdims=True), (rows, LANES)) for mx in mxs]

    def weigh(start, width, accs):
        ps = [jnp.exp2(ref[:, pl.ds(start, width)] - jnp.concatenate([m] * (width // LANES), axis=1)).astype(BF16)
              for ref, m in zip(s_refs, ms)]
        return tuple(acc + w for acc, w in zip(accs, weigh_fn(ps, start, width)))

    accs = chunks(weigh, (jnp.zeros((rows, acc_lanes), F32),) * len(s_refs))
    if merge_tail:
        return lax.switch(tail, [functools.partial(weigh, mid, w) for w in tail_widths], accs)
    return weigh(q0, tq, tiles(weigh, accs))


def _dot_nt(a, b):
    return lax.dot_general(a, b, (((1,), (1,)), ((), ())), preferred_element_type=F32)


def _ffn_ln_kernel(x_ref, w1_ref, w3_ref, w2_ref, g_ref, b_ref, o_ref, xb_ref, *, alpha):
    j = pl.program_id(1)

    @pl.when(j == 0)
    def _():
        xb_ref[...] = x_ref[...].astype(BF16)
        o_ref[...] = jnp.zeros_like(o_ref)

    xb = xb_ref[...]
    a = _dot(xb, w1_ref[...])
    b = _dot(xb, w3_ref[...])
    hm = (a / (1.0 + jnp.exp(-a))) * b
    o_ref[...] += _dot(hm.astype(BF16), w2_ref[...])

    @pl.when(j == pl.num_programs(1) - 1)
    def _():
        y = alpha * x_ref[...] + 0.5 * o_ref[...]
        o_ref[...] = _layer_norm(y, g_ref[...], b_ref[...])


def _ffn_ln(x, w1, w3, w2, g, b, *, layer, slot, alpha, tm, tf):
    n, d = x.shape
    f = w1.shape[-1]
    return pl.pallas_call(
        functools.partial(_ffn_ln_kernel, alpha=alpha),
        grid=(n // tm, f // tf),
        in_specs=[
            pl.BlockSpec((tm, d), lambda i, j: (i, 0)),
            pl.BlockSpec((None, None, d, tf), lambda i, j: (layer, slot, 0, j)),
            pl.BlockSpec((None, None, d, tf), lambda i, j: (layer, slot, 0, j)),
            pl.BlockSpec((None, None, tf, d), lambda i, j: (layer, slot, j, 0)),
            pl.BlockSpec((1, d), lambda i, j: (0, 0)),
            pl.BlockSpec((1, d), lambda i, j: (0, 0)),
        ],
        out_specs=pl.BlockSpec((tm, d), lambda i, j: (i, 0)),
        out_shape=jax.ShapeDtypeStruct((n, d), F32),
        scratch_shapes=[pltpu.VMEM((tm, d), BF16)],
        compiler_params=_params(("parallel", "arbitrary")),
        name="ffn_ln",
    )(x, w1, w3, w2, g.reshape(1, d), b.reshape(1, d))


def _proj_kernel(x_ref, w_ref, oa_ref, of_ref, xb_ref, *, n_a):
    j = pl.program_id(1)

    @pl.when(j == 0)
    def _():
        xb_ref[...] = x_ref[...].astype(BF16)

    @pl.when(j < n_a)
    def _():
        oa_ref[...] = _dot(xb_ref[...], w_ref[...]).astype(oa_ref.dtype)

    @pl.when(j >= n_a)
    def _():
        of_ref[...] = _dot(xb_ref[...], w_ref[...])


def _proj(x, w, *, layer, width_a, tm, tn):
    n, d = x.shape
    m = w.shape[-1]
    n_a = width_a // tn
    return pl.pallas_call(
        functools.partial(_proj_kernel, n_a=n_a),
        grid=(n // tm, m // tn),
        in_specs=[pl.BlockSpec((tm, d), lambda i, j: (i, 0)),
                  pl.BlockSpec((None, d, tn), lambda i, j: (layer, 0, j))],
        out_specs=[pl.BlockSpec((tm, tn), lambda i, j: (i, jnp.minimum(j, n_a - 1))),
                   pl.BlockSpec((tm, tn), lambda i, j: (i, jnp.maximum(j - n_a, 0)))],
        out_shape=[jax.ShapeDtypeStruct((n, width_a), BF16), jax.ShapeDtypeStruct((n, m - width_a), F32)],
        scratch_shapes=[pltpu.VMEM((tm, d), BF16)],
        compiler_params=_params(("parallel", "arbitrary")),
        name="proj",
    )(x, w)


def _sb_kernel(q_ref, k_ref, v_ref, o_ref, *, tq, pairs):
    i = pl.program_id(2)
    q0 = i * tq
    lane = lax.broadcasted_iota(jnp.int32, (tq, LANES), 1)
    tri = jnp.where(lax.broadcasted_iota(jnp.int32, (tq, tq), 0)
                    >= lax.broadcasted_iota(jnp.int32, (tq, tq), 1), 1.0, 0.0).astype(BF16)
    qpos = q0 + lax.rem(lax.broadcasted_iota(jnp.int32, (2 * tq, 1), 0), tq)
    q2 = []
    for p in range(pairs):
        qs = (q_ref[0, :, p * LANES:(p + 1) * LANES].astype(F32) * (HEAD_DIM ** -0.5)).astype(BF16)
        zero = jnp.zeros_like(qs)
        q2.append(jnp.concatenate([jnp.where(lane < HEAD_DIM, qs, zero),
                                   jnp.where(lane < HEAD_DIM, zero, qs)], axis=0))

    def block(start, state, keep=None):
        width = tq
        masked = keep is not None
        out = []
        for p, (c, acc) in enumerate(state):
            ks = k_ref[0, pl.ds(start, width), p * LANES:(p + 1) * LANES]
            vs = v_ref[0, pl.ds(start, width), p * LANES:(p + 1) * LANES]
            z = _dot_nt(q2[p], ks) * LOG2_E
            nz = -z
            lg = jnp.minimum(nz, 0.0) - jnp.log2(1.0 + jnp.exp2(jnp.minimum(z, nz)))
            if masked:
                lg = jnp.where(keep, lg, 0.0)
            hi = lg.astype(BF16)
            lo = (lg - hi.astype(F32)).astype(BF16)
            suffix = _dot(hi, tri) + _dot(lo, tri)
            a = jnp.exp2(z + suffix + c)
            if masked:
                a = jnp.where(keep, a, 0.0)
            out.append((c + suffix[:, 0:1], acc + _dot(a.astype(BF16), vs)))
        return tuple(out)

    state = ((jnp.zeros((2 * tq, 1), F32), jnp.zeros((2 * tq, LANES), F32)),) * pairs
    key = lax.broadcasted_iota(jnp.int32, (1, tq), 1)
    state = block(pl.multiple_of(q0, tq), state, keep=q0 + key < qpos)
    state = block(pl.multiple_of(jnp.maximum(q0 - tq, 0), tq), state, keep=key < q0)

    def live(state):
        top = functools.reduce(jnp.maximum, [c for c, _ in state])
        return (jnp.max(top) > EXP2_UNDERFLOW).astype(jnp.int32)

    def cond(carry):
        return (carry[0] >= 0) & (carry[1] > 0)

    def body(carry):
        kt, _, state = carry
        state = block(pl.multiple_of(kt * tq, tq), state)
        return kt - 1, live(state), state

    _, _, state = lax.while_loop(cond, body, (i - 2, live(state), state))
    o_ref[0] = jnp.concatenate([jnp.where(lane < HEAD_DIM, acc[:tq], acc[tq:]) for _, acc in state],
                               axis=1).astype(o_ref.dtype)


def _sb_attention(pa, *, tq, pairs, q_blk, k_blk, v_blk):
    b, t, _ = pa.shape
    w = pairs * LANES
    n_groups = SB_HEADS // (2 * pairs)
    return pl.pallas_call(
        functools.partial(_sb_kernel, tq=tq, pairs=pairs),
        grid=(b, n_groups, t // tq),
        in_specs=[
            pl.BlockSpec((1, tq, w), lambda bi, p, i: (bi, i, q_blk + p)),
            pl.BlockSpec((1, t, w), lambda bi, p, i: (bi, 0, k_blk + p)),
            pl.BlockSpec((1, t, w), lambda bi, p, i: (bi, 0, v_blk + p)),
        ],
        out_specs=pl.BlockSpec((1, tq, w), lambda bi, p, i: (bi, i, p)),
        out_shape=jax.ShapeDtypeStruct((b, t, SB_HEADS * HEAD_DIM), BF16),
        compiler_params=_params(("parallel", "parallel", "arbitrary")),
        name="sb_attention",
    )(pa, pa, pa)


def _diff_kernel(lam_ref, gain_ref, q_ref, k_ref, v_ref, o_ref, s_ref, *, tq, heads, lam_init):
    i = pl.program_id(2)
    q0 = pl.multiple_of(i * tq, tq)
    lane = lax.broadcasted_iota(jnp.int32, (tq, LANES), 1)
    qpos = q0 + lax.rem(lax.broadcasted_iota(jnp.int32, (2 * tq, 1), 0), tq)
    q2 = []
    for h in range(heads):
        qs = (q_ref[0, :, h * LANES:(h + 1) * LANES].astype(F32) * (HEAD_DIM ** -0.5)).astype(BF16)
        zero = jnp.zeros_like(qs)
        q2.append(jnp.concatenate([jnp.where(lane < HEAD_DIM, qs, zero),
                                   jnp.where(lane < HEAD_DIM, zero, qs)], axis=0))

    def scores(start, width, tail):
        out = []
        for h in range(heads):
            s = _dot_nt(q2[h], k_ref[0, pl.ds(start, width), h * LANES:(h + 1) * LANES]) * LOG2_E
            if tail:
                s = jnp.where(start + lax.broadcasted_iota(jnp.int32, (1, width), 1) <= qpos, s, NEG_BIG)
            out.append(s)
        return out

    def weigh(ps, start, width):
        ones = jnp.ones((width, LANES), BF16)
        return [_dot(p, jnp.concatenate([v_ref[0, pl.ds(start, width), h * LANES:(h + 1) * LANES], ones], axis=1))
                for h, p in enumerate(ps)]

    s_refs = [s_ref.at[pl.ds(h * 2 * tq, 2 * tq)] for h in range(heads)]
    accs = _softmax_sweep(s_refs, rows=2 * tq, acc_lanes=2 * LANES, q0=q0, tq=tq, big=2 * tq, merge_tail=False,
                          scores_fn=scores, weigh_fn=weigh)
    lp = lam_ref[...]
    lam = (jnp.exp(jnp.sum(lp[0:1] * lp[1:2], axis=1, keepdims=True))
           - jnp.exp(jnp.sum(lp[2:3] * lp[3:4], axis=1, keepdims=True)) + lam_init)
    out = []
    for acc in accs:
        o = acc[:, :LANES] / acc[:, LANES:]
        od = o[:tq] - lam * o[tq:]
        od = od * lax.rsqrt(jnp.mean(od * od, axis=-1, keepdims=True) + RMS_EPS)
        out.append(od * gain_ref[...] * (1.0 - lam_init))
    o_ref[0] = jnp.concatenate(out, axis=1).astype(o_ref.dtype)


def _diff_attention(pa, lam_params, gain, *, tq, heads, q_blk, k_blk, v_blk, lam_init):
    b, t, _ = pa.shape
    w = heads * LANES
    return pl.pallas_call(
        functools.partial(_diff_kernel, tq=tq, heads=heads, lam_init=lam_init),
        grid=(b, DIFF_HEADS // heads, t // tq),
        in_specs=[
            pl.BlockSpec((4, HEAD_DIM), lambda bi, h, i: (0, 0)),
            pl.BlockSpec((1, LANES), lambda bi, h, i: (0, 0)),
            pl.BlockSpec((1, tq, w), lambda bi, h, i: (bi, i, q_blk + h)),
            pl.BlockSpec((1, t, w), lambda bi, h, i: (bi, 0, k_blk + h)),
            pl.BlockSpec((1, t, w), lambda bi, h, i: (bi, 0, v_blk + h)),
        ],
        out_specs=pl.BlockSpec((1, tq, w), lambda bi, h, i: (bi, i, h)),
        out_shape=jax.ShapeDtypeStruct((b, t, DIFF_HEADS * LANES), BF16),
        scratch_shapes=[pltpu.VMEM((heads * 2 * tq, t), F32)],
        compiler_params=_params(("parallel", "parallel", "arbitrary")),
        name="diff_attention",
    )(lam_params, gain.reshape(1, LANES), pa, pa, pa)


def _compress_kernel(x_ref, pos_ref, w1_ref, w2_ref, o_ref):
    x = x_ref[0, 0]
    n16, half = x.shape
    xa = (x + pos_ref[0:1]).astype(BF16)
    xb = (x + pos_ref[1:2]).astype(BF16)
    w1 = w1_ref[0]
    first = _dot(xa, w1[:half])
    second = _dot(xb, w1[half:])
    h = first + pltpu.roll(second, n16 - 1, 0)
    g = 0.5 * h * (1.0 + jnp.tanh(math.sqrt(2.0 / math.pi) * (h + 0.044715 * (h * h * h))))
    o_ref[0, 0] = _dot(g.astype(BF16), w2_ref[0])


def _compress(x16, pos2, w1, w2, *, layer):
    b, n_streams, n16, half = x16.shape
    d = w2.shape[-1]
    return pl.pallas_call(
        _compress_kernel,
        grid=(b, n_streams),
        in_specs=[
            pl.BlockSpec((1, 1, n16, half), lambda bi, s: (bi, s, 0, 0)),
            pl.BlockSpec((None, 2, half), lambda bi, s: (layer, 0, 0)),
            pl.BlockSpec((None, 1, 2 * half, w1.shape[-1]), lambda bi, s: (layer, s // NSA_KV_HEADS, 0, 0)),
            pl.BlockSpec((None, 1, w2.shape[2], d), lambda bi, s: (layer, s // NSA_KV_HEADS, 0, 0)),
        ],
        out_specs=pl.BlockSpec((1, 1, n16, d), lambda bi, s: (bi, s, 0, 0)),
        out_shape=jax.ShapeDtypeStruct((b, n_streams, n16, d), F32),
        compiler_params=_params(("parallel", "arbitrary")),
        name="nsa_compress",
    )(x16, pos2, w1, w2)


def _nsa_kernel(q_ref, g_ref, ck_ref, cv_ref, ovt_ref, gx_ref, ks_ref, vs_ref, kw_ref, vw_ref, o_ref,
                kse_ref, kso_ref, vse_ref, vso_ref, kwd_ref, vwe_ref, vwo_ref, ckd_ref, cvd_ref, s_ref,
                *, tq, top_n):
    i = pl.program_id(2)
    grp = NSA_GROUP
    d = HEAD_DIM
    q0 = i * tq
    n_cmp = ck_ref.shape[2]
    n_slc = ovt_ref.shape[0]
    t = ks_ref.shape[1]
    rows = grp * tq
    half = rows // 2

    @pl.when(i == 0)
    def _():
        onehot = jnp.where(lax.broadcasted_iota(jnp.int32, (t, d), 0) // SLC_BLOCK
                           == lax.broadcasted_iota(jnp.int32, (t, d), 1), 1.0, 0.0)
        ones = jnp.ones((t, d), F32)

        def put(ref, left, right):
            ref[...] = jnp.concatenate([left, right], axis=1).astype(BF16)

        first_kv = pl.program_id(1) == 0

        def own_head(ref):
            both = ref[0].astype(F32)
            return jnp.where(first_kv, both[:, :d], both[:, d:])

        ks, vs, kw, vw = own_head(ks_ref), own_head(vs_ref), own_head(kw_ref), own_head(vw_ref)
        put(kse_ref, ks, onehot)
        put(kso_ref, onehot, ks)
        put(vse_ref, vs, ones)
        put(vso_ref, ones, vs)
        put(kwd_ref, kw, kw)
        put(vwe_ref, vw, ones)
        put(vwo_ref, ones, vw)
        put(ckd_ref, ck_ref[0, 0], ck_ref[0, 0])
        put(cvd_ref, cv_ref[0, 0], cv_ref[0, 0])

    lane = lax.broadcasted_iota(jnp.int32, (tq, LANES), 1)
    left = lane < d
    qs = (q_ref[0].astype(F32) * (d ** -0.5)).astype(BF16)
    pairs = [qs[:, :LANES], qs[:, LANES:]]
    zero = jnp.zeros((tq, LANES), BF16)
    qz = jnp.concatenate([jnp.where(left, pairs[0], zero), jnp.where(left, pairs[1], zero),
                          jnp.where(left, zero, pairs[0]), jnp.where(left, zero, pairs[1])], axis=0)
    t4 = q0 + lax.rem(lax.broadcasted_iota(jnp.int32, (rows, 1), 0), tq)

    def weigh(p, start, width, ve_ref, vo_ref):
        return jnp.concatenate([_dot(p[:half], ve_ref[pl.ds(start, width), :]),
                                _dot(p[half:], vo_ref[pl.ds(start, width), :])], axis=0)

    def normalise(acc):
        out = []
        for pr in range(2):
            even, odd = acc[pr * tq:(pr + 1) * tq], acc[half + pr * tq:half + (pr + 1) * tq]
            den = pltpu.roll(jnp.where(left, odd, even), d, 1)
            out.append(jnp.where(left, even, odd) / den)
        return out

    w_width = (-(-WINDOW // tq) + 1) * tq
    w_start = pl.multiple_of(jnp.maximum(q0 + tq - w_width, 0), tq)
    sw = _dot_nt(qz, kwd_ref[pl.ds(w_start, w_width), :])
    kpos = w_start + lax.broadcasted_iota(jnp.int32, (1, w_width), 1)
    sw = jnp.where((kpos <= t4) & (kpos > t4 - WINDOW), sw, NEG_BIG)
    pw = jnp.exp(sw - jnp.max(sw, axis=1, keepdims=True))
    o_win = normalise(weigh(pw.astype(BF16), w_start, w_width, vwe_ref, vwo_ref))

    s = _dot_nt(qz, ckd_ref[...])
    cmp_end = CMP_STRIDE * lax.broadcasted_iota(jnp.int32, (1, n_cmp), 1) + (CMP_BLOCK - 1)
    cmask = cmp_end <= t4
    s = jnp.where(cmask, s, NEG_BIG)
    e = jnp.exp(s - jnp.max(s, axis=1, keepdims=True))
    p = e / jnp.sum(e, axis=1, keepdims=True)
    p = jnp.where(cmask, p, 0.0)
    o_cmp = _dot(p.astype(BF16), cvd_ref[...])

    pg = p[0:tq]
    for h in range(1, grp):
        pg = pg + p[h * tq:(h + 1) * tq]
    p_hi = pg.astype(BF16)
    r1 = pg - p_hi.astype(F32)
    p_mid = r1.astype(BF16)
    p_lo = (r1 - p_mid.astype(F32)).astype(BF16)
    ovt = ovt_ref[...]
    imp_t = _dot_nt(ovt, p_hi) + _dot_nt(ovt, p_mid) + _dot_nt(ovt, p_lo)
    blk = lax.broadcasted_iota(jnp.int32, (n_slc, tq), 0)
    cur = (q0 + lax.broadcasted_iota(jnp.int32, (n_slc, tq), 1)) // SLC_BLOCK
    forced = (blk == 0) | (blk == cur) | (blk == cur - 1)
    score = jnp.where(forced, jnp.inf, jnp.where(blk <= cur, imp_t, -jnp.inf))
    rank = jnp.zeros((n_slc, tq), F32)
    for j in range(n_slc):
        sj = score[j:j + 1, :]
        tie = jnp.where(blk > j, 1.0, 0.0)
        rank = rank + jnp.where(sj > score, 1.0, jnp.where(sj == score, tie, 0.0))
    sel_t = jnp.where(rank < top_n, 1.0, 0.0)
    if n_slc < d:
        sel_t = jnp.concatenate([sel_t, jnp.zeros((d - n_slc, tq), F32)], axis=0)
    sel_t2 = jnp.concatenate([sel_t, sel_t], axis=0).astype(BF16)
    eye = jnp.where(lax.broadcasted_iota(jnp.int32, (tq, tq), 0)
                    == lax.broadcasted_iota(jnp.int32, (tq, tq), 1), 1.0, 0.0).astype(BF16)
    sel2 = _dot_nt(eye, sel_t2)
    bias = ((sel2 - 1.0) * (-NEG_BIG)).astype(BF16)
    q_even = jnp.concatenate([jnp.where(left, pairs[0], bias), jnp.where(left, pairs[1], bias)], axis=0)
    q_odd = jnp.concatenate([jnp.where(left, bias, pairs[0]), jnp.where(left, bias, pairs[1])], axis=0)

    def slc_scores(start, width, tail):
        s = jnp.concatenate([_dot_nt(q_even, kse_ref[pl.ds(start, width), :]),
                             _dot_nt(q_odd, kso_ref[pl.ds(start, width), :])], axis=0) * LOG2_E
        if tail:
            s = jnp.where(start + lax.broadcasted_iota(jnp.int32, (1, width), 1) <= t4, s, NEG_BIG)
        return s

    (acc,) = _softmax_sweep((s_ref,), rows=rows, acc_lanes=LANES, q0=pl.multiple_of(q0, tq), tq=tq,
                            big=4 * tq, merge_tail=True, scores_fn=lambda *a: (slc_scores(*a),),
                            weigh_fn=lambda ps, start, width: (weigh(ps[0], start, width, vse_ref, vso_ref),))
    o_slc = normalise(acc)

    gate = 1.0 / (1.0 + jnp.exp(-g_ref[0]))
    g_hi = gate.astype(BF16)
    g_r = gate - g_hi.astype(F32)
    g_mid = g_r.astype(BF16)
    g_lo = (g_r - g_mid.astype(F32)).astype(BF16)
    spread = gx_ref[...]
    gx = _dot(g_hi, spread) + _dot(g_mid, spread) + _dot(g_lo, spread)

    out = []
    for pr in range(2):
        cmp_pr = jnp.where(left, o_cmp[pr * tq:(pr + 1) * tq], o_cmp[half + pr * tq:half + (pr + 1) * tq])
        mixed = 0.0
        for br, o_br in enumerate((cmp_pr, o_slc[pr], o_win[pr])):
            blk_idx = pr * N_BRANCH + br
            mixed = mixed + gx[:, blk_idx * LANES:(blk_idx + 1) * LANES] * o_br
        out.append(mixed)
    o_ref[0] = jnp.concatenate(out, axis=1).astype(o_ref.dtype)


def _nsa_attention(pa, pf, ckv, ovt, *, tq, q_blk, kv_blk, g_blk, top_n):
    b, t, _ = pa.shape
    n16 = ckv.shape[2]
    n_slc = ovt.shape[0]
    qw = NSA_GROUP * HEAD_DIM
    kv_specs = [pl.BlockSpec((1, t, LANES), functools.partial(lambda bi, j, i, c: (bi, 0, c), c=kv_blk + n))
                for n in range(4)]
    return pl.pallas_call(
        functools.partial(_nsa_kernel, tq=tq, top_n=top_n),
        grid=(b, NSA_KV_HEADS, t // tq),
        in_specs=[
            pl.BlockSpec((1, tq, qw), lambda bi, j, i: (bi, i, q_blk + j)),
            pl.BlockSpec((1, tq, LANES), lambda bi, j, i: (bi, i, g_blk + j)),
            pl.BlockSpec((1, 1, n16, HEAD_DIM), lambda bi, j, i: (bi, j, 0, 0)),
            pl.BlockSpec((1, 1, n16, HEAD_DIM), lambda bi, j, i: (bi, NSA_KV_HEADS + j, 0, 0)),
            pl.BlockSpec((n_slc, n16), lambda bi, j, i: (0, 0)),
            pl.BlockSpec((LANES, 2 * N_BRANCH * LANES), lambda bi, j, i: (0, 0)),
            *kv_specs,
        ],
        out_specs=pl.BlockSpec((1, tq, qw), lambda bi, j, i: (bi, i, j)),
        out_shape=jax.ShapeDtypeStruct((b, t, NSA_HEADS * HEAD_DIM), BF16),
        scratch_shapes=[pltpu.VMEM((t, LANES), BF16)] * 7 + [pltpu.VMEM((n16, LANES), BF16)] * 2
                       + [pltpu.VMEM((NSA_GROUP * tq, t), F32)],
        compiler_params=_params(("arbitrary", "arbitrary", "arbitrary")),
        name="nsa_attention",
    )(pa, pf, ckv, ckv, ovt, _gate_spread(), pa, pa, pa, pa)


def _conv_kernel(cb_ref, cc_ref, ch_ref, w_ref, o_ref, prev_ref):
    @pl.when(pl.program_id(1) == 0)
    def _():
        prev_ref[...] = jnp.zeros_like(prev_ref)

    u = cc_ref[0] * ch_ref[0]
    tt = u.shape[0]
    row = lax.broadcasted_iota(jnp.int32, u.shape, 0)
    last1 = prev_ref[7:8]
    last2 = prev_ref[6:7]
    u1 = jnp.where(row >= 1, pltpu.roll(u, 1, 0), last1)
    u2 = jnp.where(row >= 2, pltpu.roll(u, 2, 0), jnp.where(row == 1, last1, last2))
    w = w_ref[...]
    o_ref[0] = (cb_ref[0] * (w[0:1] * u2 + w[1:2] * u1 + w[2:3] * u)).astype(o_ref.dtype)
    prev_ref[...] = u[tt - 8:tt]


def _conv_mixer(pf, conv_w, *, tt):
    b, t, _ = pf.shape
    c = CONV_CH
    return pl.pallas_call(
        _conv_kernel,
        grid=(b, t // tt),
        in_specs=[
            pl.BlockSpec((1, tt, c), lambda bi, i: (bi, i, 0)),
            pl.BlockSpec((1, tt, c), lambda bi, i: (bi, i, 1)),
            pl.BlockSpec((1, tt, c), lambda bi, i: (bi, i, 2)),
            pl.BlockSpec((CONV_WIDTH, c), lambda bi, i: (0, 0)),
        ],
        out_specs=pl.BlockSpec((1, tt, c), lambda bi, i: (bi, i, 0)),
        out_shape=jax.ShapeDtypeStruct((b, t, c), BF16),
        scratch_shapes=[pltpu.VMEM((8, c), F32)],
        compiler_params=_params(("parallel", "arbitrary")),
        name="conv_mixer",
    )(pf, pf, pf, conv_w)


def _out_ln_kernel(a_ref, b_ref, c_ref, d_ref, w_ref, x_ref, g_ref, beta_ref, o_ref, *, alpha):
    kw = a_ref.shape[1]
    mix = _dot(a_ref[...], w_ref[0:kw])
    for n, r in enumerate((b_ref, c_ref, d_ref), start=1):
        mix = mix + _dot(r[...], w_ref[n * kw:(n + 1) * kw])
    o_ref[...] = _layer_norm(alpha * x_ref[...] + mix, g_ref[...], beta_ref[...])


def _out_ln(parts, w, x, g, b, *, layer, alpha, tm):
    n, d = x.shape
    kw = parts[0].shape[1]
    part_spec = pl.BlockSpec((tm, kw), lambda i: (i, 0))
    return pl.pallas_call(
        functools.partial(_out_ln_kernel, alpha=alpha),
        grid=(n // tm,),
        in_specs=[part_spec, part_spec, part_spec, part_spec,
                  pl.BlockSpec((None,) + w.shape[1:], lambda i: (layer, 0, 0)),
                  pl.BlockSpec((tm, d), lambda i: (i, 0)),
                  pl.BlockSpec((1, d), lambda i: (0, 0)),
                  pl.BlockSpec((1, d), lambda i: (0, 0))],
        out_specs=pl.BlockSpec((tm, d), lambda i: (i, 0)),
        out_shape=jax.ShapeDtypeStruct((n, d), F32),
        compiler_params=_params(("parallel",)),
        name="out_ln",
    )(*parts, w, x, g.reshape(1, d), b.reshape(1, d))


def _gate_spread():
    m = np.zeros((LANES, 2 * N_BRANCH * LANES), np.float32)
    for pr in range(2):
        for br in range(N_BRANCH):
            for lane in range(LANES):
                head = 2 * pr + lane // HEAD_DIM
                m[N_BRANCH * head + br, (pr * N_BRANCH + br) * LANES + lane] = 1.0
    return jnp.asarray(m, BF16)


def _overlap_t(t):
    n16 = t // CMP_STRIDE
    n_slc = t // SLC_BLOCK
    c_start = CMP_STRIDE * np.arange(n16)
    j_start = SLC_BLOCK * np.arange(n_slc)
    ov = ((c_start[None, :] < j_start[:, None] + SLC_BLOCK)
          & (c_start[None, :] + CMP_BLOCK > j_start[:, None])).astype(np.float32)
    ov[:, n16 - 1] = 0.0
    return jnp.asarray(ov, BF16)


def _mixer(hf, batch, w_proj, w_out, ln_g, ln_b, diff_lam, diff_gain, pos2, cmp_w1, cmp_w2,
           conv_w, layer, alpha):
    n, _ = hf.shape
    t = n // batch
    pa, pf = _proj(hf, w_proj, layer=layer, width_a=ATT_WIDTH, tm=1024, tn=1024)
    pa, pf = pa.reshape(batch, t, -1), pf.reshape(batch, t, -1)

    o_sb = _sb_attention(pa, tq=256, pairs=4, q_blk=0, k_blk=1, v_blk=2)
    lam_init = 0.8 - 0.6 * math.exp(-0.3 * layer)
    o_df = _diff_attention(pa, diff_lam[layer], diff_gain[layer], tq=512, heads=2, q_blk=6, k_blk=8, v_blk=10,
                           lam_init=lam_init)

    n16 = t // CMP_STRIDE
    kvc = pf[:, :, 3 * CONV_CH:3 * CONV_CH + 2 * LANES].reshape(batch, t, 2, NSA_KV_HEADS, HEAD_DIM)
    x16 = kvc.transpose(0, 2, 3, 1, 4).reshape(batch, 2 * NSA_KV_HEADS, n16, CMP_STRIDE * HEAD_DIM)
    ckv = _compress(x16, pos2, cmp_w1, cmp_w2, layer=layer)
    n_slc = t // SLC_BLOCK
    o_ns = _nsa_attention(pa, pf, ckv, _overlap_t(t), tq=256, q_blk=12, kv_blk=28, g_blk=14,
                          top_n=min(SLC_TOPN, n_slc))

    o_cv = _conv_mixer(pf, conv_w[layer], tt=512)
    parts = [o.reshape(n, -1) for o in (o_sb, o_df, o_ns, o_cv)]
    return _out_ln(parts, w_out, hf, ln_g, ln_b, layer=layer, alpha=alpha, tm=512)


def _seg(w_in, name, lo=0, hi=None):
    off = _SEG_OFF[name]
    hi = _SEG_W[name] if hi is None else hi
    return w_in[:, :, off + lo:off + hi]


def kernel(x, ln_g, ln_b, ffn_w1, ffn_w3, ffn_w2, w_in, w_out, diff_lam, diff_gain, cmp_pos, cmp_wk1,
           cmp_wk2, cmp_wv1, cmp_wv2, conv_w):
    batch, t, d = x.shape
    depth = ln_g.shape[0]
    alpha = (2 * depth) ** 0.25
    n = batch * t

    w1b, w3b, w2b = ffn_w1.astype(BF16), ffn_w3.astype(BF16), ffn_w2.astype(BF16)
    gate_pad = jnp.zeros(w_in.shape[:2] + (LANES - _GATES_PER_KV,), w_in.dtype)
    cols = [w_in[:, :, :_SEG_OFF["ns_kc"]], w_in[:, :, _SEG_OFF["ns_ks"]:_SEG_OFF["ns_g"]],
            _seg(w_in, "cv_b"), _seg(w_in, "cv_c"), _seg(w_in, "cv_h"), _seg(w_in, "ns_kc"), _seg(w_in, "ns_vc")]
    for j in range(NSA_KV_HEADS):
        cols += [_seg(w_in, "ns_g", j * _GATES_PER_KV, (j + 1) * _GATES_PER_KV), gate_pad]
    w_proj = jnp.concatenate(cols, axis=-1).astype(BF16)
    w_outb = w_out.astype(BF16)
    half = CMP_STRIDE * HEAD_DIM
    pos2 = cmp_pos.reshape(depth, 2, half)
    cmp_w1 = jnp.stack([cmp_wk1, cmp_wv1], axis=1).astype(BF16)
    cmp_w2 = jnp.stack([cmp_wk2, cmp_wv2], axis=1).astype(BF16)

    hf = x.reshape(n, d)
    for l in range(depth):
        hf = _ffn_ln(hf, w1b, w3b, w2b, ln_g[l, 0], ln_b[l, 0], layer=l, slot=0, alpha=alpha, tm=FFN_ROWS,
                     tf=FFN_TILE)
        hf = _mixer(hf, batch, w_proj, w_outb, ln_g[l, 1], ln_b[l, 1], diff_lam, diff_gain,
                    pos2, cmp_w1, cmp_w2, conv_w, l, alpha)
        hf = _ffn_ln(hf, w1b, w3b, w2b, ln_g[l, 2], ln_b[l, 2], layer=l, slot=1, alpha=alpha, tm=FFN_ROWS,
                     tf=FFN_TILE)
    return hf.reshape(batch, t, d)
```

```python
import functools
import math

import numpy as np
import jax
import jax.numpy as jnp
from jax import lax
from jax.experimental import pallas as pl
from jax.experimental.pallas import tpu as pltpu

F32 = jnp.float32
BF16 = jnp.bfloat16

HEAD_DIM = 64
SB_HEADS = 8
DIFF_HEADS = 4
NSA_HEADS = 8
NSA_KV_HEADS = 2
NSA_GROUP = NSA_HEADS // NSA_KV_HEADS
CMP_BLOCK = 32
CMP_STRIDE = 16
SLC_BLOCK = 64
SLC_TOPN = 16
WINDOW = 512
N_BRANCH = 3
CONV_CH = 512
CONV_WIDTH = 3
LN_EPS = 1e-5
RMS_EPS = 1e-5
NEG_BIG = -1e30
LOG2_E = math.log2(math.e)
EXP2_UNDERFLOW = -150.0
LANES = 128
VMEM_LIMIT = 56 * 1024 * 1024
FFN_TILE = 256
FFN_ROWS = 1024

_SEG_NAMES = ("sb_q", "sb_k", "sb_v", "df_q", "df_k", "df_v", "ns_q", "ns_kc", "ns_vc",
              "ns_ks", "ns_vs", "ns_kw", "ns_vw", "ns_g", "cv_b", "cv_c", "cv_h")
_SEG_WIDTHS = (512, 512, 512, 512, 512, 512, 512, 128, 128, 128, 128, 128, 128,
               NSA_HEADS * N_BRANCH, 512, 512, 512)
_SEG_OFF = dict(zip(_SEG_NAMES, np.cumsum((0,) + _SEG_WIDTHS[:-1]).tolist()))
_SEG_W = dict(zip(_SEG_NAMES, _SEG_WIDTHS))


_GATES_PER_KV = NSA_GROUP * N_BRANCH
_ATT_ORDER = ("sb_q", "sb_k", "sb_v", "df_q", "df_k", "df_v", "ns_q", "ns_ks", "ns_vs", "ns_kw", "ns_vw")
_ATT_OFF = dict(zip(_ATT_ORDER, np.cumsum([0] + [_SEG_W[s] for s in _ATT_ORDER[:-1]]).tolist()))
ATT_WIDTH = sum(_SEG_W[s] for s in _ATT_ORDER)
_F32_ORDER = ("cv_b", "cv_c", "cv_h", "ns_kc", "ns_vc")
_F32_OFF = dict(zip(_F32_ORDER, np.cumsum([0] + [_SEG_W[s] for s in _F32_ORDER[:-1]]).tolist()))
_GATE_OFF = sum(_SEG_W[s] for s in _F32_ORDER)

SB_TQ, SB_PAIRS = 256, 4
DIFF_TQ, DIFF_HEADS_PER_STEP = 512, 2
NSA_TQ = 256
PROJ_ROWS, PROJ_COLS = 1024, 1024
OUT_ROWS = 512
CONV_ROWS = 512


def _params(sem):
    return pltpu.CompilerParams(dimension_semantics=sem, vmem_limit_bytes=VMEM_LIMIT)


def _layer_norm(y, g, b):
    mu = jnp.mean(y, axis=-1, keepdims=True)
    d = y - mu
    var = jnp.mean(d * d, axis=-1, keepdims=True)
    return d * lax.rsqrt(var + LN_EPS) * g + b


def _dot(a, b):
    return jnp.dot(a, b, preferred_element_type=F32)


def _lane_group_max(s, mx):
    for g in range(s.shape[1] // LANES):
        mx = jnp.maximum(mx, s[:, g * LANES:(g + 1) * LANES])
    return mx


def _softmax_sweep(s_refs, *, rows, acc_lanes, q0, tq, big, merge_tail, scores_fn, weigh_fn):
    n_big = q0 // big
    mid = pl.multiple_of(n_big * big, tq)
    tail = (q0 - mid) // tq
    tail_widths = [(k + 1) * tq for k in range(big // tq)]

    def chunks(fn, init):
        return lax.fori_loop(0, n_big, lambda c, carry: fn(pl.multiple_of(c * big, big), big, carry), init)

    def scores(start, width, mxs, is_tail=False):
        out = []
        for ref, s, mx in zip(s_refs, scores_fn(start, width, is_tail), mxs):
            ref[:, pl.ds(start, width)] = s
            out.append(_lane_group_max(s, mx))
        return tuple(out)

    def tiles(fn, init):
        return lax.fori_loop(0, tail, lambda c, carry: fn(pl.multiple_of(mid + c * tq, tq), tq, carry), init)

    mxs = chunks(scores, (jnp.full((rows, LANES), NEG_BIG, F32),) * len(s_refs))
    if merge_tail:
        mxs = lax.switch(tail, [functools.partial(scores, mid, w, is_tail=True) for w in tail_widths], mxs)
    else:
        mxs = scores(q0, tq, tiles(scores, mxs), is_tail=True)
    ms = [jnp.broadcast_to(jnp.max(mx, axis=1, keepdims=True), (rows, LANES)) for mx in mxs]

    def weigh(start, width, accs):
        ps = [jnp.exp2(ref[:, pl.ds(start, width)] - jnp.concatenate([m] * (width // LANES), axis=1)).astype(BF16)
              for ref, m in zip(s_refs, ms)]
        return tuple(acc + w for acc, w in zip(accs, weigh_fn(ps, start, width)))

    accs = chunks(weigh, (jnp.zeros((rows, acc_lanes), F32),) * len(s_refs))
    if merge_tail:
        return lax.switch(tail, [functools.partial(weigh, mid, w) for w in tail_widths], accs)
    return weigh(q0, tq, tiles(weigh, accs))


def _dot_nt(a, b):
    return lax.dot_general(a, b, (((1,), (1,)), ((), ())), preferred_element_type=F32)


def _ffn_ln_kernel(x_ref, w1_ref, w3_ref, w2_ref, g_ref, b_ref, o_ref, xb_ref, *, alpha):
    j = pl.program_id(1)

    @pl.when(j == 0)
    def _():
        xb_ref[...] = x_ref[...].astype(BF16)
        o_ref[...] = jnp.zeros_like(o_ref)

    xb = xb_ref[...]
    a = _dot(xb, w1_ref[...])
    b = _dot(xb, w3_ref[...])
    hm = (a / (1.0 + jnp.exp(-a))) * b
    o_ref[...] += _dot(hm.astype(BF16), w2_ref[...])

    @pl.when(j == pl.num_programs(1) - 1)
    def _():
        y = alpha * x_ref[...] + 0.5 * o_ref[...]
        o_ref[...] = _layer_norm(y, g_ref[...], b_ref[...])


def _ffn_ln(x, w1, w3, w2, g, b, *, layer, slot, alpha, tm, tf):
    n, d = x.shape
    f = w1.shape[-1]
    return pl.pallas_call(
        functools.partial(_ffn_ln_kernel, alpha=alpha),
        grid=(n // tm, f // tf),
        in_specs=[
            pl.BlockSpec((tm, d), lambda i, j: (i, 0)),
            pl.BlockSpec((None, None, d, tf), lambda i, j: (layer, slot, 0, j)),
            pl.BlockSpec((None, None, d, tf), lambda i, j: (layer, slot, 0, j)),
            pl.BlockSpec((None, None, tf, d), lambda i, j: (layer, slot, j, 0)),
            pl.BlockSpec((1, d), lambda i, j: (0, 0)),
            pl.BlockSpec((1, d), lambda i, j: (0, 0)),
        ],
        out_specs=pl.BlockSpec((tm, d), lambda i, j: (i, 0)),
        out_shape=jax.ShapeDtypeStruct((n, d), F32),
        scratch_shapes=[pltpu.VMEM((tm, d), BF16)],
        compiler_params=_params(("parallel", "arbitrary")),
        name="ffn_ln",
    )(x, w1, w3, w2, g.reshape(1, d), b.reshape(1, d))


def _proj_kernel(x_ref, w_ref, oa_ref, of_ref, xb_ref, *, n_a):
    j = pl.program_id(1)

    @pl.when(j == 0)
    def _():
        xb_ref[...] = x_ref[...].astype(BF16)

    @pl.when(j < n_a)
    def _():
        oa_ref[...] = _dot(xb_ref[...], w_ref[...]).astype(oa_ref.dtype)

    @pl.when(j >= n_a)
    def _():
        of_ref[...] = _dot(xb_ref[...], w_ref[...])


def _proj(x, w, *, layer, width_a, tm, tn):
    n, d = x.shape
    m = w.shape[-1]
    n_a = width_a // tn
    return pl.pallas_call(
        functools.partial(_proj_kernel, n_a=n_a),
        grid=(n // tm, m // tn),
        in_specs=[pl.BlockSpec((tm, d), lambda i, j: (i, 0)),
                  pl.BlockSpec((None, d, tn), lambda i, j: (layer, 0, j))],
        out_specs=[pl.BlockSpec((tm, tn), lambda i, j: (i, jnp.minimum(j, n_a - 1))),
                   pl.BlockSpec((tm, tn), lambda i, j: (i, jnp.maximum(j - n_a, 0)))],
        out_shape=[jax.ShapeDtypeStruct((n, width_a), BF16), jax.ShapeDtypeStruct((n, m - width_a), F32)],
        scratch_shapes=[pltpu.VMEM((tm, d), BF16)],
        compiler_params=_params(("parallel", "arbitrary")),
        name="proj",
    )(x, w)


def _sb_kernel(q_ref, k_ref, v_ref, o_ref, *, tq, pairs):
    i = pl.program_id(2)
    q0 = i * tq
    lane = lax.broadcasted_iota(jnp.int32, (tq, LANES), 1)
    tri = jnp.where(lax.broadcasted_iota(jnp.int32, (tq, tq), 0)
                    >= lax.broadcasted_iota(jnp.int32, (tq, tq), 1), 1.0, 0.0).astype(BF16)
    qpos = q0 + lax.rem(lax.broadcasted_iota(jnp.int32, (2 * tq, 1), 0), tq)
    q2 = []
    for p in range(pairs):
        qs = (q_ref[0, :, p * LANES:(p + 1) * LANES].astype(F32) * (HEAD_DIM ** -0.5)).astype(BF16)
        zero = jnp.zeros_like(qs)
        q2.append(jnp.concatenate([jnp.where(lane < HEAD_DIM, qs, zero),
                                   jnp.where(lane < HEAD_DIM, zero, qs)], axis=0))

    def block(start, state, keep=None):
        width = tq
        masked = keep is not None
        out = []
        for p, (c, acc) in enumerate(state):
            ks = k_ref[0, pl.ds(start, width), p * LANES:(p + 1) * LANES]
            vs = v_ref[0, pl.ds(start, width), p * LANES:(p + 1) * LANES]
            z = _dot_nt(q2[p], ks) * LOG2_E
            nz = -z
            lg = jnp.minimum(nz, 0.0) - jnp.log2(1.0 + jnp.exp2(jnp.minimum(z, nz)))
            if masked:
                lg = jnp.where(keep, lg, 0.0)
            hi = lg.astype(BF16)
            lo = (lg - hi.astype(F32)).astype(BF16)
            suffix = _dot(hi, tri) + _dot(lo, tri)
            a = jnp.exp2(z + suffix + c)
            if masked:
                a = jnp.where(keep, a, 0.0)
            out.append((c + suffix[:, 0:1], acc + _dot(a.astype(BF16), vs)))
        return tuple(out)

    state = ((jnp.zeros((2 * tq, 1), F32), jnp.zeros((2 * tq, LANES), F32)),) * pairs
    key = lax.broadcasted_iota(jnp.int32, (1, tq), 1)
    state = block(pl.multiple_of(q0, tq), state, keep=q0 + key < qpos)
    state = block(pl.multiple_of(jnp.maximum(q0 - tq, 0), tq), state, keep=key < q0)

    def live(state):
        top = functools.reduce(jnp.maximum, [c for c, _ in state])
        return (jnp.max(top) > EXP2_UNDERFLOW).astype(jnp.int32)

    def cond(carry):
        return (carry[0] >= 0) & (carry[1] > 0)

    def body(carry):
        kt, _, state = carry
        state = block(pl.multiple_of(kt * tq, tq), state)
        return kt - 1, live(state), state

    _, _, state = lax.while_loop(cond, body, (i - 2, live(state), state))
    o_ref[0] = jnp.concatenate([jnp.where(lane < HEAD_DIM, acc[:tq], acc[tq:]) for _, acc in state],
                               axis=1).astype(o_ref.dtype)


def _sb_attention(pa, *, tq, pairs, q_blk, k_blk, v_blk):
    b, t, _ = pa.shape
    w = pairs * LANES
    n_groups = SB_HEADS // (2 * pairs)
    return pl.pallas_call(
        functools.partial(_sb_kernel, tq=tq, pairs=pairs),
        grid=(b, n_groups, t // tq),
        in_specs=[
            pl.BlockSpec((1, tq, w), lambda bi, p, i: (bi, i, q_blk + p)),
            pl.BlockSpec((1, t, w), lambda bi, p, i: (bi, 0, k_blk + p)),
            pl.BlockSpec((1, t, w), lambda bi, p, i: (bi, 0, v_blk + p)),
        ],
        out_specs=pl.BlockSpec((1, tq, w), lambda bi, p, i: (bi, i, p)),
        out_shape=jax.ShapeDtypeStruct((b, t, SB_HEADS * HEAD_DIM), BF16),
        compiler_params=_params(("parallel", "parallel", "arbitrary")),
        name="sb_attention",
    )(pa, pa, pa)


def _diff_kernel(lam_ref, gain_ref, q_ref, k_ref, v_ref, o_ref, s_ref, *, tq, heads, lam_init):
    i = pl.program_id(2)
    q0 = pl.multiple_of(i * tq, tq)
    lane = lax.broadcasted_iota(jnp.int32, (tq, LANES), 1)
    qpos = q0 + lax.rem(lax.broadcasted_iota(jnp.int32, (2 * tq, 1), 0), tq)
    q2 = []
    for h in range(heads):
        qs = (q_ref[0, :, h * LANES:(h + 1) * LANES].astype(F32) * (HEAD_DIM ** -0.5)).astype(BF16)
        zero = jnp.zeros_like(qs)
        q2.append(jnp.concatenate([jnp.where(lane < HEAD_DIM, qs, zero),
                                   jnp.where(lane < HEAD_DIM, zero, qs)], axis=0))

    def scores(start, width, tail):
        out = []
        for h in range(heads):
            s = _dot_nt(q2[h], k_ref[0, pl.ds(start, width), h * LANES:(h + 1) * LANES]) * LOG2_E
            if tail:
                s = jnp.where(start + lax.broadcasted_iota(jnp.int32, (1, width), 1) <= qpos, s, NEG_BIG)
            out.append(s)
        return out

    def weigh(ps, start, width):
        ones = jnp.ones((width, LANES), BF16)
        return [_dot(p, jnp.concatenate([v_ref[0, pl.ds(start, width), h * LANES:(h + 1) * LANES], ones], axis=1))
                for h, p in enumerate(ps)]

    s_refs = [s_ref.at[pl.ds(h * 2 * tq, 2 * tq)] for h in range(heads)]
    accs = _softmax_sweep(s_refs, rows=2 * tq, acc_lanes=2 * LANES, q0=q0, tq=tq, big=2 * tq, merge_tail=False,
                          scores_fn=scores, weigh_fn=weigh)
    lp = lam_ref[...]
    lam = (jnp.exp(jnp.sum(lp[0:1] * lp[1:2], axis=1, keepdims=True))
           - jnp.exp(jnp.sum(lp[2:3] * lp[3:4], axis=1, keepdims=True)) + lam_init)
    out = []
    for acc in accs:
        o = acc[:, :LANES] / acc[:, LANES:]
        od = o[:tq] - lam * o[tq:]
        od = od * lax.rsqrt(jnp.mean(od * od, axis=-1, keepdims=True) + RMS_EPS)
        out.append(od * gain_ref[...] * (1.0 - lam_init))
    o_ref[0] = jnp.concatenate(out, axis=1).astype(o_ref.dtype)


def _diff_attention(pa, lam_params, gain, *, tq, heads, q_blk, k_blk, v_blk, lam_init):
    b, t, _ = pa.shape
    w = heads * LANES
    return pl.pallas_call(
        functools.partial(_diff_kernel, tq=tq, heads=heads, lam_init=lam_init),
        grid=(b, DIFF_HEADS // heads, t // tq),
        in_specs=[
            pl.BlockSpec((4, HEAD_DIM), lambda bi, h, i: (0, 0)),
            pl.BlockSpec((1, LANES), lambda bi, h, i: (0, 0)),
            pl.BlockSpec((1, tq, w), lambda bi, h, i: (bi, i, q_blk + h)),
            pl.BlockSpec((1, t, w), lambda bi, h, i: (bi, 0, k_blk + h)),
            pl.BlockSpec((1, t, w), lambda bi, h, i: (bi, 0, v_blk + h)),
        ],
        out_specs=pl.BlockSpec((1, tq, w), lambda bi, h, i: (bi, i, h)),
        out_shape=jax.ShapeDtypeStruct((b, t, DIFF_HEADS * LANES), BF16),
        scratch_shapes=[pltpu.VMEM((heads * 2 * tq, t), F32)],
        compiler_params=_params(("parallel", "parallel", "arbitrary")),
        name="diff_attention",
    )(lam_params, gain.reshape(1, LANES), pa, pa, pa)


def _compress_kernel(x_ref, pos_ref, w1_ref, w2_ref, o_ref):
    x = x_ref[0, 0]
    n16, half = x.shape
    xa = (x + pos_ref[0:1]).astype(BF16)
    xb = (x + pos_ref[1:2]).astype(BF16)
    w1 = w1_ref[0]
    first = _dot(xa, w1[:half])
    second = _dot(xb, w1[half:])
    h = first + pltpu.roll(second, n16 - 1, 0)
    g = 0.5 * h * (1.0 + jnp.tanh(math.sqrt(2.0 / math.pi) * (h + 0.044715 * (h * h * h))))
    o_ref[0, 0] = _dot(g.astype(BF16), w2_ref[0])


def _compress(x16, pos2, w1, w2, *, layer):
    b, n_streams, n16, half = x16.shape
    d = w2.shape[-1]
    return pl.pallas_call(
        _compress_kernel,
        grid=(b, n_streams),
        in_specs=[
            pl.BlockSpec((1, 1, n16, half), lambda bi, s: (bi, s, 0, 0)),
            pl.BlockSpec((None, 2, half), lambda bi, s: (layer, 0, 0)),
            pl.BlockSpec((None, 1, 2 * half, w1.shape[-1]), lambda bi, s: (layer, s // NSA_KV_HEADS, 0, 0)),
            pl.BlockSpec((None, 1, w2.shape[2], d), lambda bi, s: (layer, s // NSA_KV_HEADS, 0, 0)),
        ],
        out_specs=pl.BlockSpec((1, 1, n16, d), lambda bi, s: (bi, s, 0, 0)),
        out_shape=jax.ShapeDtypeStruct((b, n_streams, n16, d), F32),
        compiler_params=_params(("parallel", "arbitrary")),
        name="nsa_compress",
    )(x16, pos2, w1, w2)


def _nsa_kernel(q_ref, g_ref, ck_ref, cv_ref, ovt_ref, gx_ref, ks_ref, vs_ref, kw_ref, vw_ref, o_ref,
                kse_ref, kso_ref, vse_ref, vso_ref, kwd_ref, vwe_ref, vwo_ref, ckd_ref, cvd_ref, s_ref,
                *, tq, top_n):
    i = pl.program_id(2)
    grp = NSA_GROUP
    d = HEAD_DIM
    q0 = i * tq
    n_cmp = ck_ref.shape[2]
    n_slc = ovt_ref.shape[0]
    t = ks_ref.shape[1]
    rows = grp * tq
    half = rows // 2

    @pl.when(i == 0)
    def _():
        onehot = jnp.where(lax.broadcasted_iota(jnp.int32, (t, d), 0) // SLC_BLOCK
                           == lax.broadcasted_iota(jnp.int32, (t, d), 1), 1.0, 0.0)
        ones = jnp.ones((t, d), F32)

        def put(ref, left, right):
            ref[...] = jnp.concatenate([left, right], axis=1).astype(BF16)

        first_kv = pl.program_id(1) == 0

        def own_head(ref):
            both = ref[0].astype(F32)
            return jnp.where(first_kv, both[:, :d], both[:, d:])

        ks, vs, kw, vw = own_head(ks_ref), own_head(vs_ref), own_head(kw_ref), own_head(vw_ref)
        put(kse_ref, ks, onehot)
        put(kso_ref, onehot, ks)
        put(vse_ref, vs, ones)
        put(vso_ref, ones, vs)
        put(kwd_ref, kw, kw)
        put(vwe_ref, vw, ones)
        put(vwo_ref, ones, vw)
        put(ckd_ref, ck_ref[0, 0], ck_ref[0, 0])
        put(cvd_ref, cv_ref[0, 0], cv_ref[0, 0])

    lane = lax.broadcasted_iota(jnp.int32, (tq, LANES), 1)
    left = lane < d
    qs = (q_ref[0].astype(F32) * (d ** -0.5)).astype(BF16)
    pairs = [qs[:, :LANES], qs[:, LANES:]]
    zero = jnp.zeros((tq, LANES), BF16)
    qz = jnp.concatenate([jnp.where(left, pairs[0], zero), jnp.where(left, pairs[1], zero),
                          jnp.where(left, zero, pairs[0]), jnp.where(left, zero, pairs[1])], axis=0)
    t4 = q0 + lax.rem(lax.broadcasted_iota(jnp.int32, (rows, 1), 0), tq)

    def weigh(p, start, width, ve_ref, vo_ref):
        return jnp.concatenate([_dot(p[:half], ve_ref[pl.ds(start, width), :]),
                                _dot(p[half:], vo_ref[pl.ds(start, width), :])], axis=0)

    def normalise(acc):
        out = []
        for pr in range(2):
            even, odd = acc[pr * tq:(pr + 1) * tq], acc[half + pr * tq:half + (pr + 1) * tq]
            den = pltpu.roll(jnp.where(left, odd, even), d, 1)
            out.append(jnp.where(left, even, odd) / den)
        return out

    w_width = (-(-WINDOW // tq) + 1) * tq
    w_start = pl.multiple_of(jnp.maximum(q0 + tq - w_width, 0), tq)
    sw = _dot_nt(qz, kwd_ref[pl.ds(w_start, w_width), :])
    kpos = w_start + lax.broadcasted_iota(jnp.int32, (1, w_width), 1)
    sw = jnp.where((kpos <= t4) & (kpos > t4 - WINDOW), sw, NEG_BIG)
    pw = jnp.exp(sw - jnp.max(sw, axis=1, keepdims=True))
    o_win = normalise(weigh(pw.astype(BF16), w_start, w_width, vwe_ref, vwo_ref))

    gate = 1.0 / (1.0 + jnp.exp(-g_ref[0]))
    g_hi = gate.astype(BF16)
    g_r = gate - g_hi.astype(F32)
    g_mid = g_r.astype(BF16)
    g_lo = (g_r - g_mid.astype(F32)).astype(BF16)
    spread = gx_ref[...]
    gx = _dot(g_hi, spread) + _dot(g_mid, spread) + _dot(g_lo, spread)

    s = _dot_nt(qz, ckd_ref[...])
    cmp_end = CMP_STRIDE * lax.broadcasted_iota(jnp.int32, (1, n_cmp), 1) + (CMP_BLOCK - 1)
    cmask = cmp_end <= t4
    s = jnp.where(cmask, s, NEG_BIG)
    e = jnp.exp(s - jnp.max(s, axis=1, keepdims=True))
    p = e / jnp.sum(e, axis=1, keepdims=True)
    p = jnp.where(cmask, p, 0.0)
    o_cmp = _dot(p.astype(BF16), cvd_ref[...])

    pg = p[0:tq]
    for h in range(1, grp):
        pg = pg + p[h * tq:(h + 1) * tq]
    p_hi = pg.astype(BF16)
    r1 = pg - p_hi.astype(F32)
    p_mid = r1.astype(BF16)
    p_lo = (r1 - p_mid.astype(F32)).astype(BF16)
    ovt = ovt_ref[...]
    imp_t = _dot_nt(ovt, p_hi) + _dot_nt(ovt, p_mid) + _dot_nt(ovt, p_lo)
    blk = lax.broadcasted_iota(jnp.int32, (n_slc, tq), 0)
    cur = (q0 + lax.broadcasted_iota(jnp.int32, (n_slc, tq), 1)) // SLC_BLOCK
    forced = (blk == 0) | (blk == cur) | (blk == cur - 1)
    score = jnp.where(forced, jnp.inf, jnp.where(blk <= cur, imp_t, -jnp.inf))
    rank = jnp.zeros((n_slc, tq), F32)
    for j in range(n_slc):
        sj = score[j:j + 1, :]
        tie = jnp.where(blk > j, 1.0, 0.0)
        rank = rank + jnp.where(sj > score, 1.0, jnp.where(sj == score, tie, 0.0))
    sel_t = jnp.where(rank < top_n, 1.0, 0.0)
    if n_slc < d:
        sel_t = jnp.concatenate([sel_t, jnp.zeros((d - n_slc, tq), F32)], axis=0)
    sel_t2 = jnp.concatenate([sel_t, sel_t], axis=0).astype(BF16)
    eye = jnp.where(lax.broadcasted_iota(jnp.int32, (tq, tq), 0)
                    == lax.broadcasted_iota(jnp.int32, (tq, tq), 1), 1.0, 0.0).astype(BF16)
    sel2 = _dot_nt(eye, sel_t2)
    bias = ((sel2 - 1.0) * (-NEG_BIG)).astype(BF16)
    q_even = jnp.concatenate([jnp.where(left, pairs[0], bias), jnp.where(left, pairs[1], bias)], axis=0)
    q_odd = jnp.concatenate([jnp.where(left, bias, pairs[0]), jnp.where(left, bias, pairs[1])], axis=0)

    def slc_scores(start, width, tail):
        s = jnp.concatenate([_dot_nt(q_even, kse_ref[pl.ds(start, width), :]),
                             _dot_nt(q_odd, kso_ref[pl.ds(start, width), :])], axis=0) * LOG2_E
        if tail:
            s = jnp.where(start + lax.broadcasted_iota(jnp.int32, (1, width), 1) <= t4, s, NEG_BIG)
        return s

    (acc,) = _softmax_sweep((s_ref,), rows=rows, acc_lanes=LANES, q0=pl.multiple_of(q0, tq), tq=tq,
                            big=4 * tq, merge_tail=True, scores_fn=lambda *a: (slc_scores(*a),),
                            weigh_fn=lambda ps, start, width: (weigh(ps[0], start, width, vse_ref, vso_ref),))
    o_slc = normalise(acc)

    out = []
    for pr in range(2):
        cmp_pr = jnp.where(left, o_cmp[pr * tq:(pr + 1) * tq], o_cmp[half + pr * tq:half + (pr + 1) * tq])
        mixed = 0.0
        for br, o_br in enumerate((cmp_pr, o_slc[pr], o_win[pr])):
            blk_idx = pr * N_BRANCH + br
            mixed = mixed + gx[:, blk_idx * LANES:(blk_idx + 1) * LANES] * o_br
        out.append(mixed)
    o_ref[0] = jnp.concatenate(out, axis=1).astype(o_ref.dtype)


def _nsa_attention(pa, pf, ckv, ovt, *, tq, q_blk, kv_blk, g_blk, top_n):
    b, t, _ = pa.shape
    n16 = ckv.shape[2]
    n_slc = ovt.shape[0]
    qw = NSA_GROUP * HEAD_DIM
    kv_specs = [pl.BlockSpec((1, t, LANES), functools.partial(lambda bi, j, i, c: (bi, 0, c), c=kv_blk + n))
                for n in range(4)]
    return pl.pallas_call(
        functools.partial(_nsa_kernel, tq=tq, top_n=top_n),
        grid=(b, NSA_KV_HEADS, t // tq),
        in_specs=[
            pl.BlockSpec((1, tq, qw), lambda bi, j, i: (bi, i, q_blk + j)),
            pl.BlockSpec((1, tq, LANES), lambda bi, j, i: (bi, i, g_blk + j)),
            pl.BlockSpec((1, 1, n16, HEAD_DIM), lambda bi, j, i: (bi, j, 0, 0)),
            pl.BlockSpec((1, 1, n16, HEAD_DIM), lambda bi, j, i: (bi, NSA_KV_HEADS + j, 0, 0)),
            pl.BlockSpec((n_slc, n16), lambda bi, j, i: (0, 0)),
            pl.BlockSpec((LANES, 2 * N_BRANCH * LANES), lambda bi, j, i: (0, 0)),
            *kv_specs,
        ],
        out_specs=pl.BlockSpec((1, tq, qw), lambda bi, j, i: (bi, i, j)),
        out_shape=jax.ShapeDtypeStruct((b, t, NSA_HEADS * HEAD_DIM), BF16),
        scratch_shapes=[pltpu.VMEM((t, LANES), BF16)] * 7 + [pltpu.VMEM((n16, LANES), BF16)] * 2
                       + [pltpu.VMEM((NSA_GROUP * tq, t), F32)],
        compiler_params=_params(("arbitrary", "arbitrary", "arbitrary")),
        name="nsa_attention",
    )(pa, pf, ckv, ckv, ovt, _gate_spread(), pa, pa, pa, pa)


def _conv_kernel(cb_ref, cc_ref, ch_ref, w_ref, o_ref, prev_ref):
    @pl.when(pl.program_id(1) == 0)
    def _():
        prev_ref[...] = jnp.zeros_like(prev_ref)

    u = cc_ref[0] * ch_ref[0]
    tt = u.shape[0]
    row = lax.broadcasted_iota(jnp.int32, u.shape, 0)
    last1 = prev_ref[7:8]
    last2 = prev_ref[6:7]
    u1 = jnp.where(row >= 1, pltpu.roll(u, 1, 0), last1)
    u2 = jnp.where(row >= 2, pltpu.roll(u, 2, 0), jnp.where(row == 1, last1, last2))
    w = w_ref[...]
    o_ref[0] = (cb_ref[0] * (w[0:1] * u2 + w[1:2] * u1 + w[2:3] * u)).astype(o_ref.dtype)
    prev_ref[...] = u[tt - 8:tt]


def _conv_mixer(pf, conv_w, *, tt):
    b, t, _ = pf.shape
    c = CONV_CH
    return pl.pallas_call(
        _conv_kernel,
        grid=(b, t // tt),
        in_specs=[
            pl.BlockSpec((1, tt, c), lambda bi, i: (bi, i, 0)),
            pl.BlockSpec((1, tt, c), lambda bi, i: (bi, i, 1)),
            pl.BlockSpec((1, tt, c), lambda bi, i: (bi, i, 2)),
            pl.BlockSpec((CONV_WIDTH, c), lambda bi, i: (0, 0)),
        ],
        out_specs=pl.BlockSpec((1, tt, c), lambda bi, i: (bi, i, 0)),
        out_shape=jax.ShapeDtypeStruct((b, t, c), BF16),
        scratch_shapes=[pltpu.VMEM((8, c), F32)],
        compiler_params=_params(("parallel", "arbitrary")),
        name="conv_mixer",
    )(pf, pf, pf, conv_w)


def _out_ln_kernel(a_ref, b_ref, c_ref, d_ref, w_ref, x_ref, g_ref, beta_ref, o_ref, *, alpha):
    y = jnp.concatenate([a_ref[...], b_ref[...], c_ref[...], d_ref[...]], axis=1)
    o_ref[...] = _layer_norm(alpha * x_ref[...] + _dot(y, w_ref[...]), g_ref[...], beta_ref[...])


def _out_ln(parts, w, x, g, b, *, layer, alpha, tm):
    n, d = x.shape
    kw = parts[0].shape[1]
    part_spec = pl.BlockSpec((tm, kw), lambda i: (i, 0))
    return pl.pallas_call(
        functools.partial(_out_ln_kernel, alpha=alpha),
        grid=(n // tm,),
        in_specs=[part_spec, part_spec, part_spec, part_spec,
                  pl.BlockSpec((None,) + w.shape[1:], lambda i: (layer, 0, 0)),
                  pl.BlockSpec((tm, d), lambda i: (i, 0)),
                  pl.BlockSpec((1, d), lambda i: (0, 0)),
                  pl.BlockSpec((1, d), lambda i: (0, 0))],
        out_specs=pl.BlockSpec((tm, d), lambda i: (i, 0)),
        out_shape=jax.ShapeDtypeStruct((n, d), F32),
        compiler_params=_params(("parallel",)),
        name="out_ln",
    )(*parts, w, x, g.reshape(1, d), b.reshape(1, d))


def _gate_spread():
    m = np.zeros((LANES, 2 * N_BRANCH * LANES), np.float32)
    for pr in range(2):
        for br in range(N_BRANCH):
            for lane in range(LANES):
                head = 2 * pr + lane // HEAD_DIM
                m[N_BRANCH * head + br, (pr * N_BRANCH + br) * LANES + lane] = 1.0
    return jnp.asarray(m, BF16)


def _overlap_t(t):
    n16 = t // CMP_STRIDE
    n_slc = t // SLC_BLOCK
    c_start = CMP_STRIDE * np.arange(n16)
    j_start = SLC_BLOCK * np.arange(n_slc)
    ov = ((c_start[None, :] < j_start[:, None] + SLC_BLOCK)
          & (c_start[None, :] + CMP_BLOCK > j_start[:, None])).astype(np.float32)
    ov[:, n16 - 1] = 0.0
    return jnp.asarray(ov, BF16)


def _mixer(hf, batch, w_proj, w_out, ln_g, ln_b, diff_lam, diff_gain, pos2, cmp_w1, cmp_w2,
           conv_w, layer, alpha):
    n, _ = hf.shape
    t = n // batch
    n_slc = t // SLC_BLOCK
    assert t % (4 * NSA_TQ) == 0 and t % (2 * DIFF_TQ) == 0 and t % SB_TQ == 0 and n_slc <= HEAD_DIM
    assert t >= WINDOW + NSA_TQ and n % PROJ_ROWS == 0 and n % FFN_ROWS == 0 and n % OUT_ROWS == 0
    pa, pf = _proj(hf, w_proj, layer=layer, width_a=ATT_WIDTH, tm=PROJ_ROWS, tn=PROJ_COLS)
    pa, pf = pa.reshape(batch, t, -1), pf.reshape(batch, t, -1)

    def att_blk(name, lanes):
        return _ATT_OFF[name] // lanes

    w = SB_PAIRS * LANES
    o_sb = _sb_attention(pa, tq=SB_TQ, pairs=SB_PAIRS, q_blk=att_blk("sb_q", w), k_blk=att_blk("sb_k", w),
                         v_blk=att_blk("sb_v", w))
    lam_init = 0.8 - 0.6 * math.exp(-0.3 * layer)
    w = DIFF_HEADS_PER_STEP * LANES
    o_df = _diff_attention(pa, diff_lam[layer], diff_gain[layer], tq=DIFF_TQ, heads=DIFF_HEADS_PER_STEP,
                           q_blk=att_blk("df_q", w), k_blk=att_blk("df_k", w), v_blk=att_blk("df_v", w),
                           lam_init=lam_init)

    n16 = t // CMP_STRIDE
    kvc = pf[:, :, _F32_OFF["ns_kc"]:_F32_OFF["ns_kc"] + 2 * LANES].reshape(batch, t, 2, NSA_KV_HEADS, HEAD_DIM)
    x16 = kvc.transpose(0, 2, 3, 1, 4).reshape(batch, 2 * NSA_KV_HEADS, n16, CMP_STRIDE * HEAD_DIM)
    ckv = _compress(x16, pos2, cmp_w1, cmp_w2, layer=layer)
    o_ns = _nsa_attention(pa, pf, ckv, _overlap_t(t), tq=NSA_TQ, q_blk=att_blk("ns_q", NSA_GROUP * HEAD_DIM),
                          kv_blk=att_blk("ns_ks", LANES), g_blk=_GATE_OFF // LANES, top_n=min(SLC_TOPN, n_slc))

    o_cv = _conv_mixer(pf, conv_w[layer], tt=CONV_ROWS)
    parts = [o.reshape(n, -1) for o in (o_sb, o_df, o_ns, o_cv)]
    return _out_ln(parts, w_out, hf, ln_g, ln_b, layer=layer, alpha=alpha, tm=OUT_ROWS)


def _seg(w_in, name, lo=0, hi=None):
    off = _SEG_OFF[name]
    hi = _SEG_W[name] if hi is None else hi
    return w_in[:, :, off + lo:off + hi]


def kernel(x, ln_g, ln_b, ffn_w1, ffn_w3, ffn_w2, w_in, w_out, diff_lam, diff_gain, cmp_pos, cmp_wk1,
           cmp_wk2, cmp_wv1, cmp_wv2, conv_w):
    batch, t, d = x.shape
    depth = ln_g.shape[0]
    alpha = (2 * depth) ** 0.25
    n = batch * t

    w1b, w3b, w2b = ffn_w1.astype(BF16), ffn_w3.astype(BF16), ffn_w2.astype(BF16)
    gate_pad = jnp.zeros(w_in.shape[:2] + (LANES - _GATES_PER_KV,), w_in.dtype)
    cols = [_seg(w_in, name) for name in _ATT_ORDER + _F32_ORDER]
    for j in range(NSA_KV_HEADS):
        cols += [_seg(w_in, "ns_g", j * _GATES_PER_KV, (j + 1) * _GATES_PER_KV), gate_pad]
    w_proj = jnp.concatenate(cols, axis=-1).astype(BF16)
    w_outb = w_out.astype(BF16)
    half = CMP_STRIDE * HEAD_DIM
    pos2 = cmp_pos.reshape(depth, 2, half)
    cmp_w1 = jnp.stack([cmp_wk1, cmp_wv1], axis=1).astype(BF16)
    cmp_w2 = jnp.stack([cmp_wk2, cmp_wv2], axis=1).astype(BF16)

    hf = x.reshape(n, d)
    for l in range(depth):
        hf = _ffn_ln(hf, w1b, w3b, w2b, ln_g[l, 0], ln_b[l, 0], layer=l, slot=0, alpha=alpha, tm=FFN_ROWS,
                     tf=FFN_TILE)
        hf = _mixer(hf, batch, w_proj, w_outb, ln_g[l, 1], ln_b[l, 1], diff_lam, diff_gain,
                    pos2, cmp_w1, cmp_w2, conv_w, l, alpha)
        hf = _ffn_ln(hf, w1b, w3b, w2b, ln_g[l, 2], ln_b[l, 2], layer=l, slot=1, alpha=alpha, tm=FFN_ROWS,
                     tf=FFN_TILE)
    return hf.reshape(batch, t, d)
```

```python
import functools
import math

import numpy as np
import jax
import jax.numpy as jnp
from jax import lax
from jax.experimental import pallas as pl
from jax.experimental.pallas import tpu as pltpu

F32 = jnp.float32
BF16 = jnp.bfloat16

HEAD_DIM = 64
SB_HEADS = 8
DIFF_HEADS = 4
NSA_HEADS = 8
NSA_KV_HEADS = 2
NSA_GROUP = NSA_HEADS // NSA_KV_HEADS
CMP_BLOCK = 32
CMP_STRIDE = 16
SLC_BLOCK = 64
SLC_TOPN = 16
WINDOW = 512
N_BRANCH = 3
CONV_CH = 512
CONV_WIDTH = 3
LN_EPS = 1e-5
RMS_EPS = 1e-5
NEG_BIG = -1e30
LOG2_E = math.log2(math.e)
EXP2_UNDERFLOW = -150.0
LANES = 128
VMEM_LIMIT = 56 * 1024 * 1024
FFN_TILE = 256
FFN_ROWS = 1024

_SEG_NAMES = ("sb_q", "sb_k", "sb_v", "df_q", "df_k", "df_v", "ns_q", "ns_kc", "ns_vc",
              "ns_ks", "ns_vs", "ns_kw", "ns_vw", "ns_g", "cv_b", "cv_c", "cv_h")
_SEG_WIDTHS = (512, 512, 512, 512, 512, 512, 512, 128, 128, 128, 128, 128, 128,
               NSA_HEADS * N_BRANCH, 512, 512, 512)
_SEG_OFF = dict(zip(_SEG_NAMES, np.cumsum((0,) + _SEG_WIDTHS[:-1]).tolist()))
_SEG_W = dict(zip(_SEG_NAMES, _SEG_WIDTHS))


_GATES_PER_KV = NSA_GROUP * N_BRANCH
_ATT_ORDER = ("sb_q", "sb_k", "sb_v", "df_q", "df_k", "df_v", "ns_q", "ns_ks", "ns_vs", "ns_kw", "ns_vw")
_ATT_OFF = dict(zip(_ATT_ORDER, np.cumsum([0] + [_SEG_W[s] for s in _ATT_ORDER[:-1]]).tolist()))
ATT_WIDTH = sum(_SEG_W[s] for s in _ATT_ORDER)
_F32_ORDER = ("cv_b", "cv_c", "cv_h", "ns_kc", "ns_vc")
_F32_OFF = dict(zip(_F32_ORDER, np.cumsum([0] + [_SEG_W[s] for s in _F32_ORDER[:-1]]).tolist()))
_GATE_OFF = sum(_SEG_W[s] for s in _F32_ORDER)

SB_TQ, SB_PAIRS = 256, 4
DIFF_TQ, DIFF_HEADS_PER_STEP = 512, 2
NSA_TQ = 256
PROJ_ROWS, PROJ_COLS = 1024, 1024
OUT_ROWS = 512
CONV_ROWS = 512


def _params(sem):
    return pltpu.CompilerParams(dimension_semantics=sem, vmem_limit_bytes=VMEM_LIMIT)


def _layer_norm(y, g, b):
    mu = jnp.mean(y, axis=-1, keepdims=True)
    d = y - mu
    var = jnp.mean(d * d, axis=-1, keepdims=True)
    return d * lax.rsqrt(var + LN_EPS) * g + b


def _dot(a, b):
    return jnp.dot(a, b, preferred_element_type=F32)


def _lane_group_max(s, mx):
    for g in range(s.shape[1] // LANES):
        mx = jnp.maximum(mx, s[:, g * LANES:(g + 1) * LANES])
    return mx


def _softmax_sweep(s_refs, *, rows, acc_lanes, q0, tq, big, merge_tail, scores_fn, weigh_fn):
    n_big = q0 // big
    mid = pl.multiple_of(n_big * big, tq)
    tail = (q0 - mid) // tq
    tail_widths = [(k + 1) * tq for k in range(big // tq)]

    def chunks(fn, init):
        return lax.fori_loop(0, n_big, lambda c, carry: fn(pl.multiple_of(c * big, big), big, carry), init)

    def scores(start, width, mxs, is_tail=False):
        out = []
        for ref, s, mx in zip(s_refs, scores_fn(start, width, is_tail), mxs):
            ref[:, pl.ds(start, width)] = s
            out.append(_lane_group_max(s, mx))
        return tuple(out)

    def tiles(fn, init):
        return lax.fori_loop(0, tail, lambda c, carry: fn(pl.multiple_of(mid + c * tq, tq), tq, carry), init)

    mxs = chunks(scores, (jnp.full((rows, LANES), NEG_BIG, F32),) * len(s_refs))
    if merge_tail:
        mxs = lax.switch(tail, [functools.partial(scores, mid, w, is_tail=True) for w in tail_widths], mxs)
    else:
        mxs = scores(q0, tq, tiles(scores, mxs), is_tail=True)
    ms = [jnp.broadcast_to(jnp.max(mx, axis=1, keepdims=True), (rows, LANES)) for mx in mxs]

    def weigh(start, width, accs):
        ps = [jnp.exp2(ref[:, pl.ds(start, width)] - jnp.concatenate([m] * (width // LANES), axis=1)).astype(BF16)
              for ref, m in zip(s_refs, ms)]
        return tuple(acc + w for acc, w in zip(accs, weigh_fn(ps, start, width)))

    accs = chunks(weigh, (jnp.zeros((rows, acc_lanes), F32),) * len(s_refs))
    if merge_tail:
        return lax.switch(tail, [functools.partial(weigh, mid, w) for w in tail_widths], accs)
    return weigh(q0, tq, tiles(weigh, accs))


def _dot_nt(a, b):
    return lax.dot_general(a, b, (((1,), (1,)), ((), ())), preferred_element_type=F32)


def _ffn_ln_kernel(x_ref, w1_ref, w3_ref, w2_ref, g_ref, b_ref, o_ref, xb_ref, *, alpha):
    j = pl.program_id(1)

    @pl.when(j == 0)
    def _():
        xb_ref[...] = x_ref[...].astype(BF16)
        o_ref[...] = jnp.zeros_like(o_ref)

    xb = xb_ref[...]
    a = _dot(xb, w1_ref[...])
    b = _dot(xb, w3_ref[...])
    hm = (a / (1.0 + jnp.exp(-a))) * b
    o_ref[...] += _dot(hm.astype(BF16), w2_ref[...])

    @pl.when(j == pl.num_programs(1) - 1)
    def _():
        y = alpha * x_ref[...] + 0.5 * o_ref[...]
        o_ref[...] = _layer_norm(y, g_ref[...], b_ref[...])


def _ffn_ln(x, w1, w3, w2, g, b, *, layer, slot, alpha, tm, tf):
    n, d = x.shape
    f = w1.shape[-1]
    return pl.pallas_call(
        functools.partial(_ffn_ln_kernel, alpha=alpha),
        grid=(n // tm, f // tf),
        in_specs=[
            pl.BlockSpec((tm, d), lambda i, j: (i, 0)),
            pl.BlockSpec((None, None, d, tf), lambda i, j: (layer, slot, 0, j)),
            pl.BlockSpec((None, None, d, tf), lambda i, j: (layer, slot, 0, j)),
            pl.BlockSpec((None, None, tf, d), lambda i, j: (layer, slot, j, 0)),
            pl.BlockSpec((1, d), lambda i, j: (0, 0)),
            pl.BlockSpec((1, d), lambda i, j: (0, 0)),
        ],
        out_specs=pl.BlockSpec((tm, d), lambda i, j: (i, 0)),
        out_shape=jax.ShapeDtypeStruct((n, d), F32),
        scratch_shapes=[pltpu.VMEM((tm, d), BF16)],
        compiler_params=_params(("parallel", "arbitrary")),
        name="ffn_ln",
    )(x, w1, w3, w2, g.reshape(1, d), b.reshape(1, d))


def _proj_kernel(x_ref, w_ref, oa_ref, of_ref, xb_ref, *, n_a):
    j = pl.program_id(1)

    @pl.when(j == 0)
    def _():
        xb_ref[...] = x_ref[...].astype(BF16)

    @pl.when(j < n_a)
    def _():
        oa_ref[...] = _dot(xb_ref[...], w_ref[...]).astype(oa_ref.dtype)

    @pl.when(j >= n_a)
    def _():
        of_ref[...] = _dot(xb_ref[...], w_ref[...])


def _proj(x, w, *, layer, width_a, tm, tn):
    n, d = x.shape
    m = w.shape[-1]
    n_a = width_a // tn
    return pl.pallas_call(
        functools.partial(_proj_kernel, n_a=n_a),
        grid=(n // tm, m // tn),
        in_specs=[pl.BlockSpec((tm, d), lambda i, j: (i, 0)),
                  pl.BlockSpec((None, d, tn), lambda i, j: (layer, 0, j))],
        out_specs=[pl.BlockSpec((tm, tn), lambda i, j: (i, jnp.minimum(j, n_a - 1))),
                   pl.BlockSpec((tm, tn), lambda i, j: (i, jnp.maximum(j - n_a, 0)))],
        out_shape=[jax.ShapeDtypeStruct((n, width_a), BF16), jax.ShapeDtypeStruct((n, m - width_a), F32)],
        scratch_shapes=[pltpu.VMEM((tm, d), BF16)],
        compiler_params=_params(("parallel", "arbitrary")),
        name="proj",
    )(x, w)


def _sb_kernel(q_ref, k_ref, v_ref, o_ref, *, tq, pairs):
    i = pl.program_id(2)
    q0 = i * tq
    lane = lax.broadcasted_iota(jnp.int32, (tq, LANES), 1)
    tri = jnp.where(lax.broadcasted_iota(jnp.int32, (tq, tq), 0)
                    >= lax.broadcasted_iota(jnp.int32, (tq, tq), 1), 1.0, 0.0).astype(BF16)
    qpos = q0 + lax.rem(lax.broadcasted_iota(jnp.int32, (2 * tq, 1), 0), tq)
    q2 = []
    for p in range(pairs):
        qs = (q_ref[0, :, p * LANES:(p + 1) * LANES].astype(F32) * (HEAD_DIM ** -0.5)).astype(BF16)
        zero = jnp.zeros_like(qs)
        q2.append(jnp.concatenate([jnp.where(lane < HEAD_DIM, qs, zero),
                                   jnp.where(lane < HEAD_DIM, zero, qs)], axis=0))

    def block(start, state, keep=None):
        width = tq
        masked = keep is not None
        out = []
        for p, (c, acc) in enumerate(state):
            ks = k_ref[0, pl.ds(start, width), p * LANES:(p + 1) * LANES]
            vs = v_ref[0, pl.ds(start, width), p * LANES:(p + 1) * LANES]
            z = _dot_nt(q2[p], ks) * LOG2_E
            nz = -z
            lg = jnp.minimum(nz, 0.0) - jnp.log2(1.0 + jnp.exp2(jnp.minimum(z, nz)))
            if masked:
                lg = jnp.where(keep, lg, 0.0)
            hi = lg.astype(BF16)
            lo = (lg - hi.astype(F32)).astype(BF16)
            suffix = _dot(hi, tri) + _dot(lo, tri)
            a = jnp.exp2(z + suffix + c)
            if masked:
                a = jnp.where(keep, a, 0.0)
            out.append((c + suffix[:, 0:1], acc + _dot(a.astype(BF16), vs)))
        return tuple(out)

    state = ((jnp.zeros((2 * tq, 1), F32), jnp.zeros((2 * tq, LANES), F32)),) * pairs
    key = lax.broadcasted_iota(jnp.int32, (1, tq), 1)
    state = block(pl.multiple_of(q0, tq), state, keep=q0 + key < qpos)
    state = block(pl.multiple_of(jnp.maximum(q0 - tq, 0), tq), state, keep=key < q0)

    def live(state):
        top = functools.reduce(jnp.maximum, [c for c, _ in state])
        return (jnp.max(top) > EXP2_UNDERFLOW).astype(jnp.int32)

    def cond(carry):
        return (carry[0] >= 0) & (carry[1] > 0)

    def body(carry):
        kt, _, state = carry
        state = block(pl.multiple_of(kt * tq, tq), state)
        return kt - 1, live(state), state

    _, _, state = lax.while_loop(cond, body, (i - 2, live(state), state))
    o_ref[0] = jnp.concatenate([jnp.where(lane < HEAD_DIM, acc[:tq], acc[tq:]) for _, acc in state],
                               axis=1).astype(o_ref.dtype)


def _sb_attention(pa, *, tq, pairs, q_blk, k_blk, v_blk):
    b, t, _ = pa.shape
    w = pairs * LANES
    n_groups = SB_HEADS // (2 * pairs)
    return pl.pallas_call(
        functools.partial(_sb_kernel, tq=tq, pairs=pairs),
        grid=(b, n_groups, t // tq),
        in_specs=[
            pl.BlockSpec((1, tq, w), lambda bi, p, i: (bi, i, q_blk + p)),
            pl.BlockSpec((1, t, w), lambda bi, p, i: (bi, 0, k_blk + p)),
            pl.BlockSpec((1, t, w), lambda bi, p, i: (bi, 0, v_blk + p)),
        ],
        out_specs=pl.BlockSpec((1, tq, w), lambda bi, p, i: (bi, i, p)),
        out_shape=jax.ShapeDtypeStruct((b, t, SB_HEADS * HEAD_DIM), BF16),
        compiler_params=_params(("parallel", "parallel", "arbitrary")),
        name="sb_attention",
    )(pa, pa, pa)


def _diff_kernel(lam_ref, gain_ref, q_ref, k_ref, v_ref, o_ref, s_ref, *, tq, heads, lam_init):
    i = pl.program_id(2)
    q0 = pl.multiple_of(i * tq, tq)
    lane = lax.broadcasted_iota(jnp.int32, (tq, LANES), 1)
    qpos = q0 + lax.rem(lax.broadcasted_iota(jnp.int32, (2 * tq, 1), 0), tq)
    q2 = []
    for h in range(heads):
        qs = (q_ref[0, :, h * LANES:(h + 1) * LANES].astype(F32) * (HEAD_DIM ** -0.5)).astype(BF16)
        zero = jnp.zeros_like(qs)
        q2.append(jnp.concatenate([jnp.where(lane < HEAD_DIM, qs, zero),
                                   jnp.where(lane < HEAD_DIM, zero, qs)], axis=0))

    def scores(start, width, tail):
        out = []
        for h in range(heads):
            s = _dot_nt(q2[h], k_ref[0, pl.ds(start, width), h * LANES:(h + 1) * LANES]) * LOG2_E
            if tail:
                s = jnp.where(start + lax.broadcasted_iota(jnp.int32, (1, width), 1) <= qpos, s, NEG_BIG)
            out.append(s)
        return out

    def weigh(ps, start, width):
        ones = jnp.ones((width, LANES), BF16)
        return [_dot(p, jnp.concatenate([v_ref[0, pl.ds(start, width), h * LANES:(h + 1) * LANES], ones], axis=1))
                for h, p in enumerate(ps)]

    s_refs = [s_ref.at[pl.ds(h * 2 * tq, 2 * tq)] for h in range(heads)]
    accs = _softmax_sweep(s_refs, rows=2 * tq, acc_lanes=2 * LANES, q0=q0, tq=tq, big=2 * tq, merge_tail=False,
                          scores_fn=scores, weigh_fn=weigh)
    lp = lam_ref[...]
    lam = (jnp.exp(jnp.sum(lp[0:1] * lp[1:2], axis=1, keepdims=True))
           - jnp.exp(jnp.sum(lp[2:3] * lp[3:4], axis=1, keepdims=True)) + lam_init)
    out = []
    for acc in accs:
        o = acc[:, :LANES] / acc[:, LANES:]
        od = o[:tq] - lam * o[tq:]
        od = od * lax.rsqrt(jnp.mean(od * od, axis=-1, keepdims=True) + RMS_EPS)
        out.append(od * gain_ref[...] * (1.0 - lam_init))
    o_ref[0] = jnp.concatenate(out, axis=1).astype(o_ref.dtype)


def _diff_attention(pa, lam_params, gain, *, tq, heads, q_blk, k_blk, v_blk, lam_init):
    b, t, _ = pa.shape
    w = heads * LANES
    return pl.pallas_call(
        functools.partial(_diff_kernel, tq=tq, heads=heads, lam_init=lam_init),
        grid=(b, DIFF_HEADS // heads, t // tq),
        in_specs=[
            pl.BlockSpec((4, HEAD_DIM), lambda bi, h, i: (0, 0)),
            pl.BlockSpec((1, LANES), lambda bi, h, i: (0, 0)),
            pl.BlockSpec((1, tq, w), lambda bi, h, i: (bi, i, q_blk + h)),
            pl.BlockSpec((1, t, w), lambda bi, h, i: (bi, 0, k_blk + h)),
            pl.BlockSpec((1, t, w), lambda bi, h, i: (bi, 0, v_blk + h)),
        ],
        out_specs=pl.BlockSpec((1, tq, w), lambda bi, h, i: (bi, i, h)),
        out_shape=jax.ShapeDtypeStruct((b, t, DIFF_HEADS * LANES), BF16),
        scratch_shapes=[pltpu.VMEM((heads * 2 * tq, t), F32)],
        compiler_params=_params(("parallel", "parallel", "arbitrary")),
        name="diff_attention",
    )(lam_params, gain.reshape(1, LANES), pa, pa, pa)


def _compress_kernel(x_ref, pos_ref, w1_ref, w2_ref, o_ref):
    x = x_ref[0, 0]
    n16, half = x.shape
    xa = (x + pos_ref[0:1]).astype(BF16)
    xb = (x + pos_ref[1:2]).astype(BF16)
    w1 = w1_ref[0]
    first = _dot(xa, w1[:half])
    second = _dot(xb, w1[half:])
    h = first + pltpu.roll(second, n16 - 1, 0)
    g = 0.5 * h * (1.0 + jnp.tanh(math.sqrt(2.0 / math.pi) * (h + 0.044715 * (h * h * h))))
    o_ref[0, 0] = _dot(g.astype(BF16), w2_ref[0])


def _compress(x16, pos2, w1, w2, *, layer):
    b, n_streams, n16, half = x16.shape
    d = w2.shape[-1]
    return pl.pallas_call(
        _compress_kernel,
        grid=(b, n_streams),
        in_specs=[
            pl.BlockSpec((1, 1, n16, half), lambda bi, s: (bi, s, 0, 0)),
            pl.BlockSpec((None, 2, half), lambda bi, s: (layer, 0, 0)),
            pl.BlockSpec((None, 1, 2 * half, w1.shape[-1]), lambda bi, s: (layer, s // NSA_KV_HEADS, 0, 0)),
            pl.BlockSpec((None, 1, w2.shape[2], d), lambda bi, s: (layer, s // NSA_KV_HEADS, 0, 0)),
        ],
        out_specs=pl.BlockSpec((1, 1, n16, d), lambda bi, s: (bi, s, 0, 0)),
        out_shape=jax.ShapeDtypeStruct((b, n_streams, n16, d), F32),
        compiler_params=_params(("parallel", "arbitrary")),
        name="nsa_compress",
    )(x16, pos2, w1, w2)


def _nsa_kernel(q_ref, g_ref, ck_ref, cv_ref, ovt_ref, gx_ref, ks_ref, vs_ref, kw_ref, vw_ref, o_ref,
                kse_ref, kso_ref, vse_ref, vso_ref, kwd_ref, vwe_ref, vwo_ref, ckd_ref, cvd_ref, s_ref,
                *, tq, top_n):
    i = pl.program_id(2)
    grp = NSA_GROUP
    d = HEAD_DIM
    q0 = i * tq
    n_cmp = ck_ref.shape[2]
    n_slc = ovt_ref.shape[0]
    t = ks_ref.shape[1]
    rows = grp * tq
    half = rows // 2

    @pl.when(i == 0)
    def _():
        onehot = jnp.where(lax.broadcasted_iota(jnp.int32, (t, d), 0) // SLC_BLOCK
                           == lax.broadcasted_iota(jnp.int32, (t, d), 1), 1.0, 0.0)
        ones = jnp.ones((t, d), F32)

        def put(ref, left, right):
            ref[...] = jnp.concatenate([left, right], axis=1).astype(BF16)

        first_kv = pl.program_id(1) == 0

        def own_head(ref):
            both = ref[0].astype(F32)
            return jnp.where(first_kv, both[:, :d], both[:, d:])

        ks, vs, kw, vw = own_head(ks_ref), own_head(vs_ref), own_head(kw_ref), own_head(vw_ref)
        put(kse_ref, ks, onehot)
        put(kso_ref, onehot, ks)
        put(vse_ref, vs, ones)
        put(vso_ref, ones, vs)
        put(kwd_ref, kw, kw)
        put(vwe_ref, vw, ones)
        put(vwo_ref, ones, vw)
        put(ckd_ref, ck_ref[0, 0], ck_ref[0, 0])
        put(cvd_ref, cv_ref[0, 0], cv_ref[0, 0])

    lane = lax.broadcasted_iota(jnp.int32, (tq, LANES), 1)
    left = lane < d
    qs = (q_ref[0].astype(F32) * (d ** -0.5)).astype(BF16)
    pairs = [qs[:, :LANES], qs[:, LANES:]]
    zero = jnp.zeros((tq, LANES), BF16)
    qz = jnp.concatenate([jnp.where(left, pairs[0], zero), jnp.where(left, pairs[1], zero),
                          jnp.where(left, zero, pairs[0]), jnp.where(left, zero, pairs[1])], axis=0)
    t4 = q0 + lax.rem(lax.broadcasted_iota(jnp.int32, (rows, 1), 0), tq)

    def weigh(p, start, width, ve_ref, vo_ref):
        return jnp.concatenate([_dot(p[:half], ve_ref[pl.ds(start, width), :]),
                                _dot(p[half:], vo_ref[pl.ds(start, width), :])], axis=0)

    def normalise(acc):
        out = []
        for pr in range(2):
            even, odd = acc[pr * tq:(pr + 1) * tq], acc[half + pr * tq:half + (pr + 1) * tq]
            den = pltpu.roll(jnp.where(left, odd, even), d, 1)
            out.append(jnp.where(left, even, odd) / den)
        return out

    w_width = (-(-WINDOW // tq) + 1) * tq
    w_start = pl.multiple_of(jnp.maximum(q0 + tq - w_width, 0), tq)
    back = (q0 + lax.broadcasted_iota(jnp.int32, (tq, 1), 0)
            - (w_start + lax.broadcasted_iota(jnp.int32, (1, w_width), 1)))
    off = jnp.where((back >= 0) & (back < WINDOW), 0.0, NEG_BIG)
    sw = _dot_nt(qz, kwd_ref[pl.ds(w_start, w_width), :]) + jnp.concatenate([off] * grp, axis=0)
    pw = jnp.exp(sw - jnp.max(sw, axis=1, keepdims=True))
    o_win = normalise(weigh(pw.astype(BF16), w_start, w_width, vwe_ref, vwo_ref))

    gate = 1.0 / (1.0 + jnp.exp(-g_ref[0]))
    g_hi = gate.astype(BF16)
    g_r = gate - g_hi.astype(F32)
    g_mid = g_r.astype(BF16)
    g_lo = (g_r - g_mid.astype(F32)).astype(BF16)
    spread = gx_ref[...]
    gx = _dot(g_hi, spread) + _dot(g_mid, spread) + _dot(g_lo, spread)

    s = _dot_nt(qz, ckd_ref[...])
    cmp_end = CMP_STRIDE * lax.broadcasted_iota(jnp.int32, (1, n_cmp), 1) + (CMP_BLOCK - 1)
    cmask = cmp_end <= t4
    s = jnp.where(cmask, s, NEG_BIG)
    e = jnp.exp(s - jnp.max(s, axis=1, keepdims=True))
    p = e / jnp.sum(e, axis=1, keepdims=True)
    p = jnp.where(cmask, p, 0.0)
    o_cmp = _dot(p.astype(BF16), cvd_ref[...])

    pg = p[0:tq]
    for h in range(1, grp):
        pg = pg + p[h * tq:(h + 1) * tq]
    p_hi = pg.astype(BF16)
    r1 = pg - p_hi.astype(F32)
    p_mid = r1.astype(BF16)
    p_lo = (r1 - p_mid.astype(F32)).astype(BF16)
    ovt = ovt_ref[...]
    imp_t = _dot_nt(ovt, p_hi) + _dot_nt(ovt, p_mid) + _dot_nt(ovt, p_lo)
    blk = lax.broadcasted_iota(jnp.int32, (n_slc, tq), 0)
    cur = (q0 + lax.broadcasted_iota(jnp.int32, (n_slc, tq), 1)) // SLC_BLOCK
    forced = (blk == 0) | (blk == cur) | (blk == cur - 1)
    score = jnp.where(forced, jnp.inf, jnp.where(blk <= cur, imp_t, -jnp.inf))
    sub = lax.broadcasted_iota(jnp.int32, (8, tq), 0)
    ranks = []
    for g0 in range(0, n_slc, 8):
        sg = score[g0:g0 + 8]
        rg = jnp.zeros((8, tq), F32)
        for j in range(n_slc):
            sj = score[j:j + 1, :]
            if j < g0:
                beats = sj >= sg
            elif j >= g0 + 8:
                beats = sj > sg
            else:
                beats = (sj > sg) | ((sj == sg) & (sub > j - g0))
            rg = rg + jnp.where(beats, 1.0, 0.0)
        ranks.append(rg)
    rank = jnp.concatenate(ranks, axis=0)
    sel_t = jnp.where(rank < top_n, 1.0, 0.0)
    if n_slc < d:
        sel_t = jnp.concatenate([sel_t, jnp.zeros((d - n_slc, tq), F32)], axis=0)
    sel_t2 = jnp.concatenate([sel_t, sel_t], axis=0).astype(BF16)
    eye = jnp.where(lax.broadcasted_iota(jnp.int32, (tq, tq), 0)
                    == lax.broadcasted_iota(jnp.int32, (tq, tq), 1), 1.0, 0.0).astype(BF16)
    sel2 = _dot_nt(eye, sel_t2)
    bias = ((sel2 - 1.0) * (-NEG_BIG)).astype(BF16)
    q_even = jnp.concatenate([jnp.where(left, pairs[0], bias), jnp.where(left, pairs[1], bias)], axis=0)
    q_odd = jnp.concatenate([jnp.where(left, bias, pairs[0]), jnp.where(left, bias, pairs[1])], axis=0)

    def slc_scores(start, width, tail):
        s = jnp.concatenate([_dot_nt(q_even, kse_ref[pl.ds(start, width), :]),
                             _dot_nt(q_odd, kso_ref[pl.ds(start, width), :])], axis=0) * LOG2_E
        if tail:
            s = jnp.where(start + lax.broadcasted_iota(jnp.int32, (1, width), 1) <= t4, s, NEG_BIG)
        return s

    (acc,) = _softmax_sweep((s_ref,), rows=rows, acc_lanes=LANES, q0=pl.multiple_of(q0, tq), tq=tq,
                            big=4 * tq, merge_tail=True, scores_fn=lambda *a: (slc_scores(*a),),
                            weigh_fn=lambda ps, start, width: (weigh(ps[0], start, width, vse_ref, vso_ref),))
    o_slc = normalise(acc)

    out = []
    for pr in range(2):
        cmp_pr = jnp.where(left, o_cmp[pr * tq:(pr + 1) * tq], o_cmp[half + pr * tq:half + (pr + 1) * tq])
        mixed = 0.0
        for br, o_br in enumerate((cmp_pr, o_slc[pr], o_win[pr])):
            blk_idx = pr * N_BRANCH + br
            mixed = mixed + gx[:, blk_idx * LANES:(blk_idx + 1) * LANES] * o_br
        out.append(mixed)
    o_ref[0] = jnp.concatenate(out, axis=1).astype(o_ref.dtype)


def _nsa_attention(pa, pf, ckv, ovt, *, tq, q_blk, kv_blk, g_blk, top_n):
    b, t, _ = pa.shape
    n16 = ckv.shape[2]
    n_slc = ovt.shape[0]
    qw = NSA_GROUP * HEAD_DIM
    kv_specs = [pl.BlockSpec((1, t, LANES), functools.partial(lambda bi, j, i, c: (bi, 0, c), c=kv_blk + n))
                for n in range(4)]
    return pl.pallas_call(
        functools.partial(_nsa_kernel, tq=tq, top_n=top_n),
        grid=(b, NSA_KV_HEADS, t // tq),
        in_specs=[
            pl.BlockSpec((1, tq, qw), lambda bi, j, i: (bi, i, q_blk + j)),
            pl.BlockSpec((1, tq, LANES), lambda bi, j, i: (bi, i, g_blk + j)),
            pl.BlockSpec((1, 1, n16, HEAD_DIM), lambda bi, j, i: (bi, j, 0, 0)),
            pl.BlockSpec((1, 1, n16, HEAD_DIM), lambda bi, j, i: (bi, NSA_KV_HEADS + j, 0, 0)),
            pl.BlockSpec((n_slc, n16), lambda bi, j, i: (0, 0)),
            pl.BlockSpec((LANES, 2 * N_BRANCH * LANES), lambda bi, j, i: (0, 0)),
            *kv_specs,
        ],
        out_specs=pl.BlockSpec((1, tq, qw), lambda bi, j, i: (bi, i, j)),
        out_shape=jax.ShapeDtypeStruct((b, t, NSA_HEADS * HEAD_DIM), BF16),
        scratch_shapes=[pltpu.VMEM((t, LANES), BF16)] * 7 + [pltpu.VMEM((n16, LANES), BF16)] * 2
                       + [pltpu.VMEM((NSA_GROUP * tq, t), F32)],
        compiler_params=_params(("arbitrary", "arbitrary", "arbitrary")),
        name="nsa_attention",
    )(pa, pf, ckv, ckv, ovt, _gate_spread(), pa, pa, pa, pa)


def _conv_kernel(cb_ref, cc_ref, ch_ref, w_ref, o_ref, prev_ref):
    @pl.when(pl.program_id(1) == 0)
    def _():
        prev_ref[...] = jnp.zeros_like(prev_ref)

    u = cc_ref[0] * ch_ref[0]
    tt = u.shape[0]
    row = lax.broadcasted_iota(jnp.int32, u.shape, 0)
    last1 = prev_ref[7:8]
    last2 = prev_ref[6:7]
    u1 = jnp.where(row >= 1, pltpu.roll(u, 1, 0), last1)
    u2 = jnp.where(row >= 2, pltpu.roll(u, 2, 0), jnp.where(row == 1, last1, last2))
    w = w_ref[...]
    o_ref[0] = (cb_ref[0] * (w[0:1] * u2 + w[1:2] * u1 + w[2:3] * u)).astype(o_ref.dtype)
    prev_ref[...] = u[tt - 8:tt]


def _conv_mixer(pf, conv_w, *, tt):
    b, t, _ = pf.shape
    c = CONV_CH
    return pl.pallas_call(
        _conv_kernel,
        grid=(b, t // tt),
        in_specs=[
            pl.BlockSpec((1, tt, c), lambda bi, i: (bi, i, 0)),
            pl.BlockSpec((1, tt, c), lambda bi, i: (bi, i, 1)),
            pl.BlockSpec((1, tt, c), lambda bi, i: (bi, i, 2)),
            pl.BlockSpec((CONV_WIDTH, c), lambda bi, i: (0, 0)),
        ],
        out_specs=pl.BlockSpec((1, tt, c), lambda bi, i: (bi, i, 0)),
        out_shape=jax.ShapeDtypeStruct((b, t, c), BF16),
        scratch_shapes=[pltpu.VMEM((8, c), F32)],
        compiler_params=_params(("parallel", "arbitrary")),
        name="conv_mixer",
    )(pf, pf, pf, conv_w)


def _out_ln_kernel(a_ref, b_ref, c_ref, d_ref, w_ref, x_ref, g_ref, beta_ref, o_ref, *, alpha):
    y = jnp.concatenate([a_ref[...], b_ref[...], c_ref[...], d_ref[...]], axis=1)
    o_ref[...] = _layer_norm(alpha * x_ref[...] + _dot(y, w_ref[...]), g_ref[...], beta_ref[...])


def _out_ln(parts, w, x, g, b, *, layer, alpha, tm):
    n, d = x.shape
    kw = parts[0].shape[1]
    part_spec = pl.BlockSpec((tm, kw), lambda i: (i, 0))
    return pl.pallas_call(
        functools.partial(_out_ln_kernel, alpha=alpha),
        grid=(n // tm,),
        in_specs=[part_spec, part_spec, part_spec, part_spec,
                  pl.BlockSpec((None,) + w.shape[1:], lambda i: (layer, 0, 0)),
                  pl.BlockSpec((tm, d), lambda i: (i, 0)),
                  pl.BlockSpec((1, d), lambda i: (0, 0)),
                  pl.BlockSpec((1, d), lambda i: (0, 0))],
        out_specs=pl.BlockSpec((tm, d), lambda i: (i, 0)),
        out_shape=jax.ShapeDtypeStruct((n, d), F32),
        compiler_params=_params(("parallel",)),
        name="out_ln",
    )(*parts, w, x, g.reshape(1, d), b.reshape(1, d))


def _gate_spread():
    m = np.zeros((LANES, 2 * N_BRANCH * LANES), np.float32)
    for pr in range(2):
        for br in range(N_BRANCH):
            for lane in range(LANES):
                head = 2 * pr + lane // HEAD_DIM
                m[N_BRANCH * head + br, (pr * N_BRANCH + br) * LANES + lane] = 1.0
    return jnp.asarray(m, BF16)


def _overlap_t(t):
    n16 = t // CMP_STRIDE
    n_slc = t // SLC_BLOCK
    c_start = CMP_STRIDE * np.arange(n16)
    j_start = SLC_BLOCK * np.arange(n_slc)
    ov = ((c_start[None, :] < j_start[:, None] + SLC_BLOCK)
          & (c_start[None, :] + CMP_BLOCK > j_start[:, None])).astype(np.float32)
    ov[:, n16 - 1] = 0.0
    return jnp.asarray(ov, BF16)


def _mixer(hf, batch, w_proj, w_out, ln_g, ln_b, diff_lam, diff_gain, pos2, cmp_w1, cmp_w2,
           conv_w, layer, alpha):
    n, _ = hf.shape
    t = n // batch
    n_slc = t // SLC_BLOCK
    assert t % (4 * NSA_TQ) == 0 and t % (2 * DIFF_TQ) == 0 and t % SB_TQ == 0 and n_slc <= HEAD_DIM
    assert t >= WINDOW + NSA_TQ and n % PROJ_ROWS == 0 and n % FFN_ROWS == 0 and n % OUT_ROWS == 0
    pa, pf = _proj(hf, w_proj, layer=layer, width_a=ATT_WIDTH, tm=PROJ_ROWS, tn=PROJ_COLS)
    pa, pf = pa.reshape(batch, t, -1), pf.reshape(batch, t, -1)

    def att_blk(name, lanes):
        return _ATT_OFF[name] // lanes

    w = SB_PAIRS * LANES
    o_sb = _sb_attention(pa, tq=SB_TQ, pairs=SB_PAIRS, q_blk=att_blk("sb_q", w), k_blk=att_blk("sb_k", w),
                         v_blk=att_blk("sb_v", w))
    lam_init = 0.8 - 0.6 * math.exp(-0.3 * layer)
    w = DIFF_HEADS_PER_STEP * LANES
    o_df = _diff_attention(pa, diff_lam[layer], diff_gain[layer], tq=DIFF_TQ, heads=DIFF_HEADS_PER_STEP,
                           q_blk=att_blk("df_q", w), k_blk=att_blk("df_k", w), v_blk=att_blk("df_v", w),
                           lam_init=lam_init)

    n16 = t // CMP_STRIDE
    kvc = pf[:, :, _F32_OFF["ns_kc"]:_F32_OFF["ns_kc"] + 2 * LANES].reshape(batch, t, 2, NSA_KV_HEADS, HEAD_DIM)
    x16 = kvc.transpose(0, 2, 3, 1, 4).reshape(batch, 2 * NSA_KV_HEADS, n16, CMP_STRIDE * HEAD_DIM)
    ckv = _compress(x16, pos2, cmp_w1, cmp_w2, layer=layer)
    o_ns = _nsa_attention(pa, pf, ckv, _overlap_t(t), tq=NSA_TQ, q_blk=att_blk("ns_q", NSA_GROUP * HEAD_DIM),
                          kv_blk=att_blk("ns_ks", LANES), g_blk=_GATE_OFF // LANES, top_n=min(SLC_TOPN, n_slc))

    o_cv = _conv_mixer(pf, conv_w[layer], tt=CONV_ROWS)
    parts = [o.reshape(n, -1) for o in (o_sb, o_df, o_ns, o_cv)]
    return _out_ln(parts, w_out, hf, ln_g, ln_b, layer=layer, alpha=alpha, tm=OUT_ROWS)


def _seg(w_in, name, lo=0, hi=None):
    off = _SEG_OFF[name]
    hi = _SEG_W[name] if hi is None else hi
    return w_in[:, :, off + lo:off + hi]


def kernel(x, ln_g, ln_b, ffn_w1, ffn_w3, ffn_w2, w_in, w_out, diff_lam, diff_gain, cmp_pos, cmp_wk1,
           cmp_wk2, cmp_wv1, cmp_wv2, conv_w):
    batch, t, d = x.shape
    depth = ln_g.shape[0]
    alpha = (2 * depth) ** 0.25
    n = batch * t

    w1b, w3b, w2b = ffn_w1.astype(BF16), ffn_w3.astype(BF16), ffn_w2.astype(BF16)
    gate_pad = jnp.zeros(w_in.shape[:2] + (LANES - _GATES_PER_KV,), w_in.dtype)
    cols = [_seg(w_in, name) for name in _ATT_ORDER + _F32_ORDER]
    for j in range(NSA_KV_HEADS):
        cols += [_seg(w_in, "ns_g", j * _GATES_PER_KV, (j + 1) * _GATES_PER_KV), gate_pad]
    w_proj = jnp.concatenate(cols, axis=-1).astype(BF16)
    w_outb = w_out.astype(BF16)
    half = CMP_STRIDE * HEAD_DIM
    pos2 = cmp_pos.reshape(depth, 2, half)
    cmp_w1 = jnp.stack([cmp_wk1, cmp_wv1], axis=1).astype(BF16)
    cmp_w2 = jnp.stack([cmp_wk2, cmp_wv2], axis=1).astype(BF16)

    hf = x.reshape(n, d)
    for l in range(depth):
        hf = _ffn_ln(hf, w1b, w3b, w2b, ln_g[l, 0], ln_b[l, 0], layer=l, slot=0, alpha=alpha, tm=FFN_ROWS,
                     tf=FFN_TILE)
        hf = _mixer(hf, batch, w_proj, w_outb, ln_g[l, 1], ln_b[l, 1], diff_lam, diff_gain,
                    pos2, cmp_w1, cmp_w2, conv_w, l, alpha)
        hf = _ffn_ln(hf, w1b, w3b, w2b, ln_g[l, 2], ln_b[l, 2], layer=l, slot=1, alpha=alpha, tm=FFN_ROWS,
                     tf=FFN_TILE)
    return hf.reshape(batch, t, d)
```

```python
import functools
import math

import numpy as np
import jax
import jax.numpy as jnp
from jax import lax
from jax.experimental import pallas as pl
from jax.experimental.pallas import tpu as pltpu

F32 = jnp.float32
BF16 = jnp.bfloat16

HEAD_DIM = 64
SB_HEADS = 8
DIFF_HEADS = 4
NSA_HEADS = 8
NSA_KV_HEADS = 2
NSA_GROUP = NSA_HEADS // NSA_KV_HEADS
CMP_BLOCK = 32
CMP_STRIDE = 16
SLC_BLOCK = 64
SLC_TOPN = 16
WINDOW = 512
N_BRANCH = 3
CONV_CH = 512
CONV_WIDTH = 3
LN_EPS = 1e-5
RMS_EPS = 1e-5
NEG_BIG = -1e30
LOG2_E = math.log2(math.e)
EXP2_UNDERFLOW = -150.0
LANES = 128
VMEM_LIMIT = 60 * 1024 * 1024
FFN_TILE = 512
FFN_ACC_COLS = 512
FFN_LN_ROWS = 128
FFN_ROWS = 1024

_SEG_NAMES = ("sb_q", "sb_k", "sb_v", "df_q", "df_k", "df_v", "ns_q", "ns_kc", "ns_vc",
              "ns_ks", "ns_vs", "ns_kw", "ns_vw", "ns_g", "cv_b", "cv_c", "cv_h")
_SEG_WIDTHS = (512, 512, 512, 512, 512, 512, 512, 128, 128, 128, 128, 128, 128,
               NSA_HEADS * N_BRANCH, 512, 512, 512)
_SEG_OFF = dict(zip(_SEG_NAMES, np.cumsum((0,) + _SEG_WIDTHS[:-1]).tolist()))
_SEG_W = dict(zip(_SEG_NAMES, _SEG_WIDTHS))


_GATES_PER_KV = NSA_GROUP * N_BRANCH
_ATT_ORDER = ("sb_q", "sb_k", "sb_v", "df_q", "df_k", "df_v", "ns_q", "ns_ks", "ns_vs", "ns_kw", "ns_vw")
_ATT_OFF = dict(zip(_ATT_ORDER, np.cumsum([0] + [_SEG_W[s] for s in _ATT_ORDER[:-1]]).tolist()))
ATT_WIDTH = sum(_SEG_W[s] for s in _ATT_ORDER)
_F32_ORDER = ("cv_b", "cv_c", "cv_h", "ns_kc", "ns_vc")
_F32_OFF = dict(zip(_F32_ORDER, np.cumsum([0] + [_SEG_W[s] for s in _F32_ORDER[:-1]]).tolist()))
_GATE_OFF = sum(_SEG_W[s] for s in _F32_ORDER)

SB_TQ, SB_PAIRS = 256, 4
DIFF_TQ, DIFF_HEADS_PER_STEP = 512, 2
DIFF_CHUNK = 1024
NSA_TQ = 256
PROJ_ROWS, PROJ_COLS = 1024, 1024
OUT_ROWS = 512


def _params(sem):
    return pltpu.CompilerParams(dimension_semantics=sem, vmem_limit_bytes=VMEM_LIMIT)


def _layer_norm(y, g, b):
    mu = jnp.mean(y, axis=-1, keepdims=True)
    d = y - mu
    var = jnp.mean(d * d, axis=-1, keepdims=True)
    return d * lax.rsqrt(var + LN_EPS) * g + b


def _dot(a, b):
    return jnp.dot(a, b, preferred_element_type=F32)


def _lane_group_max(s, mx):
    for g in range(s.shape[1] // LANES):
        mx = jnp.maximum(mx, s[:, g * LANES:(g + 1) * LANES])
    return mx


def _softmax_sweep(s_refs, *, rows, acc_lanes, q0, tq, big, merge_tail, scores_fn, weigh_fn):
    n_big = q0 // big
    mid = pl.multiple_of(n_big * big, tq)
    tail = (q0 - mid) // tq
    tail_widths = [(k + 1) * tq for k in range(big // tq)]

    def chunks(fn, init):
        return lax.fori_loop(0, n_big, lambda c, carry: fn(pl.multiple_of(c * big, big), big, carry), init)

    def scores(start, width, mxs, is_tail=False):
        out = []
        for ref, s, mx in zip(s_refs, scores_fn(start, width, is_tail), mxs):
            ref[:, pl.ds(start, width)] = s
            out.append(_lane_group_max(s, mx))
        return tuple(out)

    def tiles(fn, init):
        return lax.fori_loop(0, tail, lambda c, carry: fn(pl.multiple_of(mid + c * tq, tq), tq, carry), init)

    mxs = chunks(scores, (jnp.full((rows, LANES), NEG_BIG, F32),) * len(s_refs))
    if merge_tail:
        mxs = lax.switch(tail, [functools.partial(scores, mid, w, is_tail=True) for w in tail_widths], mxs)
    else:
        mxs = scores(q0, tq, tiles(scores, mxs), is_tail=True)
    ms = [jnp.broadcast_to(jnp.max(mx, axis=1, keepdims=True), (rows, LANES)) for mx in mxs]

    def weigh(start, width, accs):
        ps = [jnp.exp2(ref[:, pl.ds(start, width)] - jnp.concatenate([m] * (width // LANES), axis=1)).astype(BF16)
              for ref, m in zip(s_refs, ms)]
        return tuple(acc + w for acc, w in zip(accs, weigh_fn(ps, start, width)))

    accs = chunks(weigh, (jnp.zeros((rows, acc_lanes), F32),) * len(s_refs))
    if merge_tail:
        return lax.switch(tail, [functools.partial(weigh, mid, w) for w in tail_widths], accs)
    return weigh(q0, tq, tiles(weigh, accs))


def _dot_nt(a, b):
    return lax.dot_general(a, b, (((1,), (1,)), ((), ())), preferred_element_type=F32)


def _ffn_ln_kernel(x_ref, w1_ref, w3_ref, w2_ref, g_ref, b_ref, o_ref, xb_ref, *, alpha):
    j = pl.program_id(1)

    @pl.when(j == 0)
    def _():
        xb_ref[...] = x_ref[...].astype(BF16)
        o_ref[...] = jnp.zeros_like(o_ref)

    xb = xb_ref[...]
    a = _dot(xb, w1_ref[...])
    b = _dot(xb, w3_ref[...])
    hm = (a / (1.0 + jnp.exp(-a))) * b
    hb = hm.astype(BF16)
    for c in range(0, o_ref.shape[1], FFN_ACC_COLS):
        o_ref[:, c:c + FFN_ACC_COLS] += _dot(hb, w2_ref[:, c:c + FFN_ACC_COLS])

    @pl.when(j == pl.num_programs(1) - 1)
    def _():
        for r in range(0, o_ref.shape[0], FFN_LN_ROWS):
            rows = slice(r, r + FFN_LN_ROWS)
            y = alpha * x_ref[rows] + 0.5 * o_ref[rows]
            o_ref[rows] = _layer_norm(y, g_ref[...], b_ref[...])


def _ffn_ln(x, w1, w3, w2, g, b, *, layer, slot, alpha, tm, tf):
    n, d = x.shape
    f = w1.shape[-1]
    return pl.pallas_call(
        functools.partial(_ffn_ln_kernel, alpha=alpha),
        grid=(n // tm, f // tf),
        in_specs=[
            pl.BlockSpec((tm, d), lambda i, j: (i, 0)),
            pl.BlockSpec((None, None, d, tf), lambda i, j: (layer, slot, 0, j)),
            pl.BlockSpec((None, None, d, tf), lambda i, j: (layer, slot, 0, j)),
            pl.BlockSpec((None, None, tf, d), lambda i, j: (layer, slot, j, 0)),
            pl.BlockSpec((1, d), lambda i, j: (0, 0)),
            pl.BlockSpec((1, d), lambda i, j: (0, 0)),
        ],
        out_specs=pl.BlockSpec((tm, d), lambda i, j: (i, 0)),
        out_shape=jax.ShapeDtypeStruct((n, d), F32),
        scratch_shapes=[pltpu.VMEM((tm, d), BF16)],
        compiler_params=_params(("parallel", "arbitrary")),
        name="ffn_ln",
    )(x, w1, w3, w2, g.reshape(1, d), b.reshape(1, d))


def _proj_kernel(x_ref, w_ref, oa_ref, of_ref, xb_ref, *, n_a):
    j = pl.program_id(1)

    @pl.when(j == 0)
    def _():
        xb_ref[...] = x_ref[...].astype(BF16)

    @pl.when(j < n_a)
    def _():
        oa_ref[...] = _dot(xb_ref[...], w_ref[...]).astype(oa_ref.dtype)

    @pl.when(j >= n_a)
    def _():
        of_ref[...] = _dot(xb_ref[...], w_ref[...])


def _proj(x, w, *, layer, width_a, tm, tn):
    n, d = x.shape
    m = w.shape[-1]
    n_a = width_a // tn
    return pl.pallas_call(
        functools.partial(_proj_kernel, n_a=n_a),
        grid=(n // tm, m // tn),
        in_specs=[pl.BlockSpec((tm, d), lambda i, j: (i, 0)),
                  pl.BlockSpec((None, d, tn), lambda i, j: (layer, 0, j))],
        out_specs=[pl.BlockSpec((tm, tn), lambda i, j: (i, jnp.minimum(j, n_a - 1))),
                   pl.BlockSpec((tm, tn), lambda i, j: (i, jnp.maximum(j - n_a, 0)))],
        out_shape=[jax.ShapeDtypeStruct((n, width_a), BF16), jax.ShapeDtypeStruct((n, m - width_a), F32)],
        scratch_shapes=[pltpu.VMEM((tm, d), BF16)],
        compiler_params=_params(("parallel", "arbitrary")),
        name="proj",
    )(x, w)


def _sb_kernel(q_ref, k_ref, v_ref, o_ref, *, tq, pairs):
    i = pl.program_id(2)
    q0 = i * tq
    lane = lax.broadcasted_iota(jnp.int32, (tq, LANES), 1)
    tri = jnp.where(lax.broadcasted_iota(jnp.int32, (tq, tq), 0)
                    >= lax.broadcasted_iota(jnp.int32, (tq, tq), 1), 1.0, 0.0).astype(BF16)
    qpos = q0 + lax.rem(lax.broadcasted_iota(jnp.int32, (2 * tq, 1), 0), tq)
    q2 = []
    for p in range(pairs):
        qs = (q_ref[0, :, p * LANES:(p + 1) * LANES].astype(F32) * (HEAD_DIM ** -0.5)).astype(BF16)
        zero = jnp.zeros_like(qs)
        q2.append(jnp.concatenate([jnp.where(lane < HEAD_DIM, qs, zero),
                                   jnp.where(lane < HEAD_DIM, zero, qs)], axis=0))

    def block(start, state, keep=None):
        width = tq
        masked = keep is not None
        out = []
        for p, (c, acc) in enumerate(state):
            ks = k_ref[0, pl.ds(start, width), p * LANES:(p + 1) * LANES]
            vs = v_ref[0, pl.ds(start, width), p * LANES:(p + 1) * LANES]
            z = _dot_nt(q2[p], ks) * LOG2_E
            nz = -z
            lg = jnp.minimum(nz, 0.0) - jnp.log2(1.0 + jnp.exp2(jnp.minimum(z, nz)))
            if masked:
                lg = jnp.where(keep, lg, 0.0)
            hi = lg.astype(BF16)
            lo = (lg - hi.astype(F32)).astype(BF16)
            suffix = _dot(hi, tri) + _dot(lo, tri)
            a = jnp.exp2(z + suffix + c)
            if masked:
                a = jnp.where(keep, a, 0.0)
            out.append((c + suffix[:, 0:1], acc + _dot(a.astype(BF16), vs)))
        return tuple(out)

    state = ((jnp.zeros((2 * tq, 1), F32), jnp.zeros((2 * tq, LANES), F32)),) * pairs
    key = lax.broadcasted_iota(jnp.int32, (1, tq), 1)
    state = block(pl.multiple_of(q0, tq), state, keep=q0 + key < qpos)
    no_prev = jnp.where(i > 0, 0.0, NEG_BIG)
    state = block(pl.multiple_of(jnp.maximum(q0 - tq, 0), tq), tuple((c + no_prev, acc) for c, acc in state))

    def live(state):
        top = functools.reduce(jnp.maximum, [c for c, _ in state])
        return (jnp.max(top) > EXP2_UNDERFLOW).astype(jnp.int32)

    def cond(carry):
        return (carry[0] >= 0) & (carry[1] > 0)

    def body(carry):
        kt, _, state = carry
        state = block(pl.multiple_of(kt * tq, tq), state)
        return kt - 1, live(state), state

    _, _, state = lax.while_loop(cond, body, (i - 2, live(state), state))
    o_ref[0] = jnp.concatenate([jnp.where(lane < HEAD_DIM, acc[:tq], acc[tq:]) for _, acc in state],
                               axis=1).astype(o_ref.dtype)


def _sb_attention(pa, *, tq, pairs, q_blk, k_blk, v_blk):
    b, t, _ = pa.shape
    w = pairs * LANES
    n_groups = SB_HEADS // (2 * pairs)
    return pl.pallas_call(
        functools.partial(_sb_kernel, tq=tq, pairs=pairs),
        grid=(b, n_groups, t // tq),
        in_specs=[
            pl.BlockSpec((1, tq, w), lambda bi, p, i: (bi, i, q_blk + p)),
            pl.BlockSpec((1, t, w), lambda bi, p, i: (bi, 0, k_blk + p)),
            pl.BlockSpec((1, t, w), lambda bi, p, i: (bi, 0, v_blk + p)),
        ],
        out_specs=pl.BlockSpec((1, tq, w), lambda bi, p, i: (bi, i, p)),
        out_shape=jax.ShapeDtypeStruct((b, t, SB_HEADS * HEAD_DIM), BF16),
        compiler_params=_params(("parallel", "parallel", "arbitrary")),
        name="sb_attention",
    )(pa, pa, pa)


def _diff_kernel(lam_ref, gain_ref, q_ref, k_ref, v_ref, o_ref, s_ref, *, tq, heads, lam_init):
    i = pl.program_id(2)
    q0 = pl.multiple_of(i * tq, tq)
    lane = lax.broadcasted_iota(jnp.int32, (tq, LANES), 1)
    qpos = q0 + lax.rem(lax.broadcasted_iota(jnp.int32, (2 * tq, 1), 0), tq)
    q2 = []
    for h in range(heads):
        qs = (q_ref[0, :, h * LANES:(h + 1) * LANES].astype(F32) * (HEAD_DIM ** -0.5)).astype(BF16)
        zero = jnp.zeros_like(qs)
        q2.append(jnp.concatenate([jnp.where(lane < HEAD_DIM, qs, zero),
                                   jnp.where(lane < HEAD_DIM, zero, qs)], axis=0))

    def scores(start, width, tail):
        out = []
        for h in range(heads):
            s = _dot_nt(q2[h], k_ref[0, pl.ds(start, width), h * LANES:(h + 1) * LANES]) * LOG2_E
            if tail:
                s = jnp.where(start + lax.broadcasted_iota(jnp.int32, (1, width), 1) <= qpos, s, NEG_BIG)
            out.append(s)
        return out

    def weigh(ps, start, width):
        ones = jnp.ones((width, LANES), BF16)
        return [_dot(p, jnp.concatenate([v_ref[0, pl.ds(start, width), h * LANES:(h + 1) * LANES], ones], axis=1))
                for h, p in enumerate(ps)]

    s_refs = [s_ref.at[pl.ds(h * 2 * tq, 2 * tq)] for h in range(heads)]
    accs = _softmax_sweep(s_refs, rows=2 * tq, acc_lanes=2 * LANES, q0=q0, tq=tq, big=max(tq, DIFF_CHUNK),
                          merge_tail=False, scores_fn=scores, weigh_fn=weigh)
    lp = lam_ref[...]
    lam = (jnp.exp(jnp.sum(lp[0:1] * lp[1:2], axis=1, keepdims=True))
           - jnp.exp(jnp.sum(lp[2:3] * lp[3:4], axis=1, keepdims=True)) + lam_init)
    out = []
    for acc in accs:
        o = acc[:, :LANES] / acc[:, LANES:]
        od = o[:tq] - lam * o[tq:]
        od = od * lax.rsqrt(jnp.mean(od * od, axis=-1, keepdims=True) + RMS_EPS)
        out.append(od * gain_ref[...] * (1.0 - lam_init))
    o_ref[0] = jnp.concatenate(out, axis=1).astype(o_ref.dtype)


def _diff_attention(pa, lam_params, gain, *, tq, heads, q_blk, k_blk, v_blk, lam_init):
    b, t, _ = pa.shape
    w = heads * LANES
    return pl.pallas_call(
        functools.partial(_diff_kernel, tq=tq, heads=heads, lam_init=lam_init),
        grid=(b, DIFF_HEADS // heads, t // tq),
        in_specs=[
            pl.BlockSpec((4, HEAD_DIM), lambda bi, h, i: (0, 0)),
            pl.BlockSpec((1, LANES), lambda bi, h, i: (0, 0)),
            pl.BlockSpec((1, tq, w), lambda bi, h, i: (bi, i, q_blk + h)),
            pl.BlockSpec((1, t, w), lambda bi, h, i: (bi, 0, k_blk + h)),
            pl.BlockSpec((1, t, w), lambda bi, h, i: (bi, 0, v_blk + h)),
        ],
        out_specs=pl.BlockSpec((1, tq, w), lambda bi, h, i: (bi, i, h)),
        out_shape=jax.ShapeDtypeStruct((b, t, DIFF_HEADS * LANES), BF16),
        scratch_shapes=[pltpu.VMEM((heads * 2 * tq, t), F32)],
        compiler_params=_params(("parallel", "parallel", "arbitrary")),
        name="diff_attention",
    )(lam_params, gain.reshape(1, LANES), pa, pa, pa)


def _compress_kernel(x_ref, pos_ref, w1_ref, w2_ref, o_ref):
    x = x_ref[0, 0]
    n16, half = x.shape
    xa = (x + pos_ref[0:1]).astype(BF16)
    xb = (x + pos_ref[1:2]).astype(BF16)
    w1 = w1_ref[0]
    first = _dot(xa, w1[:half])
    second = _dot(xb, w1[half:])
    h = first + pltpu.roll(second, n16 - 1, 0)
    g = 0.5 * h * (1.0 + jnp.tanh(math.sqrt(2.0 / math.pi) * (h + 0.044715 * (h * h * h))))
    o_ref[0, 0] = _dot(g.astype(BF16), w2_ref[0])


def _compress(x16, pos2, w1, w2, *, layer):
    b, n_streams, n16, half = x16.shape
    d = w2.shape[-1]
    return pl.pallas_call(
        _compress_kernel,
        grid=(b, n_streams),
        in_specs=[
            pl.BlockSpec((1, 1, n16, half), lambda bi, s: (bi, s, 0, 0)),
            pl.BlockSpec((None, 2, half), lambda bi, s: (layer, 0, 0)),
            pl.BlockSpec((None, 1, 2 * half, w1.shape[-1]), lambda bi, s: (layer, s // NSA_KV_HEADS, 0, 0)),
            pl.BlockSpec((None, 1, w2.shape[2], d), lambda bi, s: (layer, s // NSA_KV_HEADS, 0, 0)),
        ],
        out_specs=pl.BlockSpec((1, 1, n16, d), lambda bi, s: (bi, s, 0, 0)),
        out_shape=jax.ShapeDtypeStruct((b, n_streams, n16, d), F32),
        compiler_params=_params(("parallel", "arbitrary")),
        name="nsa_compress",
    )(x16, pos2, w1, w2)


def _nsa_kernel(q_ref, g_ref, ck_ref, cv_ref, ovt_ref, gx_ref, ks_ref, vs_ref, kw_ref, vw_ref, o_ref,
                kse_ref, kso_ref, vse_ref, vso_ref, kwd_ref, vwe_ref, vwo_ref, ckd_ref, cvd_ref, s_ref,
                *, tq, top_n):
    i = pl.program_id(2)
    grp = NSA_GROUP
    d = HEAD_DIM
    q0 = i * tq
    n_cmp = ck_ref.shape[2]
    n_slc = ovt_ref.shape[0]
    t = ks_ref.shape[1]
    rows = grp * tq
    half = rows // 2

    @pl.when(i == 0)
    def _():
        onehot = jnp.where(lax.broadcasted_iota(jnp.int32, (t, d), 0) // SLC_BLOCK
                           == lax.broadcasted_iota(jnp.int32, (t, d), 1), 1.0, 0.0)
        ones = jnp.ones((t, d), F32)

        def put(ref, left, right):
            ref[...] = jnp.concatenate([left, right], axis=1).astype(BF16)

        first_kv = pl.program_id(1) == 0

        def own_head(ref):
            both = ref[0].astype(F32)
            return jnp.where(first_kv, both[:, :d], both[:, d:])

        ks, vs, kw, vw = own_head(ks_ref), own_head(vs_ref), own_head(kw_ref), own_head(vw_ref)
        put(kse_ref, ks, onehot)
        put(kso_ref, onehot, ks)
        put(vse_ref, vs, ones)
        put(vso_ref, ones, vs)
        put(kwd_ref, kw, kw)
        put(vwe_ref, vw, ones)
        put(vwo_ref, ones, vw)
        put(ckd_ref, ck_ref[0, 0], ck_ref[0, 0])
        put(cvd_ref, cv_ref[0, 0], cv_ref[0, 0])

    lane = lax.broadcasted_iota(jnp.int32, (tq, LANES), 1)
    left = lane < d
    qs = (q_ref[0].astype(F32) * (d ** -0.5)).astype(BF16)
    pairs = [qs[:, :LANES], qs[:, LANES:]]
    zero = jnp.zeros((tq, LANES), BF16)
    qz = jnp.concatenate([jnp.where(left, pairs[0], zero), jnp.where(left, pairs[1], zero),
                          jnp.where(left, zero, pairs[0]), jnp.where(left, zero, pairs[1])], axis=0)
    t4 = q0 + lax.rem(lax.broadcasted_iota(jnp.int32, (rows, 1), 0), tq)

    def weigh(p, start, width, ve_ref, vo_ref):
        return jnp.concatenate([_dot(p[:half], ve_ref[pl.ds(start, width), :]),
                                _dot(p[half:], vo_ref[pl.ds(start, width), :])], axis=0)

    def normalise(acc):
        out = []
        for pr in range(2):
            even, odd = acc[pr * tq:(pr + 1) * tq], acc[half + pr * tq:half + (pr + 1) * tq]
            den = pltpu.roll(jnp.where(left, odd, even), d, 1)
            out.append(jnp.where(left, even, odd) / den)
        return out

    w_width = (-(-WINDOW // tq) + 1) * tq
    w_start = pl.multiple_of(jnp.maximum(q0 + tq - w_width, 0), tq)
    back = (q0 + lax.broadcasted_iota(jnp.int32, (tq, 1), 0)
            - (w_start + lax.broadcasted_iota(jnp.int32, (1, w_width), 1)))
    off = jnp.where((back >= 0) & (back < WINDOW), 0.0, NEG_BIG)
    sw = _dot_nt(qz, kwd_ref[pl.ds(w_start, w_width), :]) + jnp.concatenate([off] * grp, axis=0)
    pw = jnp.exp(sw - jnp.max(sw, axis=1, keepdims=True))
    o_win = normalise(weigh(pw.astype(BF16), w_start, w_width, vwe_ref, vwo_ref))

    gate = 1.0 / (1.0 + jnp.exp(-g_ref[0]))
    g_hi = gate.astype(BF16)
    g_r = gate - g_hi.astype(F32)
    g_mid = g_r.astype(BF16)
    g_lo = (g_r - g_mid.astype(F32)).astype(BF16)
    spread = gx_ref[...]
    gx = _dot(g_hi, spread) + _dot(g_mid, spread) + _dot(g_lo, spread)

    s = _dot_nt(qz, ckd_ref[...])
    cmp_end = CMP_STRIDE * lax.broadcasted_iota(jnp.int32, (1, n_cmp), 1) + (CMP_BLOCK - 1)
    cmask = cmp_end <= t4
    s = jnp.where(cmask, s, NEG_BIG)
    e = jnp.exp(s - jnp.max(s, axis=1, keepdims=True))
    p = e / jnp.sum(e, axis=1, keepdims=True)
    p = jnp.where(cmask, p, 0.0)
    o_cmp = _dot(p.astype(BF16), cvd_ref[...])

    pg = p[0:tq]
    for h in range(1, grp):
        pg = pg + p[h * tq:(h + 1) * tq]
    p_hi = pg.astype(BF16)
    r1 = pg - p_hi.astype(F32)
    p_mid = r1.astype(BF16)
    p_lo = (r1 - p_mid.astype(F32)).astype(BF16)
    ovt = ovt_ref[...]
    imp_t = _dot_nt(ovt, p_hi) + _dot_nt(ovt, p_mid) + _dot_nt(ovt, p_lo)
    blk = lax.broadcasted_iota(jnp.int32, (n_slc, tq), 0)
    cur = (q0 + lax.broadcasted_iota(jnp.int32, (n_slc, tq), 1)) // SLC_BLOCK
    forced = (blk == 0) | (blk == cur) | (blk == cur - 1)
    score = jnp.where(forced, jnp.inf, jnp.where(blk <= cur, imp_t, -jnp.inf))
    sub = lax.broadcasted_iota(jnp.int32, (8, tq), 0)
    ranks = []
    for g0 in range(0, n_slc, 8):
        sg = score[g0:g0 + 8]
        rg = jnp.zeros((8, tq), F32)
        for j in range(n_slc):
            sj = score[j:j + 1, :]
            if j < g0:
                beats = sj >= sg
            elif j >= g0 + 8:
                beats = sj > sg
            else:
                beats = (sj > sg) | ((sj == sg) & (sub > j - g0))
            rg = rg + jnp.where(beats, 1.0, 0.0)
        ranks.append(rg)
    rank = jnp.concatenate(ranks, axis=0)
    sel_t = jnp.where(rank < top_n, 1.0, 0.0)
    if n_slc < d:
        sel_t = jnp.concatenate([sel_t, jnp.zeros((d - n_slc, tq), F32)], axis=0)
    sel_t2 = jnp.concatenate([sel_t, sel_t], axis=0).astype(BF16)
    eye = jnp.where(lax.broadcasted_iota(jnp.int32, (tq, tq), 0)
                    == lax.broadcasted_iota(jnp.int32, (tq, tq), 1), 1.0, 0.0).astype(BF16)
    sel2 = _dot_nt(eye, sel_t2)
    bias = ((sel2 - 1.0) * (-NEG_BIG)).astype(BF16)
    q_even = jnp.concatenate([jnp.where(left, pairs[0], bias), jnp.where(left, pairs[1], bias)], axis=0)
    q_odd = jnp.concatenate([jnp.where(left, bias, pairs[0]), jnp.where(left, bias, pairs[1])], axis=0)

    def slc_scores(start, width, tail):
        s = jnp.concatenate([_dot_nt(q_even, kse_ref[pl.ds(start, width), :]),
                             _dot_nt(q_odd, kso_ref[pl.ds(start, width), :])], axis=0) * LOG2_E
        if tail:
            s = jnp.where(start + lax.broadcasted_iota(jnp.int32, (1, width), 1) <= t4, s, NEG_BIG)
        return s

    (acc,) = _softmax_sweep((s_ref,), rows=rows, acc_lanes=LANES, q0=pl.multiple_of(q0, tq), tq=tq,
                            big=4 * tq, merge_tail=True, scores_fn=lambda *a: (slc_scores(*a),),
                            weigh_fn=lambda ps, start, width: (weigh(ps[0], start, width, vse_ref, vso_ref),))
    o_slc = normalise(acc)

    out = []
    for pr in range(2):
        cmp_pr = jnp.where(left, o_cmp[pr * tq:(pr + 1) * tq], o_cmp[half + pr * tq:half + (pr + 1) * tq])
        mixed = 0.0
        for br, o_br in enumerate((cmp_pr, o_slc[pr], o_win[pr])):
            blk_idx = pr * N_BRANCH + br
            mixed = mixed + gx[:, blk_idx * LANES:(blk_idx + 1) * LANES] * o_br
        out.append(mixed)
    o_ref[0] = jnp.concatenate(out, axis=1).astype(o_ref.dtype)


def _nsa_attention(pa, pf, ckv, ovt, *, tq, q_blk, kv_blk, g_blk, top_n):
    b, t, _ = pa.shape
    n16 = ckv.shape[2]
    n_slc = ovt.shape[0]
    qw = NSA_GROUP * HEAD_DIM
    kv_specs = [pl.BlockSpec((1, t, LANES), functools.partial(lambda bi, j, i, c: (bi, 0, c), c=kv_blk + n))
                for n in range(4)]
    return pl.pallas_call(
        functools.partial(_nsa_kernel, tq=tq, top_n=top_n),
        grid=(b, NSA_KV_HEADS, t // tq),
        in_specs=[
            pl.BlockSpec((1, tq, qw), lambda bi, j, i: (bi, i, q_blk + j)),
            pl.BlockSpec((1, tq, LANES), lambda bi, j, i: (bi, i, g_blk + j)),
            pl.BlockSpec((1, 1, n16, HEAD_DIM), lambda bi, j, i: (bi, j, 0, 0)),
            pl.BlockSpec((1, 1, n16, HEAD_DIM), lambda bi, j, i: (bi, NSA_KV_HEADS + j, 0, 0)),
            pl.BlockSpec((n_slc, n16), lambda bi, j, i: (0, 0)),
            pl.BlockSpec((LANES, 2 * N_BRANCH * LANES), lambda bi, j, i: (0, 0)),
            *kv_specs,
        ],
        out_specs=pl.BlockSpec((1, tq, qw), lambda bi, j, i: (bi, i, j)),
        out_shape=jax.ShapeDtypeStruct((b, t, NSA_HEADS * HEAD_DIM), BF16),
        scratch_shapes=[pltpu.VMEM((t, LANES), BF16)] * 7 + [pltpu.VMEM((n16, LANES), BF16)] * 2
                       + [pltpu.VMEM((NSA_GROUP * tq, t), F32)],
        compiler_params=_params(("arbitrary", "arbitrary", "arbitrary")),
        name="nsa_attention",
    )(pa, pf, ckv, ckv, ovt, _gate_spread(), pa, pa, pa, pa)


def _out_ln_kernel(a_ref, b_ref, c_ref, cb_ref, cc_ref, ch_ref, cw_ref, w_ref, x_ref, g_ref, beta_ref, o_ref,
                   prev_ref, *, alpha, tiles_per_seq):
    @pl.when(pl.program_id(0) % tiles_per_seq == 0)
    def _():
        prev_ref[...] = jnp.zeros_like(prev_ref)

    u = cc_ref[...] * ch_ref[...]
    tm = u.shape[0]
    row = lax.broadcasted_iota(jnp.int32, u.shape, 0)
    last1 = prev_ref[7:8]
    last2 = prev_ref[6:7]
    u1 = jnp.where(row >= 1, pltpu.roll(u, 1, 0), last1)
    u2 = jnp.where(row >= 2, pltpu.roll(u, 2, 0), jnp.where(row == 1, last1, last2))
    cw = cw_ref[...]
    o_cv = cb_ref[...] * (cw[0:1] * u2 + cw[1:2] * u1 + cw[2:3] * u)
    prev_ref[...] = u[tm - 8:tm]

    y = jnp.concatenate([a_ref[...], b_ref[...], c_ref[...], o_cv.astype(BF16)], axis=1)
    o_ref[...] = _layer_norm(alpha * x_ref[...] + _dot(y, w_ref[...]), g_ref[...], beta_ref[...])


def _out_ln(parts, pf, conv_w, w, x, g, b, *, layer, seq_len, alpha, tm):
    n, d = x.shape
    kw = parts[0].shape[1]
    part_spec = pl.BlockSpec((tm, kw), lambda i: (i, 0))
    return pl.pallas_call(
        functools.partial(_out_ln_kernel, alpha=alpha, tiles_per_seq=seq_len // tm),
        grid=(n // tm,),
        in_specs=[part_spec, part_spec, part_spec,
                  pl.BlockSpec((tm, CONV_CH), lambda i: (i, 0)),
                  pl.BlockSpec((tm, CONV_CH), lambda i: (i, 1)),
                  pl.BlockSpec((tm, CONV_CH), lambda i: (i, 2)),
                  pl.BlockSpec((None, CONV_WIDTH, CONV_CH), lambda i: (layer, 0, 0)),
                  pl.BlockSpec((None,) + w.shape[1:], lambda i: (layer, 0, 0)),
                  pl.BlockSpec((tm, d), lambda i: (i, 0)),
                  pl.BlockSpec((1, d), lambda i: (0, 0)),
                  pl.BlockSpec((1, d), lambda i: (0, 0))],
        out_specs=pl.BlockSpec((tm, d), lambda i: (i, 0)),
        out_shape=jax.ShapeDtypeStruct((n, d), F32),
        scratch_shapes=[pltpu.VMEM((8, CONV_CH), F32)],
        compiler_params=_params(("arbitrary",)),
        name="out_ln",
    )(*parts, pf, pf, pf, conv_w, w, x, g.reshape(1, d), b.reshape(1, d))


def _gate_spread():
    m = np.zeros((LANES, 2 * N_BRANCH * LANES), np.float32)
    for pr in range(2):
        for br in range(N_BRANCH):
            for lane in range(LANES):
                head = 2 * pr + lane // HEAD_DIM
                m[N_BRANCH * head + br, (pr * N_BRANCH + br) * LANES + lane] = 1.0
    return jnp.asarray(m, BF16)


def _overlap_t(t):
    n16 = t // CMP_STRIDE
    n_slc = t // SLC_BLOCK
    c_start = CMP_STRIDE * np.arange(n16)
    j_start = SLC_BLOCK * np.arange(n_slc)
    ov = ((c_start[None, :] < j_start[:, None] + SLC_BLOCK)
          & (c_start[None, :] + CMP_BLOCK > j_start[:, None])).astype(np.float32)
    ov[:, n16 - 1] = 0.0
    return jnp.asarray(ov, BF16)


def _mixer(hf, batch, w_proj, w_out, ln_g, ln_b, diff_lam, diff_gain, pos2, cmp_w1, cmp_w2,
           conv_w, layer, alpha):
    n, _ = hf.shape
    t = n // batch
    n_slc = t // SLC_BLOCK
    assert t % (4 * NSA_TQ) == 0 and t % max(DIFF_TQ, DIFF_CHUNK) == 0 and t % SB_TQ == 0 and n_slc <= HEAD_DIM
    assert t >= WINDOW + NSA_TQ and n % PROJ_ROWS == 0 and n % FFN_ROWS == 0 and t % OUT_ROWS == 0
    pa, pf = _proj(hf, w_proj, layer=layer, width_a=ATT_WIDTH, tm=PROJ_ROWS, tn=PROJ_COLS)
    pa, pf = pa.reshape(batch, t, -1), pf.reshape(batch, t, -1)

    def att_blk(name, lanes):
        return _ATT_OFF[name] // lanes

    w = SB_PAIRS * LANES
    o_sb = _sb_attention(pa, tq=SB_TQ, pairs=SB_PAIRS, q_blk=att_blk("sb_q", w), k_blk=att_blk("sb_k", w),
                         v_blk=att_blk("sb_v", w))
    lam_init = 0.8 - 0.6 * math.exp(-0.3 * layer)
    w = DIFF_HEADS_PER_STEP * LANES
    o_df = _diff_attention(pa, diff_lam[layer], diff_gain[layer], tq=DIFF_TQ, heads=DIFF_HEADS_PER_STEP,
                           q_blk=att_blk("df_q", w), k_blk=att_blk("df_k", w), v_blk=att_blk("df_v", w),
                           lam_init=lam_init)

    n16 = t // CMP_STRIDE
    kvc = pf[:, :, _F32_OFF["ns_kc"]:_F32_OFF["ns_kc"] + 2 * LANES].reshape(batch, t, 2, NSA_KV_HEADS, HEAD_DIM)
    x16 = kvc.transpose(0, 2, 3, 1, 4).reshape(batch, 2 * NSA_KV_HEADS, n16, CMP_STRIDE * HEAD_DIM)
    ckv = _compress(x16, pos2, cmp_w1, cmp_w2, layer=layer)
    o_ns = _nsa_attention(pa, pf, ckv, _overlap_t(t), tq=NSA_TQ, q_blk=att_blk("ns_q", NSA_GROUP * HEAD_DIM),
                          kv_blk=att_blk("ns_ks", LANES), g_blk=_GATE_OFF // LANES, top_n=min(SLC_TOPN, n_slc))

    assert _F32_OFF["cv_b"] == 0 and _F32_OFF["cv_c"] == CONV_CH and _F32_OFF["cv_h"] == 2 * CONV_CH
    parts = [o.reshape(n, -1) for o in (o_sb, o_df, o_ns)]
    return _out_ln(parts, pf.reshape(n, -1), conv_w, w_out, hf, ln_g, ln_b, layer=layer, seq_len=t, alpha=alpha,
                   tm=OUT_ROWS)


def _seg(w_in, name, lo=0, hi=None):
    off = _SEG_OFF[name]
    hi = _SEG_W[name] if hi is None else hi
    return w_in[:, :, off + lo:off + hi]


def kernel(x, ln_g, ln_b, ffn_w1, ffn_w3, ffn_w2, w_in, w_out, diff_lam, diff_gain, cmp_pos, cmp_wk1,
           cmp_wk2, cmp_wv1, cmp_wv2, conv_w):
    batch, t, d = x.shape
    depth = ln_g.shape[0]
    alpha = (2 * depth) ** 0.25
    n = batch * t

    w1b, w3b, w2b = ffn_w1.astype(BF16), ffn_w3.astype(BF16), ffn_w2.astype(BF16)
    gate_pad = jnp.zeros(w_in.shape[:2] + (LANES - _GATES_PER_KV,), w_in.dtype)
    cols = [_seg(w_in, name) for name in _ATT_ORDER + _F32_ORDER]
    for j in range(NSA_KV_HEADS):
        cols += [_seg(w_in, "ns_g", j * _GATES_PER_KV, (j + 1) * _GATES_PER_KV), gate_pad]
    w_proj = jnp.concatenate(cols, axis=-1).astype(BF16)
    w_outb = w_out.astype(BF16)
    half = CMP_STRIDE * HEAD_DIM
    pos2 = cmp_pos.reshape(depth, 2, half)
    cmp_w1 = jnp.stack([cmp_wk1, cmp_wv1], axis=1).astype(BF16)
    cmp_w2 = jnp.stack([cmp_wk2, cmp_wv2], axis=1).astype(BF16)

    hf = x.reshape(n, d)
    for l in range(depth):
        hf = _ffn_ln(hf, w1b, w3b, w2b, ln_g[l, 0], ln_b[l, 0], layer=l, slot=0, alpha=alpha, tm=FFN_ROWS,
                     tf=FFN_TILE)
        hf = _mixer(hf, batch, w_proj, w_outb, ln_g[l, 1], ln_b[l, 1], diff_lam, diff_gain,
                    pos2, cmp_w1, cmp_w2, conv_w, l, alpha)
        hf = _ffn_ln(hf, w1b, w3b, w2b, ln_g[l, 2], ln_b[l, 2], layer=l, slot=1, alpha=alpha, tm=FFN_ROWS,
                     tf=FFN_TILE)
    return hf.reshape(batch, t, d)
```

```python
import functools
import math

import numpy as np
import jax
import jax.numpy as jnp
from jax import lax
from jax.experimental import pallas as pl
from jax.experimental.pallas import tpu as pltpu

F32 = jnp.float32
BF16 = jnp.bfloat16

HEAD_DIM = 64
SB_HEADS = 8
DIFF_HEADS = 4
NSA_HEADS = 8
NSA_KV_HEADS = 2
NSA_GROUP = NSA_HEADS // NSA_KV_HEADS
CMP_BLOCK = 32
CMP_STRIDE = 16
SLC_BLOCK = 64
SLC_TOPN = 16
WINDOW = 512
N_BRANCH = 3
CONV_CH = 512
CONV_WIDTH = 3
LN_EPS = 1e-5
RMS_EPS = 1e-5
NEG_BIG = -1e30
LOG2_E = math.log2(math.e)
EXP2_UNDERFLOW = -150.0
LANES = 128
VMEM_LIMIT = 60 * 1024 * 1024
FFN_TILE = 512
FFN_ACC_COLS = 512
FFN_LN_ROWS = 128
FFN_ROWS = 1024

_SEG_NAMES = ("sb_q", "sb_k", "sb_v", "df_q", "df_k", "df_v", "ns_q", "ns_kc", "ns_vc",
              "ns_ks", "ns_vs", "ns_kw", "ns_vw", "ns_g", "cv_b", "cv_c", "cv_h")
_SEG_WIDTHS = (512, 512, 512, 512, 512, 512, 512, 128, 128, 128, 128, 128, 128,
               NSA_HEADS * N_BRANCH, 512, 512, 512)
_SEG_OFF = dict(zip(_SEG_NAMES, np.cumsum((0,) + _SEG_WIDTHS[:-1]).tolist()))
_SEG_W = dict(zip(_SEG_NAMES, _SEG_WIDTHS))


_GATES_PER_KV = NSA_GROUP * N_BRANCH
_ATT_ORDER = ("sb_q", "sb_k", "sb_v", "df_q", "df_k", "df_v", "ns_q", "ns_ks", "ns_vs", "ns_kw", "ns_vw")
_ATT_OFF = dict(zip(_ATT_ORDER, np.cumsum([0] + [_SEG_W[s] for s in _ATT_ORDER[:-1]]).tolist()))
ATT_WIDTH = sum(_SEG_W[s] for s in _ATT_ORDER)
_F32_ORDER = ("cv_b", "cv_c", "cv_h", "ns_kc", "ns_vc")
_F32_OFF = dict(zip(_F32_ORDER, np.cumsum([0] + [_SEG_W[s] for s in _F32_ORDER[:-1]]).tolist()))
_GATE_OFF = sum(_SEG_W[s] for s in _F32_ORDER)

SB_TQ, SB_PAIRS = 256, 4
DIFF_TQ, DIFF_HEADS_PER_STEP = 512, 2
DIFF_CHUNK = 1024
NSA_TQ = 256
PROJ_ROWS, PROJ_COLS = 1024, 1024
OUT_ROWS = 512


def _params(sem):
    return pltpu.CompilerParams(dimension_semantics=sem, vmem_limit_bytes=VMEM_LIMIT)


def _layer_norm(y, g, b):
    mu = jnp.mean(y, axis=-1, keepdims=True)
    d = y - mu
    var = jnp.mean(d * d, axis=-1, keepdims=True)
    return d * lax.rsqrt(var + LN_EPS) * g + b


def _dot(a, b):
    return jnp.dot(a, b, preferred_element_type=F32)


def _lane_group_max(s, mx):
    for g in range(s.shape[1] // LANES):
        mx = jnp.maximum(mx, s[:, g * LANES:(g + 1) * LANES])
    return mx


def _softmax_sweep(s_refs, *, rows, acc_lanes, q0, tq, big, merge_tail, scores_fn, weigh_fn):
    n_big = q0 // big
    mid = pl.multiple_of(n_big * big, tq)
    tail = (q0 - mid) // tq
    tail_widths = [(k + 1) * tq for k in range(big // tq)]

    def chunks(fn, init):
        return lax.fori_loop(0, n_big, lambda c, carry: fn(pl.multiple_of(c * big, big), big, carry), init)

    def scores(start, width, mxs, is_tail=False):
        out = []
        for ref, s, mx in zip(s_refs, scores_fn(start, width, is_tail), mxs):
            ref[:, pl.ds(start, width)] = s
            out.append(_lane_group_max(s, mx))
        return tuple(out)

    def tiles(fn, init):
        return lax.fori_loop(0, tail, lambda c, carry: fn(pl.multiple_of(mid + c * tq, tq), tq, carry), init)

    mxs = chunks(scores, (jnp.full((rows, LANES), NEG_BIG, F32),) * len(s_refs))
    if merge_tail:
        mxs = lax.switch(tail, [functools.partial(scores, mid, w, is_tail=True) for w in tail_widths], mxs)
    else:
        mxs = scores(q0, tq, tiles(scores, mxs), is_tail=True)
    ms = [jnp.broadcast_to(jnp.max(mx, axis=1, keepdims=True), (rows, LANES)) for mx in mxs]

    def weigh(start, width, accs):
        ps = [jnp.exp2(ref[:, pl.ds(start, width)] - jnp.concatenate([m] * (width // LANES), axis=1)).astype(BF16)
              for ref, m in zip(s_refs, ms)]
        return tuple(acc + w for acc, w in zip(accs, weigh_fn(ps, start, width)))

    accs = chunks(weigh, (jnp.zeros((rows, acc_lanes), F32),) * len(s_refs))
    if merge_tail:
        return lax.switch(tail, [functools.partial(weigh, mid, w) for w in tail_widths], accs)
    return weigh(q0, tq, tiles(weigh, accs))


def _dot_nt(a, b):
    return lax.dot_general(a, b, (((1,), (1,)), ((), ())), preferred_element_type=F32)


def _ffn_ln_kernel(x_ref, w1_ref, w3_ref, w2_ref, g_ref, b_ref, o_ref, xb_ref, *, alpha):
    j = pl.program_id(1)

    @pl.when(j == 0)
    def _():
        xb_ref[...] = x_ref[...].astype(BF16)
        o_ref[...] = jnp.zeros_like(o_ref)

    xb = xb_ref[...]
    a = _dot(xb, w1_ref[...])
    b = _dot(xb, w3_ref[...])
    hm = (a / (1.0 + jnp.exp(-a))) * b
    hb = hm.astype(BF16)
    for c in range(0, o_ref.shape[1], FFN_ACC_COLS):
        o_ref[:, c:c + FFN_ACC_COLS] += _dot(hb, w2_ref[:, c:c + FFN_ACC_COLS])

    @pl.when(j == pl.num_programs(1) - 1)
    def _():
        for r in range(0, o_ref.shape[0], FFN_LN_ROWS):
            rows = slice(r, r + FFN_LN_ROWS)
            y = alpha * x_ref[rows] + 0.5 * o_ref[rows]
            o_ref[rows] = _layer_norm(y, g_ref[...], b_ref[...])


def _ffn_ln(x, w1, w3, w2, g, b, *, layer, slot, alpha, tm, tf):
    n, d = x.shape
    f = w1.shape[-1]
    return pl.pallas_call(
        functools.partial(_ffn_ln_kernel, alpha=alpha),
        grid=(n // tm, f // tf),
        in_specs=[
            pl.BlockSpec((tm, d), lambda i, j: (i, 0)),
            pl.BlockSpec((None, None, d, tf), lambda i, j: (layer, slot, 0, j)),
            pl.BlockSpec((None, None, d, tf), lambda i, j: (layer, slot, 0, j)),
            pl.BlockSpec((None, None, tf, d), lambda i, j: (layer, slot, j, 0)),
            pl.BlockSpec((1, d), lambda i, j: (0, 0)),
            pl.BlockSpec((1, d), lambda i, j: (0, 0)),
        ],
        out_specs=pl.BlockSpec((tm, d), lambda i, j: (i, 0)),
        out_shape=jax.ShapeDtypeStruct((n, d), F32),
        scratch_shapes=[pltpu.VMEM((tm, d), BF16)],
        compiler_params=_params(("parallel", "arbitrary")),
        name="ffn_ln",
    )(x, w1, w3, w2, g.reshape(1, d), b.reshape(1, d))


def _proj_kernel(x_ref, w_ref, oa_ref, of_ref, xb_ref, *, n_a):
    j = pl.program_id(1)

    @pl.when(j == 0)
    def _():
        xb_ref[...] = x_ref[...].astype(BF16)

    @pl.when(j < n_a)
    def _():
        oa_ref[...] = _dot(xb_ref[...], w_ref[...]).astype(oa_ref.dtype)

    @pl.when(j >= n_a)
    def _():
        of_ref[...] = _dot(xb_ref[...], w_ref[...])


def _proj(x, w, *, layer, width_a, tm, tn):
    n, d = x.shape
    m = w.shape[-1]
    n_a = width_a // tn
    return pl.pallas_call(
        functools.partial(_proj_kernel, n_a=n_a),
        grid=(n // tm, m // tn),
        in_specs=[pl.BlockSpec((tm, d), lambda i, j: (i, 0)),
                  pl.BlockSpec((None, d, tn), lambda i, j: (layer, 0, j))],
        out_specs=[pl.BlockSpec((tm, tn), lambda i, j: (i, jnp.minimum(j, n_a - 1))),
                   pl.BlockSpec((tm, tn), lambda i, j: (i, jnp.maximum(j - n_a, 0)))],
        out_shape=[jax.ShapeDtypeStruct((n, width_a), BF16), jax.ShapeDtypeStruct((n, m - width_a), F32)],
        scratch_shapes=[pltpu.VMEM((tm, d), BF16)],
        compiler_params=_params(("parallel", "arbitrary")),
        name="proj",
    )(x, w)


def _sb_kernel(q_ref, k_ref, v_ref, o_ref, *, tq, pairs):
    i = pl.program_id(2)
    q0 = i * tq
    lane = lax.broadcasted_iota(jnp.int32, (tq, LANES), 1)
    tri = jnp.where(lax.broadcasted_iota(jnp.int32, (tq, tq), 0)
                    >= lax.broadcasted_iota(jnp.int32, (tq, tq), 1), 1.0, 0.0).astype(BF16)
    qpos = q0 + lax.rem(lax.broadcasted_iota(jnp.int32, (2 * tq, 1), 0), tq)
    q2 = []
    for p in range(pairs):
        qs = (q_ref[0, :, p * LANES:(p + 1) * LANES].astype(F32) * (HEAD_DIM ** -0.5)).astype(BF16)
        zero = jnp.zeros_like(qs)
        q2.append(jnp.concatenate([jnp.where(lane < HEAD_DIM, qs, zero),
                                   jnp.where(lane < HEAD_DIM, zero, qs)], axis=0))

    def block(start, state, keep=None):
        width = tq
        masked = keep is not None
        out = []
        for p, (c, acc) in enumerate(state):
            ks = k_ref[0, pl.ds(start, width), p * LANES:(p + 1) * LANES]
            vs = v_ref[0, pl.ds(start, width), p * LANES:(p + 1) * LANES]
            z = _dot_nt(q2[p], ks) * LOG2_E
            nz = -z
            lg = jnp.minimum(nz, 0.0) - jnp.log2(1.0 + jnp.exp2(jnp.minimum(z, nz)))
            if masked:
                lg = jnp.where(keep, lg, 0.0)
            hi = lg.astype(BF16)
            lo = (lg - hi.astype(F32)).astype(BF16)
            suffix = _dot(hi, tri) + _dot(lo, tri)
            a = jnp.exp2(z + suffix + c)
            if masked:
                a = jnp.where(keep, a, 0.0)
            out.append((c + suffix[:, 0:1], acc + _dot(a.astype(BF16), vs)))
        return tuple(out)

    state = ((jnp.zeros((2 * tq, 1), F32), jnp.zeros((2 * tq, LANES), F32)),) * pairs
    key = lax.broadcasted_iota(jnp.int32, (1, tq), 1)
    state = block(pl.multiple_of(q0, tq), state, keep=q0 + key < qpos)
    no_prev = jnp.where(i > 0, 0.0, NEG_BIG)
    state = block(pl.multiple_of(jnp.maximum(q0 - tq, 0), tq), tuple((c + no_prev, acc) for c, acc in state))

    def live(state):
        top = functools.reduce(jnp.maximum, [c for c, _ in state])
        return (jnp.max(top) > EXP2_UNDERFLOW).astype(jnp.int32)

    def cond(carry):
        return (carry[0] >= 0) & (carry[1] > 0)

    def body(carry):
        kt, _, state = carry
        state = block(pl.multiple_of(kt * tq, tq), state)
        return kt - 1, live(state), state

    _, _, state = lax.while_loop(cond, body, (i - 2, live(state), state))
    o_ref[0] = jnp.concatenate([jnp.where(lane < HEAD_DIM, acc[:tq], acc[tq:]) for _, acc in state],
                               axis=1).astype(o_ref.dtype)


def _sb_attention(pa, *, tq, pairs, q_blk, k_blk, v_blk):
    b, t, _ = pa.shape
    w = pairs * LANES
    n_groups = SB_HEADS // (2 * pairs)
    return pl.pallas_call(
        functools.partial(_sb_kernel, tq=tq, pairs=pairs),
        grid=(b, n_groups, t // tq),
        in_specs=[
            pl.BlockSpec((1, tq, w), lambda bi, p, i: (bi, i, q_blk + p)),
            pl.BlockSpec((1, t, w), lambda bi, p, i: (bi, 0, k_blk + p)),
            pl.BlockSpec((1, t, w), lambda bi, p, i: (bi, 0, v_blk + p)),
        ],
        out_specs=pl.BlockSpec((1, tq, w), lambda bi, p, i: (bi, i, p)),
        out_shape=jax.ShapeDtypeStruct((b, t, SB_HEADS * HEAD_DIM), BF16),
        compiler_params=_params(("parallel", "parallel", "arbitrary")),
        name="sb_attention",
    )(pa, pa, pa)


def _diff_kernel(lam_ref, gain_ref, q_ref, k_ref, v_ref, o_ref, s_ref, *, tq, heads, lam_init):
    i = pl.program_id(2)
    q0 = pl.multiple_of(i * tq, tq)
    lane = lax.broadcasted_iota(jnp.int32, (tq, LANES), 1)
    qpos = q0 + lax.rem(lax.broadcasted_iota(jnp.int32, (2 * tq, 1), 0), tq)
    q2 = []
    for h in range(heads):
        qs = (q_ref[0, :, h * LANES:(h + 1) * LANES].astype(F32) * (HEAD_DIM ** -0.5)).astype(BF16)
        zero = jnp.zeros_like(qs)
        q2.append(jnp.concatenate([jnp.where(lane < HEAD_DIM, qs, zero),
                                   jnp.where(lane < HEAD_DIM, zero, qs)], axis=0))

    def scores(start, width, tail):
        out = []
        for h in range(heads):
            s = _dot_nt(q2[h], k_ref[0, pl.ds(start, width), h * LANES:(h + 1) * LANES]) * LOG2_E
            if tail:
                s = jnp.where(start + lax.broadcasted_iota(jnp.int32, (1, width), 1) <= qpos, s, NEG_BIG)
            out.append(s)
        return out

    def weigh(ps, start, width):
        ones = jnp.ones((width, LANES), BF16)
        return [_dot(p, jnp.concatenate([v_ref[0, pl.ds(start, width), h * LANES:(h + 1) * LANES], ones], axis=1))
                for h, p in enumerate(ps)]

    s_refs = [s_ref.at[pl.ds(h * 2 * tq, 2 * tq)] for h in range(heads)]
    accs = _softmax_sweep(s_refs, rows=2 * tq, acc_lanes=2 * LANES, q0=q0, tq=tq, big=max(tq, DIFF_CHUNK),
                          merge_tail=False, scores_fn=scores, weigh_fn=weigh)
    lp = lam_ref[...]
    lam = (jnp.exp(jnp.sum(lp[0:1] * lp[1:2], axis=1, keepdims=True))
           - jnp.exp(jnp.sum(lp[2:3] * lp[3:4], axis=1, keepdims=True)) + lam_init)
    out = []
    for acc in accs:
        o = acc[:, :LANES] / acc[:, LANES:]
        od = o[:tq] - lam * o[tq:]
        od = od * lax.rsqrt(jnp.mean(od * od, axis=-1, keepdims=True) + RMS_EPS)
        out.append(od * gain_ref[...] * (1.0 - lam_init))
    o_ref[0] = jnp.concatenate(out, axis=1).astype(o_ref.dtype)


def _diff_attention(pa, lam_params, gain, *, tq, heads, q_blk, k_blk, v_blk, lam_init):
    b, t, _ = pa.shape
    w = heads * LANES
    return pl.pallas_call(
        functools.partial(_diff_kernel, tq=tq, heads=heads, lam_init=lam_init),
        grid=(b, DIFF_HEADS // heads, t // tq),
        in_specs=[
            pl.BlockSpec((4, HEAD_DIM), lambda bi, h, i: (0, 0)),
            pl.BlockSpec((1, LANES), lambda bi, h, i: (0, 0)),
            pl.BlockSpec((1, tq, w), lambda bi, h, i: (bi, i, q_blk + h)),
            pl.BlockSpec((1, t, w), lambda bi, h, i: (bi, 0, k_blk + h)),
            pl.BlockSpec((1, t, w), lambda bi, h, i: (bi, 0, v_blk + h)),
        ],
        out_specs=pl.BlockSpec((1, tq, w), lambda bi, h, i: (bi, i, h)),
        out_shape=jax.ShapeDtypeStruct((b, t, DIFF_HEADS * LANES), BF16),
        scratch_shapes=[pltpu.VMEM((heads * 2 * tq, t), F32)],
        compiler_params=_params(("parallel", "parallel", "arbitrary")),
        name="diff_attention",
    )(lam_params, gain.reshape(1, LANES), pa, pa, pa)


def _compress_kernel(x_ref, pos_ref, w1_ref, w2_ref, o_ref):
    x = x_ref[0, 0]
    n16, half = x.shape
    xa = (x + pos_ref[0:1]).astype(BF16)
    xb = (x + pos_ref[1:2]).astype(BF16)
    w1 = w1_ref[0]
    first = _dot(xa, w1[:half])
    second = _dot(xb, w1[half:])
    h = first + pltpu.roll(second, n16 - 1, 0)
    g = 0.5 * h * (1.0 + jnp.tanh(math.sqrt(2.0 / math.pi) * (h + 0.044715 * (h * h * h))))
    o_ref[0, 0] = _dot(g.astype(BF16), w2_ref[0])


def _compress(x16, pos2, w1, w2, *, layer):
    b, n_streams, n16, half = x16.shape
    d = w2.shape[-1]
    return pl.pallas_call(
        _compress_kernel,
        grid=(b, n_streams),
        in_specs=[
            pl.BlockSpec((1, 1, n16, half), lambda bi, s: (bi, s, 0, 0)),
            pl.BlockSpec((None, 2, half), lambda bi, s: (layer, 0, 0)),
            pl.BlockSpec((None, 1, 2 * half, w1.shape[-1]), lambda bi, s: (layer, s // NSA_KV_HEADS, 0, 0)),
            pl.BlockSpec((None, 1, w2.shape[2], d), lambda bi, s: (layer, s // NSA_KV_HEADS, 0, 0)),
        ],
        out_specs=pl.BlockSpec((1, 1, n16, d), lambda bi, s: (bi, s, 0, 0)),
        out_shape=jax.ShapeDtypeStruct((b, n_streams, n16, d), F32),
        compiler_params=_params(("parallel", "arbitrary")),
        name="nsa_compress",
    )(x16, pos2, w1, w2)


def _nsa_kernel(q_ref, g_ref, ck_ref, cv_ref, ovt_ref, gx_ref, ks_ref, vs_ref, kw_ref, vw_ref, o_ref,
                kse_ref, kso_ref, vse_ref, vso_ref, kwd_ref, vwe_ref, vwo_ref, ckd_ref, cvd_ref, s_ref,
                *, tq, top_n):
    i = pl.program_id(2)
    grp = NSA_GROUP
    d = HEAD_DIM
    q0 = i * tq
    n_cmp = ck_ref.shape[2]
    n_slc = ovt_ref.shape[0]
    t = ks_ref.shape[1]
    rows = grp * tq
    half = rows // 2

    @pl.when(i == 0)
    def _():
        onehot = jnp.where(lax.broadcasted_iota(jnp.int32, (t, d), 0) // SLC_BLOCK
                           == lax.broadcasted_iota(jnp.int32, (t, d), 1), 1.0, 0.0)
        ones = jnp.ones((t, d), F32)

        def put(ref, left, right):
            ref[...] = jnp.concatenate([left, right], axis=1).astype(BF16)

        first_kv = pl.program_id(1) == 0

        def own_head(ref):
            both = ref[0].astype(F32)
            return jnp.where(first_kv, both[:, :d], both[:, d:])

        ks, vs, kw, vw = own_head(ks_ref), own_head(vs_ref), own_head(kw_ref), own_head(vw_ref)
        put(kse_ref, ks, onehot)
        put(kso_ref, onehot, ks)
        put(vse_ref, vs, ones)
        put(vso_ref, ones, vs)
        put(kwd_ref, kw, kw)
        put(vwe_ref, vw, ones)
        put(vwo_ref, ones, vw)
        put(ckd_ref, ck_ref[0, 0], ck_ref[0, 0])
        put(cvd_ref, cv_ref[0, 0], cv_ref[0, 0])

    lane = lax.broadcasted_iota(jnp.int32, (tq, LANES), 1)
    left = lane < d
    qs = (q_ref[0].astype(F32) * (d ** -0.5)).astype(BF16)
    pairs = [qs[:, :LANES], qs[:, LANES:]]
    zero = jnp.zeros((tq, LANES), BF16)
    qz = jnp.concatenate([jnp.where(left, pairs[0], zero), jnp.where(left, pairs[1], zero),
                          jnp.where(left, zero, pairs[0]), jnp.where(left, zero, pairs[1])], axis=0)
    t4 = q0 + lax.rem(lax.broadcasted_iota(jnp.int32, (rows, 1), 0), tq)

    def weigh(p, start, width, ve_ref, vo_ref):
        return jnp.concatenate([_dot(p[:half], ve_ref[pl.ds(start, width), :]),
                                _dot(p[half:], vo_ref[pl.ds(start, width), :])], axis=0)

    def normalise(acc):
        out = []
        for pr in range(2):
            even, odd = acc[pr * tq:(pr + 1) * tq], acc[half + pr * tq:half + (pr + 1) * tq]
            den = pltpu.roll(jnp.where(left, odd, even), d, 1)
            out.append(jnp.where(left, even, odd) / den)
        return out

    w_width = (-(-WINDOW // tq) + 1) * tq
    w_start = pl.multiple_of(jnp.maximum(q0 + tq - w_width, 0), tq)
    back = (q0 + lax.broadcasted_iota(jnp.int32, (tq, 1), 0)
            - (w_start + lax.broadcasted_iota(jnp.int32, (1, w_width), 1)))
    off = jnp.where((back >= 0) & (back < WINDOW), 0.0, NEG_BIG)
    sw = _dot_nt(qz, kwd_ref[pl.ds(w_start, w_width), :]) + jnp.concatenate([off] * grp, axis=0)
    pw = jnp.exp(sw - jnp.max(sw, axis=1, keepdims=True))
    o_win = normalise(weigh(pw.astype(BF16), w_start, w_width, vwe_ref, vwo_ref))

    gate = 1.0 / (1.0 + jnp.exp(-g_ref[0]))
    g_hi = gate.astype(BF16)
    g_r = gate - g_hi.astype(F32)
    g_mid = g_r.astype(BF16)
    g_lo = (g_r - g_mid.astype(F32)).astype(BF16)
    spread = gx_ref[...]
    gx = _dot(g_hi, spread) + _dot(g_mid, spread) + _dot(g_lo, spread)

    s = _dot_nt(qz, ckd_ref[...])
    cmp_end = CMP_STRIDE * lax.broadcasted_iota(jnp.int32, (1, n_cmp), 1) + (CMP_BLOCK - 1)
    cmask = cmp_end <= t4
    s = jnp.where(cmask, s, NEG_BIG)
    e = jnp.exp(s - jnp.max(s, axis=1, keepdims=True))
    p = e / jnp.sum(e, axis=1, keepdims=True)
    p = jnp.where(cmask, p, 0.0)
    o_cmp = _dot(p.astype(BF16), cvd_ref[...])

    pg = p[0:tq]
    for h in range(1, grp):
        pg = pg + p[h * tq:(h + 1) * tq]
    p_hi = pg.astype(BF16)
    r1 = pg - p_hi.astype(F32)
    p_mid = r1.astype(BF16)
    p_lo = (r1 - p_mid.astype(F32)).astype(BF16)
    ovt = ovt_ref[...]
    imp_t = _dot_nt(ovt, p_hi) + _dot_nt(ovt, p_mid) + _dot_nt(ovt, p_lo)
    blk = lax.broadcasted_iota(jnp.int32, (n_slc, tq), 0)
    cur = (q0 + lax.broadcasted_iota(jnp.int32, (n_slc, tq), 1)) // SLC_BLOCK
    forced = (blk == 0) | (blk == cur) | (blk == cur - 1)
    score = jnp.where(forced, jnp.inf, jnp.where(blk <= cur, imp_t, -jnp.inf))
    sub = lax.broadcasted_iota(jnp.int32, (8, tq), 0)
    ranks = []
    for g0 in range(0, n_slc, 8):
        sg = score[g0:g0 + 8]
        rg = jnp.zeros((8, tq), F32)
        for j in range(n_slc):
            sj = score[j:j + 1, :]
            if j < g0:
                beats = sj >= sg
            elif j >= g0 + 8:
                beats = sj > sg
            else:
                beats = (sj > sg) | ((sj == sg) & (sub > j - g0))
            rg = rg + jnp.where(beats, 1.0, 0.0)
        ranks.append(rg)
    rank = jnp.concatenate(ranks, axis=0)
    sel_t = jnp.where(rank < top_n, 1.0, 0.0)
    if n_slc < d:
        sel_t = jnp.concatenate([sel_t, jnp.zeros((d - n_slc, tq), F32)], axis=0)
    sel_t2 = jnp.concatenate([sel_t, sel_t], axis=0).astype(BF16)
    eye = jnp.where(lax.broadcasted_iota(jnp.int32, (tq, tq), 0)
                    == lax.broadcasted_iota(jnp.int32, (tq, tq), 1), 1.0, 0.0).astype(BF16)
    sel2 = _dot_nt(eye, sel_t2)
    bias = ((sel2 - 1.0) * (-NEG_BIG)).astype(BF16)
    q_even = jnp.concatenate([jnp.where(left, pairs[0], bias), jnp.where(left, pairs[1], bias)], axis=0)
    q_odd = jnp.concatenate([jnp.where(left, bias, pairs[0]), jnp.where(left, bias, pairs[1])], axis=0)

    def slc_scores(start, width, tail):
        s = jnp.concatenate([_dot_nt(q_even, kse_ref[pl.ds(start, width), :]),
                             _dot_nt(q_odd, kso_ref[pl.ds(start, width), :])], axis=0) * LOG2_E
        if tail:
            s = jnp.where(start + lax.broadcasted_iota(jnp.int32, (1, width), 1) <= t4, s, NEG_BIG)
        return s

    (acc,) = _softmax_sweep((s_ref,), rows=rows, acc_lanes=LANES, q0=pl.multiple_of(q0, tq), tq=tq,
                            big=4 * tq, merge_tail=True, scores_fn=lambda *a: (slc_scores(*a),),
                            weigh_fn=lambda ps, start, width: (weigh(ps[0], start, width, vse_ref, vso_ref),))
    o_slc = normalise(acc)

    out = []
    for pr in range(2):
        cmp_pr = jnp.where(left, o_cmp[pr * tq:(pr + 1) * tq], o_cmp[half + pr * tq:half + (pr + 1) * tq])
        mixed = 0.0
        for br, o_br in enumerate((cmp_pr, o_slc[pr], o_win[pr])):
            blk_idx = pr * N_BRANCH + br
            mixed = mixed + gx[:, blk_idx * LANES:(blk_idx + 1) * LANES] * o_br
        out.append(mixed)
    o_ref[0] = jnp.concatenate(out, axis=1).astype(o_ref.dtype)


def _nsa_attention(pa, pf, ckv, ovt, *, tq, q_blk, kv_blk, g_blk, top_n):
    b, t, _ = pa.shape
    n16 = ckv.shape[2]
    n_slc = ovt.shape[0]
    qw = NSA_GROUP * HEAD_DIM
    kv_specs = [pl.BlockSpec((1, t, LANES), functools.partial(lambda bi, j, i, c: (bi, 0, c), c=kv_blk + n))
                for n in range(4)]
    return pl.pallas_call(
        functools.partial(_nsa_kernel, tq=tq, top_n=top_n),
        grid=(b, NSA_KV_HEADS, t // tq),
        in_specs=[
            pl.BlockSpec((1, tq, qw), lambda bi, j, i: (bi, i, q_blk + j)),
            pl.BlockSpec((1, tq, LANES), lambda bi, j, i: (bi, i, g_blk + j)),
            pl.BlockSpec((1, 1, n16, HEAD_DIM), lambda bi, j, i: (bi, j, 0, 0)),
            pl.BlockSpec((1, 1, n16, HEAD_DIM), lambda bi, j, i: (bi, NSA_KV_HEADS + j, 0, 0)),
            pl.BlockSpec((n_slc, n16), lambda bi, j, i: (0, 0)),
            pl.BlockSpec((LANES, 2 * N_BRANCH * LANES), lambda bi, j, i: (0, 0)),
            *kv_specs,
        ],
        out_specs=pl.BlockSpec((1, tq, qw), lambda bi, j, i: (bi, i, j)),
        out_shape=jax.ShapeDtypeStruct((b, t, NSA_HEADS * HEAD_DIM), BF16),
        scratch_shapes=[pltpu.VMEM((t, LANES), BF16)] * 7 + [pltpu.VMEM((n16, LANES), BF16)] * 2
                       + [pltpu.VMEM((NSA_GROUP * tq, t), F32)],
        compiler_params=_params(("arbitrary", "arbitrary", "arbitrary")),
        name="nsa_attention",
    )(pa, pf, ckv, ckv, ovt, _gate_spread(), pa, pa, pa, pa)


def _out_ln_kernel(a_ref, b_ref, c_ref, cb_ref, cc_ref, ch_ref, cw_ref, w_ref, x_ref, g_ref, beta_ref, o_ref,
                   prev_ref, *, alpha, tiles_per_seq):
    @pl.when(pl.program_id(0) % tiles_per_seq == 0)
    def _():
        prev_ref[...] = jnp.zeros_like(prev_ref)

    u = cc_ref[...] * ch_ref[...]
    tm = u.shape[0]
    row = lax.broadcasted_iota(jnp.int32, u.shape, 0)
    last1 = prev_ref[7:8]
    last2 = prev_ref[6:7]
    u1 = jnp.where(row >= 1, pltpu.roll(u, 1, 0), last1)
    u2 = jnp.where(row >= 2, pltpu.roll(u, 2, 0), jnp.where(row == 1, last1, last2))
    cw = cw_ref[...]
    o_cv = cb_ref[...] * (cw[0:1] * u2 + cw[1:2] * u1 + cw[2:3] * u)
    prev_ref[...] = u[tm - 8:tm]

    y = jnp.concatenate([a_ref[...], b_ref[...], c_ref[...], o_cv.astype(BF16)], axis=1)
    o_ref[...] = _layer_norm(alpha * x_ref[...] + _dot(y, w_ref[...]), g_ref[...], beta_ref[...])


def _out_ln(parts, pf, conv_w, w, x, g, b, *, layer, seq_len, alpha, tm):
    n, d = x.shape
    kw = parts[0].shape[1]
    part_spec = pl.BlockSpec((tm, kw), lambda i: (i, 0))
    return pl.pallas_call(
        functools.partial(_out_ln_kernel, alpha=alpha, tiles_per_seq=seq_len // tm),
        grid=(n // tm,),
        in_specs=[part_spec, part_spec, part_spec,
                  pl.BlockSpec((tm, CONV_CH), lambda i: (i, 0)),
                  pl.BlockSpec((tm, CONV_CH), lambda i: (i, 1)),
                  pl.BlockSpec((tm, CONV_CH), lambda i: (i, 2)),
                  pl.BlockSpec((None, CONV_WIDTH, CONV_CH), lambda i: (layer, 0, 0)),
                  pl.BlockSpec((None,) + w.shape[1:], lambda i: (layer, 0, 0)),
                  pl.BlockSpec((tm, d), lambda i: (i, 0)),
                  pl.BlockSpec((1, d), lambda i: (0, 0)),
                  pl.BlockSpec((1, d), lambda i: (0, 0))],
        out_specs=pl.BlockSpec((tm, d), lambda i: (i, 0)),
        out_shape=jax.ShapeDtypeStruct((n, d), F32),
        scratch_shapes=[pltpu.VMEM((8, CONV_CH), F32)],
        compiler_params=_params(("arbitrary",)),
        name="out_ln",
    )(*parts, pf, pf, pf, conv_w, w, x, g.reshape(1, d), b.reshape(1, d))


def _gate_spread():
    m = np.zeros((LANES, 2 * N_BRANCH * LANES), np.float32)
    for pr in range(2):
        for br in range(N_BRANCH):
            for lane in range(LANES):
                head = 2 * pr + lane // HEAD_DIM
                m[N_BRANCH * head + br, (pr * N_BRANCH + br) * LANES + lane] = 1.0
    return jnp.asarray(m, BF16)


def _overlap_t(t):
    n16 = t // CMP_STRIDE
    n_slc = t // SLC_BLOCK
    c_start = CMP_STRIDE * np.arange(n16)
    j_start = SLC_BLOCK * np.arange(n_slc)
    ov = ((c_start[None, :] < j_start[:, None] + SLC_BLOCK)
          & (c_start[None, :] + CMP_BLOCK > j_start[:, None])).astype(np.float32)
    ov[:, n16 - 1] = 0.0
    return jnp.asarray(ov, BF16)


def _mixer(hf, batch, w_proj, w_out, ln_g, ln_b, diff_lam, diff_gain, pos2, cmp_w1, cmp_w2,
           conv_w, layer, alpha):
    n, _ = hf.shape
    t = n // batch
    n_slc = t // SLC_BLOCK
    assert t % (4 * NSA_TQ) == 0 and t % max(DIFF_TQ, DIFF_CHUNK) == 0 and t % SB_TQ == 0 and n_slc <= HEAD_DIM
    assert t >= WINDOW + NSA_TQ and n % PROJ_ROWS == 0 and n % FFN_ROWS == 0 and t % OUT_ROWS == 0
    pa, pf = _proj(hf, w_proj, layer=layer, width_a=ATT_WIDTH, tm=PROJ_ROWS, tn=PROJ_COLS)
    pa, pf = pa.reshape(batch, t, -1), pf.reshape(batch, t, -1)

    def att_blk(name, lanes):
        return _ATT_OFF[name] // lanes

    w = SB_PAIRS * LANES
    o_sb = _sb_attention(pa, tq=SB_TQ, pairs=SB_PAIRS, q_blk=att_blk("sb_q", w), k_blk=att_blk("sb_k", w),
                         v_blk=att_blk("sb_v", w))
    lam_init = 0.8 - 0.6 * math.exp(-0.3 * layer)
    w = DIFF_HEADS_PER_STEP * LANES
    o_df = _diff_attention(pa, diff_lam[layer], diff_gain[layer], tq=DIFF_TQ, heads=DIFF_HEADS_PER_STEP,
                           q_blk=att_blk("df_q", w), k_blk=att_blk("df_k", w), v_blk=att_blk("df_v", w),
                           lam_init=lam_init)

    n16 = t // CMP_STRIDE
    kvc = pf[:, :, _F32_OFF["ns_kc"]:_F32_OFF["ns_kc"] + 2 * LANES].reshape(batch, t, 2, NSA_KV_HEADS, HEAD_DIM)
    x16 = kvc.transpose(0, 2, 3, 1, 4).reshape(batch, 2 * NSA_KV_HEADS, n16, CMP_STRIDE * HEAD_DIM)
    ckv = _compress(x16, pos2, cmp_w1, cmp_w2, layer=layer)
    o_ns = _nsa_attention(pa, pf, ckv, _overlap_t(t), tq=NSA_TQ, q_blk=att_blk("ns_q", NSA_GROUP * HEAD_DIM),
                          kv_blk=att_blk("ns_ks", LANES), g_blk=_GATE_OFF // LANES, top_n=min(SLC_TOPN, n_slc))

    assert _F32_OFF["cv_b"] == 0 and _F32_OFF["cv_c"] == CONV_CH and _F32_OFF["cv_h"] == 2 * CONV_CH
    parts = [o.reshape(n, -1) for o in (o_sb, o_df, o_ns)]
    return _out_ln(parts, pf.reshape(n, -1), conv_w, w_out, hf, ln_g, ln_b, layer=layer, seq_len=t, alpha=alpha,
                   tm=OUT_ROWS)


def _proj_weight_kernel(w_ref, o_ref):
    w = w_ref[...]
    pieces = [w[:, _SEG_OFF[s]:_SEG_OFF[s] + _SEG_W[s]] for s in _ATT_ORDER + _F32_ORDER]
    pad = jnp.zeros((w.shape[0], LANES - _GATES_PER_KV), w.dtype)
    for j in range(NSA_KV_HEADS):
        g0 = _SEG_OFF["ns_g"] + j * _GATES_PER_KV
        pieces += [w[:, g0:g0 + _GATES_PER_KV], pad]
    o_ref[...] = jnp.concatenate(pieces, axis=1).astype(o_ref.dtype)


def _proj_weight(w_in, *, rows):
    depth, d, in_w = w_in.shape
    out_w = ATT_WIDTH + _GATE_OFF + NSA_KV_HEADS * LANES
    return pl.pallas_call(
        _proj_weight_kernel,
        grid=(depth, d // rows),
        in_specs=[pl.BlockSpec((None, rows, in_w), lambda l, i: (l, i, 0))],
        out_specs=pl.BlockSpec((None, rows, out_w), lambda l, i: (l, i, 0)),
        out_shape=jax.ShapeDtypeStruct((depth, d, out_w), BF16),
        compiler_params=_params(("parallel", "parallel")),
        name="proj_weight",
    )(w_in)


def kernel(x, ln_g, ln_b, ffn_w1, ffn_w3, ffn_w2, w_in, w_out, diff_lam, diff_gain, cmp_pos, cmp_wk1,
           cmp_wk2, cmp_wv1, cmp_wv2, conv_w):
    batch, t, d = x.shape
    depth = ln_g.shape[0]
    alpha = (2 * depth) ** 0.25
    n = batch * t

    w1b, w3b, w2b = ffn_w1.astype(BF16), ffn_w3.astype(BF16), ffn_w2.astype(BF16)
    w_proj = _proj_weight(w_in, rows=256)
    w_outb = w_out.astype(BF16)
    half = CMP_STRIDE * HEAD_DIM
    pos2 = cmp_pos.reshape(depth, 2, half)
    cmp_w1 = jnp.stack([cmp_wk1, cmp_wv1], axis=1).astype(BF16)
    cmp_w2 = jnp.stack([cmp_wk2, cmp_wv2], axis=1).astype(BF16)

    hf = x.reshape(n, d)
    for l in range(depth):
        hf = _ffn_ln(hf, w1b, w3b, w2b, ln_g[l, 0], ln_b[l, 0], layer=l, slot=0, alpha=alpha, tm=FFN_ROWS,
                     tf=FFN_TILE)
        hf = _mixer(hf, batch, w_proj, w_outb, ln_g[l, 1], ln_b[l, 1], diff_lam, diff_gain,
                    pos2, cmp_w1, cmp_w2, conv_w, l, alpha)
        hf = _ffn_ln(hf, w1b, w3b, w2b, ln_g[l, 2], ln_b[l, 2], layer=l, slot=1, alpha=alpha, tm=FFN_ROWS,
                     tf=FFN_TILE)
    return hf.reshape(batch, t, d)
```

```python
import functools
import math

import numpy as np
import jax
import jax.numpy as jnp
from jax import lax
from jax.experimental import pallas as pl
from jax.experimental.pallas import tpu as pltpu

F32 = jnp.float32
BF16 = jnp.bfloat16

HEAD_DIM = 64
SB_HEADS = 8
DIFF_HEADS = 4
NSA_HEADS = 8
NSA_KV_HEADS = 2
NSA_GROUP = NSA_HEADS // NSA_KV_HEADS
CMP_BLOCK = 32
CMP_STRIDE = 16
SLC_BLOCK = 64
SLC_TOPN = 16
WINDOW = 512
N_BRANCH = 3
CONV_CH = 512
CONV_WIDTH = 3
LN_EPS = 1e-5
RMS_EPS = 1e-5
NEG_BIG = -1e30
LOG2_E = math.log2(math.e)
EXP2_UNDERFLOW = -150.0
LANES = 128
VMEM_LIMIT = 63 * 1024 * 1024
FFN_TILE = 512
FFN_ACC_COLS = 512
FFN_LN_ROWS = 128
FFN_ROWS = 1024

_SEG_NAMES = ("sb_q", "sb_k", "sb_v", "df_q", "df_k", "df_v", "ns_q", "ns_kc", "ns_vc",
              "ns_ks", "ns_vs", "ns_kw", "ns_vw", "ns_g", "cv_b", "cv_c", "cv_h")
_SEG_WIDTHS = (512, 512, 512, 512, 512, 512, 512, 128, 128, 128, 128, 128, 128,
               NSA_HEADS * N_BRANCH, 512, 512, 512)
_SEG_OFF = dict(zip(_SEG_NAMES, np.cumsum((0,) + _SEG_WIDTHS[:-1]).tolist()))
_SEG_W = dict(zip(_SEG_NAMES, _SEG_WIDTHS))


_GATES_PER_KV = NSA_GROUP * N_BRANCH
_ATT_ORDER = ("sb_q", "sb_k", "sb_v", "df_q", "df_k", "df_v", "ns_q", "ns_ks", "ns_vs", "ns_kw", "ns_vw")
_ATT_OFF = dict(zip(_ATT_ORDER, np.cumsum([0] + [_SEG_W[s] for s in _ATT_ORDER[:-1]]).tolist()))
ATT_WIDTH = sum(_SEG_W[s] for s in _ATT_ORDER)
_F32_ORDER = ("cv_b", "cv_c", "cv_h", "ns_kc", "ns_vc")
_F32_OFF = dict(zip(_F32_ORDER, np.cumsum([0] + [_SEG_W[s] for s in _F32_ORDER[:-1]]).tolist()))
_GATE_OFF = sum(_SEG_W[s] for s in _F32_ORDER)

SB_TQ, SB_PAIRS = 256, 4
DIFF_TQ, DIFF_HEADS_PER_STEP = 512, 2
DIFF_CHUNK = 1024
NSA_TQ = 256
PROJ_ROWS, PROJ_COLS = 1024, 1024
OUT_ROWS = 512


def _params(sem):
    return pltpu.CompilerParams(dimension_semantics=sem, vmem_limit_bytes=VMEM_LIMIT)


def _layer_norm(y, g, b):
    mu = jnp.mean(y, axis=-1, keepdims=True)
    d = y - mu
    var = jnp.mean(d * d, axis=-1, keepdims=True)
    return d * lax.rsqrt(var + LN_EPS) * g + b


def _dot(a, b):
    return jnp.dot(a, b, preferred_element_type=F32)


def _lane_group_max(s, mx):
    for g in range(s.shape[1] // LANES):
        mx = jnp.maximum(mx, s[:, g * LANES:(g + 1) * LANES])
    return mx


def _softmax_sweep(s_refs, *, rows, acc_lanes, q0, tq, big, merge_tail, scores_fn, weigh_fn):
    n_big = q0 // big
    mid = pl.multiple_of(n_big * big, tq)
    tail = (q0 - mid) // tq
    tail_widths = [(k + 1) * tq for k in range(big // tq)]

    def chunks(fn, init):
        return lax.fori_loop(0, n_big, lambda c, carry: fn(pl.multiple_of(c * big, big), big, carry), init)

    def scores(start, width, mxs, is_tail=False):
        out = []
        for ref, s, mx in zip(s_refs, scores_fn(start, width, is_tail), mxs):
            ref[:, pl.ds(start, width)] = s
            out.append(_lane_group_max(s, mx))
        return tuple(out)

    def tiles(fn, init):
        return lax.fori_loop(0, tail, lambda c, carry: fn(pl.multiple_of(mid + c * tq, tq), tq, carry), init)

    mxs = chunks(scores, (jnp.full((rows, LANES), NEG_BIG, F32),) * len(s_refs))
    if merge_tail:
        mxs = lax.switch(tail, [functools.partial(scores, mid, w, is_tail=True) for w in tail_widths], mxs)
    else:
        mxs = scores(q0, tq, tiles(scores, mxs), is_tail=True)
    ms = [jnp.broadcast_to(jnp.max(mx, axis=1, keepdims=True), (rows, LANES)) for mx in mxs]

    def weigh(start, width, accs):
        ps = [jnp.exp2(ref[:, pl.ds(start, width)] - jnp.concatenate([m] * (width // LANES), axis=1)).astype(BF16)
              for ref, m in zip(s_refs, ms)]
        return tuple(acc + w for acc, w in zip(accs, weigh_fn(ps, start, width)))

    accs = chunks(weigh, (jnp.zeros((rows, acc_lanes), F32),) * len(s_refs))
    if merge_tail:
        return lax.switch(tail, [functools.partial(weigh, mid, w) for w in tail_widths], accs)
    return weigh(q0, tq, tiles(weigh, accs))


def _dot_nt(a, b):
    return lax.dot_general(a, b, (((1,), (1,)), ((), ())), preferred_element_type=F32)


def _ffn_ln_kernel(x_ref, w1_ref, w3_ref, w2_ref, g_ref, b_ref, o_ref, xb_ref, *, alpha):
    j = pl.program_id(1)

    @pl.when(j == 0)
    def _():
        xb_ref[...] = x_ref[...].astype(BF16)
        o_ref[...] = jnp.zeros_like(o_ref)

    xb = xb_ref[...]
    a = _dot(xb, w1_ref[...])
    b = _dot(xb, w3_ref[...])
    hm = (a / (1.0 + jnp.exp(-a))) * b
    hb = hm.astype(BF16)
    for c in range(0, o_ref.shape[1], FFN_ACC_COLS):
        o_ref[:, c:c + FFN_ACC_COLS] += _dot(hb, w2_ref[:, c:c + FFN_ACC_COLS].astype(BF16))

    @pl.when(j == pl.num_programs(1) - 1)
    def _():
        for r in range(0, o_ref.shape[0], FFN_LN_ROWS):
            rows = slice(r, r + FFN_LN_ROWS)
            y = alpha * x_ref[rows] + 0.5 * o_ref[rows]
            o_ref[rows] = _layer_norm(y, g_ref[...], b_ref[...])


def _ffn_ln(x, w1, w3, w2, g, b, *, layer, slot, alpha, tm, tf):
    n, d = x.shape
    f = w1.shape[-1]
    return pl.pallas_call(
        functools.partial(_ffn_ln_kernel, alpha=alpha),
        grid=(n // tm, f // tf),
        in_specs=[
            pl.BlockSpec((tm, d), lambda i, j: (i, 0)),
            pl.BlockSpec((None, None, d, tf), lambda i, j: (layer, slot, 0, j)),
            pl.BlockSpec((None, None, d, tf), lambda i, j: (layer, slot, 0, j)),
            pl.BlockSpec((None, None, tf, d), lambda i, j: (layer, slot, j, 0)),
            pl.BlockSpec((1, d), lambda i, j: (0, 0)),
            pl.BlockSpec((1, d), lambda i, j: (0, 0)),
        ],
        out_specs=pl.BlockSpec((tm, d), lambda i, j: (i, 0)),
        out_shape=jax.ShapeDtypeStruct((n, d), F32),
        scratch_shapes=[pltpu.VMEM((tm, d), BF16)],
        compiler_params=_params(("parallel", "arbitrary")),
        name="ffn_ln",
    )(x, w1, w3, w2, g.reshape(1, d), b.reshape(1, d))


def _proj_kernel(x_ref, w_ref, oa_ref, of_ref, xb_ref, *, n_a):
    j = pl.program_id(1)

    @pl.when(j == 0)
    def _():
        xb_ref[...] = x_ref[...].astype(BF16)

    @pl.when(j < n_a)
    def _():
        oa_ref[...] = _dot(xb_ref[...], w_ref[...]).astype(oa_ref.dtype)

    @pl.when(j >= n_a)
    def _():
        of_ref[...] = _dot(xb_ref[...], w_ref[...])


def _proj(x, w, *, layer, width_a, tm, tn):
    n, d = x.shape
    m = w.shape[-1]
    n_a = width_a // tn
    return pl.pallas_call(
        functools.partial(_proj_kernel, n_a=n_a),
        grid=(n // tm, m // tn),
        in_specs=[pl.BlockSpec((tm, d), lambda i, j: (i, 0)),
                  pl.BlockSpec((None, d, tn), lambda i, j: (layer, 0, j))],
        out_specs=[pl.BlockSpec((tm, tn), lambda i, j: (i, jnp.minimum(j, n_a - 1))),
                   pl.BlockSpec((tm, tn), lambda i, j: (i, jnp.maximum(j - n_a, 0)))],
        out_shape=[jax.ShapeDtypeStruct((n, width_a), BF16), jax.ShapeDtypeStruct((n, m - width_a), F32)],
        scratch_shapes=[pltpu.VMEM((tm, d), BF16)],
        compiler_params=_params(("parallel", "arbitrary")),
        name="proj",
    )(x, w)


def _sb_kernel(q_ref, k_ref, v_ref, o_ref, *, tq, pairs):
    i = pl.program_id(2)
    q0 = i * tq
    lane = lax.broadcasted_iota(jnp.int32, (tq, LANES), 1)
    tri = jnp.where(lax.broadcasted_iota(jnp.int32, (tq, tq), 0)
                    >= lax.broadcasted_iota(jnp.int32, (tq, tq), 1), 1.0, 0.0).astype(BF16)
    qpos = q0 + lax.rem(lax.broadcasted_iota(jnp.int32, (2 * tq, 1), 0), tq)
    q2 = []
    for p in range(pairs):
        qs = (q_ref[0, :, p * LANES:(p + 1) * LANES].astype(F32) * (HEAD_DIM ** -0.5)).astype(BF16)
        zero = jnp.zeros_like(qs)
        q2.append(jnp.concatenate([jnp.where(lane < HEAD_DIM, qs, zero),
                                   jnp.where(lane < HEAD_DIM, zero, qs)], axis=0))

    def block(start, state, keep=None):
        width = tq
        masked = keep is not None
        out = []
        for p, (c, acc) in enumerate(state):
            ks = k_ref[0, pl.ds(start, width), p * LANES:(p + 1) * LANES]
            vs = v_ref[0, pl.ds(start, width), p * LANES:(p + 1) * LANES]
            z = _dot_nt(q2[p], ks) * LOG2_E
            nz = -z
            lg = jnp.minimum(nz, 0.0) - jnp.log2(1.0 + jnp.exp2(jnp.minimum(z, nz)))
            if masked:
                lg = jnp.where(keep, lg, 0.0)
            hi = lg.astype(BF16)
            lo = (lg - hi.astype(F32)).astype(BF16)
            suffix = _dot(hi, tri) + _dot(lo, tri)
            a = jnp.exp2(z + suffix + c)
            if masked:
                a = jnp.where(keep, a, 0.0)
            out.append((c + suffix[:, 0:1], acc + _dot(a.astype(BF16), vs)))
        return tuple(out)

    state = ((jnp.zeros((2 * tq, 1), F32), jnp.zeros((2 * tq, LANES), F32)),) * pairs
    key = lax.broadcasted_iota(jnp.int32, (1, tq), 1)
    state = block(pl.multiple_of(q0, tq), state, keep=q0 + key < qpos)
    no_prev = jnp.where(i > 0, 0.0, NEG_BIG)
    state = block(pl.multiple_of(jnp.maximum(q0 - tq, 0), tq), tuple((c + no_prev, acc) for c, acc in state))

    def live(state):
        top = functools.reduce(jnp.maximum, [c for c, _ in state])
        return (jnp.max(top) > EXP2_UNDERFLOW).astype(jnp.int32)

    def cond(carry):
        return (carry[0] >= 0) & (carry[1] > 0)

    def body(carry):
        kt, _, state = carry
        state = block(pl.multiple_of(kt * tq, tq), state)
        return kt - 1, live(state), state

    _, _, state = lax.while_loop(cond, body, (i - 2, live(state), state))
    o_ref[0] = jnp.concatenate([jnp.where(lane < HEAD_DIM, acc[:tq], acc[tq:]) for _, acc in state],
                               axis=1).astype(o_ref.dtype)


def _sb_attention(pa, *, tq, pairs, q_blk, k_blk, v_blk):
    b, t, _ = pa.shape
    w = pairs * LANES
    n_groups = SB_HEADS // (2 * pairs)
    return pl.pallas_call(
        functools.partial(_sb_kernel, tq=tq, pairs=pairs),
        grid=(b, n_groups, t // tq),
        in_specs=[
            pl.BlockSpec((1, tq, w), lambda bi, p, i: (bi, i, q_blk + p)),
            pl.BlockSpec((1, t, w), lambda bi, p, i: (bi, 0, k_blk + p)),
            pl.BlockSpec((1, t, w), lambda bi, p, i: (bi, 0, v_blk + p)),
        ],
        out_specs=pl.BlockSpec((1, tq, w), lambda bi, p, i: (bi, i, p)),
        out_shape=jax.ShapeDtypeStruct((b, t, SB_HEADS * HEAD_DIM), BF16),
        compiler_params=_params(("parallel", "parallel", "arbitrary")),
        name="sb_attention",
    )(pa, pa, pa)


def _diff_kernel(lam_ref, gain_ref, q_ref, k_ref, v_ref, o_ref, s_ref, *, tq, heads, lam_init):
    i = pl.program_id(2)
    q0 = pl.multiple_of(i * tq, tq)
    lane = lax.broadcasted_iota(jnp.int32, (tq, LANES), 1)
    qpos = q0 + lax.rem(lax.broadcasted_iota(jnp.int32, (2 * tq, 1), 0), tq)
    q2 = []
    for h in range(heads):
        qs = (q_ref[0, :, h * LANES:(h + 1) * LANES].astype(F32) * (HEAD_DIM ** -0.5)).astype(BF16)
        zero = jnp.zeros_like(qs)
        q2.append(jnp.concatenate([jnp.where(lane < HEAD_DIM, qs, zero),
                                   jnp.where(lane < HEAD_DIM, zero, qs)], axis=0))

    def scores(start, width, tail):
        out = []
        for h in range(heads):
            s = _dot_nt(q2[h], k_ref[0, pl.ds(start, width), h * LANES:(h + 1) * LANES]) * LOG2_E
            if tail:
                s = jnp.where(start + lax.broadcasted_iota(jnp.int32, (1, width), 1) <= qpos, s, NEG_BIG)
            out.append(s)
        return out

    def weigh(ps, start, width):
        ones = jnp.ones((width, LANES), BF16)
        return [_dot(p, jnp.concatenate([v_ref[0, pl.ds(start, width), h * LANES:(h + 1) * LANES], ones], axis=1))
                for h, p in enumerate(ps)]

    s_refs = [s_ref.at[pl.ds(h * 2 * tq, 2 * tq)] for h in range(heads)]
    accs = _softmax_sweep(s_refs, rows=2 * tq, acc_lanes=2 * LANES, q0=q0, tq=tq, big=max(tq, DIFF_CHUNK),
                          merge_tail=False, scores_fn=scores, weigh_fn=weigh)
    lp = lam_ref[...]
    lam = (jnp.exp(jnp.sum(lp[0:1] * lp[1:2], axis=1, keepdims=True))
           - jnp.exp(jnp.sum(lp[2:3] * lp[3:4], axis=1, keepdims=True)) + lam_init)
    out = []
    for acc in accs:
        o = acc[:, :LANES] / acc[:, LANES:]
        od = o[:tq] - lam * o[tq:]
        od = od * lax.rsqrt(jnp.mean(od * od, axis=-1, keepdims=True) + RMS_EPS)
        out.append(od * gain_ref[...] * (1.0 - lam_init))
    o_ref[0] = jnp.concatenate(out, axis=1).astype(o_ref.dtype)


def _diff_attention(pa, lam_params, gain, *, tq, heads, q_blk, k_blk, v_blk, lam_init):
    b, t, _ = pa.shape
    w = heads * LANES
    return pl.pallas_call(
        functools.partial(_diff_kernel, tq=tq, heads=heads, lam_init=lam_init),
        grid=(b, DIFF_HEADS // heads, t // tq),
        in_specs=[
            pl.BlockSpec((4, HEAD_DIM), lambda bi, h, i: (0, 0)),
            pl.BlockSpec((1, LANES), lambda bi, h, i: (0, 0)),
            pl.BlockSpec((1, tq, w), lambda bi, h, i: (bi, i, q_blk + h)),
            pl.BlockSpec((1, t, w), lambda bi, h, i: (bi, 0, k_blk + h)),
            pl.BlockSpec((1, t, w), lambda bi, h, i: (bi, 0, v_blk + h)),
        ],
        out_specs=pl.BlockSpec((1, tq, w), lambda bi, h, i: (bi, i, h)),
        out_shape=jax.ShapeDtypeStruct((b, t, DIFF_HEADS * LANES), BF16),
        scratch_shapes=[pltpu.VMEM((heads * 2 * tq, t), F32)],
        compiler_params=_params(("parallel", "parallel", "arbitrary")),
        name="diff_attention",
    )(lam_params, gain.reshape(1, LANES), pa, pa, pa)


def _compress_kernel(x_ref, pos_ref, w1_ref, w2_ref, o_ref):
    x = x_ref[0, 0]
    n16, half = x.shape
    xa = (x + pos_ref[0:1]).astype(BF16)
    xb = (x + pos_ref[1:2]).astype(BF16)
    w1 = w1_ref[0]
    first = _dot(xa, w1[:half])
    second = _dot(xb, w1[half:])
    h = first + pltpu.roll(second, n16 - 1, 0)
    g = 0.5 * h * (1.0 + jnp.tanh(math.sqrt(2.0 / math.pi) * (h + 0.044715 * (h * h * h))))
    o_ref[0, 0] = _dot(g.astype(BF16), w2_ref[0])


def _compress(x16, pos2, w1, w2, *, layer):
    b, n_streams, n16, half = x16.shape
    d = w2.shape[-1]
    return pl.pallas_call(
        _compress_kernel,
        grid=(b, n_streams),
        in_specs=[
            pl.BlockSpec((1, 1, n16, half), lambda bi, s: (bi, s, 0, 0)),
            pl.BlockSpec((None, 2, half), lambda bi, s: (layer, 0, 0)),
            pl.BlockSpec((None, 1, 2 * half, w1.shape[-1]), lambda bi, s: (layer, s // NSA_KV_HEADS, 0, 0)),
            pl.BlockSpec((None, 1, w2.shape[2], d), lambda bi, s: (layer, s // NSA_KV_HEADS, 0, 0)),
        ],
        out_specs=pl.BlockSpec((1, 1, n16, d), lambda bi, s: (bi, s, 0, 0)),
        out_shape=jax.ShapeDtypeStruct((b, n_streams, n16, d), F32),
        compiler_params=_params(("parallel", "arbitrary")),
        name="nsa_compress",
    )(x16, pos2, w1, w2)


def _nsa_kernel(q_ref, g_ref, ck_ref, cv_ref, ovt_ref, gx_ref, ks_ref, vs_ref, kw_ref, vw_ref, o_ref,
                kse_ref, kso_ref, vse_ref, vso_ref, kwd_ref, vwe_ref, vwo_ref, ckd_ref, cvd_ref, s_ref,
                *, tq, top_n):
    i = pl.program_id(2)
    grp = NSA_GROUP
    d = HEAD_DIM
    q0 = i * tq
    n_cmp = ck_ref.shape[2]
    n_slc = ovt_ref.shape[0]
    t = ks_ref.shape[1]
    rows = grp * tq
    half = rows // 2

    @pl.when(i == 0)
    def _():
        onehot = jnp.where(lax.broadcasted_iota(jnp.int32, (t, d), 0) // SLC_BLOCK
                           == lax.broadcasted_iota(jnp.int32, (t, d), 1), 1.0, 0.0)
        ones = jnp.ones((t, d), F32)

        def put(ref, left, right):
            ref[...] = jnp.concatenate([left, right], axis=1).astype(BF16)

        first_kv = pl.program_id(1) == 0

        def own_head(ref):
            both = ref[0].astype(F32)
            return jnp.where(first_kv, both[:, :d], both[:, d:])

        ks, vs, kw, vw = own_head(ks_ref), own_head(vs_ref), own_head(kw_ref), own_head(vw_ref)
        put(kse_ref, ks, onehot)
        put(kso_ref, onehot, ks)
        put(vse_ref, vs, ones)
        put(vso_ref, ones, vs)
        put(kwd_ref, kw, kw)
        put(vwe_ref, vw, ones)
        put(vwo_ref, ones, vw)
        put(ckd_ref, ck_ref[0, 0], ck_ref[0, 0])
        put(cvd_ref, cv_ref[0, 0], cv_ref[0, 0])

    lane = lax.broadcasted_iota(jnp.int32, (tq, LANES), 1)
    left = lane < d
    qs = (q_ref[0].astype(F32) * (d ** -0.5)).astype(BF16)
    pairs = [qs[:, :LANES], qs[:, LANES:]]
    zero = jnp.zeros((tq, LANES), BF16)
    qz = jnp.concatenate([jnp.where(left, pairs[0], zero), jnp.where(left, pairs[1], zero),
                          jnp.where(left, zero, pairs[0]), jnp.where(left, zero, pairs[1])], axis=0)
    t4 = q0 + lax.rem(lax.broadcasted_iota(jnp.int32, (rows, 1), 0), tq)

    def weigh(p, start, width, ve_ref, vo_ref):
        return jnp.concatenate([_dot(p[:half], ve_ref[pl.ds(start, width), :]),
                                _dot(p[half:], vo_ref[pl.ds(start, width), :])], axis=0)

    def normalise(acc):
        out = []
        for pr in range(2):
            even, odd = acc[pr * tq:(pr + 1) * tq], acc[half + pr * tq:half + (pr + 1) * tq]
            den = pltpu.roll(jnp.where(left, odd, even), d, 1)
            out.append(jnp.where(left, even, odd) / den)
        return out

    w_width = (-(-WINDOW // tq) + 1) * tq
    w_start = pl.multiple_of(jnp.maximum(q0 + tq - w_width, 0), tq)
    back = (q0 + lax.broadcasted_iota(jnp.int32, (tq, 1), 0)
            - (w_start + lax.broadcasted_iota(jnp.int32, (1, w_width), 1)))
    off = jnp.where((back >= 0) & (back < WINDOW), 0.0, NEG_BIG)
    sw = _dot_nt(qz, kwd_ref[pl.ds(w_start, w_width), :]) + jnp.concatenate([off] * grp, axis=0)
    pw = jnp.exp(sw - jnp.max(sw, axis=1, keepdims=True))
    o_win = normalise(weigh(pw.astype(BF16), w_start, w_width, vwe_ref, vwo_ref))

    gate = 1.0 / (1.0 + jnp.exp(-g_ref[0]))
    g_hi = gate.astype(BF16)
    g_r = gate - g_hi.astype(F32)
    g_mid = g_r.astype(BF16)
    g_lo = (g_r - g_mid.astype(F32)).astype(BF16)
    spread = gx_ref[...]
    gx = _dot(g_hi, spread) + _dot(g_mid, spread) + _dot(g_lo, spread)

    s = _dot_nt(qz, ckd_ref[...])
    cmp_end = CMP_STRIDE * lax.broadcasted_iota(jnp.int32, (1, n_cmp), 1) + (CMP_BLOCK - 1)
    cmask = cmp_end <= t4
    s = jnp.where(cmask, s, NEG_BIG)
    e = jnp.exp(s - jnp.max(s, axis=1, keepdims=True))
    p = e / jnp.sum(e, axis=1, keepdims=True)
    p = jnp.where(cmask, p, 0.0)
    o_cmp = _dot(p.astype(BF16), cvd_ref[...])

    pg = p[0:tq]
    for h in range(1, grp):
        pg = pg + p[h * tq:(h + 1) * tq]
    p_hi = pg.astype(BF16)
    r1 = pg - p_hi.astype(F32)
    p_mid = r1.astype(BF16)
    p_lo = (r1 - p_mid.astype(F32)).astype(BF16)
    ovt = ovt_ref[...]
    imp_t = _dot_nt(ovt, p_hi) + _dot_nt(ovt, p_mid) + _dot_nt(ovt, p_lo)
    blk = lax.broadcasted_iota(jnp.int32, (n_slc, tq), 0)
    cur = (q0 + lax.broadcasted_iota(jnp.int32, (n_slc, tq), 1)) // SLC_BLOCK
    forced = (blk == 0) | (blk == cur) | (blk == cur - 1)
    score = jnp.where(forced, jnp.inf, jnp.where(blk <= cur, imp_t, -jnp.inf))
    sub = lax.broadcasted_iota(jnp.int32, (8, tq), 0)
    ranks = []
    for g0 in range(0, n_slc, 8):
        sg = score[g0:g0 + 8]
        rg = jnp.zeros((8, tq), F32)
        for j in range(n_slc):
            sj = score[j:j + 1, :]
            if j < g0:
                beats = sj >= sg
            elif j >= g0 + 8:
                beats = sj > sg
            else:
                beats = (sj > sg) | ((sj == sg) & (sub > j - g0))
            rg = rg + jnp.where(beats, 1.0, 0.0)
        ranks.append(rg)
    rank = jnp.concatenate(ranks, axis=0)
    sel_t = jnp.where(rank < top_n, 1.0, 0.0)
    if n_slc < d:
        sel_t = jnp.concatenate([sel_t, jnp.zeros((d - n_slc, tq), F32)], axis=0)
    sel_t2 = jnp.concatenate([sel_t, sel_t], axis=0).astype(BF16)
    eye = jnp.where(lax.broadcasted_iota(jnp.int32, (tq, tq), 0)
                    == lax.broadcasted_iota(jnp.int32, (tq, tq), 1), 1.0, 0.0).astype(BF16)
    sel2 = _dot_nt(eye, sel_t2)
    bias = ((sel2 - 1.0) * (-NEG_BIG)).astype(BF16)
    q_even = jnp.concatenate([jnp.where(left, pairs[0], bias), jnp.where(left, pairs[1], bias)], axis=0)
    q_odd = jnp.concatenate([jnp.where(left, bias, pairs[0]), jnp.where(left, bias, pairs[1])], axis=0)

    def slc_scores(start, width, tail):
        s = jnp.concatenate([_dot_nt(q_even, kse_ref[pl.ds(start, width), :]),
                             _dot_nt(q_odd, kso_ref[pl.ds(start, width), :])], axis=0) * LOG2_E
        if tail:
            s = jnp.where(start + lax.broadcasted_iota(jnp.int32, (1, width), 1) <= t4, s, NEG_BIG)
        return s

    (acc,) = _softmax_sweep((s_ref,), rows=rows, acc_lanes=LANES, q0=pl.multiple_of(q0, tq), tq=tq,
                            big=4 * tq, merge_tail=True, scores_fn=lambda *a: (slc_scores(*a),),
                            weigh_fn=lambda ps, start, width: (weigh(ps[0], start, width, vse_ref, vso_ref),))
    o_slc = normalise(acc)

    out = []
    for pr in range(2):
        cmp_pr = jnp.where(left, o_cmp[pr * tq:(pr + 1) * tq], o_cmp[half + pr * tq:half + (pr + 1) * tq])
        mixed = 0.0
        for br, o_br in enumerate((cmp_pr, o_slc[pr], o_win[pr])):
            blk_idx = pr * N_BRANCH + br
            mixed = mixed + gx[:, blk_idx * LANES:(blk_idx + 1) * LANES] * o_br
        out.append(mixed)
    o_ref[0] = jnp.concatenate(out, axis=1).astype(o_ref.dtype)


def _nsa_attention(pa, pf, ckv, ovt, *, tq, q_blk, kv_blk, g_blk, top_n):
    b, t, _ = pa.shape
    n16 = ckv.shape[2]
    n_slc = ovt.shape[0]
    qw = NSA_GROUP * HEAD_DIM
    kv_specs = [pl.BlockSpec((1, t, LANES), functools.partial(lambda bi, j, i, c: (bi, 0, c), c=kv_blk + n))
                for n in range(4)]
    return pl.pallas_call(
        functools.partial(_nsa_kernel, tq=tq, top_n=top_n),
        grid=(b, NSA_KV_HEADS, t // tq),
        in_specs=[
            pl.BlockSpec((1, tq, qw), lambda bi, j, i: (bi, i, q_blk + j)),
            pl.BlockSpec((1, tq, LANES), lambda bi, j, i: (bi, i, g_blk + j)),
            pl.BlockSpec((1, 1, n16, HEAD_DIM), lambda bi, j, i: (bi, j, 0, 0)),
            pl.BlockSpec((1, 1, n16, HEAD_DIM), lambda bi, j, i: (bi, NSA_KV_HEADS + j, 0, 0)),
            pl.BlockSpec((n_slc, n16), lambda bi, j, i: (0, 0)),
            pl.BlockSpec((LANES, 2 * N_BRANCH * LANES), lambda bi, j, i: (0, 0)),
            *kv_specs,
        ],
        out_specs=pl.BlockSpec((1, tq, qw), lambda bi, j, i: (bi, i, j)),
        out_shape=jax.ShapeDtypeStruct((b, t, NSA_HEADS * HEAD_DIM), BF16),
        scratch_shapes=[pltpu.VMEM((t, LANES), BF16)] * 7 + [pltpu.VMEM((n16, LANES), BF16)] * 2
                       + [pltpu.VMEM((NSA_GROUP * tq, t), F32)],
        compiler_params=_params(("arbitrary", "arbitrary", "arbitrary")),
        name="nsa_attention",
    )(pa, pf, ckv, ckv, ovt, _gate_spread(), pa, pa, pa, pa)


def _out_ln_kernel(a_ref, b_ref, c_ref, cb_ref, cc_ref, ch_ref, cw_ref, w_ref, x_ref, g_ref, beta_ref, o_ref,
                   prev_ref, *, alpha, tiles_per_seq):
    @pl.when(pl.program_id(0) % tiles_per_seq == 0)
    def _():
        prev_ref[...] = jnp.zeros_like(prev_ref)

    u = cc_ref[...] * ch_ref[...]
    tm = u.shape[0]
    row = lax.broadcasted_iota(jnp.int32, u.shape, 0)
    last1 = prev_ref[7:8]
    last2 = prev_ref[6:7]
    u1 = jnp.where(row >= 1, pltpu.roll(u, 1, 0), last1)
    u2 = jnp.where(row >= 2, pltpu.roll(u, 2, 0), jnp.where(row == 1, last1, last2))
    cw = cw_ref[...]
    o_cv = cb_ref[...] * (cw[0:1] * u2 + cw[1:2] * u1 + cw[2:3] * u)
    prev_ref[...] = u[tm - 8:tm]

    y = jnp.concatenate([a_ref[...], b_ref[...], c_ref[...], o_cv.astype(BF16)], axis=1)
    o_ref[...] = _layer_norm(alpha * x_ref[...] + _dot(y, w_ref[...]), g_ref[...], beta_ref[...])


def _out_ln(parts, pf, conv_w, w, x, g, b, *, layer, seq_len, alpha, tm):
    n, d = x.shape
    kw = parts[0].shape[1]
    part_spec = pl.BlockSpec((tm, kw), lambda i: (i, 0))
    return pl.pallas_call(
        functools.partial(_out_ln_kernel, alpha=alpha, tiles_per_seq=seq_len // tm),
        grid=(n // tm,),
        in_specs=[part_spec, part_spec, part_spec,
                  pl.BlockSpec((tm, CONV_CH), lambda i: (i, 0)),
                  pl.BlockSpec((tm, CONV_CH), lambda i: (i, 1)),
                  pl.BlockSpec((tm, CONV_CH), lambda i: (i, 2)),
                  pl.BlockSpec((None, CONV_WIDTH, CONV_CH), lambda i: (layer, 0, 0)),
                  pl.BlockSpec((None,) + w.shape[1:], lambda i: (layer, 0, 0)),
                  pl.BlockSpec((tm, d), lambda i: (i, 0)),
                  pl.BlockSpec((1, d), lambda i: (0, 0)),
                  pl.BlockSpec((1, d), lambda i: (0, 0))],
        out_specs=pl.BlockSpec((tm, d), lambda i: (i, 0)),
        out_shape=jax.ShapeDtypeStruct((n, d), F32),
        scratch_shapes=[pltpu.VMEM((8, CONV_CH), F32)],
        compiler_params=_params(("arbitrary",)),
        name="out_ln",
    )(*parts, pf, pf, pf, conv_w, w, x, g.reshape(1, d), b.reshape(1, d))


def _gate_spread():
    m = np.zeros((LANES, 2 * N_BRANCH * LANES), np.float32)
    for pr in range(2):
        for br in range(N_BRANCH):
            for lane in range(LANES):
                head = 2 * pr + lane // HEAD_DIM
                m[N_BRANCH * head + br, (pr * N_BRANCH + br) * LANES + lane] = 1.0
    return jnp.asarray(m, BF16)


def _overlap_t(t):
    n16 = t // CMP_STRIDE
    n_slc = t // SLC_BLOCK
    c_start = CMP_STRIDE * np.arange(n16)
    j_start = SLC_BLOCK * np.arange(n_slc)
    ov = ((c_start[None, :] < j_start[:, None] + SLC_BLOCK)
          & (c_start[None, :] + CMP_BLOCK > j_start[:, None])).astype(np.float32)
    ov[:, n16 - 1] = 0.0
    return jnp.asarray(ov, BF16)


def _mixer(hf, batch, w_proj, w_out, ln_g, ln_b, diff_lam, diff_gain, pos2, cmp_w1, cmp_w2,
           conv_w, layer, alpha):
    n, _ = hf.shape
    t = n // batch
    n_slc = t // SLC_BLOCK
    assert t % (4 * NSA_TQ) == 0 and t % max(DIFF_TQ, DIFF_CHUNK) == 0 and t % SB_TQ == 0 and n_slc <= HEAD_DIM
    assert t >= WINDOW + NSA_TQ and n % PROJ_ROWS == 0 and n % FFN_ROWS == 0 and t % OUT_ROWS == 0
    pa, pf = _proj(hf, w_proj, layer=layer, width_a=ATT_WIDTH, tm=PROJ_ROWS, tn=PROJ_COLS)
    pa, pf = pa.reshape(batch, t, -1), pf.reshape(batch, t, -1)

    def att_blk(name, lanes):
        return _ATT_OFF[name] // lanes

    w = SB_PAIRS * LANES
    o_sb = _sb_attention(pa, tq=SB_TQ, pairs=SB_PAIRS, q_blk=att_blk("sb_q", w), k_blk=att_blk("sb_k", w),
                         v_blk=att_blk("sb_v", w))
    lam_init = 0.8 - 0.6 * math.exp(-0.3 * layer)
    w = DIFF_HEADS_PER_STEP * LANES
    o_df = _diff_attention(pa, diff_lam[layer], diff_gain[layer], tq=DIFF_TQ, heads=DIFF_HEADS_PER_STEP,
                           q_blk=att_blk("df_q", w), k_blk=att_blk("df_k", w), v_blk=att_blk("df_v", w),
                           lam_init=lam_init)

    n16 = t // CMP_STRIDE
    kvc = pf[:, :, _F32_OFF["ns_kc"]:_F32_OFF["ns_kc"] + 2 * LANES].reshape(batch, t, 2, NSA_KV_HEADS, HEAD_DIM)
    x16 = kvc.transpose(0, 2, 3, 1, 4).reshape(batch, 2 * NSA_KV_HEADS, n16, CMP_STRIDE * HEAD_DIM)
    ckv = _compress(x16, pos2, cmp_w1, cmp_w2, layer=layer)
    o_ns = _nsa_attention(pa, pf, ckv, _overlap_t(t), tq=NSA_TQ, q_blk=att_blk("ns_q", NSA_GROUP * HEAD_DIM),
                          kv_blk=att_blk("ns_ks", LANES), g_blk=_GATE_OFF // LANES, top_n=min(SLC_TOPN, n_slc))

    assert _F32_OFF["cv_b"] == 0 and _F32_OFF["cv_c"] == CONV_CH and _F32_OFF["cv_h"] == 2 * CONV_CH
    parts = [o.reshape(n, -1) for o in (o_sb, o_df, o_ns)]
    return _out_ln(parts, pf.reshape(n, -1), conv_w, w_out, hf, ln_g, ln_b, layer=layer, seq_len=t, alpha=alpha,
                   tm=OUT_ROWS)


def _proj_weight_kernel(w_ref, o_ref):
    w = w_ref[...]
    pieces = [w[_SEG_OFF[s]:_SEG_OFF[s] + _SEG_W[s]] for s in _ATT_ORDER + _F32_ORDER]
    n_g = _SEG_W["ns_g"]
    gates = jnp.concatenate([w[_SEG_OFF["ns_g"]:_SEG_OFF["ns_g"] + n_g],
                             jnp.zeros((LANES - n_g, w.shape[1]), w.dtype)], axis=0)
    own = lax.broadcasted_iota(jnp.int32, gates.shape, 0) < _GATES_PER_KV
    for j in range(NSA_KV_HEADS):
        shifted = gates if j == 0 else pltpu.roll(gates, LANES - j * _GATES_PER_KV, 0)
        pieces.append(jnp.where(own, shifted, 0.0))
    o_ref[...] = jnp.concatenate(pieces, axis=0).T.astype(o_ref.dtype)


def _proj_weight(w_in, *, cols):
    depth, d, in_w = w_in.shape
    out_w = ATT_WIDTH + _GATE_OFF + NSA_KV_HEADS * LANES
    return pl.pallas_call(
        _proj_weight_kernel,
        grid=(depth, d // cols),
        in_specs=[pl.BlockSpec((None, in_w, cols), lambda l, i: (l, 0, i))],
        out_specs=pl.BlockSpec((None, cols, out_w), lambda l, i: (l, i, 0)),
        out_shape=jax.ShapeDtypeStruct((depth, d, out_w), BF16),
        compiler_params=_params(("parallel", "parallel")),
        name="proj_weight",
    )(jnp.swapaxes(w_in, 1, 2))


def kernel(x, ln_g, ln_b, ffn_w1, ffn_w3, ffn_w2, w_in, w_out, diff_lam, diff_gain, cmp_pos, cmp_wk1,
           cmp_wk2, cmp_wv1, cmp_wv2, conv_w):
    batch, t, d = x.shape
    depth = ln_g.shape[0]
    alpha = (2 * depth) ** 0.25
    n = batch * t

    w1b, w3b, w2b = ffn_w1.astype(BF16), ffn_w3.astype(BF16), ffn_w2
    w_proj = _proj_weight(w_in, cols=256)
    w_outb = w_out.astype(BF16)
    half = CMP_STRIDE * HEAD_DIM
    pos2 = cmp_pos.reshape(depth, 2, half)
    cmp_w1 = jnp.stack([cmp_wk1, cmp_wv1], axis=1).astype(BF16)
    cmp_w2 = jnp.stack([cmp_wk2, cmp_wv2], axis=1).astype(BF16)

    hf = x.reshape(n, d)
    for l in range(depth):
        hf = _ffn_ln(hf, w1b, w3b, w2b, ln_g[l, 0], ln_b[l, 0], layer=l, slot=0, alpha=alpha, tm=FFN_ROWS,
                     tf=FFN_TILE)
        hf = _mixer(hf, batch, w_proj, w_outb, ln_g[l, 1], ln_b[l, 1], diff_lam, diff_gain,
                    pos2, cmp_w1, cmp_w2, conv_w, l, alpha)
        hf = _ffn_ln(hf, w1b, w3b, w2b, ln_g[l, 2], ln_b[l, 2], layer=l, slot=1, alpha=alpha, tm=FFN_ROWS,
                     tf=FFN_TILE)
    return hf.reshape(batch, t, d)
```
